```python
import math
import jax, jax.numpy as jnp
from jax import lax
import numpy as np

D_MODEL = 1024
BATCH = 4
SEQ = 4096
DEPTH = 2
DEC_BATCH = 1
DEC_SEQ = 16384
PAST_LEN = 128

A_WIDTH = D_MODEL // 2
B_WIDTH = D_MODEL - A_WIDTH
A_HEAD_DIM = 64
A_V_DIM = 2 * A_HEAD_DIM
A_HEADS = A_WIDTH // A_V_DIM
B_V_DIM = 128
B_HEADS = B_WIDTH // B_V_DIM
B_NOPE = 128
B_ROPE = 64
B_QK_DIM = B_NOPE + B_ROPE
Q_LORA = D_MODEL // 4
KV_LORA = D_MODEL // 8
ROPE_THETA = 10000.0
Q_BLOCK = 128
EPS = 1e-6

A_Q_COLS = A_HEADS * 2 * A_HEAD_DIM
A_K_COLS = A_HEADS * 2 * A_HEAD_DIM
A_V_COLS = A_HEADS * A_V_DIM
A_G_COLS = A_WIDTH
B_CQ_COLS = Q_LORA
B_CKV_COLS = KV_LORA
B_KR_COLS = B_ROPE
B_G_COLS = B_WIDTH
COL_SIZES = [A_Q_COLS, A_K_COLS, A_V_COLS, A_G_COLS, B_CQ_COLS, B_CKV_COLS, B_KR_COLS, B_G_COLS]
IN_COLS = sum(COL_SIZES)
SPLIT_POINTS = [int(v) for v in np.cumsum(COL_SIZES)[:-1]]

kernel_name = "hymba_diffattn_mla_encoder"


def rms_norm(x, g):
    xf = x.astype(jnp.float32)
    y = xf * lax.rsqrt(jnp.mean(xf * xf, axis=-1, keepdims=True) + EPS)
    return (y * g.astype(jnp.float32)).astype(x.dtype)


def alibi_slopes(n):
    return jnp.exp2(-8.0 * jnp.arange(1, n + 1, dtype=jnp.float32) / n)


def lambda_init_value(layer_idx):
    return 0.8 - 0.6 * math.exp(-0.3 * layer_idx)


def apply_rope(x):
    s = x.shape[1]
    half = B_ROPE // 2
    inv = ROPE_THETA ** (-jnp.arange(0, B_ROPE, 2, dtype=jnp.float32) / B_ROPE)
    ang = jnp.arange(s, dtype=jnp.float32)[:, None] * inv[None, :]
    cos = jnp.cos(ang)[None, :, None, :]
    sin = jnp.sin(ang)[None, :, None, :]
    xf = x.astype(jnp.float32)
    x1, x2 = xf[..., :half], xf[..., half:]
    return jnp.concatenate([x1 * cos - x2 * sin, x1 * sin + x2 * cos], axis=-1).astype(x.dtype)


def diff_attention(q, k, v, lam):
    b, s, h = q.shape[:3]
    nb = s // Q_BLOCK
    qb = q.reshape(b, nb, Q_BLOCK, h, 2, A_HEAD_DIM).transpose(1, 0, 2, 3, 4, 5)
    starts = jnp.arange(nb, dtype=jnp.int32) * Q_BLOCK
    slopes = alibi_slopes(h)
    kpos = jnp.arange(s, dtype=jnp.int32)

    def block(args):
        qi, s0 = args
        qpos = s0 + jnp.arange(Q_BLOCK, dtype=jnp.int32)
        dist = jnp.abs(qpos[:, None] - kpos[None, :]).astype(jnp.float32)
        bias = -slopes[:, None, None, None] * dist[None, None]
        sc = jnp.einsum('bqhcd,bkhcd->bhcqk', qi, k).astype(jnp.float32) + bias
        p = jax.nn.softmax(sc, axis=-1)
        p = p[:, :, 0] - lam * p[:, :, 1]
        return jnp.einsum('bhqk,bkhd->bqhd', p.astype(v.dtype), v)

    o = lax.map(block, (qb, starts))
    return o.transpose(1, 0, 2, 3, 4).reshape(b, s, h, A_V_DIM)


def latent_attention(q, k, v):
    b, s, h = q.shape[:3]
    nb = s // Q_BLOCK
    qb = q.reshape(b, nb, Q_BLOCK, h, B_QK_DIM).transpose(1, 0, 2, 3, 4)

    def block(qi):
        sc = jnp.einsum('bqhd,bkhd->bhqk', qi, k).astype(jnp.float32)
        p = jax.nn.softmax(sc, axis=-1)
        return jnp.einsum('bhqk,bkhd->bqhd', p.astype(v.dtype), v)

    o = lax.map(block, qb)
    return o.transpose(1, 0, 2, 3, 4).reshape(b, s, h, B_V_DIM)


def encoder_layer(x, layer_idx, norm_w, w_in, a_q_norm, a_k_norm, a_lq1, a_lk1, a_lq2, a_lk2,
                  a_subln, b_cq_norm, b_w_uq, b_ckv_norm, b_w_ukv, b_q_norm, b_k_norm, w_out):
    b, s, _ = x.shape
    h = rms_norm(x, norm_w)
    proj = h @ w_in
    aq, ak, av, ag, cq, ckv, kr, bg = jnp.split(proj, SPLIT_POINTS, axis=-1)

    aq = rms_norm(aq.reshape(b, s, A_HEADS, 2, A_HEAD_DIM), a_q_norm) * (A_HEAD_DIM ** -0.5)
    ak = rms_norm(ak.reshape(b, s, A_HEADS, 2, A_HEAD_DIM), a_k_norm)
    av = av.reshape(b, s, A_HEADS, A_V_DIM)
    lam_init = lambda_init_value(layer_idx)
    lam = (jnp.exp(jnp.sum(a_lq1.astype(jnp.float32) * a_lk1.astype(jnp.float32)))
           - jnp.exp(jnp.sum(a_lq2.astype(jnp.float32) * a_lk2.astype(jnp.float32)))
           + lam_init)
    oa = diff_attention(aq, ak, av, lam)
    oa = rms_norm(oa, a_subln) * (1.0 - lam_init)
    ya = oa.reshape(b, s, A_WIDTH) * jax.nn.silu(ag)

    cq = rms_norm(cq, b_cq_norm)
    q = (cq @ b_w_uq).reshape(b, s, B_HEADS, B_QK_DIM)
    ckv = rms_norm(ckv, b_ckv_norm)
    kv = (ckv @ b_w_ukv).reshape(b, s, B_HEADS, B_NOPE + B_V_DIM)
    k_nope, bv = kv[..., :B_NOPE], kv[..., B_NOPE:]
    k_rope = jnp.broadcast_to(kr[:, :, None, :], (b, s, B_HEADS, B_ROPE))
    k = jnp.concatenate([k_nope, k_rope], axis=-1)
    q = rms_norm(q, b_q_norm)
    k = rms_norm(k, b_k_norm)
    q = jnp.concatenate([q[..., :B_NOPE], apply_rope(q[..., B_NOPE:])], axis=-1) * (B_QK_DIM ** -0.5)
    k = jnp.concatenate([k[..., :B_NOPE], apply_rope(k[..., B_NOPE:])], axis=-1)
    ob = latent_attention(q, k, bv)
    yb = ob.reshape(b, s, B_WIDTH) * jax.nn.silu(bg)

    out = jnp.concatenate([ya, yb], axis=-1) @ w_out
    return x + out


def setup_inputs(seed: int = 0) -> dict:
    key = jax.random.key(seed)
    ks = jax.random.split(key, 20)
    f32 = jnp.float32

    def nrm(k, shape, scale):
        return jax.random.normal(k, shape, f32) * scale

    def gain(k, shape):
        return 1.0 + 0.1 * jax.random.normal(k, shape, f32)

    return {
        "x_prompt": jax.random.normal(ks[0], (BATCH, SEQ, D_MODEL), f32),
        "x_sample": jax.random.normal(ks[1], (DEC_BATCH, DEC_SEQ, D_MODEL), f32),
        "norm_w": gain(ks[2], (DEPTH, D_MODEL)),
        "w_in": nrm(ks[3], (DEPTH, D_MODEL, IN_COLS), D_MODEL ** -0.5),
        "a_q_norm": gain(ks[4], (DEPTH, A_HEAD_DIM)),
        "a_k_norm": gain(ks[5], (DEPTH, A_HEAD_DIM)),
        "a_lq1": nrm(ks[6], (DEPTH, A_HEAD_DIM), 0.1),
        "a_lk1": nrm(ks[7], (DEPTH, A_HEAD_DIM), 0.1),
        "a_lq2": nrm(ks[8], (DEPTH, A_HEAD_DIM), 0.1),
        "a_lk2": nrm(ks[9], (DEPTH, A_HEAD_DIM), 0.1),
        "a_subln": gain(ks[10], (DEPTH, A_V_DIM)),
        "b_cq_norm": gain(ks[11], (DEPTH, Q_LORA)),
        "b_w_uq": nrm(ks[12], (DEPTH, Q_LORA, B_HEADS * B_QK_DIM), Q_LORA ** -0.5),
        "b_ckv_norm": gain(ks[13], (DEPTH, KV_LORA)),
        "b_w_ukv": nrm(ks[14], (DEPTH, KV_LORA, B_HEADS * (B_NOPE + B_V_DIM)), KV_LORA ** -0.5),
        "b_q_norm": gain(ks[15], (DEPTH, B_QK_DIM)),
        "b_k_norm": gain(ks[16], (DEPTH, B_QK_DIM)),
        "w_out": nrm(ks[17], (DEPTH, D_MODEL, D_MODEL), D_MODEL ** -0.5),
    }


def reference(x_prompt, x_sample, norm_w, w_in, a_q_norm, a_k_norm, a_lq1, a_lk1, a_lq2, a_lk2,
              a_subln, b_cq_norm, b_w_uq, b_ckv_norm, b_w_ukv, b_q_norm, b_k_norm, w_out):
    y_prompt = x_prompt
    y_sample = x_sample
    for l in range(DEPTH):
        p = (norm_w[l], w_in[l], a_q_norm[l], a_k_norm[l], a_lq1[l], a_lk1[l], a_lq2[l], a_lk2[l],
             a_subln[l], b_cq_norm[l], b_w_uq[l], b_ckv_norm[l], b_w_ukv[l], b_q_norm[l], b_k_norm[l], w_out[l])
        y_prompt = encoder_layer(y_prompt, l, *p)
        y_sample = encoder_layer(y_sample, l, *p)
    return (y_prompt, y_sample)
```

```python
import functools
import math

import numpy as np
import jax
import jax.numpy as jnp
from jax import lax
from jax.experimental import pallas as pl
from jax.experimental.pallas import tpu as pltpu

D_MODEL = 1024
A_HEADS = 4
A_HEAD_DIM = 64
A_V_DIM = 128
A_UNITS = 2 * A_HEADS
B_HEADS = 4
B_NOPE = 128
B_ROPE = 64
B_QK_DIM = B_NOPE + B_ROPE
B_V_DIM = 128
Q_LORA = 256
KV_LORA = 128
ROPE_THETA = 10000.0
EPS = 1e-6
LANES = 128

_SPLITS = np.cumsum([0, 512, 512, 512, 512, Q_LORA, KV_LORA, B_ROPE, 512])
_P_AQ, _P_AK, _P_AV, _P_AG, _P_CQ, _P_CKV, _P_KR, _P_BG, _P_END = np.cumsum(
    [0, A_UNITS * LANES, A_UNITS * LANES, 512, 512, Q_LORA, KV_LORA, LANES, 512])

_VMEM_LIMIT = 56 * 1024 * 1024

_F32 = jnp.float32
_BF16 = jnp.bfloat16


def _lambda_init(layer_idx):
    return 0.8 - 0.6 * math.exp(-0.3 * layer_idx)


def _rope_swap_index():
    half = B_ROPE // 2
    return np.concatenate([np.arange(half, B_ROPE), np.arange(0, half)])


def _silu(x):
    return x / (1.0 + jnp.exp(-x))


def _proj_kernel(x_ref, nw_ref, win_ref, gaq_ref, gak_ref, gcq_ref, wuq_ref, gckv_ref, wukv_ref,
                 gbq_ref, gbkn_ref, gbkr_ref, ct_ref, st_ref,
                 qa_ref, ka_ref, va_ref, ga_ref, qb_ref, kb_ref, vb_ref, gb_ref):
    tm = x_ref.shape[0]
    x = x_ref[...]
    h = x * lax.rsqrt(jnp.mean(x * x, axis=-1, keepdims=True) + EPS) * nw_ref[...]
    hb = h.astype(_BF16)

    def proj(lo, hi):
        return jnp.dot(hb, win_ref[:, lo:hi], preferred_element_type=_F32)

    lane = lax.broadcasted_iota(jnp.int32, (tm, LANES), 1)
    low_half = lane < B_ROPE
    ones_col = (lane == 0).astype(_F32)
    ct = ct_ref[...]
    st = st_ref[...]

    aq = proj(_P_AQ, _P_AK)
    ak = proj(_P_AK, _P_AV)
    for u in range(A_UNITS):
        xq = aq[:, u * LANES:(u + 1) * LANES]
        rq = lax.rsqrt(jnp.sum(xq * xq, axis=-1, keepdims=True) * (1.0 / A_HEAD_DIM) + EPS)
        qa_ref[u] = (xq * rq * gaq_ref[...] * (A_HEAD_DIM ** -0.5)).astype(_BF16)
        xk = ak[:, u * LANES:(u + 1) * LANES]
        rk = lax.rsqrt(jnp.sum(xk * xk, axis=-1, keepdims=True) * (1.0 / A_HEAD_DIM) + EPS)
        ka_ref[u] = (xk * rk * gak_ref[...]).astype(_BF16)
    av = proj(_P_AV, _P_AG)
    for hd in range(A_HEADS):
        va_ref[hd] = jnp.concatenate([av[:, hd * LANES:(hd + 1) * LANES], ones_col], axis=1).astype(_BF16)
    ga_ref[...] = _silu(proj(_P_AG, _P_CQ)).astype(_BF16)

    cq = proj(_P_CQ, _P_CKV)
    cqn = cq * lax.rsqrt(jnp.mean(cq * cq, axis=-1, keepdims=True) + EPS) * gcq_ref[...]
    qall = jnp.dot(cqn.astype(_BF16), wuq_ref[...], preferred_element_type=_F32)
    for hd in range(B_HEADS):
        q0 = qall[:, hd * 256:hd * 256 + LANES]
        q1 = qall[:, hd * 256 + LANES:(hd + 1) * 256]
        ss = (jnp.sum(q0 * q0, axis=-1, keepdims=True)
              + jnp.sum(jnp.where(low_half, q1 * q1, 0.0), axis=-1, keepdims=True))
        r = lax.rsqrt(ss * (1.0 / B_QK_DIM) + EPS)
        q0n = q0 * r * gbq_ref[:, :LANES]
        q1n = q1 * r * gbq_ref[:, LANES:]
        q1r = q1n * ct + pltpu.roll(q1n, B_ROPE, 1) * st
        qb_ref[hd] = (jnp.concatenate([q0n, q1r], axis=1) * (B_QK_DIM ** -0.5)).astype(_BF16)

    ckv = proj(_P_CKV, _P_KR)
    ckvn = ckv * lax.rsqrt(jnp.mean(ckv * ckv, axis=-1, keepdims=True) + EPS) * gckv_ref[...]
    kv = jnp.dot(ckvn.astype(_BF16), wukv_ref[...], preferred_element_type=_F32)
    kr2 = proj(_P_KR, _P_BG)
    kr_ss = jnp.sum(jnp.where(low_half, kr2 * kr2, 0.0), axis=-1, keepdims=True)
    krg = kr2 * gbkr_ref[...]
    krr = krg * ct + pltpu.roll(krg, B_ROPE, 1) * st
    for hd in range(B_HEADS):
        kn = kv[:, hd * 256:hd * 256 + LANES]
        vv = kv[:, hd * 256 + LANES:(hd + 1) * 256]
        r = lax.rsqrt((jnp.sum(kn * kn, axis=-1, keepdims=True) + kr_ss) * (1.0 / B_QK_DIM) + EPS)
        kb_ref[hd] = jnp.concatenate([kn * r * gbkn_ref[...], krr * r], axis=1).astype(_BF16)
        vb_ref[hd] = jnp.concatenate([vv, ones_col], axis=1).astype(_BF16)
    gb_ref[...] = _silu(proj(_P_BG, _P_END)).astype(_BF16)


def _proj_call(x2d, seq_len, lw, tm):
    t = x2d.shape[0]
    blocks_per_seq = seq_len // tm
    const = lambda i: (0, 0)
    row = lambda i: (i, 0)
    unit_row = lambda i: (0, i, 0)
    rope_row = lambda i: (i % blocks_per_seq, 0)
    in_specs = [
        pl.BlockSpec((tm, D_MODEL), row),
        pl.BlockSpec((1, D_MODEL), const),
        pl.BlockSpec((D_MODEL, int(_P_END)), const),
        pl.BlockSpec((1, LANES), const),
        pl.BlockSpec((1, LANES), const),
        pl.BlockSpec((1, Q_LORA), const),
        pl.BlockSpec((Q_LORA, B_HEADS * 256), const),
        pl.BlockSpec((1, KV_LORA), const),
        pl.BlockSpec((KV_LORA, B_HEADS * 256), const),
        pl.BlockSpec((1, 256), const),
        pl.BlockSpec((1, LANES), const),
        pl.BlockSpec((1, LANES), const),
        pl.BlockSpec((tm, LANES), rope_row),
        pl.BlockSpec((tm, LANES), rope_row),
    ]
    out_shape = [
        jax.ShapeDtypeStruct((A_UNITS, t, LANES), _BF16),
        jax.ShapeDtypeStruct((A_UNITS, t, LANES), _BF16),
        jax.ShapeDtypeStruct((A_HEADS, t, 256), _BF16),
        jax.ShapeDtypeStruct((t, 512), _BF16),
        jax.ShapeDtypeStruct((B_HEADS, t, 256), _BF16),
        jax.ShapeDtypeStruct((B_HEADS, t, 256), _BF16),
        jax.ShapeDtypeStruct((B_HEADS, t, 256), _BF16),
        jax.ShapeDtypeStruct((t, 512), _BF16),
    ]
    out_specs = [
        pl.BlockSpec((A_UNITS, tm, LANES), unit_row),
        pl.BlockSpec((A_UNITS, tm, LANES), unit_row),
        pl.BlockSpec((A_HEADS, tm, 256), unit_row),
        pl.BlockSpec((tm, 512), row),
        pl.BlockSpec((B_HEADS, tm, 256), unit_row),
        pl.BlockSpec((B_HEADS, tm, 256), unit_row),
        pl.BlockSpec((B_HEADS, tm, 256), unit_row),
        pl.BlockSpec((tm, 512), row),
    ]
    return pl.pallas_call(
        _proj_kernel,
        grid=(t // tm,),
        in_specs=in_specs,
        out_specs=out_specs,
        out_shape=out_shape,
        compiler_params=pltpu.CompilerParams(
            dimension_semantics=("parallel",), vmem_limit_bytes=_VMEM_LIMIT),
        name="proj",
    )(x2d, lw["norm_w"], lw["w_in"], lw["g_aq"], lw["g_ak"], lw["g_cq"], lw["w_uq"], lw["g_ckv"],
      lw["w_ukv"], lw["g_bq"], lw["g_bkn"], lw["g_bkr"], lw["rope_c"][seq_len], lw["rope_s"][seq_len])


def _flash_kernel(slope_ref, q_ref, k_ref, v_ref, o_ref, m_sc, acc_sc, *, alibi, heads_per_slope):
    kj = pl.program_id(3)
    tq = q_ref.shape[2]
    tk = k_ref.shape[2]

    @pl.when(kj == 0)
    def _():
        m_sc[...] = jnp.full(m_sc.shape, -jnp.inf, _F32)
        acc_sc[...] = jnp.zeros(acc_sc.shape, _F32)

    s = lax.dot_general(q_ref[0, 0], k_ref[0, 0], (((1,), (1,)), ((), ())),
                        preferred_element_type=_F32)
    if alibi:
        slope = slope_ref[pl.program_id(0) // heads_per_slope]
        qpos = pl.program_id(2) * tq + lax.broadcasted_iota(jnp.int32, (tq, tk), 0)
        kpos = kj * tk + lax.broadcasted_iota(jnp.int32, (tq, tk), 1)
        s = s - slope * jnp.abs(qpos - kpos).astype(_F32)
    m_prev = m_sc[...]
    m_new = jnp.maximum(m_prev, jnp.max(s, axis=-1, keepdims=True))
    alpha = jnp.exp(m_prev - m_new)
    p = jnp.exp(s - m_new)
    acc_sc[...] = alpha * acc_sc[...] + jnp.dot(p.astype(_BF16), v_ref[0, 0],
                                                 preferred_element_type=_F32)
    m_sc[...] = m_new

    @pl.when(kj == pl.num_programs(3) - 1)
    def _():
        acc = acc_sc[...]
        o_ref[0, 0] = acc[:, :LANES] / acc[:, LANES:LANES + 1]


def _flash_call(q, k, v, slopes, *, alibi, units_per_v, tq, tk):
    u, b, s, dk = q.shape
    kern = functools.partial(_flash_kernel, alibi=alibi, heads_per_slope=units_per_v)
    grid_spec = pltpu.PrefetchScalarGridSpec(
        num_scalar_prefetch=1,
        grid=(u, b, s // tq, s // tk),
        in_specs=[
            pl.BlockSpec((1, 1, tq, dk), lambda ui, bi, qi, ki, sl: (ui, bi, qi, 0)),
            pl.BlockSpec((1, 1, tk, dk), lambda ui, bi, qi, ki, sl: (ui, bi, ki, 0)),
            pl.BlockSpec((1, 1, tk, 256), lambda ui, bi, qi, ki, sl: (ui // units_per_v, bi, ki, 0)),
        ],
        out_specs=pl.BlockSpec((1, 1, tq, LANES), lambda ui, bi, qi, ki, sl: (ui, bi, qi, 0)),
        scratch_shapes=[pltpu.VMEM((tq, 1), _F32), pltpu.VMEM((tq, 256), _F32)],
    )
    return pl.pallas_call(
        kern,
        grid_spec=grid_spec,
        out_shape=jax.ShapeDtypeStruct((u, b, s, LANES), _F32),
        compiler_params=pltpu.CompilerParams(
            dimension_semantics=("parallel", "parallel", "parallel", "arbitrary"),
            vmem_limit_bytes=_VMEM_LIMIT),
        name="flash_a" if alibi else "flash_b",
    )(slopes, q, k, v)


def _out_kernel(oa_ref, ob_ref, ga_ref, gb_ref, x_ref, lq1_ref, lk1_ref, lq2_ref, lk2_ref, subln_ref,
                wout_ref, y_ref, *, lam_init):
    lam = (jnp.exp(jnp.sum(lq1_ref[...] * lk1_ref[...], axis=-1, keepdims=True))
           - jnp.exp(jnp.sum(lq2_ref[...] * lk2_ref[...], axis=-1, keepdims=True)) + lam_init)
    ga = ga_ref[...].astype(_F32)
    gb = gb_ref[...].astype(_F32)
    pieces = []
    for hd in range(A_HEADS):
        d = oa_ref[2 * hd] - lam * oa_ref[2 * hd + 1]
        n = d * lax.rsqrt(jnp.mean(d * d, axis=-1, keepdims=True) + EPS) * subln_ref[...]
        pieces.append(n * (1.0 - lam_init) * ga[:, hd * LANES:(hd + 1) * LANES])
    for hd in range(B_HEADS):
        pieces.append(ob_ref[hd] * gb[:, hd * LANES:(hd + 1) * LANES])
    y = jnp.concatenate(pieces, axis=1).astype(_BF16)
    y_ref[...] = x_ref[...] + jnp.dot(y, wout_ref[...], preferred_element_type=_F32)


def _out_call(oa, ob, ga, gb, x2d, lw, lam_init, tm):
    t = x2d.shape[0]
    const = lambda i: (0, 0)
    row = lambda i: (i, 0)
    unit_row = lambda i: (0, i, 0)
    return pl.pallas_call(
        functools.partial(_out_kernel, lam_init=lam_init),
        grid=(t // tm,),
        in_specs=[
            pl.BlockSpec((A_UNITS, tm, LANES), unit_row),
            pl.BlockSpec((B_HEADS, tm, LANES), unit_row),
            pl.BlockSpec((tm, 512), row),
            pl.BlockSpec((tm, 512), row),
            pl.BlockSpec((tm, D_MODEL), row),
            pl.BlockSpec((1, A_HEAD_DIM), const),
            pl.BlockSpec((1, A_HEAD_DIM), const),
            pl.BlockSpec((1, A_HEAD_DIM), const),
            pl.BlockSpec((1, A_HEAD_DIM), const),
            pl.BlockSpec((1, A_V_DIM), const),
            pl.BlockSpec((D_MODEL, D_MODEL), const),
        ],
        out_specs=pl.BlockSpec((tm, D_MODEL), row),
        out_shape=jax.ShapeDtypeStruct((t, D_MODEL), _F32),
        compiler_params=pltpu.CompilerParams(
            dimension_semantics=("parallel",), vmem_limit_bytes=_VMEM_LIMIT),
        name="out",
    )(oa, ob, ga, gb, x2d, lw["lq1"], lw["lk1"], lw["lq2"], lw["lk2"], lw["subln"], lw["w_out"])


def _rope_tables(seq_len):
    inv = ROPE_THETA ** (-jnp.arange(0, B_ROPE, 2, dtype=_F32) / B_ROPE)
    ang = jnp.arange(seq_len, dtype=_F32)[:, None] * inv[None, :]
    cos, sin = jnp.cos(ang), jnp.sin(ang)
    zeros = jnp.zeros((seq_len, LANES - B_ROPE), _F32)
    return (jnp.concatenate([cos, cos, zeros], axis=1),
            jnp.concatenate([-sin, sin, zeros], axis=1))


def _pad_lanes(v, width):
    return jnp.pad(v, ((0, 0), (0, width - v.shape[1])))


def _layer_weights(l, p, rope_c, rope_s):
    swap = _rope_swap_index()
    w = p["w_in"][l]
    cols = [w[:, _SPLITS[i]:_SPLITS[i + 1]] for i in range(8)]
    pad_units = lambda c: jnp.pad(c.reshape(D_MODEL, A_UNITS, A_HEAD_DIM),
                                  ((0, 0), (0, 0), (0, LANES - A_HEAD_DIM))).reshape(D_MODEL, A_UNITS * LANES)
    kr = cols[6]
    w_in = jnp.concatenate([pad_units(cols[0]), pad_units(cols[1]), cols[2], cols[3], cols[4], cols[5],
                            kr, kr[:, swap], cols[7]], axis=1).astype(_BF16)
    wuq = p["b_w_uq"][l].reshape(Q_LORA, B_HEADS, B_QK_DIM)
    w_uq = jnp.concatenate([wuq, wuq[:, :, B_NOPE:][:, :, swap]], axis=2).reshape(Q_LORA, B_HEADS * 256)
    gq = p["b_q_norm"][l]
    gk = p["b_k_norm"][l]
    row = lambda v: v.reshape(1, -1).astype(_F32)
    return {
        "norm_w": row(p["norm_w"][l]),
        "w_in": w_in,
        "g_aq": _pad_lanes(row(p["a_q_norm"][l]), LANES),
        "g_ak": _pad_lanes(row(p["a_k_norm"][l]), LANES),
        "g_cq": row(p["b_cq_norm"][l]),
        "w_uq": w_uq.astype(_BF16),
        "g_ckv": row(p["b_ckv_norm"][l]),
        "w_ukv": p["b_w_ukv"][l].astype(_BF16),
        "g_bq": row(jnp.concatenate([gq, gq[B_NOPE:][swap]])),
        "g_bkn": row(gk[:B_NOPE]),
        "g_bkr": row(jnp.concatenate([gk[B_NOPE:], gk[B_NOPE:][swap]])),
        "rope_c": rope_c,
        "rope_s": rope_s,
        "lq1": row(p["a_lq1"][l]), "lk1": row(p["a_lk1"][l]),
        "lq2": row(p["a_lq2"][l]), "lk2": row(p["a_lk2"][l]),
        "subln": row(p["a_subln"][l]),
        "w_out": p["w_out"][l].astype(_BF16),
    }


def _tile(n, pref):
    return pref if n % pref == 0 else n


def _encoder_layer(x, lw, lam_init, slopes):
    b, s, _ = x.shape
    t = b * s
    x2d = x.reshape(t, D_MODEL)
    tm = _tile(s, 256)
    qa, ka, va, ga, qb, kb, vb, gb = _proj_call(x2d, s, lw, tm)
    split = lambda a: a.reshape(a.shape[0], b, s, a.shape[2])
    tq = _tile(s, 512)
    tk = _tile(s, 512)
    oa = _flash_call(split(qa), split(ka), split(va), slopes, alibi=True, units_per_v=2, tq=tq, tk=tk)
    ob = _flash_call(split(qb), split(kb), split(vb), slopes, alibi=False, units_per_v=1, tq=tq, tk=tk)
    y = _out_call(oa.reshape(A_UNITS, t, LANES), ob.reshape(B_HEADS, t, LANES), ga, gb, x2d, lw, lam_init, tm)
    return y.reshape(b, s, D_MODEL)


def kernel(x_prompt, x_sample, norm_w, w_in, a_q_norm, a_k_norm, a_lq1, a_lk1, a_lq2, a_lk2, a_subln,
           b_cq_norm, b_w_uq, b_ckv_norm, b_w_ukv, b_q_norm, b_k_norm, w_out):
    p = dict(norm_w=norm_w, w_in=w_in, a_q_norm=a_q_norm, a_k_norm=a_k_norm, a_lq1=a_lq1, a_lk1=a_lk1,
             a_lq2=a_lq2, a_lk2=a_lk2, a_subln=a_subln, b_cq_norm=b_cq_norm, b_w_uq=b_w_uq,
             b_ckv_norm=b_ckv_norm, b_w_ukv=b_w_ukv, b_q_norm=b_q_norm, b_k_norm=b_k_norm, w_out=w_out)
    depth = norm_w.shape[0]
    rope_c, rope_s = {}, {}
    for s in {x_prompt.shape[1], x_sample.shape[1]}:
        rope_c[s], rope_s[s] = _rope_tables(s)
    slopes = jnp.exp2(-8.0 * jnp.arange(1, A_HEADS + 1, dtype=_F32) / A_HEADS)
    y_prompt, y_sample = x_prompt, x_sample
    for l in range(depth):
        lw = _layer_weights(l, p, rope_c, rope_s)
        lam_init = _lambda_init(l)
        y_prompt = _encoder_layer(y_prompt, lw, lam_init, slopes)
        y_sample = _encoder_layer(y_sample, lw, lam_init, slopes)
    return (y_prompt, y_sample)
```

```python
import functools
import math

import numpy as np
import jax
import jax.numpy as jnp
from jax import lax
from jax.experimental import pallas as pl
from jax.experimental.pallas import tpu as pltpu

D_MODEL = 1024
A_HEADS = 4
A_HEAD_DIM = 64
A_V_DIM = 128
A_UNITS = 2 * A_HEADS
B_HEADS = 4
B_NOPE = 128
B_ROPE = 64
B_QK_DIM = B_NOPE + B_ROPE
B_V_DIM = 128
Q_LORA = 256
KV_LORA = 128
ROPE_THETA = 10000.0
EPS = 1e-6
LANES = 128

_SPLITS = np.cumsum([0, 512, 512, 512, 512, Q_LORA, KV_LORA, B_ROPE, 512])
_P_AQ, _P_AK, _P_AV, _P_AG, _P_CQ, _P_CKV, _P_KR, _P_BG, _P_END = np.cumsum(
    [0, A_UNITS * LANES, A_UNITS * LANES, 512, 512, Q_LORA, KV_LORA, LANES, 512])

_VMEM_LIMIT = 56 * 1024 * 1024
LOG2E = math.log2(math.e)
MAX_STATIC_SHIFT = 32.0
POS_SPLIT = 128
SIGMA_PIECES = 3

_A_SHIFT = A_HEAD_DIM
_A_QHI = _A_SHIFT + 1
_A_QLO = _A_QHI + SIGMA_PIECES
_A_KHI = _A_QLO + SIGMA_PIECES
_A_KLO = _A_KHI + SIGMA_PIECES
_A_AUG_END = _A_KLO + SIGMA_PIECES
_B_SHIFT = B_QK_DIM - LANES

_F32 = jnp.float32
_BF16 = jnp.bfloat16


def _lambda_init(layer_idx):
    return 0.8 - 0.6 * math.exp(-0.3 * layer_idx)


def _alibi_slope(head):
    return 2.0 ** (-8.0 * (head + 1) / A_HEADS)


def _sigma_pieces(head):
    rest = _alibi_slope(head) * LOG2E
    pieces = []
    for _ in range(SIGMA_PIECES):
        p = float(np.asarray(rest, np.float32).astype(_BF16).astype(np.float32))
        pieces.append(p)
        rest -= p
    return pieces


def _rope_swap_index():
    half = B_ROPE // 2
    return np.concatenate([np.arange(half, B_ROPE), np.arange(0, half)])


def _silu(x):
    return x / (1.0 + jnp.exp(-x))


def _proj_kernel(x_ref, nw_ref, win_ref, gaq_ref, gak_ref, gcq_ref, wuq_ref, gckv_ref, wukv_ref,
                 gbq_ref, gbkn_ref, gbkr_ref, ct_ref, st_ref, augq_ref, augk_ref, augb_ref,
                 qa_ref, ka_ref, va_ref, ga_ref, qb_ref, kb_ref, vb_ref, gb_ref, *, fixed_shift, seq_len):
    tm = x_ref.shape[0]
    x = x_ref[...]
    h = x * lax.rsqrt(jnp.mean(x * x, axis=-1, keepdims=True) + EPS) * nw_ref[...]
    hb = h.astype(_BF16)

    def proj(lo, hi):
        return jnp.dot(hb, win_ref[:, lo:hi], preferred_element_type=_F32)

    lane = lax.broadcasted_iota(jnp.int32, (tm, LANES), 1)
    low_half = lane < B_ROPE
    ones_col = (lane == 0).astype(_F32)
    ct = ct_ref[...]
    st = st_ref[...]

    if fixed_shift:
        a_qscale = (A_HEAD_DIM ** -0.5) * LOG2E
        b_qscale = (B_QK_DIM ** -0.5) * LOG2E
        pos = (pl.program_id(0) % (seq_len // tm)) * tm + lax.broadcasted_iota(jnp.int32, (tm, LANES), 0)
        pos = pos - seq_len // 2
        pos_hi = ((pos >> 7) * POS_SPLIT).astype(_F32)
        pos_lo = (pos & (POS_SPLIT - 1)).astype(_F32)
        q_hi_lanes = (lane >= _A_QHI) & (lane < _A_QLO)
        q_lo_lanes = (lane >= _A_QLO) & (lane < _A_KHI)
        k_hi_lanes = (lane >= _A_KHI) & (lane < _A_KLO)
        k_lo_lanes = (lane >= _A_KLO) & (lane < _A_AUG_END)
        q_pos = jnp.where(q_hi_lanes, pos_hi, jnp.where(q_lo_lanes, pos_lo, 0.0))
        k_pos = jnp.where(k_hi_lanes, pos_hi, jnp.where(k_lo_lanes, pos_lo, 0.0))
        bias_lanes = (lane >= _A_QHI) & (lane < _A_AUG_END)
    else:
        a_qscale = A_HEAD_DIM ** -0.5
        b_qscale = B_QK_DIM ** -0.5

    aq = proj(_P_AQ, _P_AK)
    ak = proj(_P_AK, _P_AV)
    for u in range(A_UNITS):
        xq = aq[:, u * LANES:(u + 1) * LANES]
        rq = lax.rsqrt(jnp.sum(xq * xq, axis=-1, keepdims=True) * (1.0 / A_HEAD_DIM) + EPS)
        qn = xq * rq * gaq_ref[...] * a_qscale
        xk = ak[:, u * LANES:(u + 1) * LANES]
        rk = lax.rsqrt(jnp.sum(xk * xk, axis=-1, keepdims=True) * (1.0 / A_HEAD_DIM) + EPS)
        kn = xk * rk * gak_ref[...]
        if fixed_shift:
            q_keys_before = qn + augq_ref[u] + q_pos
            qa_ref[u, 0] = q_keys_before.astype(_BF16)
            qa_ref[u, 1] = jnp.where(bias_lanes, -q_keys_before, q_keys_before).astype(_BF16)
            ka_ref[u] = (kn + augk_ref[u] + k_pos).astype(_BF16)
        else:
            qa_ref[u, 0] = qn.astype(_BF16)
            ka_ref[u] = kn.astype(_BF16)
    av = proj(_P_AV, _P_AG)
    for hd in range(A_HEADS):
        va_ref[hd] = jnp.concatenate([av[:, hd * LANES:(hd + 1) * LANES], ones_col], axis=1).astype(_BF16)
    ga_ref[...] = _silu(proj(_P_AG, _P_CQ)).astype(_BF16)

    cq = proj(_P_CQ, _P_CKV)
    cqn = cq * lax.rsqrt(jnp.mean(cq * cq, axis=-1, keepdims=True) + EPS) * gcq_ref[...]
    qall = jnp.dot(cqn.astype(_BF16), wuq_ref[...], preferred_element_type=_F32)
    for hd in range(B_HEADS):
        q0 = qall[:, hd * 256:hd * 256 + LANES]
        q1 = qall[:, hd * 256 + LANES:(hd + 1) * 256]
        ss = (jnp.sum(q0 * q0, axis=-1, keepdims=True)
              + jnp.sum(jnp.where(low_half, q1 * q1, 0.0), axis=-1, keepdims=True))
        r = lax.rsqrt(ss * (1.0 / B_QK_DIM) + EPS)
        q0n = q0 * r * gbq_ref[:, :LANES] * b_qscale
        q1n = q1 * r * gbq_ref[:, LANES:]
        q1r = (q1n * ct + pltpu.roll(q1n, B_ROPE, 1) * st) * b_qscale
        if fixed_shift:
            q1r = q1r + augb_ref[0:1, :]
        qb_ref[hd, 0] = jnp.concatenate([q0n, q1r], axis=1).astype(_BF16)

    ckv = proj(_P_CKV, _P_KR)
    ckvn = ckv * lax.rsqrt(jnp.mean(ckv * ckv, axis=-1, keepdims=True) + EPS) * gckv_ref[...]
    kv = jnp.dot(ckvn.astype(_BF16), wukv_ref[...], preferred_element_type=_F32)
    kr2 = proj(_P_KR, _P_BG)
    kr_ss = jnp.sum(jnp.where(low_half, kr2 * kr2, 0.0), axis=-1, keepdims=True)
    krg = kr2 * gbkr_ref[...]
    krr = krg * ct + pltpu.roll(krg, B_ROPE, 1) * st
    for hd in range(B_HEADS):
        kn = kv[:, hd * 256:hd * 256 + LANES]
        vv = kv[:, hd * 256 + LANES:(hd + 1) * 256]
        r = lax.rsqrt((jnp.sum(kn * kn, axis=-1, keepdims=True) + kr_ss) * (1.0 / B_QK_DIM) + EPS)
        k1 = krr * r
        if fixed_shift:
            k1 = k1 + augb_ref[1:2, :]
        kb_ref[hd] = jnp.concatenate([kn * r * gbkn_ref[...], k1], axis=1).astype(_BF16)
        vb_ref[hd] = jnp.concatenate([vv, ones_col], axis=1).astype(_BF16)
    gb_ref[...] = _silu(proj(_P_BG, _P_END)).astype(_BF16)


def _proj_call(x2d, seq_len, lw, tm, fixed_shift):
    t = x2d.shape[0]
    blocks_per_seq = seq_len // tm
    n_var = 2 if fixed_shift else 1
    const = lambda i: (0, 0)
    const3 = lambda i: (0, 0, 0)
    row = lambda i: (i, 0)
    unit_row = lambda i: (0, i, 0)
    var_row = lambda i: (0, 0, i, 0)
    rope_row = lambda i: (i % blocks_per_seq, 0)
    in_specs = [
        pl.BlockSpec((tm, D_MODEL), row),
        pl.BlockSpec((1, D_MODEL), const),
        pl.BlockSpec((D_MODEL, int(_P_END)), const),
        pl.BlockSpec((1, LANES), const),
        pl.BlockSpec((1, LANES), const),
        pl.BlockSpec((1, Q_LORA), const),
        pl.BlockSpec((Q_LORA, B_HEADS * 256), const),
        pl.BlockSpec((1, KV_LORA), const),
        pl.BlockSpec((KV_LORA, B_HEADS * 256), const),
        pl.BlockSpec((1, 256), const),
        pl.BlockSpec((1, LANES), const),
        pl.BlockSpec((1, LANES), const),
        pl.BlockSpec((tm, LANES), rope_row),
        pl.BlockSpec((tm, LANES), rope_row),
        pl.BlockSpec((A_UNITS, 1, LANES), const3),
        pl.BlockSpec((A_UNITS, 1, LANES), const3),
        pl.BlockSpec((2, LANES), const),
    ]
    out_shape = [
        jax.ShapeDtypeStruct((A_UNITS, n_var, t, LANES), _BF16),
        jax.ShapeDtypeStruct((A_UNITS, t, LANES), _BF16),
        jax.ShapeDtypeStruct((A_HEADS, t, 256), _BF16),
        jax.ShapeDtypeStruct((t, 512), _BF16),
        jax.ShapeDtypeStruct((B_HEADS, 1, t, 256), _BF16),
        jax.ShapeDtypeStruct((B_HEADS, t, 256), _BF16),
        jax.ShapeDtypeStruct((B_HEADS, t, 256), _BF16),
        jax.ShapeDtypeStruct((t, 512), _BF16),
    ]
    out_specs = [
        pl.BlockSpec((A_UNITS, n_var, tm, LANES), var_row),
        pl.BlockSpec((A_UNITS, tm, LANES), unit_row),
        pl.BlockSpec((A_HEADS, tm, 256), unit_row),
        pl.BlockSpec((tm, 512), row),
        pl.BlockSpec((B_HEADS, 1, tm, 256), var_row),
        pl.BlockSpec((B_HEADS, tm, 256), unit_row),
        pl.BlockSpec((B_HEADS, tm, 256), unit_row),
        pl.BlockSpec((tm, 512), row),
    ]
    return pl.pallas_call(
        functools.partial(_proj_kernel, fixed_shift=fixed_shift, seq_len=seq_len),
        grid=(t // tm,),
        in_specs=in_specs,
        out_specs=out_specs,
        out_shape=out_shape,
        compiler_params=pltpu.CompilerParams(
            dimension_semantics=("parallel",), vmem_limit_bytes=_VMEM_LIMIT),
        name="proj",
    )(x2d, lw["norm_w"], lw["w_in"], lw["g_aq"], lw["g_ak"], lw["g_cq"], lw["w_uq"], lw["g_ckv"],
      lw["w_ukv"], lw["g_bq"], lw["g_bkn"], lw["g_bkr"], lw["rope_c"][seq_len], lw["rope_s"][seq_len],
      lw["aug_q"], lw["aug_k"], lw["aug_b"])


def _flash_kernel(slope_ref, q_ref, k_ref, v_ref, o_ref, m_sc, acc_sc, *, alibi, heads_per_slope):
    kj = pl.program_id(3)
    tq = q_ref.shape[3]
    tk = k_ref.shape[2]

    @pl.when(kj == 0)
    def _():
        m_sc[...] = jnp.full(m_sc.shape, -jnp.inf, _F32)
        acc_sc[...] = jnp.zeros(acc_sc.shape, _F32)

    s = lax.dot_general(q_ref[0, 0, 0], k_ref[0, 0], (((1,), (1,)), ((), ())),
                        preferred_element_type=_F32)
    if alibi:
        slope = slope_ref[pl.program_id(0) // heads_per_slope]
        qpos = pl.program_id(2) * tq + lax.broadcasted_iota(jnp.int32, (tq, tk), 0)
        kpos = kj * tk + lax.broadcasted_iota(jnp.int32, (tq, tk), 1)
        s = s - slope * jnp.abs(qpos - kpos).astype(_F32)
    m_prev = m_sc[...]
    m_new = jnp.maximum(m_prev, jnp.max(s, axis=-1, keepdims=True))
    alpha = jnp.exp(m_prev - m_new)
    p = jnp.exp(s - m_new)
    acc_sc[...] = alpha * acc_sc[...] + jnp.dot(p.astype(_BF16), v_ref[0, 0],
                                                 preferred_element_type=_F32)
    m_sc[...] = m_new

    @pl.when(kj == pl.num_programs(3) - 1)
    def _():
        acc = acc_sc[...]
        o_ref[0, 0] = acc[:, :LANES] / acc[:, LANES:LANES + 1]


def _flash_call(q, k, v, slopes, *, alibi, units_per_v, tq, tk):
    u, _, b, s, dk = q.shape
    kern = functools.partial(_flash_kernel, alibi=alibi, heads_per_slope=units_per_v)
    grid_spec = pltpu.PrefetchScalarGridSpec(
        num_scalar_prefetch=1,
        grid=(u, b, s // tq, s // tk),
        in_specs=[
            pl.BlockSpec((1, 1, 1, tq, dk), lambda ui, bi, qi, ki, sl: (ui, 0, bi, qi, 0)),
            pl.BlockSpec((1, 1, tk, dk), lambda ui, bi, qi, ki, sl: (ui, bi, ki, 0)),
            pl.BlockSpec((1, 1, tk, 256), lambda ui, bi, qi, ki, sl: (ui // units_per_v, bi, ki, 0)),
        ],
        out_specs=pl.BlockSpec((1, 1, tq, LANES), lambda ui, bi, qi, ki, sl: (ui, bi, qi, 0)),
        scratch_shapes=[pltpu.VMEM((tq, 1), _F32), pltpu.VMEM((tq, 256), _F32)],
    )
    return pl.pallas_call(
        kern,
        grid_spec=grid_spec,
        out_shape=jax.ShapeDtypeStruct((u, b, s, LANES), _F32),
        compiler_params=pltpu.CompilerParams(
            dimension_semantics=("parallel", "parallel", "parallel", "arbitrary"),
            vmem_limit_bytes=_VMEM_LIMIT),
        name="flash_a" if alibi else "flash_b",
    )(slopes, q, k, v)


def _attn_kernel(q_ref, k_ref, v_ref, o_ref, acc_sc, *, tk, group, alibi):
    tq = q_ref.shape[3]
    n_k = k_ref.shape[2] // tk
    qi = pl.program_id(2)

    def block(q, j, correction=None):
        k = k_ref[0, 0, pl.ds(pl.multiple_of(j * tk, tk), tk), :]
        v = v_ref[0, 0, pl.ds(pl.multiple_of(j * tk, tk), tk), :]
        s = lax.dot_general(q, k, (((1,), (1,)), ((), ())), preferred_element_type=_F32)
        if correction is not None:
            s = s + correction
        return jnp.dot(jnp.exp2(s).astype(_BF16), v, preferred_element_type=_F32)

    if alibi:
        head = pl.program_id(0) // 2
        sigma = jnp.float32(_alibi_slope(A_HEADS - 1) * LOG2E)
        for hd in range(A_HEADS - 1):
            sigma = jnp.where(head == hd, jnp.float32(_alibi_slope(hd) * LOG2E), sigma)
        row = lax.broadcasted_iota(jnp.int32, (tq, tk), 0)
        col = lax.broadcasted_iota(jnp.int32, (tq, tk), 1)
        corr = (-2.0 * sigma) * jnp.maximum(col - row, 0).astype(_F32)
        n_off = n_k - 1

        def off_block(x):
            j = x + (x >= qi).astype(jnp.int32)
            return block(q_ref[0, (j > qi).astype(jnp.int32), 0], j)

        acc_sc[...] = block(q_ref[0, 0, 0], qi, corr) + off_block(n_off - 1)
        n_loop = n_off - 1
    else:
        acc_sc[...] = jnp.zeros(acc_sc.shape, _F32)
        off_block = lambda x: block(q_ref[0, 0, 0], x)
        n_loop = n_k

    def body(g, carry):
        tot = off_block(g * group)
        for t in range(1, group):
            tot = tot + off_block(g * group + t)
        acc_sc[...] += tot
        return carry

    lax.fori_loop(0, n_loop // group, body, 0)
    acc = acc_sc[...]
    o_ref[0, 0] = acc[:, :LANES] / acc[:, LANES:LANES + 1]


def _attn_call(q, k, v, *, alibi, units_per_v, tq, tk, group):
    u, n_var, b, s, dk = q.shape
    n_loop = s // tk - 2 if alibi else s // tk
    assert tq == tk and n_loop % group == 0 and n_loop >= 0, (s, tk, group)
    kern = functools.partial(_attn_kernel, tk=tk, group=group, alibi=alibi)
    resident = pl.Buffered(1)
    return pl.pallas_call(
        kern,
        grid=(u, b, s // tq),
        in_specs=[
            pl.BlockSpec((1, n_var, 1, tq, dk), lambda ui, bi, qi: (ui, 0, bi, qi, 0)),
            pl.BlockSpec((1, 1, s, dk), lambda ui, bi, qi: (ui, bi, 0, 0), pipeline_mode=resident),
            pl.BlockSpec((1, 1, s, 256), lambda ui, bi, qi: (ui // units_per_v, bi, 0, 0),
                         pipeline_mode=resident),
        ],
        out_specs=pl.BlockSpec((1, 1, tq, LANES), lambda ui, bi, qi: (ui, bi, qi, 0)),
        out_shape=jax.ShapeDtypeStruct((u, b, s, LANES), _F32),
        scratch_shapes=[pltpu.VMEM((tq, 256), _F32)],
        compiler_params=pltpu.CompilerParams(
            dimension_semantics=("parallel", "parallel", "parallel"),
            vmem_limit_bytes=_VMEM_LIMIT),
        name="attn_a" if alibi else "attn_b",
    )(q, k, v)


def _out_kernel(oa_ref, ob_ref, ga_ref, gb_ref, x_ref, lq1_ref, lk1_ref, lq2_ref, lk2_ref, subln_ref,
                wout_ref, y_ref, *, lam_init):
    lam = (jnp.exp(jnp.sum(lq1_ref[...] * lk1_ref[...], axis=-1, keepdims=True))
           - jnp.exp(jnp.sum(lq2_ref[...] * lk2_ref[...], axis=-1, keepdims=True)) + lam_init)
    ga = ga_ref[...].astype(_F32)
    gb = gb_ref[...].astype(_F32)
    pieces = []
    for hd in range(A_HEADS):
        d = oa_ref[2 * hd] - lam * oa_ref[2 * hd + 1]
        n = d * lax.rsqrt(jnp.mean(d * d, axis=-1, keepdims=True) + EPS) * subln_ref[...]
        pieces.append(n * (1.0 - lam_init) * ga[:, hd * LANES:(hd + 1) * LANES])
    for hd in range(B_HEADS):
        pieces.append(ob_ref[hd] * gb[:, hd * LANES:(hd + 1) * LANES])
    y = jnp.concatenate(pieces, axis=1).astype(_BF16)
    y_ref[...] = x_ref[...] + jnp.dot(y, wout_ref[...], preferred_element_type=_F32)


def _out_call(oa, ob, ga, gb, x2d, lw, lam_init, tm):
    t = x2d.shape[0]
    const = lambda i: (0, 0)
    row = lambda i: (i, 0)
    unit_row = lambda i: (0, i, 0)
    return pl.pallas_call(
        functools.partial(_out_kernel, lam_init=lam_init),
        grid=(t // tm,),
        in_specs=[
            pl.BlockSpec((A_UNITS, tm, LANES), unit_row),
            pl.BlockSpec((B_HEADS, tm, LANES), unit_row),
            pl.BlockSpec((tm, 512), row),
            pl.BlockSpec((tm, 512), row),
            pl.BlockSpec((tm, D_MODEL), row),
            pl.BlockSpec((1, A_HEAD_DIM), const),
            pl.BlockSpec((1, A_HEAD_DIM), const),
            pl.BlockSpec((1, A_HEAD_DIM), const),
            pl.BlockSpec((1, A_HEAD_DIM), const),
            pl.BlockSpec((1, A_V_DIM), const),
            pl.BlockSpec((D_MODEL, D_MODEL), const),
        ],
        out_specs=pl.BlockSpec((tm, D_MODEL), row),
        out_shape=jax.ShapeDtypeStruct((t, D_MODEL), _F32),
        compiler_params=pltpu.CompilerParams(
            dimension_semantics=("parallel",), vmem_limit_bytes=_VMEM_LIMIT),
        name="out",
    )(oa, ob, ga, gb, x2d, lw["lq1"], lw["lk1"], lw["lq2"], lw["lk2"], lw["subln"], lw["w_out"])


def _rope_tables(seq_len):
    inv = ROPE_THETA ** (-jnp.arange(0, B_ROPE, 2, dtype=_F32) / B_ROPE)
    ang = jnp.arange(seq_len, dtype=_F32)[:, None] * inv[None, :]
    cos, sin = jnp.cos(ang), jnp.sin(ang)
    zeros = jnp.zeros((seq_len, LANES - B_ROPE), _F32)
    return (jnp.concatenate([cos, cos, zeros], axis=1),
            jnp.concatenate([-sin, sin, zeros], axis=1))


def _pad_lanes(v, width):
    return jnp.pad(v, ((0, 0), (0, width - v.shape[1])))


def _score_bounds(l, p):
    amax = lambda v: jnp.max(jnp.abs(v[l].astype(_F32)))
    bound_a = amax(p["a_q_norm"]) * amax(p["a_k_norm"]) * (A_HEAD_DIM ** 0.5)
    bound_b = amax(p["b_q_norm"]) * amax(p["b_k_norm"]) * (B_QK_DIM ** 0.5)
    return bound_a, bound_b


def _aug_rows(bound_a, bound_b):
    aug_q = np.zeros((A_UNITS, 1, LANES), np.float32)
    aug_k = np.zeros((A_UNITS, 1, LANES), np.float32)
    for u in range(A_UNITS):
        sig = _sigma_pieces(u // 2)
        aug_k[u, 0, _A_SHIFT] = 1.0
        for a in range(SIGMA_PIECES):
            aug_k[u, 0, _A_QHI + a] = -sig[a]
            aug_k[u, 0, _A_QLO + a] = -sig[a]
            aug_q[u, 0, _A_KHI + a] = sig[a]
            aug_q[u, 0, _A_KLO + a] = sig[a]
    shift_lane = (np.arange(LANES) == _A_SHIFT).astype(np.float32).reshape(1, 1, LANES)
    aug_q = jnp.asarray(aug_q) - (bound_a * LOG2E) * shift_lane
    b_lane = (np.arange(LANES) == _B_SHIFT).astype(np.float32)
    aug_b = jnp.stack([-(bound_b * LOG2E) * b_lane, jnp.asarray(b_lane)])
    return aug_q, jnp.asarray(aug_k), aug_b


def _layer_weights(l, p, rope_c, rope_s):
    swap = _rope_swap_index()
    w = p["w_in"][l]
    cols = [w[:, _SPLITS[i]:_SPLITS[i + 1]] for i in range(8)]
    pad_units = lambda c: jnp.pad(c.reshape(D_MODEL, A_UNITS, A_HEAD_DIM),
                                  ((0, 0), (0, 0), (0, LANES - A_HEAD_DIM))).reshape(D_MODEL, A_UNITS * LANES)
    kr = cols[6]
    w_in = jnp.concatenate([pad_units(cols[0]), pad_units(cols[1]), cols[2], cols[3], cols[4], cols[5],
                            kr, kr[:, swap], cols[7]], axis=1).astype(_BF16)
    wuq = p["b_w_uq"][l].reshape(Q_LORA, B_HEADS, B_QK_DIM)
    w_uq = jnp.concatenate([wuq, wuq[:, :, B_NOPE:][:, :, swap]], axis=2).reshape(Q_LORA, B_HEADS * 256)
    gq = p["b_q_norm"][l]
    gk = p["b_k_norm"][l]
    row = lambda v: v.reshape(1, -1).astype(_F32)
    bound_a, bound_b = _score_bounds(l, p)
    aug_q, aug_k, aug_b = _aug_rows(bound_a, bound_b)
    return {
        "norm_w": row(p["norm_w"][l]),
        "w_in": w_in,
        "g_aq": _pad_lanes(row(p["a_q_norm"][l]), LANES),
        "g_ak": _pad_lanes(row(p["a_k_norm"][l]), LANES),
        "g_cq": row(p["b_cq_norm"][l]),
        "w_uq": w_uq.astype(_BF16),
        "g_ckv": row(p["b_ckv_norm"][l]),
        "w_ukv": p["b_w_ukv"][l].astype(_BF16),
        "g_bq": row(jnp.concatenate([gq, gq[B_NOPE:][swap]])),
        "g_bkn": row(gk[:B_NOPE]),
        "g_bkr": row(jnp.concatenate([gk[B_NOPE:], gk[B_NOPE:][swap]])),
        "rope_c": rope_c,
        "rope_s": rope_s,
        "aug_q": aug_q, "aug_k": aug_k, "aug_b": aug_b,
        "lq1": row(p["a_lq1"][l]), "lk1": row(p["a_lk1"][l]),
        "lq2": row(p["a_lq2"][l]), "lk2": row(p["a_lk2"][l]),
        "subln": row(p["a_subln"][l]),
        "w_out": p["w_out"][l].astype(_BF16),
    }


def _tile(n, pref):
    return pref if n % pref == 0 else n


def _largest_group(n, cap):
    return max(g for g in range(1, cap + 1) if n % g == 0) if n > 0 else 1


def _encoder_layer(x, lw, lam_init, slopes, fixed_shift):
    b, s, _ = x.shape
    t = b * s
    x2d = x.reshape(t, D_MODEL)
    tm = _tile(s, 256)
    qa, ka, va, ga, qb, kb, vb, gb = _proj_call(x2d, s, lw, tm, fixed_shift)
    split = lambda a: a.reshape(a.shape[:-2] + (b, s, a.shape[-1]))
    tq = tk = _tile(s, 512)
    if fixed_shift and s // tk >= 2:
        assert s <= POS_SPLIT * POS_SPLIT * 2
        oa = _attn_call(split(qa), split(ka), split(va), alibi=True, units_per_v=2, tq=tq, tk=tk,
                        group=_largest_group(s // tk - 2, 3))
        ob = _attn_call(split(qb), split(kb), split(vb), alibi=False, units_per_v=1, tq=tq, tk=tk,
                        group=_largest_group(s // tk, 4))
    else:
        oa = _flash_call(split(qa), split(ka), split(va), slopes, alibi=True, units_per_v=2, tq=tq, tk=tk)
        ob = _flash_call(split(qb), split(kb), split(vb), slopes, alibi=False, units_per_v=1, tq=tq, tk=tk)
    y = _out_call(oa.reshape(A_UNITS, t, LANES), ob.reshape(B_HEADS, t, LANES), ga, gb, x2d, lw, lam_init, tm)
    return y.reshape(b, s, D_MODEL)


def _forward(x_prompt, x_sample, p, fixed_shift):
    depth = p["norm_w"].shape[0]
    rope_c, rope_s = {}, {}
    for s in {x_prompt.shape[1], x_sample.shape[1]}:
        rope_c[s], rope_s[s] = _rope_tables(s)
    slopes = jnp.asarray([_alibi_slope(hd) for hd in range(A_HEADS)], _F32)
    y_prompt, y_sample = x_prompt, x_sample
    for l in range(depth):
        lw = _layer_weights(l, p, rope_c, rope_s)
        lam_init = _lambda_init(l)
        y_prompt = _encoder_layer(y_prompt, lw, lam_init, slopes, fixed_shift)
        y_sample = _encoder_layer(y_sample, lw, lam_init, slopes, fixed_shift)
    return (y_prompt, y_sample)


def kernel(x_prompt, x_sample, norm_w, w_in, a_q_norm, a_k_norm, a_lq1, a_lk1, a_lq2, a_lk2, a_subln,
           b_cq_norm, b_w_uq, b_ckv_norm, b_w_ukv, b_q_norm, b_k_norm, w_out):
    p = dict(norm_w=norm_w, w_in=w_in, a_q_norm=a_q_norm, a_k_norm=a_k_norm, a_lq1=a_lq1, a_lk1=a_lk1,
             a_lq2=a_lq2, a_lk2=a_lk2, a_subln=a_subln, b_cq_norm=b_cq_norm, b_w_uq=b_w_uq,
             b_ckv_norm=b_ckv_norm, b_w_ukv=b_w_ukv, b_q_norm=b_q_norm, b_k_norm=b_k_norm, w_out=w_out)
    bounds = jnp.stack([jnp.stack(_score_bounds(l, p)) for l in range(norm_w.shape[0])])
    shift_ok = jnp.max(bounds) <= MAX_STATIC_SHIFT
    return lax.cond(shift_ok,
                    lambda xp, xs, pp: _forward(xp, xs, pp, True),
                    lambda xp, xs, pp: _forward(xp, xs, pp, False),
                    x_prompt, x_sample, p)
```

```python
import functools
import math

import numpy as np
import jax
import jax.numpy as jnp
from jax import lax
from jax.experimental import pallas as pl
from jax.experimental.pallas import tpu as pltpu

D_MODEL = 1024
A_HEADS = 4
A_HEAD_DIM = 64
A_V_DIM = 128
A_UNITS = 2 * A_HEADS
B_HEADS = 4
B_NOPE = 128
B_ROPE = 64
B_QK_DIM = B_NOPE + B_ROPE
B_V_DIM = 128
Q_LORA = 256
KV_LORA = 128
ROPE_THETA = 10000.0
EPS = 1e-6
LANES = 128

_SPLITS = np.cumsum([0, 512, 512, 512, 512, Q_LORA, KV_LORA, B_ROPE, 512])
_P_AQ, _P_AK, _P_AV, _P_AG, _P_CQ, _P_CKV, _P_KR, _P_BG, _P_END = np.cumsum(
    [0, A_UNITS * LANES, A_UNITS * LANES, 512, 512, Q_LORA, KV_LORA, LANES, 512])

_VMEM_LIMIT = 56 * 1024 * 1024
LOG2E = math.log2(math.e)
MAX_STATIC_SHIFT = 32.0
POS_SPLIT = 128
SIGMA_PIECES = 3
FIXED_TQ = 1024
FIXED_TK = 256
MAX_UNROLL = 64

_A_SHIFT = A_HEAD_DIM
_A_QHI = _A_SHIFT + 1
_A_QLO = _A_QHI + SIGMA_PIECES
_A_KHI = _A_QLO + SIGMA_PIECES
_A_KLO = _A_KHI + SIGMA_PIECES
_A_AUG_END = _A_KLO + SIGMA_PIECES
_B_SHIFT = B_QK_DIM - LANES

_F32 = jnp.float32
_BF16 = jnp.bfloat16


def _lambda_init(layer_idx):
    return 0.8 - 0.6 * math.exp(-0.3 * layer_idx)


def _alibi_slope(head):
    return 2.0 ** (-8.0 * (head + 1) / A_HEADS)


def _sigma_pieces(head):
    rest = _alibi_slope(head) * LOG2E
    pieces = []
    for _ in range(SIGMA_PIECES):
        p = float(np.asarray(rest, np.float32).astype(_BF16).astype(np.float32))
        pieces.append(p)
        rest -= p
    return pieces


def _rope_swap_index():
    half = B_ROPE // 2
    return np.concatenate([np.arange(half, B_ROPE), np.arange(0, half)])


def _silu(x):
    return x / (1.0 + jnp.exp(-x))


def _proj_kernel(x_ref, nw_ref, win_ref, gaq_ref, gak_ref, gcq_ref, wuq_ref, gckv_ref, wukv_ref,
                 gbq_ref, gbkn_ref, gbkr_ref, ct_ref, st_ref, augq_ref, augk_ref, augb_ref,
                 qa_ref, ka_ref, va_ref, ga_ref, qb_ref, kb_ref, vb_ref, gb_ref, *, fixed_shift, seq_len):
    tm = x_ref.shape[0]
    x = x_ref[...]
    h = x * lax.rsqrt(jnp.mean(x * x, axis=-1, keepdims=True) + EPS) * nw_ref[...]
    hb = h.astype(_BF16)

    def proj(lo, hi):
        return jnp.dot(hb, win_ref[:, lo:hi], preferred_element_type=_F32)

    lane = lax.broadcasted_iota(jnp.int32, (tm, LANES), 1)
    low_half = lane < B_ROPE
    ones_col = (lane == 0).astype(_F32)
    ct = ct_ref[...]
    st = st_ref[...]

    if fixed_shift:
        a_qscale = (A_HEAD_DIM ** -0.5) * LOG2E
        b_qscale = (B_QK_DIM ** -0.5) * LOG2E
        pos = (pl.program_id(0) % (seq_len // tm)) * tm + lax.broadcasted_iota(jnp.int32, (tm, LANES), 0)
        pos = pos - seq_len // 2
        pos_hi = ((pos >> 7) * POS_SPLIT).astype(_F32)
        pos_lo = (pos & (POS_SPLIT - 1)).astype(_F32)
        q_hi_lanes = (lane >= _A_QHI) & (lane < _A_QLO)
        q_lo_lanes = (lane >= _A_QLO) & (lane < _A_KHI)
        k_hi_lanes = (lane >= _A_KHI) & (lane < _A_KLO)
        k_lo_lanes = (lane >= _A_KLO) & (lane < _A_AUG_END)
        q_pos = jnp.where(q_hi_lanes, pos_hi, jnp.where(q_lo_lanes, pos_lo, 0.0))
        k_pos = jnp.where(k_hi_lanes, pos_hi, jnp.where(k_lo_lanes, pos_lo, 0.0))
        bias_lanes = (lane >= _A_QHI) & (lane < _A_AUG_END)
    else:
        a_qscale = A_HEAD_DIM ** -0.5
        b_qscale = B_QK_DIM ** -0.5

    aq = proj(_P_AQ, _P_AK)
    ak = proj(_P_AK, _P_AV)
    for u in range(A_UNITS):
        xq = aq[:, u * LANES:(u + 1) * LANES]
        rq = lax.rsqrt(jnp.sum(xq * xq, axis=-1, keepdims=True) * (1.0 / A_HEAD_DIM) + EPS)
        qn = xq * rq * gaq_ref[...] * a_qscale
        xk = ak[:, u * LANES:(u + 1) * LANES]
        rk = lax.rsqrt(jnp.sum(xk * xk, axis=-1, keepdims=True) * (1.0 / A_HEAD_DIM) + EPS)
        kn = xk * rk * gak_ref[...]
        if fixed_shift:
            q_keys_before = qn + augq_ref[u] + q_pos
            qa_ref[u, 0] = q_keys_before.astype(_BF16)
            qa_ref[u, 1] = jnp.where(bias_lanes, -q_keys_before, q_keys_before).astype(_BF16)
            ka_ref[u] = (kn + augk_ref[u] + k_pos).astype(_BF16)
        else:
            qa_ref[u, 0] = qn.astype(_BF16)
            ka_ref[u] = kn.astype(_BF16)
    av = proj(_P_AV, _P_AG)
    for hd in range(A_HEADS):
        va_ref[hd] = jnp.concatenate([av[:, hd * LANES:(hd + 1) * LANES], ones_col], axis=1).astype(_BF16)
    ga_ref[...] = _silu(proj(_P_AG, _P_CQ)).astype(_BF16)

    cq = proj(_P_CQ, _P_CKV)
    cqn = cq * lax.rsqrt(jnp.mean(cq * cq, axis=-1, keepdims=True) + EPS) * gcq_ref[...]
    qall = jnp.dot(cqn.astype(_BF16), wuq_ref[...], preferred_element_type=_F32)
    for hd in range(B_HEADS):
        q0 = qall[:, hd * 256:hd * 256 + LANES]
        q1 = qall[:, hd * 256 + LANES:(hd + 1) * 256]
        ss = (jnp.sum(q0 * q0, axis=-1, keepdims=True)
              + jnp.sum(jnp.where(low_half, q1 * q1, 0.0), axis=-1, keepdims=True))
        r = lax.rsqrt(ss * (1.0 / B_QK_DIM) + EPS)
        q0n = q0 * r * gbq_ref[:, :LANES] * b_qscale
        q1n = q1 * r * gbq_ref[:, LANES:]
        q1r = (q1n * ct + pltpu.roll(q1n, B_ROPE, 1) * st) * b_qscale
        if fixed_shift:
            q1r = q1r + augb_ref[0:1, :]
        qb_ref[hd, 0] = jnp.concatenate([q0n, q1r], axis=1).astype(_BF16)

    ckv = proj(_P_CKV, _P_KR)
    ckvn = ckv * lax.rsqrt(jnp.mean(ckv * ckv, axis=-1, keepdims=True) + EPS) * gckv_ref[...]
    kv = jnp.dot(ckvn.astype(_BF16), wukv_ref[...], preferred_element_type=_F32)
    kr2 = proj(_P_KR, _P_BG)
    kr_ss = jnp.sum(jnp.where(low_half, kr2 * kr2, 0.0), axis=-1, keepdims=True)
    krg = kr2 * gbkr_ref[...]
    krr = krg * ct + pltpu.roll(krg, B_ROPE, 1) * st
    for hd in range(B_HEADS):
        kn = kv[:, hd * 256:hd * 256 + LANES]
        vv = kv[:, hd * 256 + LANES:(hd + 1) * 256]
        r = lax.rsqrt((jnp.sum(kn * kn, axis=-1, keepdims=True) + kr_ss) * (1.0 / B_QK_DIM) + EPS)
        k1 = krr * r
        if fixed_shift:
            k1 = k1 + augb_ref[1:2, :]
        kb_ref[hd] = jnp.concatenate([kn * r * gbkn_ref[...], k1], axis=1).astype(_BF16)
        vb_ref[hd] = jnp.concatenate([vv, ones_col], axis=1).astype(_BF16)
    gb_ref[...] = _silu(proj(_P_BG, _P_END)).astype(_BF16)


def _proj_call(x2d, seq_len, lw, tm, fixed_shift):
    t = x2d.shape[0]
    blocks_per_seq = seq_len // tm
    n_var = 2 if fixed_shift else 1
    const = lambda i: (0, 0)
    const3 = lambda i: (0, 0, 0)
    row = lambda i: (i, 0)
    unit_row = lambda i: (0, i, 0)
    var_row = lambda i: (0, 0, i, 0)
    rope_row = lambda i: (i % blocks_per_seq, 0)
    in_specs = [
        pl.BlockSpec((tm, D_MODEL), row),
        pl.BlockSpec((1, D_MODEL), const),
        pl.BlockSpec((D_MODEL, int(_P_END)), const),
        pl.BlockSpec((1, LANES), const),
        pl.BlockSpec((1, LANES), const),
        pl.BlockSpec((1, Q_LORA), const),
        pl.BlockSpec((Q_LORA, B_HEADS * 256), const),
        pl.BlockSpec((1, KV_LORA), const),
        pl.BlockSpec((KV_LORA, B_HEADS * 256), const),
        pl.BlockSpec((1, 256), const),
        pl.BlockSpec((1, LANES), const),
        pl.BlockSpec((1, LANES), const),
        pl.BlockSpec((tm, LANES), rope_row),
        pl.BlockSpec((tm, LANES), rope_row),
        pl.BlockSpec((A_UNITS, 1, LANES), const3),
        pl.BlockSpec((A_UNITS, 1, LANES), const3),
        pl.BlockSpec((2, LANES), const),
    ]
    out_shape = [
        jax.ShapeDtypeStruct((A_UNITS, n_var, t, LANES), _BF16),
        jax.ShapeDtypeStruct((A_UNITS, t, LANES), _BF16),
        jax.ShapeDtypeStruct((A_HEADS, t, 256), _BF16),
        jax.ShapeDtypeStruct((t, 512), _BF16),
        jax.ShapeDtypeStruct((B_HEADS, 1, t, 256), _BF16),
        jax.ShapeDtypeStruct((B_HEADS, t, 256), _BF16),
        jax.ShapeDtypeStruct((B_HEADS, t, 256), _BF16),
        jax.ShapeDtypeStruct((t, 512), _BF16),
    ]
    out_specs = [
        pl.BlockSpec((A_UNITS, n_var, tm, LANES), var_row),
        pl.BlockSpec((A_UNITS, tm, LANES), unit_row),
        pl.BlockSpec((A_HEADS, tm, 256), unit_row),
        pl.BlockSpec((tm, 512), row),
        pl.BlockSpec((B_HEADS, 1, tm, 256), var_row),
        pl.BlockSpec((B_HEADS, tm, 256), unit_row),
        pl.BlockSpec((B_HEADS, tm, 256), unit_row),
        pl.BlockSpec((tm, 512), row),
    ]
    return pl.pallas_call(
        functools.partial(_proj_kernel, fixed_shift=fixed_shift, seq_len=seq_len),
        grid=(t // tm,),
        in_specs=in_specs,
        out_specs=out_specs,
        out_shape=out_shape,
        compiler_params=pltpu.CompilerParams(
            dimension_semantics=("parallel",), vmem_limit_bytes=_VMEM_LIMIT),
        name="proj",
    )(x2d, lw["norm_w"], lw["w_in"], lw["g_aq"], lw["g_ak"], lw["g_cq"], lw["w_uq"], lw["g_ckv"],
      lw["w_ukv"], lw["g_bq"], lw["g_bkn"], lw["g_bkr"], lw["rope_c"][seq_len], lw["rope_s"][seq_len],
      lw["aug_q"], lw["aug_k"], lw["aug_b"])


def _flash_kernel(slope_ref, q_ref, k_ref, v_ref, o_ref, m_sc, acc_sc, *, alibi, heads_per_slope):
    kj = pl.program_id(3)
    tq = q_ref.shape[3]
    tk = k_ref.shape[2]

    @pl.when(kj == 0)
    def _():
        m_sc[...] = jnp.full(m_sc.shape, -jnp.inf, _F32)
        acc_sc[...] = jnp.zeros(acc_sc.shape, _F32)

    s = lax.dot_general(q_ref[0, 0, 0], k_ref[0, 0], (((1,), (1,)), ((), ())),
                        preferred_element_type=_F32)
    if alibi:
        slope = slope_ref[pl.program_id(0) // heads_per_slope]
        qpos = pl.program_id(2) * tq + lax.broadcasted_iota(jnp.int32, (tq, tk), 0)
        kpos = kj * tk + lax.broadcasted_iota(jnp.int32, (tq, tk), 1)
        s = s - slope * jnp.abs(qpos - kpos).astype(_F32)
    m_prev = m_sc[...]
    m_new = jnp.maximum(m_prev, jnp.max(s, axis=-1, keepdims=True))
    alpha = jnp.exp(m_prev - m_new)
    p = jnp.exp(s - m_new)
    acc_sc[...] = alpha * acc_sc[...] + jnp.dot(p.astype(_BF16), v_ref[0, 0],
                                                 preferred_element_type=_F32)
    m_sc[...] = m_new

    @pl.when(kj == pl.num_programs(3) - 1)
    def _():
        acc = acc_sc[...]
        o_ref[0, 0] = acc[:, :LANES] / acc[:, LANES:LANES + 1]


def _flash_call(q, k, v, slopes, *, alibi, units_per_v, tq, tk):
    u, _, b, s, dk = q.shape
    kern = functools.partial(_flash_kernel, alibi=alibi, heads_per_slope=units_per_v)
    grid_spec = pltpu.PrefetchScalarGridSpec(
        num_scalar_prefetch=1,
        grid=(u, b, s // tq, s // tk),
        in_specs=[
            pl.BlockSpec((1, 1, 1, tq, dk), lambda ui, bi, qi, ki, sl: (ui, 0, bi, qi, 0)),
            pl.BlockSpec((1, 1, tk, dk), lambda ui, bi, qi, ki, sl: (ui, bi, ki, 0)),
            pl.BlockSpec((1, 1, tk, 256), lambda ui, bi, qi, ki, sl: (ui // units_per_v, bi, ki, 0)),
        ],
        out_specs=pl.BlockSpec((1, 1, tq, LANES), lambda ui, bi, qi, ki, sl: (ui, bi, qi, 0)),
        scratch_shapes=[pltpu.VMEM((tq, 1), _F32), pltpu.VMEM((tq, 256), _F32)],
    )
    return pl.pallas_call(
        kern,
        grid_spec=grid_spec,
        out_shape=jax.ShapeDtypeStruct((u, b, s, LANES), _F32),
        compiler_params=pltpu.CompilerParams(
            dimension_semantics=("parallel", "parallel", "parallel", "arbitrary"),
            vmem_limit_bytes=_VMEM_LIMIT),
        name="flash_a" if alibi else "flash_b",
    )(slopes, q, k, v)


def _attn_kernel(q_ref, k_ref, v_ref, o_ref, acc_sc, *, tk, group, alibi):
    tq = q_ref.shape[3]
    n_k = k_ref.shape[2] // tk
    qi = pl.program_id(2)

    def block(q, j, correction=None):
        k = k_ref[0, 0, pl.ds(pl.multiple_of(j * tk, tk), tk), :]
        v = v_ref[0, 0, pl.ds(pl.multiple_of(j * tk, tk), tk), :]
        s = lax.dot_general(q, k, (((1,), (1,)), ((), ())), preferred_element_type=_F32)
        if correction is not None:
            s = s + correction
        return jnp.dot(jnp.exp2(s).astype(_BF16), v, preferred_element_type=_F32)

    if alibi:
        n_diag = tq // tk
        head = pl.program_id(0) // 2
        sigma = jnp.float32(_alibi_slope(A_HEADS - 1) * LOG2E)
        for hd in range(A_HEADS - 1):
            sigma = jnp.where(head == hd, jnp.float32(_alibi_slope(hd) * LOG2E), sigma)
        ahead = (lax.broadcasted_iota(jnp.int32, (tq, tk), 1) - lax.broadcasted_iota(jnp.int32, (tq, tk), 0))
        first_diag = qi * n_diag
        acc = None
        for d in range(n_diag):
            corr = (-2.0 * sigma) * jnp.maximum(ahead + d * tk, 0).astype(_F32)
            part = block(q_ref[0, 0, 0], first_diag + d, corr)
            acc = part if acc is None else acc + part
        n_loop = n_k - n_diag

        def off_block(x):
            after = (x >= first_diag).astype(jnp.int32)
            return block(q_ref[0, after, 0], x + n_diag * after)
    else:
        acc = None
        off_block = lambda x: block(q_ref[0, 0, 0], x)
        n_loop = n_k

    if n_loop == group:
        for x in range(n_loop):
            part = off_block(x)
            acc = part if acc is None else acc + part
    else:
        acc_sc[...] = jnp.zeros(acc_sc.shape, _F32) if acc is None else acc

        def body(g, carry):
            tot = off_block(g * group)
            for t in range(1, group):
                tot = tot + off_block(g * group + t)
            acc_sc[...] += tot
            return carry

        lax.fori_loop(0, n_loop // group, body, 0)
        acc = acc_sc[...]
    o_ref[0, 0] = acc[:, :LANES] / acc[:, LANES:LANES + 1]


def _attn_call(q, k, v, *, alibi, units_per_v, tq, tk, group):
    u, n_var, b, s, dk = q.shape
    n_loop = (s - tq) // tk if alibi else s // tk
    assert tq % tk == 0 and s % tq == 0 and n_loop % group == 0, (s, tq, tk, group)
    kern = functools.partial(_attn_kernel, tk=tk, group=group, alibi=alibi)
    resident = pl.Buffered(1)
    return pl.pallas_call(
        kern,
        grid=(u, b, s // tq),
        in_specs=[
            pl.BlockSpec((1, n_var, 1, tq, dk), lambda ui, bi, qi: (ui, 0, bi, qi, 0)),
            pl.BlockSpec((1, 1, s, dk), lambda ui, bi, qi: (ui, bi, 0, 0), pipeline_mode=resident),
            pl.BlockSpec((1, 1, s, 256), lambda ui, bi, qi: (ui // units_per_v, bi, 0, 0),
                         pipeline_mode=resident),
        ],
        out_specs=pl.BlockSpec((1, 1, tq, LANES), lambda ui, bi, qi: (ui, bi, qi, 0)),
        out_shape=jax.ShapeDtypeStruct((u, b, s, LANES), _F32),
        scratch_shapes=[pltpu.VMEM((tq, 256), _F32)],
        compiler_params=pltpu.CompilerParams(
            dimension_semantics=("parallel", "parallel", "parallel"),
            vmem_limit_bytes=_VMEM_LIMIT),
        name="attn_a" if alibi else "attn_b",
    )(q, k, v)


def _out_kernel(oa_ref, ob_ref, ga_ref, gb_ref, x_ref, lq1_ref, lk1_ref, lq2_ref, lk2_ref, subln_ref,
                wout_ref, y_ref, *, lam_init):
    lam = (jnp.exp(jnp.sum(lq1_ref[...] * lk1_ref[...], axis=-1, keepdims=True))
           - jnp.exp(jnp.sum(lq2_ref[...] * lk2_ref[...], axis=-1, keepdims=True)) + lam_init)
    ga = ga_ref[...].astype(_F32)
    gb = gb_ref[...].astype(_F32)
    pieces = []
    for hd in range(A_HEADS):
        d = oa_ref[2 * hd] - lam * oa_ref[2 * hd + 1]
        n = d * lax.rsqrt(jnp.mean(d * d, axis=-1, keepdims=True) + EPS) * subln_ref[...]
        pieces.append(n * (1.0 - lam_init) * ga[:, hd * LANES:(hd + 1) * LANES])
    for hd in range(B_HEADS):
        pieces.append(ob_ref[hd] * gb[:, hd * LANES:(hd + 1) * LANES])
    y = jnp.concatenate(pieces, axis=1).astype(_BF16)
    y_ref[...] = x_ref[...] + jnp.dot(y, wout_ref[...], preferred_element_type=_F32)


def _out_call(oa, ob, ga, gb, x2d, lw, lam_init, tm):
    t = x2d.shape[0]
    const = lambda i: (0, 0)
    row = lambda i: (i, 0)
    unit_row = lambda i: (0, i, 0)
    return pl.pallas_call(
        functools.partial(_out_kernel, lam_init=lam_init),
        grid=(t // tm,),
        in_specs=[
            pl.BlockSpec((A_UNITS, tm, LANES), unit_row),
            pl.BlockSpec((B_HEADS, tm, LANES), unit_row),
            pl.BlockSpec((tm, 512), row),
            pl.BlockSpec((tm, 512), row),
            pl.BlockSpec((tm, D_MODEL), row),
            pl.BlockSpec((1, A_HEAD_DIM), const),
            pl.BlockSpec((1, A_HEAD_DIM), const),
            pl.BlockSpec((1, A_HEAD_DIM), const),
            pl.BlockSpec((1, A_HEAD_DIM), const),
            pl.BlockSpec((1, A_V_DIM), const),
            pl.BlockSpec((D_MODEL, D_MODEL), const),
        ],
        out_specs=pl.BlockSpec((tm, D_MODEL), row),
        out_shape=jax.ShapeDtypeStruct((t, D_MODEL), _F32),
        compiler_params=pltpu.CompilerParams(
            dimension_semantics=("parallel",), vmem_limit_bytes=_VMEM_LIMIT),
        name="out",
    )(oa, ob, ga, gb, x2d, lw["lq1"], lw["lk1"], lw["lq2"], lw["lk2"], lw["subln"], lw["w_out"])


def _rope_tables(seq_len):
    inv = ROPE_THETA ** (-jnp.arange(0, B_ROPE, 2, dtype=_F32) / B_ROPE)
    ang = jnp.arange(seq_len, dtype=_F32)[:, None] * inv[None, :]
    cos, sin = jnp.cos(ang), jnp.sin(ang)
    zeros = jnp.zeros((seq_len, LANES - B_ROPE), _F32)
    return (jnp.concatenate([cos, cos, zeros], axis=1),
            jnp.concatenate([-sin, sin, zeros], axis=1))


def _pad_lanes(v, width):
    return jnp.pad(v, ((0, 0), (0, width - v.shape[1])))


def _score_bounds(l, p):
    amax = lambda v: jnp.max(jnp.abs(v[l].astype(_F32)))
    bound_a = amax(p["a_q_norm"]) * amax(p["a_k_norm"]) * (A_HEAD_DIM ** 0.5)
    bound_b = amax(p["b_q_norm"]) * amax(p["b_k_norm"]) * (B_QK_DIM ** 0.5)
    return bound_a, bound_b


def _aug_rows(bound_a, bound_b):
    aug_q = np.zeros((A_UNITS, 1, LANES), np.float32)
    aug_k = np.zeros((A_UNITS, 1, LANES), np.float32)
    for u in range(A_UNITS):
        sig = _sigma_pieces(u // 2)
        aug_k[u, 0, _A_SHIFT] = 1.0
        for a in range(SIGMA_PIECES):
            aug_k[u, 0, _A_QHI + a] = -sig[a]
            aug_k[u, 0, _A_QLO + a] = -sig[a]
            aug_q[u, 0, _A_KHI + a] = sig[a]
            aug_q[u, 0, _A_KLO + a] = sig[a]
    shift_lane = (np.arange(LANES) == _A_SHIFT).astype(np.float32).reshape(1, 1, LANES)
    aug_q = jnp.asarray(aug_q) - (bound_a * LOG2E) * shift_lane
    b_lane = (np.arange(LANES) == _B_SHIFT).astype(np.float32)
    aug_b = jnp.stack([-(bound_b * LOG2E) * b_lane, jnp.asarray(b_lane)])
    return aug_q, jnp.asarray(aug_k), aug_b


def _layer_weights(l, p, rope_c, rope_s):
    swap = _rope_swap_index()
    w = p["w_in"][l]
    cols = [w[:, _SPLITS[i]:_SPLITS[i + 1]] for i in range(8)]
    pad_units = lambda c: jnp.pad(c.reshape(D_MODEL, A_UNITS, A_HEAD_DIM),
                                  ((0, 0), (0, 0), (0, LANES - A_HEAD_DIM))).reshape(D_MODEL, A_UNITS * LANES)
    kr = cols[6]
    w_in = jnp.concatenate([pad_units(cols[0]), pad_units(cols[1]), cols[2], cols[3], cols[4], cols[5],
                            kr, kr[:, swap], cols[7]], axis=1).astype(_BF16)
    wuq = p["b_w_uq"][l].reshape(Q_LORA, B_HEADS, B_QK_DIM)
    w_uq = jnp.concatenate([wuq, wuq[:, :, B_NOPE:][:, :, swap]], axis=2).reshape(Q_LORA, B_HEADS * 256)
    gq = p["b_q_norm"][l]
    gk = p["b_k_norm"][l]
    row = lambda v: v.reshape(1, -1).astype(_F32)
    bound_a, bound_b = _score_bounds(l, p)
    aug_q, aug_k, aug_b = _aug_rows(bound_a, bound_b)
    return {
        "norm_w": row(p["norm_w"][l]),
        "w_in": w_in,
        "g_aq": _pad_lanes(row(p["a_q_norm"][l]), LANES),
        "g_ak": _pad_lanes(row(p["a_k_norm"][l]), LANES),
        "g_cq": row(p["b_cq_norm"][l]),
        "w_uq": w_uq.astype(_BF16),
        "g_ckv": row(p["b_ckv_norm"][l]),
        "w_ukv": p["b_w_ukv"][l].astype(_BF16),
        "g_bq": row(jnp.concatenate([gq, gq[B_NOPE:][swap]])),
        "g_bkn": row(gk[:B_NOPE]),
        "g_bkr": row(jnp.concatenate([gk[B_NOPE:], gk[B_NOPE:][swap]])),
        "rope_c": rope_c,
        "rope_s": rope_s,
        "aug_q": aug_q, "aug_k": aug_k, "aug_b": aug_b,
        "lq1": row(p["a_lq1"][l]), "lk1": row(p["a_lk1"][l]),
        "lq2": row(p["a_lq2"][l]), "lk2": row(p["a_lk2"][l]),
        "subln": row(p["a_subln"][l]),
        "w_out": p["w_out"][l].astype(_BF16),
    }


def _tile(n, pref):
    return pref if n % pref == 0 else n


def _largest_group(n, cap):
    return max(g for g in range(1, cap + 1) if n % g == 0) if n > 0 else 1


def _encoder_layer(x, lw, lam_init, slopes, fixed_shift):
    b, s, _ = x.shape
    t = b * s
    x2d = x.reshape(t, D_MODEL)
    tm = _tile(s, 256)
    qa, ka, va, ga, qb, kb, vb, gb = _proj_call(x2d, s, lw, tm, fixed_shift)
    split = lambda a: a.reshape(a.shape[:-2] + (b, s, a.shape[-1]))
    tq = tk = _tile(s, 512)
    if fixed_shift:
        assert s <= POS_SPLIT * POS_SPLIT * 2
        ftq, ftk = _tile(s, FIXED_TQ), _tile(s, FIXED_TK)
        oa = _attn_call(split(qa), split(ka), split(va), alibi=True, units_per_v=2, tq=ftq, tk=ftk,
                        group=_largest_group((s - ftq) // ftk, MAX_UNROLL))
        ob = _attn_call(split(qb), split(kb), split(vb), alibi=False, units_per_v=1, tq=ftq, tk=ftk,
                        group=_largest_group(s // ftk, MAX_UNROLL))
    else:
        oa = _flash_call(split(qa), split(ka), split(va), slopes, alibi=True, units_per_v=2, tq=tq, tk=tk)
        ob = _flash_call(split(qb), split(kb), split(vb), slopes, alibi=False, units_per_v=1, tq=tq, tk=tk)
    y = _out_call(oa.reshape(A_UNITS, t, LANES), ob.reshape(B_HEADS, t, LANES), ga, gb, x2d, lw, lam_init, tm)
    return y.reshape(b, s, D_MODEL)


def _forward(x_prompt, x_sample, p, fixed_shift):
    depth = p["norm_w"].shape[0]
    rope_c, rope_s = {}, {}
    for s in {x_prompt.shape[1], x_sample.shape[1]}:
        rope_c[s], rope_s[s] = _rope_tables(s)
    slopes = jnp.asarray([_alibi_slope(hd) for hd in range(A_HEADS)], _F32)
    y_prompt, y_sample = x_prompt, x_sample
    for l in range(depth):
        lw = _layer_weights(l, p, rope_c, rope_s)
        lam_init = _lambda_init(l)
        y_prompt = _encoder_layer(y_prompt, lw, lam_init, slopes, fixed_shift)
        y_sample = _encoder_layer(y_sample, lw, lam_init, slopes, fixed_shift)
    return (y_prompt, y_sample)


def kernel(x_prompt, x_sample, norm_w, w_in, a_q_norm, a_k_norm, a_lq1, a_lk1, a_lq2, a_lk2, a_subln,
           b_cq_norm, b_w_uq, b_ckv_norm, b_w_ukv, b_q_norm, b_k_norm, w_out):
    p = dict(norm_w=norm_w, w_in=w_in, a_q_norm=a_q_norm, a_k_norm=a_k_norm, a_lq1=a_lq1, a_lk1=a_lk1,
             a_lq2=a_lq2, a_lk2=a_lk2, a_subln=a_subln, b_cq_norm=b_cq_norm, b_w_uq=b_w_uq,
             b_ckv_norm=b_ckv_norm, b_w_ukv=b_w_ukv, b_q_norm=b_q_norm, b_k_norm=b_k_norm, w_out=w_out)
    bounds = jnp.stack([jnp.stack(_score_bounds(l, p)) for l in range(norm_w.shape[0])])
    shift_ok = jnp.max(bounds) <= MAX_STATIC_SHIFT
    return lax.cond(shift_ok,
                    lambda xp, xs, pp: _forward(xp, xs, pp, True),
                    lambda xp, xs, pp: _forward(xp, xs, pp, False),
                    x_prompt, x_sample, p)
```

```python
import functools
import math

import numpy as np
import jax
import jax.numpy as jnp
from jax import lax
from jax.experimental import pallas as pl
from jax.experimental.pallas import tpu as pltpu

D_MODEL = 1024
A_HEADS = 4
A_HEAD_DIM = 64
A_V_DIM = 128
A_UNITS = 2 * A_HEADS
B_HEADS = 4
B_NOPE = 128
B_ROPE = 64
B_QK_DIM = B_NOPE + B_ROPE
B_V_DIM = 128
Q_LORA = 256
KV_LORA = 128
ROPE_THETA = 10000.0
EPS = 1e-6
LANES = 128

_SPLITS = np.cumsum([0, 512, 512, 512, 512, Q_LORA, KV_LORA, B_ROPE, 512])
_P_AQ, _P_AK, _P_AV, _P_AG, _P_CQ, _P_CKV, _P_KR, _P_BG, _P_END = np.cumsum(
    [0, A_UNITS * LANES, A_UNITS * LANES, 512, 512, Q_LORA, KV_LORA, LANES, 512])

_VMEM_LIMIT = 56 * 1024 * 1024
LOG2E = math.log2(math.e)
MAX_STATIC_SHIFT = 32.0
POS_SPLIT = 128
SIGMA_PIECES = 3
FIXED_TQ = 1024
FIXED_TK = 512
MAX_UNROLL = 64
VT_ROWS = 144

_A_SHIFT = A_HEAD_DIM
_A_QHI = _A_SHIFT + 1
_A_QLO = _A_QHI + SIGMA_PIECES
_A_KHI = _A_QLO + SIGMA_PIECES
_A_KLO = _A_KHI + SIGMA_PIECES
_A_AUG_END = _A_KLO + SIGMA_PIECES
_B_SHIFT = B_QK_DIM - LANES

_F32 = jnp.float32
_BF16 = jnp.bfloat16


def _lambda_init(layer_idx):
    return 0.8 - 0.6 * math.exp(-0.3 * layer_idx)


def _alibi_slope(head):
    return 2.0 ** (-8.0 * (head + 1) / A_HEADS)


def _sigma_pieces(head):
    rest = _alibi_slope(head) * LOG2E
    pieces = []
    for _ in range(SIGMA_PIECES):
        p = float(np.asarray(rest, np.float32).astype(_BF16).astype(np.float32))
        pieces.append(p)
        rest -= p
    return pieces


def _rope_swap_index():
    half = B_ROPE // 2
    return np.concatenate([np.arange(half, B_ROPE), np.arange(0, half)])


def _silu(x):
    return x / (1.0 + jnp.exp(-x))


def _proj_kernel(x_ref, nw_ref, win_ref, gaq_ref, gak_ref, gcq_ref, wuq_ref, gckv_ref, wukv_ref,
                 gbq_ref, gbkn_ref, gbkr_ref, ct_ref, st_ref, augq_ref, augk_ref, augb_ref,
                 qa_ref, ka_ref, va_ref, ga_ref, qb_ref, kb_ref, vb_ref, gb_ref, *, fixed_shift, seq_len):
    tm = x_ref.shape[0]
    x = x_ref[...]
    h = x * lax.rsqrt(jnp.mean(x * x, axis=-1, keepdims=True) + EPS) * nw_ref[...]
    hb = h.astype(_BF16)

    def proj(lo, hi):
        return jnp.dot(hb, win_ref[:, lo:hi], preferred_element_type=_F32)

    lane = lax.broadcasted_iota(jnp.int32, (tm, LANES), 1)
    low_half = lane < B_ROPE
    ones_col = (lane == 0).astype(_F32)
    ct = ct_ref[...]
    st = st_ref[...]

    if fixed_shift:
        a_qscale = (A_HEAD_DIM ** -0.5) * LOG2E
        b_qscale = (B_QK_DIM ** -0.5) * LOG2E
        pos = (pl.program_id(0) % (seq_len // tm)) * tm + lax.broadcasted_iota(jnp.int32, (tm, LANES), 0)
        pos = pos - seq_len // 2
        pos_hi = ((pos >> 7) * POS_SPLIT).astype(_F32)
        pos_lo = (pos & (POS_SPLIT - 1)).astype(_F32)
        q_hi_lanes = (lane >= _A_QHI) & (lane < _A_QLO)
        q_lo_lanes = (lane >= _A_QLO) & (lane < _A_KHI)
        k_hi_lanes = (lane >= _A_KHI) & (lane < _A_KLO)
        k_lo_lanes = (lane >= _A_KLO) & (lane < _A_AUG_END)
        q_pos = jnp.where(q_hi_lanes, pos_hi, jnp.where(q_lo_lanes, pos_lo, 0.0))
        k_pos = jnp.where(k_hi_lanes, pos_hi, jnp.where(k_lo_lanes, pos_lo, 0.0))
        bias_lanes = (lane >= _A_QHI) & (lane < _A_AUG_END)
    else:
        a_qscale = A_HEAD_DIM ** -0.5
        b_qscale = B_QK_DIM ** -0.5

    aq = proj(_P_AQ, _P_AK)
    ak = proj(_P_AK, _P_AV)
    for u in range(A_UNITS):
        xq = aq[:, u * LANES:(u + 1) * LANES]
        rq = lax.rsqrt(jnp.sum(xq * xq, axis=-1, keepdims=True) * (1.0 / A_HEAD_DIM) + EPS)
        qn = xq * rq * gaq_ref[...] * a_qscale
        xk = ak[:, u * LANES:(u + 1) * LANES]
        rk = lax.rsqrt(jnp.sum(xk * xk, axis=-1, keepdims=True) * (1.0 / A_HEAD_DIM) + EPS)
        kn = xk * rk * gak_ref[...]
        if fixed_shift:
            q_keys_before = qn + augq_ref[u] + q_pos
            qa_ref[u, 0] = q_keys_before.astype(_BF16)
            qa_ref[u, 1] = jnp.where(bias_lanes, -q_keys_before, q_keys_before).astype(_BF16)
            ka_ref[u] = (kn + augk_ref[u] + k_pos).astype(_BF16)
        else:
            qa_ref[u, 0] = qn.astype(_BF16)
            ka_ref[u] = kn.astype(_BF16)
    def store_values(v_ref, hd, vv):
        if fixed_shift:
            pad_rows = lax.broadcasted_iota(jnp.int32, (VT_ROWS - LANES, tm), 0)
            v_ref[hd, 0] = jnp.concatenate([vv.T, (pad_rows == 0).astype(_F32)], axis=0).astype(_BF16)
        else:
            v_ref[hd] = jnp.concatenate([vv, ones_col], axis=1).astype(_BF16)

    av = proj(_P_AV, _P_AG)
    for hd in range(A_HEADS):
        store_values(va_ref, hd, av[:, hd * LANES:(hd + 1) * LANES])
    ga_ref[...] = _silu(proj(_P_AG, _P_CQ)).astype(_BF16)

    cq = proj(_P_CQ, _P_CKV)
    cqn = cq * lax.rsqrt(jnp.mean(cq * cq, axis=-1, keepdims=True) + EPS) * gcq_ref[...]
    qall = jnp.dot(cqn.astype(_BF16), wuq_ref[...], preferred_element_type=_F32)
    for hd in range(B_HEADS):
        q0 = qall[:, hd * 256:hd * 256 + LANES]
        q1 = qall[:, hd * 256 + LANES:(hd + 1) * 256]
        ss = (jnp.sum(q0 * q0, axis=-1, keepdims=True)
              + jnp.sum(jnp.where(low_half, q1 * q1, 0.0), axis=-1, keepdims=True))
        r = lax.rsqrt(ss * (1.0 / B_QK_DIM) + EPS)
        q0n = q0 * r * gbq_ref[:, :LANES] * b_qscale
        q1n = q1 * r * gbq_ref[:, LANES:]
        q1r = (q1n * ct + pltpu.roll(q1n, B_ROPE, 1) * st) * b_qscale
        if fixed_shift:
            q1r = q1r + augb_ref[0:1, :]
        qb_ref[hd, 0] = jnp.concatenate([q0n, q1r], axis=1).astype(_BF16)

    ckv = proj(_P_CKV, _P_KR)
    ckvn = ckv * lax.rsqrt(jnp.mean(ckv * ckv, axis=-1, keepdims=True) + EPS) * gckv_ref[...]
    kv = jnp.dot(ckvn.astype(_BF16), wukv_ref[...], preferred_element_type=_F32)
    kr2 = proj(_P_KR, _P_BG)
    kr_ss = jnp.sum(jnp.where(low_half, kr2 * kr2, 0.0), axis=-1, keepdims=True)
    krg = kr2 * gbkr_ref[...]
    krr = krg * ct + pltpu.roll(krg, B_ROPE, 1) * st
    for hd in range(B_HEADS):
        kn = kv[:, hd * 256:hd * 256 + LANES]
        vv = kv[:, hd * 256 + LANES:(hd + 1) * 256]
        r = lax.rsqrt((jnp.sum(kn * kn, axis=-1, keepdims=True) + kr_ss) * (1.0 / B_QK_DIM) + EPS)
        k1 = krr * r
        if fixed_shift:
            k1 = k1 + augb_ref[1:2, :]
        kb_ref[hd] = jnp.concatenate([kn * r * gbkn_ref[...], k1], axis=1).astype(_BF16)
        store_values(vb_ref, hd, vv)
    gb_ref[...] = _silu(proj(_P_BG, _P_END)).astype(_BF16)


def _proj_call(x2d, seq_len, lw, tm, fixed_shift):
    t = x2d.shape[0]
    blocks_per_seq = seq_len // tm
    n_var = 2 if fixed_shift else 1
    const = lambda i: (0, 0)
    const3 = lambda i: (0, 0, 0)
    row = lambda i: (i, 0)
    unit_row = lambda i: (0, i, 0)
    var_row = lambda i: (0, 0, i, 0)
    rope_row = lambda i: (i % blocks_per_seq, 0)
    in_specs = [
        pl.BlockSpec((tm, D_MODEL), row),
        pl.BlockSpec((1, D_MODEL), const),
        pl.BlockSpec((D_MODEL, int(_P_END)), const),
        pl.BlockSpec((1, LANES), const),
        pl.BlockSpec((1, LANES), const),
        pl.BlockSpec((1, Q_LORA), const),
        pl.BlockSpec((Q_LORA, B_HEADS * 256), const),
        pl.BlockSpec((1, KV_LORA), const),
        pl.BlockSpec((KV_LORA, B_HEADS * 256), const),
        pl.BlockSpec((1, 256), const),
        pl.BlockSpec((1, LANES), const),
        pl.BlockSpec((1, LANES), const),
        pl.BlockSpec((tm, LANES), rope_row),
        pl.BlockSpec((tm, LANES), rope_row),
        pl.BlockSpec((A_UNITS, 1, LANES), const3),
        pl.BlockSpec((A_UNITS, 1, LANES), const3),
        pl.BlockSpec((2, LANES), const),
    ]
    if fixed_shift:
        v_shape = lambda heads: jax.ShapeDtypeStruct((heads, t // tm, VT_ROWS, tm), _BF16)
        v_spec = lambda heads: pl.BlockSpec((heads, 1, VT_ROWS, tm), lambda i: (0, i, 0, 0))
    else:
        v_shape = lambda heads: jax.ShapeDtypeStruct((heads, t, 256), _BF16)
        v_spec = lambda heads: pl.BlockSpec((heads, tm, 256), unit_row)
    out_shape = [
        jax.ShapeDtypeStruct((A_UNITS, n_var, t, LANES), _BF16),
        jax.ShapeDtypeStruct((A_UNITS, t, LANES), _BF16),
        v_shape(A_HEADS),
        jax.ShapeDtypeStruct((t, 512), _BF16),
        jax.ShapeDtypeStruct((B_HEADS, 1, t, 256), _BF16),
        jax.ShapeDtypeStruct((B_HEADS, t, 256), _BF16),
        v_shape(B_HEADS),
        jax.ShapeDtypeStruct((t, 512), _BF16),
    ]
    out_specs = [
        pl.BlockSpec((A_UNITS, n_var, tm, LANES), var_row),
        pl.BlockSpec((A_UNITS, tm, LANES), unit_row),
        v_spec(A_HEADS),
        pl.BlockSpec((tm, 512), row),
        pl.BlockSpec((B_HEADS, 1, tm, 256), var_row),
        pl.BlockSpec((B_HEADS, tm, 256), unit_row),
        v_spec(B_HEADS),
        pl.BlockSpec((tm, 512), row),
    ]
    return pl.pallas_call(
        functools.partial(_proj_kernel, fixed_shift=fixed_shift, seq_len=seq_len),
        grid=(t // tm,),
        in_specs=in_specs,
        out_specs=out_specs,
        out_shape=out_shape,
        compiler_params=pltpu.CompilerParams(
            dimension_semantics=("parallel",), vmem_limit_bytes=_VMEM_LIMIT),
        name="proj",
    )(x2d, lw["norm_w"], lw["w_in"], lw["g_aq"], lw["g_ak"], lw["g_cq"], lw["w_uq"], lw["g_ckv"],
      lw["w_ukv"], lw["g_bq"], lw["g_bkn"], lw["g_bkr"], lw["rope_c"][seq_len], lw["rope_s"][seq_len],
      lw["aug_q"], lw["aug_k"], lw["aug_b"])


def _flash_kernel(slope_ref, q_ref, k_ref, v_ref, o_ref, m_sc, acc_sc, *, alibi, heads_per_slope):
    kj = pl.program_id(3)
    tq = q_ref.shape[3]
    tk = k_ref.shape[2]

    @pl.when(kj == 0)
    def _():
        m_sc[...] = jnp.full(m_sc.shape, -jnp.inf, _F32)
        acc_sc[...] = jnp.zeros(acc_sc.shape, _F32)

    s = lax.dot_general(q_ref[0, 0, 0], k_ref[0, 0], (((1,), (1,)), ((), ())),
                        preferred_element_type=_F32)
    if alibi:
        slope = slope_ref[pl.program_id(0) // heads_per_slope]
        qpos = pl.program_id(2) * tq + lax.broadcasted_iota(jnp.int32, (tq, tk), 0)
        kpos = kj * tk + lax.broadcasted_iota(jnp.int32, (tq, tk), 1)
        s = s - slope * jnp.abs(qpos - kpos).astype(_F32)
    m_prev = m_sc[...]
    m_new = jnp.maximum(m_prev, jnp.max(s, axis=-1, keepdims=True))
    alpha = jnp.exp(m_prev - m_new)
    p = jnp.exp(s - m_new)
    acc_sc[...] = alpha * acc_sc[...] + jnp.dot(p.astype(_BF16), v_ref[0, 0],
                                                 preferred_element_type=_F32)
    m_sc[...] = m_new

    @pl.when(kj == pl.num_programs(3) - 1)
    def _():
        acc = acc_sc[...]
        o_ref[0, 0] = acc[:, :LANES] / acc[:, LANES:LANES + 1]


def _flash_call(q, k, v, slopes, *, alibi, units_per_v, tq, tk):
    u, _, b, s, dk = q.shape
    kern = functools.partial(_flash_kernel, alibi=alibi, heads_per_slope=units_per_v)
    grid_spec = pltpu.PrefetchScalarGridSpec(
        num_scalar_prefetch=1,
        grid=(u, b, s // tq, s // tk),
        in_specs=[
            pl.BlockSpec((1, 1, 1, tq, dk), lambda ui, bi, qi, ki, sl: (ui, 0, bi, qi, 0)),
            pl.BlockSpec((1, 1, tk, dk), lambda ui, bi, qi, ki, sl: (ui, bi, ki, 0)),
            pl.BlockSpec((1, 1, tk, 256), lambda ui, bi, qi, ki, sl: (ui // units_per_v, bi, ki, 0)),
        ],
        out_specs=pl.BlockSpec((1, 1, tq, LANES), lambda ui, bi, qi, ki, sl: (ui, bi, qi, 0)),
        scratch_shapes=[pltpu.VMEM((tq, 1), _F32), pltpu.VMEM((tq, 256), _F32)],
    )
    return pl.pallas_call(
        kern,
        grid_spec=grid_spec,
        out_shape=jax.ShapeDtypeStruct((u, b, s, LANES), _F32),
        compiler_params=pltpu.CompilerParams(
            dimension_semantics=("parallel", "parallel", "parallel", "arbitrary"),
            vmem_limit_bytes=_VMEM_LIMIT),
        name="flash_a" if alibi else "flash_b",
    )(slopes, q, k, v)


def _attn_kernel(q_ref, k_ref, vt_ref, o_ref, acc_sc, *, group, alibi):
    tq = q_ref.shape[3]
    n_k, _, tk = vt_ref.shape[2:]
    qi = pl.program_id(2)

    def block(q, j, correction=None):
        k = k_ref[0, 0, pl.ds(pl.multiple_of(j * tk, tk), tk), :]
        st = lax.dot_general(k, q, (((1,), (1,)), ((), ())), preferred_element_type=_F32)
        if correction is not None:
            st = st + correction
        return jnp.dot(vt_ref[0, 0, j], jnp.exp2(st).astype(_BF16), preferred_element_type=_F32)

    if alibi:
        n_diag = tq // tk
        head = pl.program_id(0) // 2
        sigma = jnp.float32(_alibi_slope(A_HEADS - 1) * LOG2E)
        for hd in range(A_HEADS - 1):
            sigma = jnp.where(head == hd, jnp.float32(_alibi_slope(hd) * LOG2E), sigma)
        ahead = (lax.broadcasted_iota(jnp.int32, (tk, tq), 0) - lax.broadcasted_iota(jnp.int32, (tk, tq), 1))
        first_diag = qi * n_diag
        acc = None
        for d in range(n_diag):
            corr = (-2.0 * sigma) * jnp.maximum(ahead + d * tk, 0).astype(_F32)
            part = block(q_ref[0, 0, 0], first_diag + d, corr)
            acc = part if acc is None else acc + part
        n_loop = n_k - n_diag

        def off_block(x):
            after = (x >= first_diag).astype(jnp.int32)
            return block(q_ref[0, after, 0], x + n_diag * after)
    else:
        acc = None
        off_block = lambda x: block(q_ref[0, 0, 0], x)
        n_loop = n_k

    if n_loop == group:
        for x in range(n_loop):
            part = off_block(x)
            acc = part if acc is None else acc + part
    else:
        acc_sc[...] = jnp.zeros(acc_sc.shape, _F32) if acc is None else acc

        def body(g, carry):
            tot = off_block(g * group)
            for t in range(1, group):
                tot = tot + off_block(g * group + t)
            acc_sc[...] += tot
            return carry

        lax.fori_loop(0, n_loop // group, body, 0)
        acc = acc_sc[...]
    o_ref[0, 0] = (acc[:LANES] / acc[LANES:LANES + 1]).T


def _attn_call(q, k, vt, *, alibi, units_per_v, tq, group):
    u, n_var, b, s, dk = q.shape
    n_k, _, tk = vt.shape[2:]
    n_loop = (s - tq) // tk if alibi else n_k
    assert tq % tk == 0 and s % tq == 0 and n_loop % group == 0, (s, tq, tk, group)
    kern = functools.partial(_attn_kernel, group=group, alibi=alibi)
    resident = pl.Buffered(1)
    return pl.pallas_call(
        kern,
        grid=(u, b, s // tq),
        in_specs=[
            pl.BlockSpec((1, n_var, 1, tq, dk), lambda ui, bi, qi: (ui, 0, bi, qi, 0)),
            pl.BlockSpec((1, 1, s, dk), lambda ui, bi, qi: (ui, bi, 0, 0), pipeline_mode=resident),
            pl.BlockSpec((1, 1, n_k, VT_ROWS, tk), lambda ui, bi, qi: (ui // units_per_v, bi, 0, 0, 0),
                         pipeline_mode=resident),
        ],
        out_specs=pl.BlockSpec((1, 1, tq, LANES), lambda ui, bi, qi: (ui, bi, qi, 0)),
        out_shape=jax.ShapeDtypeStruct((u, b, s, LANES), _F32),
        scratch_shapes=[pltpu.VMEM((VT_ROWS, tq), _F32)],
        compiler_params=pltpu.CompilerParams(
            dimension_semantics=("parallel", "parallel", "parallel"),
            vmem_limit_bytes=_VMEM_LIMIT),
        name="attn_a" if alibi else "attn_b",
    )(q, k, vt)


def _out_kernel(oa_ref, ob_ref, ga_ref, gb_ref, x_ref, lq1_ref, lk1_ref, lq2_ref, lk2_ref, subln_ref,
                wout_ref, y_ref, *, lam_init):
    lam = (jnp.exp(jnp.sum(lq1_ref[...] * lk1_ref[...], axis=-1, keepdims=True))
           - jnp.exp(jnp.sum(lq2_ref[...] * lk2_ref[...], axis=-1, keepdims=True)) + lam_init)
    ga = ga_ref[...].astype(_F32)
    gb = gb_ref[...].astype(_F32)
    pieces = []
    for hd in range(A_HEADS):
        d = oa_ref[2 * hd] - lam * oa_ref[2 * hd + 1]
        n = d * lax.rsqrt(jnp.mean(d * d, axis=-1, keepdims=True) + EPS) * subln_ref[...]
        pieces.append(n * (1.0 - lam_init) * ga[:, hd * LANES:(hd + 1) * LANES])
    for hd in range(B_HEADS):
        pieces.append(ob_ref[hd] * gb[:, hd * LANES:(hd + 1) * LANES])
    y = jnp.concatenate(pieces, axis=1).astype(_BF16)
    y_ref[...] = x_ref[...] + jnp.dot(y, wout_ref[...], preferred_element_type=_F32)


def _out_call(oa, ob, ga, gb, x2d, lw, lam_init, tm):
    t = x2d.shape[0]
    const = lambda i: (0, 0)
    row = lambda i: (i, 0)
    unit_row = lambda i: (0, i, 0)
    return pl.pallas_call(
        functools.partial(_out_kernel, lam_init=lam_init),
        grid=(t // tm,),
        in_specs=[
            pl.BlockSpec((A_UNITS, tm, LANES), unit_row),
            pl.BlockSpec((B_HEADS, tm, LANES), unit_row),
            pl.BlockSpec((tm, 512), row),
            pl.BlockSpec((tm, 512), row),
            pl.BlockSpec((tm, D_MODEL), row),
            pl.BlockSpec((1, A_HEAD_DIM), const),
            pl.BlockSpec((1, A_HEAD_DIM), const),
            pl.BlockSpec((1, A_HEAD_DIM), const),
            pl.BlockSpec((1, A_HEAD_DIM), const),
            pl.BlockSpec((1, A_V_DIM), const),
            pl.BlockSpec((D_MODEL, D_MODEL), const),
        ],
        out_specs=pl.BlockSpec((tm, D_MODEL), row),
        out_shape=jax.ShapeDtypeStruct((t, D_MODEL), _F32),
        compiler_params=pltpu.CompilerParams(
            dimension_semantics=("parallel",), vmem_limit_bytes=_VMEM_LIMIT),
        name="out",
    )(oa, ob, ga, gb, x2d, lw["lq1"], lw["lk1"], lw["lq2"], lw["lk2"], lw["subln"], lw["w_out"])


def _rope_tables(seq_len):
    inv = ROPE_THETA ** (-jnp.arange(0, B_ROPE, 2, dtype=_F32) / B_ROPE)
    ang = jnp.arange(seq_len, dtype=_F32)[:, None] * inv[None, :]
    cos, sin = jnp.cos(ang), jnp.sin(ang)
    zeros = jnp.zeros((seq_len, LANES - B_ROPE), _F32)
    return (jnp.concatenate([cos, cos, zeros], axis=1),
            jnp.concatenate([-sin, sin, zeros], axis=1))


def _pad_lanes(v, width):
    return jnp.pad(v, ((0, 0), (0, width - v.shape[1])))


def _score_bounds(l, p):
    amax = lambda v: jnp.max(jnp.abs(v[l].astype(_F32)))
    bound_a = amax(p["a_q_norm"]) * amax(p["a_k_norm"]) * (A_HEAD_DIM ** 0.5)
    bound_b = amax(p["b_q_norm"]) * amax(p["b_k_norm"]) * (B_QK_DIM ** 0.5)
    return bound_a, bound_b


def _aug_rows(bound_a, bound_b):
    aug_q = np.zeros((A_UNITS, 1, LANES), np.float32)
    aug_k = np.zeros((A_UNITS, 1, LANES), np.float32)
    for u in range(A_UNITS):
        sig = _sigma_pieces(u // 2)
        aug_k[u, 0, _A_SHIFT] = 1.0
        for a in range(SIGMA_PIECES):
            aug_k[u, 0, _A_QHI + a] = -sig[a]
            aug_k[u, 0, _A_QLO + a] = -sig[a]
            aug_q[u, 0, _A_KHI + a] = sig[a]
            aug_q[u, 0, _A_KLO + a] = sig[a]
    shift_lane = (np.arange(LANES) == _A_SHIFT).astype(np.float32).reshape(1, 1, LANES)
    aug_q = jnp.asarray(aug_q) - (bound_a * LOG2E) * shift_lane
    b_lane = (np.arange(LANES) == _B_SHIFT).astype(np.float32)
    aug_b = jnp.stack([-(bound_b * LOG2E) * b_lane, jnp.asarray(b_lane)])
    return aug_q, jnp.asarray(aug_k), aug_b


def _layer_weights(l, p, rope_c, rope_s):
    swap = _rope_swap_index()
    w = p["w_in"][l]
    cols = [w[:, _SPLITS[i]:_SPLITS[i + 1]] for i in range(8)]
    pad_units = lambda c: jnp.pad(c.reshape(D_MODEL, A_UNITS, A_HEAD_DIM),
                                  ((0, 0), (0, 0), (0, LANES - A_HEAD_DIM))).reshape(D_MODEL, A_UNITS * LANES)
    kr = cols[6]
    w_in = jnp.concatenate([pad_units(cols[0]), pad_units(cols[1]), cols[2], cols[3], cols[4], cols[5],
                            kr, kr[:, swap], cols[7]], axis=1).astype(_BF16)
    wuq = p["b_w_uq"][l].reshape(Q_LORA, B_HEADS, B_QK_DIM)
    w_uq = jnp.concatenate([wuq, wuq[:, :, B_NOPE:][:, :, swap]], axis=2).reshape(Q_LORA, B_HEADS * 256)
    gq = p["b_q_norm"][l]
    gk = p["b_k_norm"][l]
    row = lambda v: v.reshape(1, -1).astype(_F32)
    bound_a, bound_b = _score_bounds(l, p)
    aug_q, aug_k, aug_b = _aug_rows(bound_a, bound_b)
    return {
        "norm_w": row(p["norm_w"][l]),
        "w_in": w_in,
        "g_aq": _pad_lanes(row(p["a_q_norm"][l]), LANES),
        "g_ak": _pad_lanes(row(p["a_k_norm"][l]), LANES),
        "g_cq": row(p["b_cq_norm"][l]),
        "w_uq": w_uq.astype(_BF16),
        "g_ckv": row(p["b_ckv_norm"][l]),
        "w_ukv": p["b_w_ukv"][l].astype(_BF16),
        "g_bq": row(jnp.concatenate([gq, gq[B_NOPE:][swap]])),
        "g_bkn": row(gk[:B_NOPE]),
        "g_bkr": row(jnp.concatenate([gk[B_NOPE:], gk[B_NOPE:][swap]])),
        "rope_c": rope_c,
        "rope_s": rope_s,
        "aug_q": aug_q, "aug_k": aug_k, "aug_b": aug_b,
        "lq1": row(p["a_lq1"][l]), "lk1": row(p["a_lk1"][l]),
        "lq2": row(p["a_lq2"][l]), "lk2": row(p["a_lk2"][l]),
        "subln": row(p["a_subln"][l]),
        "w_out": p["w_out"][l].astype(_BF16),
    }


def _tile(n, pref):
    return pref if n % pref == 0 else n


def _largest_group(n, cap):
    return max(g for g in range(1, cap + 1) if n % g == 0) if n > 0 else 1


def _encoder_layer(x, lw, lam_init, slopes, fixed_shift):
    b, s, _ = x.shape
    t = b * s
    x2d = x.reshape(t, D_MODEL)
    tm = _tile(s, FIXED_TK)
    qa, ka, va, ga, qb, kb, vb, gb = _proj_call(x2d, s, lw, tm, fixed_shift)
    split = lambda a: a.reshape(a.shape[:-2] + (b, s, a.shape[-1]))
    tq = tk = _tile(s, 512)
    if fixed_shift:
        assert s <= POS_SPLIT * POS_SPLIT * 2
        ftq = _tile(s, FIXED_TQ)
        split_t = lambda a: a.reshape((a.shape[0], b, s // tm) + a.shape[2:])
        oa = _attn_call(split(qa), split(ka), split_t(va), alibi=True, units_per_v=2, tq=ftq,
                        group=_largest_group((s - ftq) // tm, MAX_UNROLL))
        ob = _attn_call(split(qb), split(kb), split_t(vb), alibi=False, units_per_v=1, tq=ftq,
                        group=_largest_group(s // tm, MAX_UNROLL))
    else:
        oa = _flash_call(split(qa), split(ka), split(va), slopes, alibi=True, units_per_v=2, tq=tq, tk=tk)
        ob = _flash_call(split(qb), split(kb), split(vb), slopes, alibi=False, units_per_v=1, tq=tq, tk=tk)
    y = _out_call(oa.reshape(A_UNITS, t, LANES), ob.reshape(B_HEADS, t, LANES), ga, gb, x2d, lw, lam_init, tm)
    return y.reshape(b, s, D_MODEL)


def _forward(x_prompt, x_sample, p, fixed_shift):
    depth = p["norm_w"].shape[0]
    rope_c, rope_s = {}, {}
    for s in {x_prompt.shape[1], x_sample.shape[1]}:
        rope_c[s], rope_s[s] = _rope_tables(s)
    slopes = jnp.asarray([_alibi_slope(hd) for hd in range(A_HEADS)], _F32)
    y_prompt, y_sample = x_prompt, x_sample
    for l in range(depth):
        lw = _layer_weights(l, p, rope_c, rope_s)
        lam_init = _lambda_init(l)
        y_prompt = _encoder_layer(y_prompt, lw, lam_init, slopes, fixed_shift)
        y_sample = _encoder_layer(y_sample, lw, lam_init, slopes, fixed_shift)
    return (y_prompt, y_sample)


def kernel(x_prompt, x_sample, norm_w, w_in, a_q_norm, a_k_norm, a_lq1, a_lk1, a_lq2, a_lk2, a_subln,
           b_cq_norm, b_w_uq, b_ckv_norm, b_w_ukv, b_q_norm, b_k_norm, w_out):
    p = dict(norm_w=norm_w, w_in=w_in, a_q_norm=a_q_norm, a_k_norm=a_k_norm, a_lq1=a_lq1, a_lk1=a_lk1,
             a_lq2=a_lq2, a_lk2=a_lk2, a_subln=a_subln, b_cq_norm=b_cq_norm, b_w_uq=b_w_uq,
             b_ckv_norm=b_ckv_norm, b_w_ukv=b_w_ukv, b_q_norm=b_q_norm, b_k_norm=b_k_norm, w_out=w_out)
    bounds = jnp.stack([jnp.stack(_score_bounds(l, p)) for l in range(norm_w.shape[0])])
    shift_ok = jnp.max(bounds) <= MAX_STATIC_SHIFT
    return lax.cond(shift_ok,
                    lambda xp, xs, pp: _forward(xp, xs, pp, True),
                    lambda xp, xs, pp: _forward(xp, xs, pp, False),
                    x_prompt, x_sample, p)
```

```python
import functools
import math

import numpy as np
import jax
import jax.numpy as jnp
from jax import lax
from jax.experimental import pallas as pl
from jax.experimental.pallas import tpu as pltpu

D_MODEL = 1024
A_HEADS = 4
A_HEAD_DIM = 64
A_V_DIM = 128
A_UNITS = 2 * A_HEADS
B_HEADS = 4
B_NOPE = 128
B_ROPE = 64
B_QK_DIM = B_NOPE + B_ROPE
B_V_DIM = 128
Q_LORA = 256
KV_LORA = 128
ROPE_THETA = 10000.0
EPS = 1e-6
LANES = 128

_SPLITS = np.cumsum([0, 512, 512, 512, 512, Q_LORA, KV_LORA, B_ROPE, 512])
_P_AQ, _P_AK, _P_AV, _P_AG, _P_CQ, _P_CKV, _P_KR, _P_BG, _P_END = np.cumsum(
    [0, A_UNITS * LANES, A_UNITS * LANES, 512, 512, Q_LORA, KV_LORA, LANES, 512])

_VMEM_LIMIT = 56 * 1024 * 1024
LOG2E = math.log2(math.e)
MAX_STATIC_SHIFT = 32.0
POS_SPLIT = 128
SIGMA_PIECES = 3
FIXED_TQ = 1024
FIXED_TK = 512
MAX_BLOCK_PAIRS = 32
VT_ROWS = 144

_A_SHIFT = A_HEAD_DIM
_A_QHI = _A_SHIFT + 1
_A_QLO = _A_QHI + SIGMA_PIECES
_A_KHI = _A_QLO + SIGMA_PIECES
_A_KLO = _A_KHI + SIGMA_PIECES
_A_AUG_END = _A_KLO + SIGMA_PIECES
_B_SHIFT = B_QK_DIM - LANES

_F32 = jnp.float32
_BF16 = jnp.bfloat16


def _lambda_init(layer_idx):
    return 0.8 - 0.6 * math.exp(-0.3 * layer_idx)


def _alibi_slope(head):
    return 2.0 ** (-8.0 * (head + 1) / A_HEADS)


def _sigma_pieces(head):
    rest = _alibi_slope(head) * LOG2E
    pieces = []
    for _ in range(SIGMA_PIECES):
        p = float(np.asarray(rest, np.float32).astype(_BF16).astype(np.float32))
        pieces.append(p)
        rest -= p
    return pieces


def _rope_swap_index():
    half = B_ROPE // 2
    return np.concatenate([np.arange(half, B_ROPE), np.arange(0, half)])


def _silu(x):
    return x / (1.0 + jnp.exp(-x))


def _proj_kernel(x_ref, nw_ref, win_ref, gaq_ref, gak_ref, gcq_ref, wuq_ref, gckv_ref, wukv_ref,
                 gbq_ref, gbkn_ref, gbkr_ref, ct_ref, st_ref, augq_ref, augk_ref, augb_ref,
                 qa_ref, ka_ref, va_ref, ga_ref, qb_ref, kb_ref, vb_ref, gb_ref, *, fixed_shift, seq_len):
    tm = x_ref.shape[0]
    x = x_ref[...]
    h = x * lax.rsqrt(jnp.mean(x * x, axis=-1, keepdims=True) + EPS) * nw_ref[...]
    hb = h.astype(_BF16)

    def proj(lo, hi):
        return jnp.dot(hb, win_ref[:, lo:hi], preferred_element_type=_F32)

    lane = lax.broadcasted_iota(jnp.int32, (tm, LANES), 1)
    low_half = lane < B_ROPE
    ones_col = (lane == 0).astype(_F32)
    ct = ct_ref[...]
    st = st_ref[...]

    if fixed_shift:
        a_qscale = (A_HEAD_DIM ** -0.5) * LOG2E
        b_qscale = (B_QK_DIM ** -0.5) * LOG2E
        pos = (pl.program_id(0) % (seq_len // tm)) * tm + lax.broadcasted_iota(jnp.int32, (tm, LANES), 0)
        pos = pos - seq_len // 2
        pos_hi = ((pos >> 7) * POS_SPLIT).astype(_F32)
        pos_lo = (pos & (POS_SPLIT - 1)).astype(_F32)
        q_hi_lanes = (lane >= _A_QHI) & (lane < _A_QLO)
        q_lo_lanes = (lane >= _A_QLO) & (lane < _A_KHI)
        k_hi_lanes = (lane >= _A_KHI) & (lane < _A_KLO)
        k_lo_lanes = (lane >= _A_KLO) & (lane < _A_AUG_END)
        q_pos = jnp.where(q_hi_lanes, pos_hi, jnp.where(q_lo_lanes, pos_lo, 0.0))
        k_pos = jnp.where(k_hi_lanes, pos_hi, jnp.where(k_lo_lanes, pos_lo, 0.0))
        sub = lax.broadcasted_iota(jnp.int32, (LANES, tm), 0)
        bias_rows = (sub >= _A_QHI) & (sub < _A_AUG_END)
    else:
        a_qscale = A_HEAD_DIM ** -0.5
        b_qscale = B_QK_DIM ** -0.5

    aq = proj(_P_AQ, _P_AK)
    ak = proj(_P_AK, _P_AV)
    for u in range(A_UNITS):
        xq = aq[:, u * LANES:(u + 1) * LANES]
        rq = lax.rsqrt(jnp.sum(xq * xq, axis=-1, keepdims=True) * (1.0 / A_HEAD_DIM) + EPS)
        qn = xq * rq * gaq_ref[...] * a_qscale
        xk = ak[:, u * LANES:(u + 1) * LANES]
        rk = lax.rsqrt(jnp.sum(xk * xk, axis=-1, keepdims=True) * (1.0 / A_HEAD_DIM) + EPS)
        kn = xk * rk * gak_ref[...]
        if fixed_shift:
            q_keys_before = (qn + augq_ref[u] + q_pos).T
            qa_ref[u, 0] = q_keys_before.astype(_BF16)
            qa_ref[u, 1] = jnp.where(bias_rows, -q_keys_before, q_keys_before).astype(_BF16)
            ka_ref[u] = (kn + augk_ref[u] + k_pos).astype(_BF16)
        else:
            qa_ref[u, 0] = qn.astype(_BF16)
            ka_ref[u] = kn.astype(_BF16)
    def store_values(v_ref, hd, vv):
        if fixed_shift:
            pad_rows = lax.broadcasted_iota(jnp.int32, (VT_ROWS - LANES, tm), 0)
            v_ref[hd, 0] = jnp.concatenate([vv.T, (pad_rows == 0).astype(_F32)], axis=0).astype(_BF16)
        else:
            v_ref[hd] = jnp.concatenate([vv, ones_col], axis=1).astype(_BF16)

    av = proj(_P_AV, _P_AG)
    for hd in range(A_HEADS):
        store_values(va_ref, hd, av[:, hd * LANES:(hd + 1) * LANES])
    ga_ref[...] = _silu(proj(_P_AG, _P_CQ)).astype(_BF16)

    cq = proj(_P_CQ, _P_CKV)
    cqn = cq * lax.rsqrt(jnp.mean(cq * cq, axis=-1, keepdims=True) + EPS) * gcq_ref[...]
    qall = jnp.dot(cqn.astype(_BF16), wuq_ref[...], preferred_element_type=_F32)
    for hd in range(B_HEADS):
        q0 = qall[:, hd * 256:hd * 256 + LANES]
        q1 = qall[:, hd * 256 + LANES:(hd + 1) * 256]
        ss = (jnp.sum(q0 * q0, axis=-1, keepdims=True)
              + jnp.sum(jnp.where(low_half, q1 * q1, 0.0), axis=-1, keepdims=True))
        r = lax.rsqrt(ss * (1.0 / B_QK_DIM) + EPS)
        q0n = q0 * r * gbq_ref[:, :LANES] * b_qscale
        q1n = q1 * r * gbq_ref[:, LANES:]
        q1r = (q1n * ct + pltpu.roll(q1n, B_ROPE, 1) * st) * b_qscale
        if fixed_shift:
            q1r = q1r + augb_ref[0:1, :]
            qb_ref[hd, 0] = jnp.concatenate([q0n.T, q1r.T], axis=0).astype(_BF16)
        else:
            qb_ref[hd, 0] = jnp.concatenate([q0n, q1r], axis=1).astype(_BF16)

    ckv = proj(_P_CKV, _P_KR)
    ckvn = ckv * lax.rsqrt(jnp.mean(ckv * ckv, axis=-1, keepdims=True) + EPS) * gckv_ref[...]
    kv = jnp.dot(ckvn.astype(_BF16), wukv_ref[...], preferred_element_type=_F32)
    kr2 = proj(_P_KR, _P_BG)
    kr_ss = jnp.sum(jnp.where(low_half, kr2 * kr2, 0.0), axis=-1, keepdims=True)
    krg = kr2 * gbkr_ref[...]
    krr = krg * ct + pltpu.roll(krg, B_ROPE, 1) * st
    for hd in range(B_HEADS):
        kn = kv[:, hd * 256:hd * 256 + LANES]
        vv = kv[:, hd * 256 + LANES:(hd + 1) * 256]
        r = lax.rsqrt((jnp.sum(kn * kn, axis=-1, keepdims=True) + kr_ss) * (1.0 / B_QK_DIM) + EPS)
        k1 = krr * r
        if fixed_shift:
            k1 = k1 + augb_ref[1:2, :]
        kb_ref[hd] = jnp.concatenate([kn * r * gbkn_ref[...], k1], axis=1).astype(_BF16)
        store_values(vb_ref, hd, vv)
    gb_ref[...] = _silu(proj(_P_BG, _P_END)).astype(_BF16)


def _proj_call(x2d, seq_len, lw, tm, fixed_shift):
    t = x2d.shape[0]
    blocks_per_seq = seq_len // tm
    n_var = 2 if fixed_shift else 1
    const = lambda i: (0, 0)
    const3 = lambda i: (0, 0, 0)
    row = lambda i: (i, 0)
    unit_row = lambda i: (0, i, 0)
    var_row = lambda i: (0, 0, i, 0)
    rope_row = lambda i: (i % blocks_per_seq, 0)
    in_specs = [
        pl.BlockSpec((tm, D_MODEL), row),
        pl.BlockSpec((1, D_MODEL), const),
        pl.BlockSpec((D_MODEL, int(_P_END)), const),
        pl.BlockSpec((1, LANES), const),
        pl.BlockSpec((1, LANES), const),
        pl.BlockSpec((1, Q_LORA), const),
        pl.BlockSpec((Q_LORA, B_HEADS * 256), const),
        pl.BlockSpec((1, KV_LORA), const),
        pl.BlockSpec((KV_LORA, B_HEADS * 256), const),
        pl.BlockSpec((1, 256), const),
        pl.BlockSpec((1, LANES), const),
        pl.BlockSpec((1, LANES), const),
        pl.BlockSpec((tm, LANES), rope_row),
        pl.BlockSpec((tm, LANES), rope_row),
        pl.BlockSpec((A_UNITS, 1, LANES), const3),
        pl.BlockSpec((A_UNITS, 1, LANES), const3),
        pl.BlockSpec((2, LANES), const),
    ]
    if fixed_shift:
        q_shape = lambda units, dk: jax.ShapeDtypeStruct((units, n_var if dk == LANES else 1, dk, t), _BF16)
        q_spec = lambda units, dk: pl.BlockSpec((units, n_var if dk == LANES else 1, dk, tm),
                                                lambda i: (0, 0, 0, i))
        v_shape = lambda heads: jax.ShapeDtypeStruct((heads, t // tm, VT_ROWS, tm), _BF16)
        v_spec = lambda heads: pl.BlockSpec((heads, 1, VT_ROWS, tm), lambda i: (0, i, 0, 0))
    else:
        q_shape = lambda units, dk: jax.ShapeDtypeStruct((units, 1, t, dk), _BF16)
        q_spec = lambda units, dk: pl.BlockSpec((units, 1, tm, dk), var_row)
        v_shape = lambda heads: jax.ShapeDtypeStruct((heads, t, 256), _BF16)
        v_spec = lambda heads: pl.BlockSpec((heads, tm, 256), unit_row)
    out_shape = [
        q_shape(A_UNITS, LANES),
        jax.ShapeDtypeStruct((A_UNITS, t, LANES), _BF16),
        v_shape(A_HEADS),
        jax.ShapeDtypeStruct((t, 512), _BF16),
        q_shape(B_HEADS, 256),
        jax.ShapeDtypeStruct((B_HEADS, t, 256), _BF16),
        v_shape(B_HEADS),
        jax.ShapeDtypeStruct((t, 512), _BF16),
    ]
    out_specs = [
        q_spec(A_UNITS, LANES),
        pl.BlockSpec((A_UNITS, tm, LANES), unit_row),
        v_spec(A_HEADS),
        pl.BlockSpec((tm, 512), row),
        q_spec(B_HEADS, 256),
        pl.BlockSpec((B_HEADS, tm, 256), unit_row),
        v_spec(B_HEADS),
        pl.BlockSpec((tm, 512), row),
    ]
    return pl.pallas_call(
        functools.partial(_proj_kernel, fixed_shift=fixed_shift, seq_len=seq_len),
        grid=(t // tm,),
        in_specs=in_specs,
        out_specs=out_specs,
        out_shape=out_shape,
        compiler_params=pltpu.CompilerParams(
            dimension_semantics=("parallel",), vmem_limit_bytes=_VMEM_LIMIT),
        name="proj",
    )(x2d, lw["norm_w"], lw["w_in"], lw["g_aq"], lw["g_ak"], lw["g_cq"], lw["w_uq"], lw["g_ckv"],
      lw["w_ukv"], lw["g_bq"], lw["g_bkn"], lw["g_bkr"], lw["rope_c"][seq_len], lw["rope_s"][seq_len],
      lw["aug_q"], lw["aug_k"], lw["aug_b"])


def _flash_kernel(slope_ref, q_ref, k_ref, v_ref, o_ref, m_sc, acc_sc, *, alibi, heads_per_slope):
    kj = pl.program_id(3)
    tq = q_ref.shape[3]
    tk = k_ref.shape[2]

    @pl.when(kj == 0)
    def _():
        m_sc[...] = jnp.full(m_sc.shape, -jnp.inf, _F32)
        acc_sc[...] = jnp.zeros(acc_sc.shape, _F32)

    s = lax.dot_general(q_ref[0, 0, 0], k_ref[0, 0], (((1,), (1,)), ((), ())),
                        preferred_element_type=_F32)
    if alibi:
        slope = slope_ref[pl.program_id(0) // heads_per_slope]
        qpos = pl.program_id(2) * tq + lax.broadcasted_iota(jnp.int32, (tq, tk), 0)
        kpos = kj * tk + lax.broadcasted_iota(jnp.int32, (tq, tk), 1)
        s = s - slope * jnp.abs(qpos - kpos).astype(_F32)
    m_prev = m_sc[...]
    m_new = jnp.maximum(m_prev, jnp.max(s, axis=-1, keepdims=True))
    alpha = jnp.exp(m_prev - m_new)
    p = jnp.exp(s - m_new)
    acc_sc[...] = alpha * acc_sc[...] + jnp.dot(p.astype(_BF16), v_ref[0, 0],
                                                 preferred_element_type=_F32)
    m_sc[...] = m_new

    @pl.when(kj == pl.num_programs(3) - 1)
    def _():
        acc = acc_sc[...]
        o_ref[0, 0] = acc[:, :LANES] / acc[:, LANES:LANES + 1]


def _flash_call(q, k, v, slopes, *, alibi, units_per_v, tq, tk):
    u, _, b, s, dk = q.shape
    kern = functools.partial(_flash_kernel, alibi=alibi, heads_per_slope=units_per_v)
    grid_spec = pltpu.PrefetchScalarGridSpec(
        num_scalar_prefetch=1,
        grid=(u, b, s // tq, s // tk),
        in_specs=[
            pl.BlockSpec((1, 1, 1, tq, dk), lambda ui, bi, qi, ki, sl: (ui, 0, bi, qi, 0)),
            pl.BlockSpec((1, 1, tk, dk), lambda ui, bi, qi, ki, sl: (ui, bi, ki, 0)),
            pl.BlockSpec((1, 1, tk, 256), lambda ui, bi, qi, ki, sl: (ui // units_per_v, bi, ki, 0)),
        ],
        out_specs=pl.BlockSpec((1, 1, tq, LANES), lambda ui, bi, qi, ki, sl: (ui, bi, qi, 0)),
        scratch_shapes=[pltpu.VMEM((tq, 1), _F32), pltpu.VMEM((tq, 256), _F32)],
    )
    return pl.pallas_call(
        kern,
        grid_spec=grid_spec,
        out_shape=jax.ShapeDtypeStruct((u, b, s, LANES), _F32),
        compiler_params=pltpu.CompilerParams(
            dimension_semantics=("parallel", "parallel", "parallel", "arbitrary"),
            vmem_limit_bytes=_VMEM_LIMIT),
        name="flash_a" if alibi else "flash_b",
    )(slopes, q, k, v)


def _attn_kernel(qt_ref, k_ref, vt_ref, o_ref, *, tq, q_steps, alibi):
    n_k, _, tk = vt_ref.shape[2:]
    q_sub = qt_ref.shape[3] // tq
    n_diag = tq // tk
    if alibi:
        head = pl.program_id(0) // 2
        sigma = jnp.float32(_alibi_slope(A_HEADS - 1) * LOG2E)
        for hd in range(A_HEADS - 1):
            sigma = jnp.where(head == hd, jnp.float32(_alibi_slope(hd) * LOG2E), sigma)
        ahead = (lax.broadcasted_iota(jnp.int32, (tk, tq), 0) - lax.broadcasted_iota(jnp.int32, (tk, tq), 1))

    for sub in range(q_sub):
        qi = sub if q_steps == 1 else pl.program_id(2) * q_sub + sub
        queries = slice(sub * tq, (sub + 1) * tq)

        def block(var, j, correction=None):
            k = k_ref[0, 0, pl.ds(pl.multiple_of(j * tk, tk), tk), :]
            st = jnp.dot(k, qt_ref[0, var, :, queries], preferred_element_type=_F32)
            if correction is not None:
                st = st + correction
            return jnp.dot(vt_ref[0, 0, j], jnp.exp2(st).astype(_BF16), preferred_element_type=_F32)

        acc = None
        if alibi:
            first_diag = qi * n_diag
            for d in range(n_diag):
                corr = (-2.0 * sigma) * jnp.maximum(ahead + d * tk, 0).astype(_F32)
                part = block(0, first_diag + d, corr)
                acc = part if acc is None else acc + part
            for x in range(n_k - n_diag):
                after = (x >= first_diag) * 1
                part = block(after, x + n_diag * after)
                acc = acc + part
        else:
            for x in range(n_k):
                part = block(0, x)
                acc = part if acc is None else acc + part
        o_ref[0, 0, queries, :] = (acc[:LANES] / acc[LANES:LANES + 1]).T


def _attn_call(qt, k, vt, *, alibi, units_per_v, tq):
    u, n_var, dk, _ = qt.shape
    _, b, s, _ = k.shape
    n_k, _, tk = vt.shape[2:]
    assert tq % tk == 0 and s % tq == 0, (s, tq, tk)
    q_sub = _largest_group(s // tq, max(1, MAX_BLOCK_PAIRS // n_k))
    q_steps = s // (tq * q_sub)
    kern = functools.partial(_attn_kernel, tq=tq, q_steps=q_steps, alibi=alibi)
    resident = pl.Buffered(1)
    return pl.pallas_call(
        kern,
        grid=(u, b, q_steps),
        in_specs=[
            pl.BlockSpec((1, n_var, dk, tq * q_sub), lambda ui, bi, qi: (ui, 0, 0, bi * q_steps + qi)),
            pl.BlockSpec((1, 1, s, dk), lambda ui, bi, qi: (ui, bi, 0, 0), pipeline_mode=resident),
            pl.BlockSpec((1, 1, n_k, VT_ROWS, tk), lambda ui, bi, qi: (ui // units_per_v, bi, 0, 0, 0),
                         pipeline_mode=resident),
        ],
        out_specs=pl.BlockSpec((1, 1, tq * q_sub, LANES), lambda ui, bi, qi: (ui, bi, qi, 0)),
        out_shape=jax.ShapeDtypeStruct((u, b, s, LANES), _F32),
        compiler_params=pltpu.CompilerParams(
            dimension_semantics=("parallel", "parallel", "parallel"),
            vmem_limit_bytes=_VMEM_LIMIT),
        name="attn_a" if alibi else "attn_b",
    )(qt, k, vt)


def _out_kernel(oa_ref, ob_ref, ga_ref, gb_ref, x_ref, lq1_ref, lk1_ref, lq2_ref, lk2_ref, subln_ref,
                wout_ref, y_ref, *, lam_init):
    lam = (jnp.exp(jnp.sum(lq1_ref[...] * lk1_ref[...], axis=-1, keepdims=True))
           - jnp.exp(jnp.sum(lq2_ref[...] * lk2_ref[...], axis=-1, keepdims=True)) + lam_init)
    ga = ga_ref[...].astype(_F32)
    gb = gb_ref[...].astype(_F32)
    pieces = []
    for hd in range(A_HEADS):
        d = oa_ref[2 * hd] - lam * oa_ref[2 * hd + 1]
        n = d * lax.rsqrt(jnp.mean(d * d, axis=-1, keepdims=True) + EPS) * subln_ref[...]
        pieces.append(n * (1.0 - lam_init) * ga[:, hd * LANES:(hd + 1) * LANES])
    for hd in range(B_HEADS):
        pieces.append(ob_ref[hd] * gb[:, hd * LANES:(hd + 1) * LANES])
    y = jnp.concatenate(pieces, axis=1).astype(_BF16)
    y_ref[...] = x_ref[...] + jnp.dot(y, wout_ref[...], preferred_element_type=_F32)


def _out_call(oa, ob, ga, gb, x2d, lw, lam_init, tm):
    t = x2d.shape[0]
    const = lambda i: (0, 0)
    row = lambda i: (i, 0)
    unit_row = lambda i: (0, i, 0)
    return pl.pallas_call(
        functools.partial(_out_kernel, lam_init=lam_init),
        grid=(t // tm,),
        in_specs=[
            pl.BlockSpec((A_UNITS, tm, LANES), unit_row),
            pl.BlockSpec((B_HEADS, tm, LANES), unit_row),
            pl.BlockSpec((tm, 512), row),
            pl.BlockSpec((tm, 512), row),
            pl.BlockSpec((tm, D_MODEL), row),
            pl.BlockSpec((1, A_HEAD_DIM), const),
            pl.BlockSpec((1, A_HEAD_DIM), const),
            pl.BlockSpec((1, A_HEAD_DIM), const),
            pl.BlockSpec((1, A_HEAD_DIM), const),
            pl.BlockSpec((1, A_V_DIM), const),
            pl.BlockSpec((D_MODEL, D_MODEL), const),
        ],
        out_specs=pl.BlockSpec((tm, D_MODEL), row),
        out_shape=jax.ShapeDtypeStruct((t, D_MODEL), _F32),
        compiler_params=pltpu.CompilerParams(
            dimension_semantics=("parallel",), vmem_limit_bytes=_VMEM_LIMIT),
        name="out",
    )(oa, ob, ga, gb, x2d, lw["lq1"], lw["lk1"], lw["lq2"], lw["lk2"], lw["subln"], lw["w_out"])


def _rope_tables(seq_len):
    inv = ROPE_THETA ** (-jnp.arange(0, B_ROPE, 2, dtype=_F32) / B_ROPE)
    ang = jnp.arange(seq_len, dtype=_F32)[:, None] * inv[None, :]
    cos, sin = jnp.cos(ang), jnp.sin(ang)
    zeros = jnp.zeros((seq_len, LANES - B_ROPE), _F32)
    return (jnp.concatenate([cos, cos, zeros], axis=1),
            jnp.concatenate([-sin, sin, zeros], axis=1))


def _pad_lanes(v, width):
    return jnp.pad(v, ((0, 0), (0, width - v.shape[1])))


def _score_bounds(l, p):
    amax = lambda v: jnp.max(jnp.abs(v[l].astype(_F32)))
    bound_a = amax(p["a_q_norm"]) * amax(p["a_k_norm"]) * (A_HEAD_DIM ** 0.5)
    bound_b = amax(p["b_q_norm"]) * amax(p["b_k_norm"]) * (B_QK_DIM ** 0.5)
    return bound_a, bound_b


def _aug_rows(bound_a, bound_b):
    aug_q = np.zeros((A_UNITS, 1, LANES), np.float32)
    aug_k = np.zeros((A_UNITS, 1, LANES), np.float32)
    for u in range(A_UNITS):
        sig = _sigma_pieces(u // 2)
        aug_k[u, 0, _A_SHIFT] = 1.0
        for a in range(SIGMA_PIECES):
            aug_k[u, 0, _A_QHI + a] = -sig[a]
            aug_k[u, 0, _A_QLO + a] = -sig[a]
            aug_q[u, 0, _A_KHI + a] = sig[a]
            aug_q[u, 0, _A_KLO + a] = sig[a]
    shift_lane = (np.arange(LANES) == _A_SHIFT).astype(np.float32).reshape(1, 1, LANES)
    aug_q = jnp.asarray(aug_q) - (bound_a * LOG2E) * shift_lane
    b_lane = (np.arange(LANES) == _B_SHIFT).astype(np.float32)
    aug_b = jnp.stack([-(bound_b * LOG2E) * b_lane, jnp.asarray(b_lane)])
    return aug_q, jnp.asarray(aug_k), aug_b


def _layer_weights(l, p, rope_c, rope_s):
    swap = _rope_swap_index()
    w = p["w_in"][l]
    cols = [w[:, _SPLITS[i]:_SPLITS[i + 1]] for i in range(8)]
    pad_units = lambda c: jnp.pad(c.reshape(D_MODEL, A_UNITS, A_HEAD_DIM),
                                  ((0, 0), (0, 0), (0, LANES - A_HEAD_DIM))).reshape(D_MODEL, A_UNITS * LANES)
    kr = cols[6]
    w_in = jnp.concatenate([pad_units(cols[0]), pad_units(cols[1]), cols[2], cols[3], cols[4], cols[5],
                            kr, kr[:, swap], cols[7]], axis=1).astype(_BF16)
    wuq = p["b_w_uq"][l].reshape(Q_LORA, B_HEADS, B_QK_DIM)
    w_uq = jnp.concatenate([wuq, wuq[:, :, B_NOPE:][:, :, swap]], axis=2).reshape(Q_LORA, B_HEADS * 256)
    gq = p["b_q_norm"][l]
    gk = p["b_k_norm"][l]
    row = lambda v: v.reshape(1, -1).astype(_F32)
    bound_a, bound_b = _score_bounds(l, p)
    aug_q, aug_k, aug_b = _aug_rows(bound_a, bound_b)
    return {
        "norm_w": row(p["norm_w"][l]),
        "w_in": w_in,
        "g_aq": _pad_lanes(row(p["a_q_norm"][l]), LANES),
        "g_ak": _pad_lanes(row(p["a_k_norm"][l]), LANES),
        "g_cq": row(p["b_cq_norm"][l]),
        "w_uq": w_uq.astype(_BF16),
        "g_ckv": row(p["b_ckv_norm"][l]),
        "w_ukv": p["b_w_ukv"][l].astype(_BF16),
        "g_bq": row(jnp.concatenate([gq, gq[B_NOPE:][swap]])),
        "g_bkn": row(gk[:B_NOPE]),
        "g_bkr": row(jnp.concatenate([gk[B_NOPE:], gk[B_NOPE:][swap]])),
        "rope_c": rope_c,
        "rope_s": rope_s,
        "aug_q": aug_q, "aug_k": aug_k, "aug_b": aug_b,
        "lq1": row(p["a_lq1"][l]), "lk1": row(p["a_lk1"][l]),
        "lq2": row(p["a_lq2"][l]), "lk2": row(p["a_lk2"][l]),
        "subln": row(p["a_subln"][l]),
        "w_out": p["w_out"][l].astype(_BF16),
    }


def _tile(n, pref):
    return pref if n % pref == 0 else n


def _largest_group(n, cap):
    return max(g for g in range(1, cap + 1) if n % g == 0) if n > 0 else 1


def _encoder_layer(x, lw, lam_init, slopes, fixed_shift):
    b, s, _ = x.shape
    t = b * s
    x2d = x.reshape(t, D_MODEL)
    tm = _tile(s, FIXED_TK)
    qa, ka, va, ga, qb, kb, vb, gb = _proj_call(x2d, s, lw, tm, fixed_shift)
    split = lambda a: a.reshape(a.shape[:-2] + (b, s, a.shape[-1]))
    tq = tk = _tile(s, 512)
    if fixed_shift:
        assert s <= POS_SPLIT * POS_SPLIT * 2
        ftq = _tile(s, FIXED_TQ)
        split_t = lambda a: a.reshape((a.shape[0], b, s // tm) + a.shape[2:])
        oa = _attn_call(qa, split(ka), split_t(va), alibi=True, units_per_v=2, tq=ftq)
        ob = _attn_call(qb, split(kb), split_t(vb), alibi=False, units_per_v=1, tq=ftq)
    else:
        oa = _flash_call(split(qa), split(ka), split(va), slopes, alibi=True, units_per_v=2, tq=tq, tk=tk)
        ob = _flash_call(split(qb), split(kb), split(vb), slopes, alibi=False, units_per_v=1, tq=tq, tk=tk)
    y = _out_call(oa.reshape(A_UNITS, t, LANES), ob.reshape(B_HEADS, t, LANES), ga, gb, x2d, lw, lam_init, tm)
    return y.reshape(b, s, D_MODEL)


def _forward(x_prompt, x_sample, p, fixed_shift):
    depth = p["norm_w"].shape[0]
    rope_c, rope_s = {}, {}
    for s in {x_prompt.shape[1], x_sample.shape[1]}:
        rope_c[s], rope_s[s] = _rope_tables(s)
    slopes = jnp.asarray([_alibi_slope(hd) for hd in range(A_HEADS)], _F32)
    y_prompt, y_sample = x_prompt, x_sample
    for l in range(depth):
        lw = _layer_weights(l, p, rope_c, rope_s)
        lam_init = _lambda_init(l)
        y_prompt = _encoder_layer(y_prompt, lw, lam_init, slopes, fixed_shift)
        y_sample = _encoder_layer(y_sample, lw, lam_init, slopes, fixed_shift)
    return (y_prompt, y_sample)


def kernel(x_prompt, x_sample, norm_w, w_in, a_q_norm, a_k_norm, a_lq1, a_lk1, a_lq2, a_lk2, a_subln,
           b_cq_norm, b_w_uq, b_ckv_norm, b_w_ukv, b_q_norm, b_k_norm, w_out):
    p = dict(norm_w=norm_w, w_in=w_in, a_q_norm=a_q_norm, a_k_norm=a_k_norm, a_lq1=a_lq1, a_lk1=a_lk1,
             a_lq2=a_lq2, a_lk2=a_lk2, a_subln=a_subln, b_cq_norm=b_cq_norm, b_w_uq=b_w_uq,
             b_ckv_norm=b_ckv_norm, b_w_ukv=b_w_ukv, b_q_norm=b_q_norm, b_k_norm=b_k_norm, w_out=w_out)
    bounds = jnp.stack([jnp.stack(_score_bounds(l, p)) for l in range(norm_w.shape[0])])
    shift_ok = jnp.max(bounds) <= MAX_STATIC_SHIFT
    return lax.cond(shift_ok,
                    lambda xp, xs, pp: _forward(xp, xs, pp, True),
                    lambda xp, xs, pp: _forward(xp, xs, pp, False),
                    x_prompt, x_sample, p)
```

```python
import functools
import math

import numpy as np
import jax
import jax.numpy as jnp
from jax import lax
from jax.experimental import pallas as pl
from jax.experimental.pallas import tpu as pltpu

D_MODEL = 1024
A_HEADS = 4
A_HEAD_DIM = 64
A_V_DIM = 128
A_UNITS = 2 * A_HEADS
B_HEADS = 4
B_NOPE = 128
B_ROPE = 64
B_QK_DIM = B_NOPE + B_ROPE
B_V_DIM = 128
Q_LORA = 256
KV_LORA = 128
ROPE_THETA = 10000.0
EPS = 1e-6
LANES = 128

_SPLITS = np.cumsum([0, 512, 512, 512, 512, Q_LORA, KV_LORA, B_ROPE, 512])
_P_AQ, _P_AK, _P_AV, _P_AG, _P_CQ, _P_CKV, _P_KR, _P_BG, _P_END = np.cumsum(
    [0, A_UNITS * LANES, A_UNITS * LANES, 512, 512, Q_LORA, KV_LORA, LANES, 512])

_VMEM_LIMIT = 56 * 1024 * 1024
LOG2E = math.log2(math.e)
MAX_STATIC_SHIFT = 32.0
POS_SPLIT = 128
SIGMA_PIECES = 3
FIXED_TQ = 1024
FIXED_TK = 512
MAX_BLOCK_PAIRS = 32
PROJ_SUB_ROWS = 256
PROJ_STAGE_SKEW = 2
VT_ROWS = 144

_A_SHIFT = A_HEAD_DIM
_A_QHI = _A_SHIFT + 1
_A_QLO = _A_QHI + SIGMA_PIECES
_A_KHI = _A_QLO + SIGMA_PIECES
_A_KLO = _A_KHI + SIGMA_PIECES
_A_AUG_END = _A_KLO + SIGMA_PIECES
_B_SHIFT = B_QK_DIM - LANES

_F32 = jnp.float32
_BF16 = jnp.bfloat16


def _lambda_init(layer_idx):
    return 0.8 - 0.6 * math.exp(-0.3 * layer_idx)


def _alibi_slope(head):
    return 2.0 ** (-8.0 * (head + 1) / A_HEADS)


def _sigma_pieces(head):
    rest = _alibi_slope(head) * LOG2E
    pieces = []
    for _ in range(SIGMA_PIECES):
        p = float(np.asarray(rest, np.float32).astype(_BF16).astype(np.float32))
        pieces.append(p)
        rest -= p
    return pieces


def _rope_swap_index():
    half = B_ROPE // 2
    return np.concatenate([np.arange(half, B_ROPE), np.arange(0, half)])


def _silu(x):
    return x / (1.0 + jnp.exp(-x))


def _proj_kernel(*refs, fixed_shift, seq_len, sub_rows):
    tiles = [_proj_tile(slice(r0, r0 + sub_rows), *refs, fixed_shift=fixed_shift, seq_len=seq_len)
             for r0 in range(0, refs[0].shape[0], sub_rows)]
    live = set(range(len(tiles)))
    step = 0
    while live:
        for t in sorted(live):
            if step >= t * PROJ_STAGE_SKEW and next(tiles[t], "done") == "done":
                live.discard(t)
        step += 1


def _proj_tile(rows, x_ref, nw_ref, win_ref, gaq_ref, gak_ref, gcq_ref, wuq_ref, gckv_ref, wukv_ref,
               gbq_ref, gbkn_ref, gbkr_ref, ct_ref, st_ref, augq_ref, augk_ref, augb_ref,
               qa_ref, ka_ref, va_ref, ga_ref, qb_ref, kb_ref, vb_ref, gb_ref, *, fixed_shift, seq_len):
    tm = rows.stop - rows.start
    x = x_ref[rows, :]
    h = x * lax.rsqrt(jnp.mean(x * x, axis=-1, keepdims=True) + EPS) * nw_ref[...]
    hb = h.astype(_BF16)

    def proj(lo, hi):
        return jnp.dot(hb, win_ref[:, lo:hi], preferred_element_type=_F32)

    lane = lax.broadcasted_iota(jnp.int32, (tm, LANES), 1)
    low_half = lane < B_ROPE
    ones_col = (lane == 0).astype(_F32)
    ct = ct_ref[rows, :]
    st = st_ref[rows, :]

    if fixed_shift:
        a_qscale = (A_HEAD_DIM ** -0.5) * LOG2E
        b_qscale = (B_QK_DIM ** -0.5) * LOG2E
        step_rows = x_ref.shape[0]
        pos = ((pl.program_id(0) % (seq_len // step_rows)) * step_rows + rows.start
               + lax.broadcasted_iota(jnp.int32, (tm, LANES), 0))
        pos = pos - seq_len // 2
        pos_hi = ((pos >> 7) * POS_SPLIT).astype(_F32)
        pos_lo = (pos & (POS_SPLIT - 1)).astype(_F32)
        q_hi_lanes = (lane >= _A_QHI) & (lane < _A_QLO)
        q_lo_lanes = (lane >= _A_QLO) & (lane < _A_KHI)
        k_hi_lanes = (lane >= _A_KHI) & (lane < _A_KLO)
        k_lo_lanes = (lane >= _A_KLO) & (lane < _A_AUG_END)
        q_pos = jnp.where(q_hi_lanes, pos_hi, jnp.where(q_lo_lanes, pos_lo, 0.0))
        k_pos = jnp.where(k_hi_lanes, pos_hi, jnp.where(k_lo_lanes, pos_lo, 0.0))
        sub = lax.broadcasted_iota(jnp.int32, (LANES, tm), 0)
        bias_rows = (sub >= _A_QHI) & (sub < _A_AUG_END)
    else:
        a_qscale = A_HEAD_DIM ** -0.5
        b_qscale = B_QK_DIM ** -0.5

    yield

    aq = proj(_P_AQ, _P_AK)
    ak = proj(_P_AK, _P_AV)
    yield
    for u in range(A_UNITS):
        xq = aq[:, u * LANES:(u + 1) * LANES]
        rq = lax.rsqrt(jnp.sum(xq * xq, axis=-1, keepdims=True) * (1.0 / A_HEAD_DIM) + EPS)
        qn = xq * rq * gaq_ref[...] * a_qscale
        xk = ak[:, u * LANES:(u + 1) * LANES]
        rk = lax.rsqrt(jnp.sum(xk * xk, axis=-1, keepdims=True) * (1.0 / A_HEAD_DIM) + EPS)
        kn = xk * rk * gak_ref[...]
        if fixed_shift:
            q_keys_before = (qn + augq_ref[u] + q_pos).T
            qa_ref[u, 0, :, rows] = q_keys_before.astype(_BF16)
            qa_ref[u, 1, :, rows] = jnp.where(bias_rows, -q_keys_before, q_keys_before).astype(_BF16)
            ka_ref[u, rows, :] = (kn + augk_ref[u] + k_pos).astype(_BF16)
        else:
            qa_ref[u, 0, rows, :] = qn.astype(_BF16)
            ka_ref[u, rows, :] = kn.astype(_BF16)
    yield

    def store_values(v_ref, hd, vv):
        if fixed_shift:
            pad_rows = lax.broadcasted_iota(jnp.int32, (VT_ROWS - LANES, tm), 0)
            v_ref[hd, 0, :, rows] = jnp.concatenate([vv.T, (pad_rows == 0).astype(_F32)], axis=0).astype(_BF16)
        else:
            v_ref[hd, rows, :] = jnp.concatenate([vv, ones_col], axis=1).astype(_BF16)

    av = proj(_P_AV, _P_AG)
    for hd in range(A_HEADS):
        store_values(va_ref, hd, av[:, hd * LANES:(hd + 1) * LANES])
    ga_ref[rows, :] = _silu(proj(_P_AG, _P_CQ)).astype(_BF16)
    yield

    cq = proj(_P_CQ, _P_CKV)
    cqn = cq * lax.rsqrt(jnp.mean(cq * cq, axis=-1, keepdims=True) + EPS) * gcq_ref[...]
    qall = jnp.dot(cqn.astype(_BF16), wuq_ref[...], preferred_element_type=_F32)
    ckv = proj(_P_CKV, _P_KR)
    ckvn = ckv * lax.rsqrt(jnp.mean(ckv * ckv, axis=-1, keepdims=True) + EPS) * gckv_ref[...]
    kv = jnp.dot(ckvn.astype(_BF16), wukv_ref[...], preferred_element_type=_F32)
    kr2 = proj(_P_KR, _P_BG)
    yield

    for hd in range(B_HEADS):
        q0 = qall[:, hd * 256:hd * 256 + LANES]
        q1 = qall[:, hd * 256 + LANES:(hd + 1) * 256]
        ss = (jnp.sum(q0 * q0, axis=-1, keepdims=True)
              + jnp.sum(jnp.where(low_half, q1 * q1, 0.0), axis=-1, keepdims=True))
        r = lax.rsqrt(ss * (1.0 / B_QK_DIM) + EPS)
        q0n = q0 * r * gbq_ref[:, :LANES] * b_qscale
        q1n = q1 * r * gbq_ref[:, LANES:]
        q1r = (q1n * ct + pltpu.roll(q1n, B_ROPE, 1) * st) * b_qscale
        if fixed_shift:
            q1r = q1r + augb_ref[0:1, :]
            qb_ref[hd, 0, :, rows] = jnp.concatenate([q0n.T, q1r.T], axis=0).astype(_BF16)
        else:
            qb_ref[hd, 0, rows, :] = jnp.concatenate([q0n, q1r], axis=1).astype(_BF16)
    yield

    kr_ss = jnp.sum(jnp.where(low_half, kr2 * kr2, 0.0), axis=-1, keepdims=True)
    krg = kr2 * gbkr_ref[...]
    krr = krg * ct + pltpu.roll(krg, B_ROPE, 1) * st
    for hd in range(B_HEADS):
        kn = kv[:, hd * 256:hd * 256 + LANES]
        vv = kv[:, hd * 256 + LANES:(hd + 1) * 256]
        r = lax.rsqrt((jnp.sum(kn * kn, axis=-1, keepdims=True) + kr_ss) * (1.0 / B_QK_DIM) + EPS)
        k1 = krr * r
        if fixed_shift:
            k1 = k1 + augb_ref[1:2, :]
        kb_ref[hd, rows, :] = jnp.concatenate([kn * r * gbkn_ref[...], k1], axis=1).astype(_BF16)
        store_values(vb_ref, hd, vv)
    yield
    gb_ref[rows, :] = _silu(proj(_P_BG, _P_END)).astype(_BF16)


def _proj_call(x2d, seq_len, lw, tm, fixed_shift):
    t = x2d.shape[0]
    blocks_per_seq = seq_len // tm
    n_var = 2 if fixed_shift else 1
    const = lambda i: (0, 0)
    const3 = lambda i: (0, 0, 0)
    row = lambda i: (i, 0)
    unit_row = lambda i: (0, i, 0)
    var_row = lambda i: (0, 0, i, 0)
    rope_row = lambda i: (i % blocks_per_seq, 0)
    in_specs = [
        pl.BlockSpec((tm, D_MODEL), row),
        pl.BlockSpec((1, D_MODEL), const),
        pl.BlockSpec((D_MODEL, int(_P_END)), const),
        pl.BlockSpec((1, LANES), const),
        pl.BlockSpec((1, LANES), const),
        pl.BlockSpec((1, Q_LORA), const),
        pl.BlockSpec((Q_LORA, B_HEADS * 256), const),
        pl.BlockSpec((1, KV_LORA), const),
        pl.BlockSpec((KV_LORA, B_HEADS * 256), const),
        pl.BlockSpec((1, 256), const),
        pl.BlockSpec((1, LANES), const),
        pl.BlockSpec((1, LANES), const),
        pl.BlockSpec((tm, LANES), rope_row),
        pl.BlockSpec((tm, LANES), rope_row),
        pl.BlockSpec((A_UNITS, 1, LANES), const3),
        pl.BlockSpec((A_UNITS, 1, LANES), const3),
        pl.BlockSpec((2, LANES), const),
    ]
    if fixed_shift:
        q_shape = lambda units, dk: jax.ShapeDtypeStruct((units, n_var if dk == LANES else 1, dk, t), _BF16)
        q_spec = lambda units, dk: pl.BlockSpec((units, n_var if dk == LANES else 1, dk, tm),
                                                lambda i: (0, 0, 0, i))
        v_shape = lambda heads: jax.ShapeDtypeStruct((heads, t // tm, VT_ROWS, tm), _BF16)
        v_spec = lambda heads: pl.BlockSpec((heads, 1, VT_ROWS, tm), lambda i: (0, i, 0, 0))
    else:
        q_shape = lambda units, dk: jax.ShapeDtypeStruct((units, 1, t, dk), _BF16)
        q_spec = lambda units, dk: pl.BlockSpec((units, 1, tm, dk), var_row)
        v_shape = lambda heads: jax.ShapeDtypeStruct((heads, t, 256), _BF16)
        v_spec = lambda heads: pl.BlockSpec((heads, tm, 256), unit_row)
    out_shape = [
        q_shape(A_UNITS, LANES),
        jax.ShapeDtypeStruct((A_UNITS, t, LANES), _BF16),
        v_shape(A_HEADS),
        jax.ShapeDtypeStruct((t, 512), _BF16),
        q_shape(B_HEADS, 256),
        jax.ShapeDtypeStruct((B_HEADS, t, 256), _BF16),
        v_shape(B_HEADS),
        jax.ShapeDtypeStruct((t, 512), _BF16),
    ]
    out_specs = [
        q_spec(A_UNITS, LANES),
        pl.BlockSpec((A_UNITS, tm, LANES), unit_row),
        v_spec(A_HEADS),
        pl.BlockSpec((tm, 512), row),
        q_spec(B_HEADS, 256),
        pl.BlockSpec((B_HEADS, tm, 256), unit_row),
        v_spec(B_HEADS),
        pl.BlockSpec((tm, 512), row),
    ]
    return pl.pallas_call(
        functools.partial(_proj_kernel, fixed_shift=fixed_shift, seq_len=seq_len,
                          sub_rows=_tile(tm, PROJ_SUB_ROWS)),
        grid=(t // tm,),
        in_specs=in_specs,
        out_specs=out_specs,
        out_shape=out_shape,
        compiler_params=pltpu.CompilerParams(
            dimension_semantics=("parallel",), vmem_limit_bytes=_VMEM_LIMIT),
        name="proj",
    )(x2d, lw["norm_w"], lw["w_in"], lw["g_aq"], lw["g_ak"], lw["g_cq"], lw["w_uq"], lw["g_ckv"],
      lw["w_ukv"], lw["g_bq"], lw["g_bkn"], lw["g_bkr"], lw["rope_c"][seq_len], lw["rope_s"][seq_len],
      lw["aug_q"], lw["aug_k"], lw["aug_b"])


def _flash_kernel(slope_ref, q_ref, k_ref, v_ref, o_ref, m_sc, acc_sc, *, alibi, heads_per_slope):
    kj = pl.program_id(3)
    tq = q_ref.shape[3]
    tk = k_ref.shape[2]

    @pl.when(kj == 0)
    def _():
        m_sc[...] = jnp.full(m_sc.shape, -jnp.inf, _F32)
        acc_sc[...] = jnp.zeros(acc_sc.shape, _F32)

    s = lax.dot_general(q_ref[0, 0, 0], k_ref[0, 0], (((1,), (1,)), ((), ())),
                        preferred_element_type=_F32)
    if alibi:
        slope = slope_ref[pl.program_id(0) // heads_per_slope]
        qpos = pl.program_id(2) * tq + lax.broadcasted_iota(jnp.int32, (tq, tk), 0)
        kpos = kj * tk + lax.broadcasted_iota(jnp.int32, (tq, tk), 1)
        s = s - slope * jnp.abs(qpos - kpos).astype(_F32)
    m_prev = m_sc[...]
    m_new = jnp.maximum(m_prev, jnp.max(s, axis=-1, keepdims=True))
    alpha = jnp.exp(m_prev - m_new)
    p = jnp.exp(s - m_new)
    acc_sc[...] = alpha * acc_sc[...] + jnp.dot(p.astype(_BF16), v_ref[0, 0],
                                                 preferred_element_type=_F32)
    m_sc[...] = m_new

    @pl.when(kj == pl.num_programs(3) - 1)
    def _():
        acc = acc_sc[...]
        o_ref[0, 0] = acc[:, :LANES] / acc[:, LANES:LANES + 1]


def _flash_call(q, k, v, slopes, *, alibi, units_per_v, tq, tk):
    u, _, b, s, dk = q.shape
    kern = functools.partial(_flash_kernel, alibi=alibi, heads_per_slope=units_per_v)
    grid_spec = pltpu.PrefetchScalarGridSpec(
        num_scalar_prefetch=1,
        grid=(u, b, s // tq, s // tk),
        in_specs=[
            pl.BlockSpec((1, 1, 1, tq, dk), lambda ui, bi, qi, ki, sl: (ui, 0, bi, qi, 0)),
            pl.BlockSpec((1, 1, tk, dk), lambda ui, bi, qi, ki, sl: (ui, bi, ki, 0)),
            pl.BlockSpec((1, 1, tk, 256), lambda ui, bi, qi, ki, sl: (ui // units_per_v, bi, ki, 0)),
        ],
        out_specs=pl.BlockSpec((1, 1, tq, LANES), lambda ui, bi, qi, ki, sl: (ui, bi, qi, 0)),
        scratch_shapes=[pltpu.VMEM((tq, 1), _F32), pltpu.VMEM((tq, 256), _F32)],
    )
    return pl.pallas_call(
        kern,
        grid_spec=grid_spec,
        out_shape=jax.ShapeDtypeStruct((u, b, s, LANES), _F32),
        compiler_params=pltpu.CompilerParams(
            dimension_semantics=("parallel", "parallel", "parallel", "arbitrary"),
            vmem_limit_bytes=_VMEM_LIMIT),
        name="flash_a" if alibi else "flash_b",
    )(slopes, q, k, v)


def _attn_kernel(qt_ref, k_ref, vt_ref, o_ref, *, tq, q_steps, alibi):
    n_k, _, tk = vt_ref.shape[2:]
    q_sub = qt_ref.shape[3] // tq
    n_diag = tq // tk
    if alibi:
        head = pl.program_id(0) // 2
        sigma = jnp.float32(_alibi_slope(A_HEADS - 1) * LOG2E)
        for hd in range(A_HEADS - 1):
            sigma = jnp.where(head == hd, jnp.float32(_alibi_slope(hd) * LOG2E), sigma)
        ahead = (lax.broadcasted_iota(jnp.int32, (tk, tq), 0) - lax.broadcasted_iota(jnp.int32, (tk, tq), 1))

    for sub in range(q_sub):
        qi = sub if q_steps == 1 else pl.program_id(2) * q_sub + sub
        queries = slice(sub * tq, (sub + 1) * tq)

        def block(var, j, correction=None):
            k = k_ref[0, 0, pl.ds(pl.multiple_of(j * tk, tk), tk), :]
            st = jnp.dot(k, qt_ref[0, var, :, queries], preferred_element_type=_F32)
            if correction is not None:
                st = st + correction
            return jnp.dot(vt_ref[0, 0, j], jnp.exp2(st).astype(_BF16), preferred_element_type=_F32)

        def accumulate(acc, part):
            return part if acc is None else acc + part

        acc = None
        if alibi:
            first_diag = qi * n_diag
            for d in range(n_diag):
                corr = (-2.0 * sigma) * jnp.maximum(ahead + d * tk, 0).astype(_F32)
                acc = accumulate(acc, block(0, first_diag + d, corr))
            for x in range(n_k - n_diag):
                after = (x >= first_diag) * 1
                acc = accumulate(acc, block(after, x + n_diag * after))
        else:
            for x in range(n_k):
                acc = accumulate(acc, block(0, x))
        o_ref[0, 0, queries, :] = (acc[:LANES] / acc[LANES:LANES + 1]).T


def _attn_call(qt, k, vt, *, alibi, units_per_v, tq):
    u, n_var, dk, _ = qt.shape
    _, b, s, _ = k.shape
    n_k, _, tk = vt.shape[2:]
    assert tq % tk == 0 and s % tq == 0, (s, tq, tk)
    q_sub = _largest_group(s // tq, max(1, MAX_BLOCK_PAIRS // n_k))
    q_steps = s // (tq * q_sub)
    kern = functools.partial(_attn_kernel, tq=tq, q_steps=q_steps, alibi=alibi)
    return pl.pallas_call(
        kern,
        grid=(u, b, q_steps),
        in_specs=[
            pl.BlockSpec((1, n_var, dk, tq * q_sub), lambda ui, bi, qi: (ui, 0, 0, bi * q_steps + qi)),
            pl.BlockSpec((1, 1, s, dk), lambda ui, bi, qi: (ui, bi, 0, 0)),
            pl.BlockSpec((1, 1, n_k, VT_ROWS, tk), lambda ui, bi, qi: (ui // units_per_v, bi, 0, 0, 0)),
        ],
        out_specs=pl.BlockSpec((1, 1, tq * q_sub, LANES), lambda ui, bi, qi: (ui, bi, qi, 0)),
        out_shape=jax.ShapeDtypeStruct((u, b, s, LANES), _F32),
        compiler_params=pltpu.CompilerParams(
            dimension_semantics=("parallel", "parallel", "parallel"),
            vmem_limit_bytes=_VMEM_LIMIT),
        name="attn_a" if alibi else "attn_b",
    )(qt, k, vt)


def _out_kernel(oa_ref, ob_ref, ga_ref, gb_ref, x_ref, lq1_ref, lk1_ref, lq2_ref, lk2_ref, subln_ref,
                wout_ref, y_ref, *, lam_init):
    lam = (jnp.exp(jnp.sum(lq1_ref[...] * lk1_ref[...], axis=-1, keepdims=True))
           - jnp.exp(jnp.sum(lq2_ref[...] * lk2_ref[...], axis=-1, keepdims=True)) + lam_init)
    ga = ga_ref[...].astype(_F32)
    gb = gb_ref[...].astype(_F32)
    pieces = []
    for hd in range(A_HEADS):
        d = oa_ref[2 * hd] - lam * oa_ref[2 * hd + 1]
        n = d * lax.rsqrt(jnp.mean(d * d, axis=-1, keepdims=True) + EPS) * subln_ref[...]
        pieces.append(n * (1.0 - lam_init) * ga[:, hd * LANES:(hd + 1) * LANES])
    for hd in range(B_HEADS):
        pieces.append(ob_ref[hd] * gb[:, hd * LANES:(hd + 1) * LANES])
    y = jnp.concatenate(pieces, axis=1).astype(_BF16)
    y_ref[...] = x_ref[...] + jnp.dot(y, wout_ref[...], preferred_element_type=_F32)


def _out_call(oa, ob, ga, gb, x2d, lw, lam_init, tm):
    t = x2d.shape[0]
    const = lambda i: (0, 0)
    row = lambda i: (i, 0)
    unit_row = lambda i: (0, i, 0)
    return pl.pallas_call(
        functools.partial(_out_kernel, lam_init=lam_init),
        grid=(t // tm,),
        in_specs=[
            pl.BlockSpec((A_UNITS, tm, LANES), unit_row),
            pl.BlockSpec((B_HEADS, tm, LANES), unit_row),
            pl.BlockSpec((tm, 512), row),
            pl.BlockSpec((tm, 512), row),
            pl.BlockSpec((tm, D_MODEL), row),
            pl.BlockSpec((1, A_HEAD_DIM), const),
            pl.BlockSpec((1, A_HEAD_DIM), const),
            pl.BlockSpec((1, A_HEAD_DIM), const),
            pl.BlockSpec((1, A_HEAD_DIM), const),
            pl.BlockSpec((1, A_V_DIM), const),
            pl.BlockSpec((D_MODEL, D_MODEL), const),
        ],
        out_specs=pl.BlockSpec((tm, D_MODEL), row),
        out_shape=jax.ShapeDtypeStruct((t, D_MODEL), _F32),
        compiler_params=pltpu.CompilerParams(
            dimension_semantics=("parallel",), vmem_limit_bytes=_VMEM_LIMIT),
        name="out",
    )(oa, ob, ga, gb, x2d, lw["lq1"], lw["lk1"], lw["lq2"], lw["lk2"], lw["subln"], lw["w_out"])


def _rope_tables(seq_len):
    inv = ROPE_THETA ** (-jnp.arange(0, B_ROPE, 2, dtype=_F32) / B_ROPE)
    ang = jnp.arange(seq_len, dtype=_F32)[:, None] * inv[None, :]
    cos, sin = jnp.cos(ang), jnp.sin(ang)
    zeros = jnp.zeros((seq_len, LANES - B_ROPE), _F32)
    return (jnp.concatenate([cos, cos, zeros], axis=1),
            jnp.concatenate([-sin, sin, zeros], axis=1))


def _pad_lanes(v, width):
    return jnp.pad(v, ((0, 0), (0, width - v.shape[1])))


def _score_bounds(l, p):
    amax = lambda v: jnp.max(jnp.abs(v[l].astype(_F32)))
    bound_a = amax(p["a_q_norm"]) * amax(p["a_k_norm"]) * (A_HEAD_DIM ** 0.5)
    bound_b = amax(p["b_q_norm"]) * amax(p["b_k_norm"]) * (B_QK_DIM ** 0.5)
    return bound_a, bound_b


def _aug_rows(bound_a, bound_b):
    aug_q = np.zeros((A_UNITS, 1, LANES), np.float32)
    aug_k = np.zeros((A_UNITS, 1, LANES), np.float32)
    for u in range(A_UNITS):
        sig = _sigma_pieces(u // 2)
        aug_k[u, 0, _A_SHIFT] = 1.0
        for a in range(SIGMA_PIECES):
            aug_k[u, 0, _A_QHI + a] = -sig[a]
            aug_k[u, 0, _A_QLO + a] = -sig[a]
            aug_q[u, 0, _A_KHI + a] = sig[a]
            aug_q[u, 0, _A_KLO + a] = sig[a]
    shift_lane = (np.arange(LANES) == _A_SHIFT).astype(np.float32).reshape(1, 1, LANES)
    aug_q = jnp.asarray(aug_q) - (bound_a * LOG2E) * shift_lane
    b_lane = (np.arange(LANES) == _B_SHIFT).astype(np.float32)
    aug_b = jnp.stack([-(bound_b * LOG2E) * b_lane, jnp.asarray(b_lane)])
    return aug_q, jnp.asarray(aug_k), aug_b


def _layer_weights(l, p, rope_c, rope_s):
    swap = _rope_swap_index()
    w = p["w_in"][l]
    cols = [w[:, _SPLITS[i]:_SPLITS[i + 1]] for i in range(8)]
    pad_units = lambda c: jnp.pad(c.reshape(D_MODEL, A_UNITS, A_HEAD_DIM),
                                  ((0, 0), (0, 0), (0, LANES - A_HEAD_DIM))).reshape(D_MODEL, A_UNITS * LANES)
    kr = cols[6]
    w_in = jnp.concatenate([pad_units(cols[0]), pad_units(cols[1]), cols[2], cols[3], cols[4], cols[5],
                            kr, kr[:, swap], cols[7]], axis=1).astype(_BF16)
    wuq = p["b_w_uq"][l].reshape(Q_LORA, B_HEADS, B_QK_DIM)
    w_uq = jnp.concatenate([wuq, wuq[:, :, B_NOPE:][:, :, swap]], axis=2).reshape(Q_LORA, B_HEADS * 256)
    gq = p["b_q_norm"][l]
    gk = p["b_k_norm"][l]
    row = lambda v: v.reshape(1, -1).astype(_F32)
    bound_a, bound_b = _score_bounds(l, p)
    aug_q, aug_k, aug_b = _aug_rows(bound_a, bound_b)
    return {
        "norm_w": row(p["norm_w"][l]),
        "w_in": w_in,
        "g_aq": _pad_lanes(row(p["a_q_norm"][l]), LANES),
        "g_ak": _pad_lanes(row(p["a_k_norm"][l]), LANES),
        "g_cq": row(p["b_cq_norm"][l]),
        "w_uq": w_uq.astype(_BF16),
        "g_ckv": row(p["b_ckv_norm"][l]),
        "w_ukv": p["b_w_ukv"][l].astype(_BF16),
        "g_bq": row(jnp.concatenate([gq, gq[B_NOPE:][swap]])),
        "g_bkn": row(gk[:B_NOPE]),
        "g_bkr": row(jnp.concatenate([gk[B_NOPE:], gk[B_NOPE:][swap]])),
        "rope_c": rope_c,
        "rope_s": rope_s,
        "aug_q": aug_q, "aug_k": aug_k, "aug_b": aug_b,
        "lq1": row(p["a_lq1"][l]), "lk1": row(p["a_lk1"][l]),
        "lq2": row(p["a_lq2"][l]), "lk2": row(p["a_lk2"][l]),
        "subln": row(p["a_subln"][l]),
        "w_out": p["w_out"][l].astype(_BF16),
    }


def _tile(n, pref):
    return pref if n % pref == 0 else n


def _largest_group(n, cap):
    return max(g for g in range(1, cap + 1) if n % g == 0) if n > 0 else 1


def _encoder_layer(x, lw, lam_init, slopes, fixed_shift):
    b, s, _ = x.shape
    t = b * s
    x2d = x.reshape(t, D_MODEL)
    tm = _tile(s, FIXED_TK)
    qa, ka, va, ga, qb, kb, vb, gb = _proj_call(x2d, s, lw, tm, fixed_shift)
    split = lambda a: a.reshape(a.shape[:-2] + (b, s, a.shape[-1]))
    tq = tk = _tile(s, 512)
    if fixed_shift:
        assert s <= POS_SPLIT * POS_SPLIT * 2
        ftq = _tile(s, FIXED_TQ)
        split_t = lambda a: a.reshape((a.shape[0], b, s // tm) + a.shape[2:])
        oa = _attn_call(qa, split(ka), split_t(va), alibi=True, units_per_v=2, tq=ftq)
        ob = _attn_call(qb, split(kb), split_t(vb), alibi=False, units_per_v=1, tq=ftq)
    else:
        oa = _flash_call(split(qa), split(ka), split(va), slopes, alibi=True, units_per_v=2, tq=tq, tk=tk)
        ob = _flash_call(split(qb), split(kb), split(vb), slopes, alibi=False, units_per_v=1, tq=tq, tk=tk)
    y = _out_call(oa.reshape(A_UNITS, t, LANES), ob.reshape(B_HEADS, t, LANES), ga, gb, x2d, lw, lam_init, tm)
    return y.reshape(b, s, D_MODEL)


def _forward(x_prompt, x_sample, p, fixed_shift):
    depth = p["norm_w"].shape[0]
    rope_c, rope_s = {}, {}
    for s in {x_prompt.shape[1], x_sample.shape[1]}:
        rope_c[s], rope_s[s] = _rope_tables(s)
    slopes = jnp.asarray([_alibi_slope(hd) for hd in range(A_HEADS)], _F32)
    y_prompt, y_sample = x_prompt, x_sample
    for l in range(depth):
        lw = _layer_weights(l, p, rope_c, rope_s)
        lam_init = _lambda_init(l)
        y_prompt = _encoder_layer(y_prompt, lw, lam_init, slopes, fixed_shift)
        y_sample = _encoder_layer(y_sample, lw, lam_init, slopes, fixed_shift)
    return (y_prompt, y_sample)


def kernel(x_prompt, x_sample, norm_w, w_in, a_q_norm, a_k_norm, a_lq1, a_lk1, a_lq2, a_lk2, a_subln,
           b_cq_norm, b_w_uq, b_ckv_norm, b_w_ukv, b_q_norm, b_k_norm, w_out):
    p = dict(norm_w=norm_w, w_in=w_in, a_q_norm=a_q_norm, a_k_norm=a_k_norm, a_lq1=a_lq1, a_lk1=a_lk1,
             a_lq2=a_lq2, a_lk2=a_lk2, a_subln=a_subln, b_cq_norm=b_cq_norm, b_w_uq=b_w_uq,
             b_ckv_norm=b_ckv_norm, b_w_ukv=b_w_ukv, b_q_norm=b_q_norm, b_k_norm=b_k_norm, w_out=w_out)
    bounds = jnp.stack([jnp.stack(_score_bounds(l, p)) for l in range(norm_w.shape[0])])
    shift_ok = jnp.max(bounds) <= MAX_STATIC_SHIFT
    return lax.cond(shift_ok,
                    lambda xp, xs, pp: _forward(xp, xs, pp, True),
                    lambda xp, xs, pp: _forward(xp, xs, pp, False),
                    x_prompt, x_sample, p)
```

```python
import functools
import math

import numpy as np
import jax
import jax.numpy as jnp
from jax import lax
from jax.experimental import pallas as pl
from jax.experimental.pallas import tpu as pltpu

D_MODEL = 1024
A_HEADS = 4
A_HEAD_DIM = 64
A_V_DIM = 128
A_UNITS = 2 * A_HEADS
B_HEADS = 4
B_NOPE = 128
B_ROPE = 64
B_QK_DIM = B_NOPE + B_ROPE
B_V_DIM = 128
Q_LORA = 256
KV_LORA = 128
ROPE_THETA = 10000.0
EPS = 1e-6
LANES = 128

_SPLITS = np.cumsum([0, 512, 512, 512, 512, Q_LORA, KV_LORA, B_ROPE, 512])
_P_AQ, _P_AK, _P_AV, _P_AG, _P_CQ, _P_CKV, _P_KR, _P_BG, _P_END = np.cumsum(
    [0, A_UNITS * LANES, A_UNITS * LANES, 512, 512, Q_LORA, KV_LORA, LANES, 512])

_VMEM_LIMIT = 56 * 1024 * 1024
LOG2E = math.log2(math.e)
MAX_STATIC_SHIFT = 32.0
POS_SPLIT = 128
SIGMA_PIECES = 3
FIXED_TQ = 1024
FIXED_TK = 512
MAX_BLOCK_PAIRS = 32
VT_ROWS = 144

_A_SHIFT = A_HEAD_DIM
_A_QHI = _A_SHIFT + 1
_A_QLO = _A_QHI + SIGMA_PIECES
_A_KHI = _A_QLO + SIGMA_PIECES
_A_KLO = _A_KHI + SIGMA_PIECES
_A_AUG_END = _A_KLO + SIGMA_PIECES
_B_SHIFT = B_QK_DIM - LANES

_F32 = jnp.float32
_BF16 = jnp.bfloat16


def _lambda_init(layer_idx):
    return 0.8 - 0.6 * math.exp(-0.3 * layer_idx)


def _alibi_slope(head):
    return 2.0 ** (-8.0 * (head + 1) / A_HEADS)


def _sigma_pieces(head):
    rest = _alibi_slope(head) * LOG2E
    pieces = []
    for _ in range(SIGMA_PIECES):
        p = float(np.asarray(rest, np.float32).astype(_BF16).astype(np.float32))
        pieces.append(p)
        rest -= p
    return pieces


def _group_sum_matrices():
    m = np.zeros((3, 256, 256), np.float32)
    m[0, :LANES, :LANES] = 1.0
    m[0, LANES:, LANES:] = 1.0
    m[1, :B_QK_DIM, :] = 1.0
    m[2, :B_NOPE, :] = 1.0
    return jnp.asarray(m, _BF16)


def _rope_swap_index():
    half = B_ROPE // 2
    return np.concatenate([np.arange(half, B_ROPE), np.arange(0, half)])


def _silu(x):
    return x / (1.0 + jnp.exp(-x))


def _proj_kernel(x_ref, nw_ref, win_ref, gaq_ref, gak_ref, gcq_ref, wuq_ref, gckv_ref, wukv_ref,
                 gbq_ref, gbkn_ref, gbkr_ref, ct_ref, st_ref, augq_ref, augk_ref, augb_ref, sum_ref,
                 qa_ref, ka_ref, va_ref, ga_ref, qb_ref, kb_ref, vb_ref, gb_ref, *, fixed_shift, seq_len):
    tm = x_ref.shape[0]
    rows = slice(0, tm)
    x = x_ref[rows, :]
    h = x * lax.rsqrt(jnp.mean(x * x, axis=-1, keepdims=True) + EPS) * nw_ref[...]
    hb = h.astype(_BF16)

    def proj(lo, hi):
        return jnp.dot(hb, win_ref[:, lo:hi], preferred_element_type=_F32)

    lane = lax.broadcasted_iota(jnp.int32, (tm, LANES), 1)
    low_half = lane < B_ROPE
    ones_col = (lane == 0).astype(_F32)
    ct = ct_ref[rows, :]
    st = st_ref[rows, :]

    if fixed_shift:
        a_qscale = (A_HEAD_DIM ** -0.5) * LOG2E
        b_qscale = (B_QK_DIM ** -0.5) * LOG2E
        pos = (pl.program_id(0) % (seq_len // tm)) * tm + lax.broadcasted_iota(jnp.int32, (tm, LANES), 0)
        pos = pos - seq_len // 2
        pos_hi = ((pos >> 7) * POS_SPLIT).astype(_F32)
        pos_lo = (pos & (POS_SPLIT - 1)).astype(_F32)
        q_hi_lanes = (lane >= _A_QHI) & (lane < _A_QLO)
        q_lo_lanes = (lane >= _A_QLO) & (lane < _A_KHI)
        k_hi_lanes = (lane >= _A_KHI) & (lane < _A_KLO)
        k_lo_lanes = (lane >= _A_KLO) & (lane < _A_AUG_END)
        q_pos = jnp.where(q_hi_lanes, pos_hi, jnp.where(q_lo_lanes, pos_lo, 0.0))
        k_pos = jnp.where(k_hi_lanes, pos_hi, jnp.where(k_lo_lanes, pos_lo, 0.0))
        sub = lax.broadcasted_iota(jnp.int32, (LANES, tm), 0)
        bias_rows = (sub >= _A_QHI) & (sub < _A_AUG_END)
    else:
        a_qscale = A_HEAD_DIM ** -0.5
        b_qscale = B_QK_DIM ** -0.5

    aq = proj(_P_AQ, _P_AK)
    ak = proj(_P_AK, _P_AV)

    def group_sums(a, which):
        return jnp.dot((a * a).astype(_BF16), sum_ref[which], preferred_element_type=_F32)

    for u in range(A_UNITS):
        if u % 2 == 0:
            pair = slice(u * LANES, (u + 2) * LANES)
            rq2 = lax.rsqrt(group_sums(aq[:, pair], 0) * (1.0 / A_HEAD_DIM) + EPS)
            rk2 = lax.rsqrt(group_sums(ak[:, pair], 0) * (1.0 / A_HEAD_DIM) + EPS)
        half = slice((u % 2) * LANES, (u % 2 + 1) * LANES)
        xq = aq[:, u * LANES:(u + 1) * LANES]
        qn = xq * rq2[:, half] * gaq_ref[...] * a_qscale
        xk = ak[:, u * LANES:(u + 1) * LANES]
        kn = xk * rk2[:, half] * gak_ref[...]
        if fixed_shift:
            q_keys_before = (qn + augq_ref[u] + q_pos).T
            qa_ref[u, 0, :, rows] = q_keys_before.astype(_BF16)
            qa_ref[u, 1, :, rows] = jnp.where(bias_rows, -q_keys_before, q_keys_before).astype(_BF16)
            ka_ref[u, rows, :] = (kn + augk_ref[u] + k_pos).astype(_BF16)
        else:
            qa_ref[u, 0, rows, :] = qn.astype(_BF16)
            ka_ref[u, rows, :] = kn.astype(_BF16)

    def store_values(v_ref, hd, vv):
        if fixed_shift:
            pad_rows = lax.broadcasted_iota(jnp.int32, (VT_ROWS - LANES, tm), 0)
            v_ref[hd, 0, :, rows] = jnp.concatenate([vv.T, (pad_rows == 0).astype(_F32)], axis=0).astype(_BF16)
        else:
            v_ref[hd, rows, :] = jnp.concatenate([vv, ones_col], axis=1).astype(_BF16)

    av = proj(_P_AV, _P_AG)
    for hd in range(A_HEADS):
        store_values(va_ref, hd, av[:, hd * LANES:(hd + 1) * LANES])
    ga_ref[rows, :] = _silu(proj(_P_AG, _P_CQ)).astype(_BF16)

    cq = proj(_P_CQ, _P_CKV)
    cqn = cq * lax.rsqrt(jnp.mean(cq * cq, axis=-1, keepdims=True) + EPS) * gcq_ref[...]
    qall = jnp.dot(cqn.astype(_BF16), wuq_ref[...], preferred_element_type=_F32)
    ckv = proj(_P_CKV, _P_KR)
    ckvn = ckv * lax.rsqrt(jnp.mean(ckv * ckv, axis=-1, keepdims=True) + EPS) * gckv_ref[...]
    kv = jnp.dot(ckvn.astype(_BF16), wukv_ref[...], preferred_element_type=_F32)
    kr2 = proj(_P_KR, _P_BG)

    for hd in range(B_HEADS):
        q0 = qall[:, hd * 256:hd * 256 + LANES]
        q1 = qall[:, hd * 256 + LANES:(hd + 1) * 256]
        r = lax.rsqrt(group_sums(qall[:, hd * 256:(hd + 1) * 256], 1) * (1.0 / B_QK_DIM) + EPS)
        q0n = q0 * r[:, :LANES] * gbq_ref[:, :LANES] * b_qscale
        q1n = q1 * r[:, LANES:] * gbq_ref[:, LANES:]
        q1r = (q1n * ct + pltpu.roll(q1n, B_ROPE, 1) * st) * b_qscale
        if fixed_shift:
            q1r = q1r + augb_ref[0:1, :]
            qb_ref[hd, 0, :, rows] = jnp.concatenate([q0n.T, q1r.T], axis=0).astype(_BF16)
        else:
            qb_ref[hd, 0, rows, :] = jnp.concatenate([q0n, q1r], axis=1).astype(_BF16)

    kr_ss = jnp.sum(jnp.where(low_half, kr2 * kr2, 0.0), axis=-1, keepdims=True)
    krg = kr2 * gbkr_ref[...]
    krr = krg * ct + pltpu.roll(krg, B_ROPE, 1) * st
    for hd in range(B_HEADS):
        kn = kv[:, hd * 256:hd * 256 + LANES]
        vv = kv[:, hd * 256 + LANES:(hd + 1) * 256]
        kn_ss = group_sums(kv[:, hd * 256:(hd + 1) * 256], 2)[:, :LANES]
        r = lax.rsqrt((kn_ss + kr_ss) * (1.0 / B_QK_DIM) + EPS)
        k1 = krr * r
        if fixed_shift:
            k1 = k1 + augb_ref[1:2, :]
        kb_ref[hd, rows, :] = jnp.concatenate([kn * r * gbkn_ref[...], k1], axis=1).astype(_BF16)
        store_values(vb_ref, hd, vv)
    gb_ref[rows, :] = _silu(proj(_P_BG, _P_END)).astype(_BF16)


def _proj_call(x2d, seq_len, lw, tm, fixed_shift):
    t = x2d.shape[0]
    blocks_per_seq = seq_len // tm
    n_var = 2 if fixed_shift else 1
    const = lambda i: (0, 0)
    const3 = lambda i: (0, 0, 0)
    row = lambda i: (i, 0)
    unit_row = lambda i: (0, i, 0)
    var_row = lambda i: (0, 0, i, 0)
    rope_row = lambda i: (i % blocks_per_seq, 0)
    in_specs = [
        pl.BlockSpec((tm, D_MODEL), row),
        pl.BlockSpec((1, D_MODEL), const),
        pl.BlockSpec((D_MODEL, int(_P_END)), const),
        pl.BlockSpec((1, LANES), const),
        pl.BlockSpec((1, LANES), const),
        pl.BlockSpec((1, Q_LORA), const),
        pl.BlockSpec((Q_LORA, B_HEADS * 256), const),
        pl.BlockSpec((1, KV_LORA), const),
        pl.BlockSpec((KV_LORA, B_HEADS * 256), const),
        pl.BlockSpec((1, 256), const),
        pl.BlockSpec((1, LANES), const),
        pl.BlockSpec((1, LANES), const),
        pl.BlockSpec((tm, LANES), rope_row),
        pl.BlockSpec((tm, LANES), rope_row),
        pl.BlockSpec((A_UNITS, 1, LANES), const3),
        pl.BlockSpec((A_UNITS, 1, LANES), const3),
        pl.BlockSpec((2, LANES), const),
        pl.BlockSpec((3, 256, 256), const3),
    ]
    if fixed_shift:
        q_shape = lambda units, dk: jax.ShapeDtypeStruct((units, n_var if dk == LANES else 1, dk, t), _BF16)
        q_spec = lambda units, dk: pl.BlockSpec((units, n_var if dk == LANES else 1, dk, tm),
                                                lambda i: (0, 0, 0, i))
        v_shape = lambda heads: jax.ShapeDtypeStruct((heads, t // tm, VT_ROWS, tm), _BF16)
        v_spec = lambda heads: pl.BlockSpec((heads, 1, VT_ROWS, tm), lambda i: (0, i, 0, 0))
    else:
        q_shape = lambda units, dk: jax.ShapeDtypeStruct((units, 1, t, dk), _BF16)
        q_spec = lambda units, dk: pl.BlockSpec((units, 1, tm, dk), var_row)
        v_shape = lambda heads: jax.ShapeDtypeStruct((heads, t, 256), _BF16)
        v_spec = lambda heads: pl.BlockSpec((heads, tm, 256), unit_row)
    out_shape = [
        q_shape(A_UNITS, LANES),
        jax.ShapeDtypeStruct((A_UNITS, t, LANES), _BF16),
        v_shape(A_HEADS),
        jax.ShapeDtypeStruct((t, 512), _BF16),
        q_shape(B_HEADS, 256),
        jax.ShapeDtypeStruct((B_HEADS, t, 256), _BF16),
        v_shape(B_HEADS),
        jax.ShapeDtypeStruct((t, 512), _BF16),
    ]
    out_specs = [
        q_spec(A_UNITS, LANES),
        pl.BlockSpec((A_UNITS, tm, LANES), unit_row),
        v_spec(A_HEADS),
        pl.BlockSpec((tm, 512), row),
        q_spec(B_HEADS, 256),
        pl.BlockSpec((B_HEADS, tm, 256), unit_row),
        v_spec(B_HEADS),
        pl.BlockSpec((tm, 512), row),
    ]
    return pl.pallas_call(
        functools.partial(_proj_kernel, fixed_shift=fixed_shift, seq_len=seq_len),
        grid=(t // tm,),
        in_specs=in_specs,
        out_specs=out_specs,
        out_shape=out_shape,
        compiler_params=pltpu.CompilerParams(
            dimension_semantics=("parallel",), vmem_limit_bytes=_VMEM_LIMIT),
        name="proj",
    )(x2d, lw["norm_w"], lw["w_in"], lw["g_aq"], lw["g_ak"], lw["g_cq"], lw["w_uq"], lw["g_ckv"],
      lw["w_ukv"], lw["g_bq"], lw["g_bkn"], lw["g_bkr"], lw["rope_c"][seq_len], lw["rope_s"][seq_len],
      lw["aug_q"], lw["aug_k"], lw["aug_b"], _group_sum_matrices())


def _flash_kernel(slope_ref, q_ref, k_ref, v_ref, o_ref, m_sc, acc_sc, *, alibi, heads_per_slope):
    kj = pl.program_id(3)
    tq = q_ref.shape[3]
    tk = k_ref.shape[2]

    @pl.when(kj == 0)
    def _():
        m_sc[...] = jnp.full(m_sc.shape, -jnp.inf, _F32)
        acc_sc[...] = jnp.zeros(acc_sc.shape, _F32)

    s = lax.dot_general(q_ref[0, 0, 0], k_ref[0, 0], (((1,), (1,)), ((), ())),
                        preferred_element_type=_F32)
    if alibi:
        slope = slope_ref[pl.program_id(0) // heads_per_slope]
        qpos = pl.program_id(2) * tq + lax.broadcasted_iota(jnp.int32, (tq, tk), 0)
        kpos = kj * tk + lax.broadcasted_iota(jnp.int32, (tq, tk), 1)
        s = s - slope * jnp.abs(qpos - kpos).astype(_F32)
    m_prev = m_sc[...]
    m_new = jnp.maximum(m_prev, jnp.max(s, axis=-1, keepdims=True))
    alpha = jnp.exp(m_prev - m_new)
    p = jnp.exp(s - m_new)
    acc_sc[...] = alpha * acc_sc[...] + jnp.dot(p.astype(_BF16), v_ref[0, 0],
                                                 preferred_element_type=_F32)
    m_sc[...] = m_new

    @pl.when(kj == pl.num_programs(3) - 1)
    def _():
        acc = acc_sc[...]
        o_ref[0, 0] = acc[:, :LANES] / acc[:, LANES:LANES + 1]


def _flash_call(q, k, v, slopes, *, alibi, units_per_v, tq, tk):
    u, _, b, s, dk = q.shape
    kern = functools.partial(_flash_kernel, alibi=alibi, heads_per_slope=units_per_v)
    grid_spec = pltpu.PrefetchScalarGridSpec(
        num_scalar_prefetch=1,
        grid=(u, b, s // tq, s // tk),
        in_specs=[
            pl.BlockSpec((1, 1, 1, tq, dk), lambda ui, bi, qi, ki, sl: (ui, 0, bi, qi, 0)),
            pl.BlockSpec((1, 1, tk, dk), lambda ui, bi, qi, ki, sl: (ui, bi, ki, 0)),
            pl.BlockSpec((1, 1, tk, 256), lambda ui, bi, qi, ki, sl: (ui // units_per_v, bi, ki, 0)),
        ],
        out_specs=pl.BlockSpec((1, 1, tq, LANES), lambda ui, bi, qi, ki, sl: (ui, bi, qi, 0)),
        scratch_shapes=[pltpu.VMEM((tq, 1), _F32), pltpu.VMEM((tq, 256), _F32)],
    )
    return pl.pallas_call(
        kern,
        grid_spec=grid_spec,
        out_shape=jax.ShapeDtypeStruct((u, b, s, LANES), _F32),
        compiler_params=pltpu.CompilerParams(
            dimension_semantics=("parallel", "parallel", "parallel", "arbitrary"),
            vmem_limit_bytes=_VMEM_LIMIT),
        name="flash_a" if alibi else "flash_b",
    )(slopes, q, k, v)


def _attn_kernel(qt_ref, k_ref, vt_ref, o_ref, *, tq, q_steps, alibi):
    n_k, _, tk = vt_ref.shape[2:]
    q_sub = qt_ref.shape[3] // tq
    n_diag = tq // tk
    if alibi:
        head = pl.program_id(0) // 2
        sigma = jnp.float32(_alibi_slope(A_HEADS - 1) * LOG2E)
        for hd in range(A_HEADS - 1):
            sigma = jnp.where(head == hd, jnp.float32(_alibi_slope(hd) * LOG2E), sigma)
        ahead = (lax.broadcasted_iota(jnp.int32, (tk, tq), 0) - lax.broadcasted_iota(jnp.int32, (tk, tq), 1))

    for sub in range(q_sub):
        qi = sub if q_steps == 1 else pl.program_id(2) * q_sub + sub
        queries = slice(sub * tq, (sub + 1) * tq)

        def block(var, j, correction=None):
            k = k_ref[0, 0, pl.ds(pl.multiple_of(j * tk, tk), tk), :]
            st = jnp.dot(k, qt_ref[0, var, :, queries], preferred_element_type=_F32)
            if correction is not None:
                st = st + correction
            return jnp.dot(vt_ref[0, 0, j], jnp.exp2(st).astype(_BF16), preferred_element_type=_F32)

        def accumulate(acc, part):
            return part if acc is None else acc + part

        acc = None
        if alibi:
            first_diag = qi * n_diag
            for d in range(n_diag):
                corr = (-2.0 * sigma) * jnp.maximum(ahead + d * tk, 0).astype(_F32)
                acc = accumulate(acc, block(0, first_diag + d, corr))
            for x in range(n_k - n_diag):
                after = (x >= first_diag) * 1
                acc = accumulate(acc, block(after, x + n_diag * after))
        else:
            for x in range(n_k):
                acc = accumulate(acc, block(0, x))
        o_ref[0, 0, queries, :] = (acc[:LANES] / acc[LANES:LANES + 1]).T


def _attn_call(qt, k, vt, *, alibi, units_per_v, tq):
    u, n_var, dk, _ = qt.shape
    _, b, s, _ = k.shape
    n_k, _, tk = vt.shape[2:]
    assert tq % tk == 0 and s % tq == 0, (s, tq, tk)
    q_sub = _largest_group(s // tq, max(1, MAX_BLOCK_PAIRS // n_k))
    q_steps = s // (tq * q_sub)
    kern = functools.partial(_attn_kernel, tq=tq, q_steps=q_steps, alibi=alibi)
    return pl.pallas_call(
        kern,
        grid=(u, b, q_steps),
        in_specs=[
            pl.BlockSpec((1, n_var, dk, tq * q_sub), lambda ui, bi, qi: (ui, 0, 0, bi * q_steps + qi)),
            pl.BlockSpec((1, 1, s, dk), lambda ui, bi, qi: (ui, bi, 0, 0)),
            pl.BlockSpec((1, 1, n_k, VT_ROWS, tk), lambda ui, bi, qi: (ui // units_per_v, bi, 0, 0, 0)),
        ],
        out_specs=pl.BlockSpec((1, 1, tq * q_sub, LANES), lambda ui, bi, qi: (ui, bi, qi, 0)),
        out_shape=jax.ShapeDtypeStruct((u, b, s, LANES), _F32),
        compiler_params=pltpu.CompilerParams(
            dimension_semantics=("parallel", "parallel", "parallel"),
            vmem_limit_bytes=_VMEM_LIMIT),
        name="attn_a" if alibi else "attn_b",
    )(qt, k, vt)


def _out_kernel(oa_ref, ob_ref, ga_ref, gb_ref, x_ref, lq1_ref, lk1_ref, lq2_ref, lk2_ref, subln_ref,
                wout_ref, y_ref, *, lam_init):
    lam = (jnp.exp(jnp.sum(lq1_ref[...] * lk1_ref[...], axis=-1, keepdims=True))
           - jnp.exp(jnp.sum(lq2_ref[...] * lk2_ref[...], axis=-1, keepdims=True)) + lam_init)
    ga = ga_ref[...].astype(_F32)
    gb = gb_ref[...].astype(_F32)
    pieces = []
    for hd in range(A_HEADS):
        d = oa_ref[2 * hd] - lam * oa_ref[2 * hd + 1]
        n = d * lax.rsqrt(jnp.mean(d * d, axis=-1, keepdims=True) + EPS) * subln_ref[...]
        pieces.append(n * (1.0 - lam_init) * ga[:, hd * LANES:(hd + 1) * LANES])
    for hd in range(B_HEADS):
        pieces.append(ob_ref[hd] * gb[:, hd * LANES:(hd + 1) * LANES])
    y = jnp.concatenate(pieces, axis=1).astype(_BF16)
    y_ref[...] = x_ref[...] + jnp.dot(y, wout_ref[...], preferred_element_type=_F32)


def _out_call(oa, ob, ga, gb, x2d, lw, lam_init, tm):
    t = x2d.shape[0]
    const = lambda i: (0, 0)
    row = lambda i: (i, 0)
    unit_row = lambda i: (0, i, 0)
    return pl.pallas_call(
        functools.partial(_out_kernel, lam_init=lam_init),
        grid=(t // tm,),
        in_specs=[
            pl.BlockSpec((A_UNITS, tm, LANES), unit_row),
            pl.BlockSpec((B_HEADS, tm, LANES), unit_row),
            pl.BlockSpec((tm, 512), row),
            pl.BlockSpec((tm, 512), row),
            pl.BlockSpec((tm, D_MODEL), row),
            pl.BlockSpec((1, A_HEAD_DIM), const),
            pl.BlockSpec((1, A_HEAD_DIM), const),
            pl.BlockSpec((1, A_HEAD_DIM), const),
            pl.BlockSpec((1, A_HEAD_DIM), const),
            pl.BlockSpec((1, A_V_DIM), const),
            pl.BlockSpec((D_MODEL, D_MODEL), const),
        ],
        out_specs=pl.BlockSpec((tm, D_MODEL), row),
        out_shape=jax.ShapeDtypeStruct((t, D_MODEL), _F32),
        compiler_params=pltpu.CompilerParams(
            dimension_semantics=("parallel",), vmem_limit_bytes=_VMEM_LIMIT),
        name="out",
    )(oa, ob, ga, gb, x2d, lw["lq1"], lw["lk1"], lw["lq2"], lw["lk2"], lw["subln"], lw["w_out"])


def _rope_tables(seq_len):
    inv = ROPE_THETA ** (-jnp.arange(0, B_ROPE, 2, dtype=_F32) / B_ROPE)
    ang = jnp.arange(seq_len, dtype=_F32)[:, None] * inv[None, :]
    cos, sin = jnp.cos(ang), jnp.sin(ang)
    zeros = jnp.zeros((seq_len, LANES - B_ROPE), _F32)
    return (jnp.concatenate([cos, cos, zeros], axis=1),
            jnp.concatenate([-sin, sin, zeros], axis=1))


def _pad_lanes(v, width):
    return jnp.pad(v, ((0, 0), (0, width - v.shape[1])))


def _score_bounds(l, p):
    amax = lambda v: jnp.max(jnp.abs(v[l].astype(_F32)))
    bound_a = amax(p["a_q_norm"]) * amax(p["a_k_norm"]) * (A_HEAD_DIM ** 0.5)
    bound_b = amax(p["b_q_norm"]) * amax(p["b_k_norm"]) * (B_QK_DIM ** 0.5)
    return bound_a, bound_b


def _aug_rows(bound_a, bound_b):
    aug_q = np.zeros((A_UNITS, 1, LANES), np.float32)
    aug_k = np.zeros((A_UNITS, 1, LANES), np.float32)
    for u in range(A_UNITS):
        sig = _sigma_pieces(u // 2)
        aug_k[u, 0, _A_SHIFT] = 1.0
        for a in range(SIGMA_PIECES):
            aug_k[u, 0, _A_QHI + a] = -sig[a]
            aug_k[u, 0, _A_QLO + a] = -sig[a]
            aug_q[u, 0, _A_KHI + a] = sig[a]
            aug_q[u, 0, _A_KLO + a] = sig[a]
    shift_lane = (np.arange(LANES) == _A_SHIFT).astype(np.float32).reshape(1, 1, LANES)
    aug_q = jnp.asarray(aug_q) - (bound_a * LOG2E) * shift_lane
    b_lane = (np.arange(LANES) == _B_SHIFT).astype(np.float32)
    aug_b = jnp.stack([-(bound_b * LOG2E) * b_lane, jnp.asarray(b_lane)])
    return aug_q, jnp.asarray(aug_k), aug_b


def _layer_weights(l, p, rope_c, rope_s):
    swap = _rope_swap_index()
    w = p["w_in"][l]
    cols = [w[:, _SPLITS[i]:_SPLITS[i + 1]] for i in range(8)]
    pad_units = lambda c: jnp.pad(c.reshape(D_MODEL, A_UNITS, A_HEAD_DIM),
                                  ((0, 0), (0, 0), (0, LANES - A_HEAD_DIM))).reshape(D_MODEL, A_UNITS * LANES)
    kr = cols[6]
    w_in = jnp.concatenate([pad_units(cols[0]), pad_units(cols[1]), cols[2], cols[3], cols[4], cols[5],
                            kr, kr[:, swap], cols[7]], axis=1).astype(_BF16)
    wuq = p["b_w_uq"][l].reshape(Q_LORA, B_HEADS, B_QK_DIM)
    w_uq = jnp.concatenate([wuq, wuq[:, :, B_NOPE:][:, :, swap]], axis=2).reshape(Q_LORA, B_HEADS * 256)
    gq = p["b_q_norm"][l]
    gk = p["b_k_norm"][l]
    row = lambda v: v.reshape(1, -1).astype(_F32)
    bound_a, bound_b = _score_bounds(l, p)
    aug_q, aug_k, aug_b = _aug_rows(bound_a, bound_b)
    return {
        "norm_w": row(p["norm_w"][l]),
        "w_in": w_in,
        "g_aq": _pad_lanes(row(p["a_q_norm"][l]), LANES),
        "g_ak": _pad_lanes(row(p["a_k_norm"][l]), LANES),
        "g_cq": row(p["b_cq_norm"][l]),
        "w_uq": w_uq.astype(_BF16),
        "g_ckv": row(p["b_ckv_norm"][l]),
        "w_ukv": p["b_w_ukv"][l].astype(_BF16),
        "g_bq": row(jnp.concatenate([gq, gq[B_NOPE:][swap]])),
        "g_bkn": row(gk[:B_NOPE]),
        "g_bkr": row(jnp.concatenate([gk[B_NOPE:], gk[B_NOPE:][swap]])),
        "rope_c": rope_c,
        "rope_s": rope_s,
        "aug_q": aug_q, "aug_k": aug_k, "aug_b": aug_b,
        "lq1": row(p["a_lq1"][l]), "lk1": row(p["a_lk1"][l]),
        "lq2": row(p["a_lq2"][l]), "lk2": row(p["a_lk2"][l]),
        "subln": row(p["a_subln"][l]),
        "w_out": p["w_out"][l].astype(_BF16),
    }


def _tile(n, pref):
    return pref if n % pref == 0 else n


def _largest_group(n, cap):
    return max(g for g in range(1, cap + 1) if n % g == 0) if n > 0 else 1


def _encoder_layer(x, lw, lam_init, slopes, fixed_shift):
    b, s, _ = x.shape
    t = b * s
    x2d = x.reshape(t, D_MODEL)
    tm = _tile(s, FIXED_TK)
    qa, ka, va, ga, qb, kb, vb, gb = _proj_call(x2d, s, lw, tm, fixed_shift)
    split = lambda a: a.reshape(a.shape[:-2] + (b, s, a.shape[-1]))
    tq = tk = _tile(s, 512)
    if fixed_shift:
        assert s <= POS_SPLIT * POS_SPLIT * 2
        ftq = _tile(s, FIXED_TQ)
        split_t = lambda a: a.reshape((a.shape[0], b, s // tm) + a.shape[2:])
        oa = _attn_call(qa, split(ka), split_t(va), alibi=True, units_per_v=2, tq=ftq)
        ob = _attn_call(qb, split(kb), split_t(vb), alibi=False, units_per_v=1, tq=ftq)
    else:
        oa = _flash_call(split(qa), split(ka), split(va), slopes, alibi=True, units_per_v=2, tq=tq, tk=tk)
        ob = _flash_call(split(qb), split(kb), split(vb), slopes, alibi=False, units_per_v=1, tq=tq, tk=tk)
    y = _out_call(oa.reshape(A_UNITS, t, LANES), ob.reshape(B_HEADS, t, LANES), ga, gb, x2d, lw, lam_init, tm)
    return y.reshape(b, s, D_MODEL)


def _forward(x_prompt, x_sample, p, fixed_shift):
    depth = p["norm_w"].shape[0]
    rope_c, rope_s = {}, {}
    for s in {x_prompt.shape[1], x_sample.shape[1]}:
        rope_c[s], rope_s[s] = _rope_tables(s)
    slopes = jnp.asarray([_alibi_slope(hd) for hd in range(A_HEADS)], _F32)
    y_prompt, y_sample = x_prompt, x_sample
    for l in range(depth):
        lw = _layer_weights(l, p, rope_c, rope_s)
        lam_init = _lambda_init(l)
        y_prompt = _encoder_layer(y_prompt, lw, lam_init, slopes, fixed_shift)
        y_sample = _encoder_layer(y_sample, lw, lam_init, slopes, fixed_shift)
    return (y_prompt, y_sample)


def kernel(x_prompt, x_sample, norm_w, w_in, a_q_norm, a_k_norm, a_lq1, a_lk1, a_lq2, a_lk2, a_subln,
           b_cq_norm, b_w_uq, b_ckv_norm, b_w_ukv, b_q_norm, b_k_norm, w_out):
    p = dict(norm_w=norm_w, w_in=w_in, a_q_norm=a_q_norm, a_k_norm=a_k_norm, a_lq1=a_lq1, a_lk1=a_lk1,
             a_lq2=a_lq2, a_lk2=a_lk2, a_subln=a_subln, b_cq_norm=b_cq_norm, b_w_uq=b_w_uq,
             b_ckv_norm=b_ckv_norm, b_w_ukv=b_w_ukv, b_q_norm=b_q_norm, b_k_norm=b_k_norm, w_out=w_out)
    bounds = jnp.stack([jnp.stack(_score_bounds(l, p)) for l in range(norm_w.shape[0])])
    shift_ok = jnp.max(bounds) <= MAX_STATIC_SHIFT
    return lax.cond(shift_ok,
                    lambda xp, xs, pp: _forward(xp, xs, pp, True),
                    lambda xp, xs, pp: _forward(xp, xs, pp, False),
                    x_prompt, x_sample, p)
```

```python
import functools
import math

import numpy as np
import jax
import jax.numpy as jnp
from jax import lax
from jax.experimental import pallas as pl
from jax.experimental.pallas import tpu as pltpu

D_MODEL = 1024
A_HEADS = 4
A_HEAD_DIM = 64
A_V_DIM = 128
A_UNITS = 2 * A_HEADS
B_HEADS = 4
B_NOPE = 128
B_ROPE = 64
B_QK_DIM = B_NOPE + B_ROPE
B_V_DIM = 128
Q_LORA = 256
KV_LORA = 128
ROPE_THETA = 10000.0
EPS = 1e-6
LANES = 128

_SPLITS = np.cumsum([0, 512, 512, 512, 512, Q_LORA, KV_LORA, B_ROPE, 512])
_P_AQ, _P_AK, _P_AV, _P_AG, _P_CQ, _P_CKV, _P_KR, _P_BG, _P_END = np.cumsum(
    [0, 512, 512, 512, 512, Q_LORA, KV_LORA, 2 * B_ROPE, 512])

_VMEM_LIMIT = 56 * 1024 * 1024
LOG2E = math.log2(math.e)
MAX_STATIC_SHIFT = 32.0
POS_SPLIT = 128
SIGMA_PIECES = 3
FIXED_TQ = 1024
FIXED_TK = 512
MAX_BLOCK_PAIRS = 32
VT_ROWS = 144

_AUG_SHIFT = 0
_AUG_QHI = _AUG_SHIFT + 1
_AUG_QLO = _AUG_QHI + SIGMA_PIECES
_AUG_KHI = _AUG_QLO + SIGMA_PIECES
_AUG_KLO = _AUG_KHI + SIGMA_PIECES
_AUG_END = _AUG_KLO + SIGMA_PIECES
_B_SHIFT = B_QK_DIM - LANES


def _aug_base(comp):
    return A_HEAD_DIM if comp == 0 else 0

_F32 = jnp.float32
_BF16 = jnp.bfloat16


def _lambda_init(layer_idx):
    return 0.8 - 0.6 * math.exp(-0.3 * layer_idx)


def _alibi_slope(head):
    return 2.0 ** (-8.0 * (head + 1) / A_HEADS)


def _sigma_pieces(head):
    rest = _alibi_slope(head) * LOG2E
    pieces = []
    for _ in range(SIGMA_PIECES):
        p = float(np.asarray(rest, np.float32).astype(_BF16).astype(np.float32))
        pieces.append(p)
        rest -= p
    return pieces


def _group_sum_matrices():
    unit = np.arange(A_UNITS * A_HEAD_DIM) // A_HEAD_DIM
    m_a = (unit[:, None] == unit[None, :]).astype(np.float32)
    m_b = np.zeros((2, 256, 256), np.float32)
    m_b[0, :B_QK_DIM, :] = 1.0
    m_b[1, :B_NOPE, :] = 1.0
    return jnp.asarray(m_a, _BF16), jnp.asarray(m_b, _BF16)


def _rope_swap_index():
    half = B_ROPE // 2
    return np.concatenate([np.arange(half, B_ROPE), np.arange(0, half)])


def _silu(x):
    return x / (1.0 + jnp.exp(-x))


def _proj_kernel(x_ref, nw_ref, win_ref, gaq_ref, gak_ref, gcq_ref, wuq_ref, gckv_ref, wukv_ref,
                 gbq_ref, gbkn_ref, gbkr_ref, ct_ref, st_ref, augq_ref, augk_ref, augb_ref, suma_ref, sumb_ref,
                 qa_ref, ka_ref, va_ref, ga_ref, qb_ref, kb_ref, vb_ref, gb_ref, *, fixed_shift, seq_len):
    tm = x_ref.shape[0]
    rows = slice(0, tm)
    x = x_ref[rows, :]
    h = x * lax.rsqrt(jnp.mean(x * x, axis=-1, keepdims=True) + EPS) * nw_ref[...]
    hb = h.astype(_BF16)

    def proj(lo, hi):
        return jnp.dot(hb, win_ref[:, lo:hi], preferred_element_type=_F32)

    lane = lax.broadcasted_iota(jnp.int32, (tm, LANES), 1)
    low_half = lane < B_ROPE
    ones_col = (lane == 0).astype(_F32)
    ct = ct_ref[rows, :]
    st = st_ref[rows, :]

    def aug_select(index, base, hi, lo):
        at = lambda off: (index >= base + off) & (index < base + off + SIGMA_PIECES)
        return at(hi), at(lo)

    if fixed_shift:
        a_qscale = (A_HEAD_DIM ** -0.5) * LOG2E
        b_qscale = (B_QK_DIM ** -0.5) * LOG2E
        first = (pl.program_id(0) % (seq_len // tm)) * tm - seq_len // 2
        split_pos = lambda p: (((p >> 7) * POS_SPLIT).astype(_F32), (p & (POS_SPLIT - 1)).astype(_F32))
        k_hi, k_lo = split_pos(first + lax.broadcasted_iota(jnp.int32, (tm, LANES), 0))
        q_hi, q_lo = split_pos(first + lax.broadcasted_iota(jnp.int32, (A_HEAD_DIM, tm), 1))
        aug_row = lax.broadcasted_iota(jnp.int32, (A_HEAD_DIM, tm), 0)
        at_hi, at_lo = aug_select(aug_row, 0, _AUG_QHI, _AUG_QLO)
        q_pos = jnp.where(at_hi, q_hi, jnp.where(at_lo, q_lo, 0.0))
        bias_rows = (aug_row >= _AUG_QHI) & (aug_row < _AUG_END)
    else:
        a_qscale = A_HEAD_DIM ** -0.5
        b_qscale = B_QK_DIM ** -0.5

    def group_sums(a, m):
        return jnp.dot((a * a).astype(_BF16), m, preferred_element_type=_F32)

    aq = proj(_P_AQ, _P_AK)
    ak = proj(_P_AK, _P_AV)
    qn_all = aq * lax.rsqrt(group_sums(aq, suma_ref[...]) * (1.0 / A_HEAD_DIM) + EPS) * gaq_ref[...] * a_qscale
    kn_all = ak * lax.rsqrt(group_sums(ak, suma_ref[...]) * (1.0 / A_HEAD_DIM) + EPS) * gak_ref[...]
    for hd in range(A_HEADS):
        qn2 = qn_all[:, hd * LANES:(hd + 1) * LANES]
        kn2 = kn_all[:, hd * LANES:(hd + 1) * LANES]
        if fixed_shift:
            qn2_t = qn2.T
        for comp in range(2):
            u = 2 * hd + comp
            own = (lane < A_HEAD_DIM) if comp == 0 else (lane >= A_HEAD_DIM)
            if fixed_shift:
                feat = qn2_t[comp * A_HEAD_DIM:(comp + 1) * A_HEAD_DIM]
                aug = jnp.concatenate([augq_ref[u]] * (tm // LANES), axis=1) + q_pos
                aug_after = jnp.where(bias_rows, -aug, aug)
                stack = (lambda a: [feat, a]) if comp == 0 else (lambda a: [a, feat])
                qa_ref[u, 0, :, rows] = jnp.concatenate(stack(aug), axis=0).astype(_BF16)
                qa_ref[u, 1, :, rows] = jnp.concatenate(stack(aug_after), axis=0).astype(_BF16)
                at_hi, at_lo = aug_select(lane, _aug_base(comp), _AUG_KHI, _AUG_KLO)
                k_pos = jnp.where(at_hi, k_hi, jnp.where(at_lo, k_lo, 0.0))
                ka_ref[u, rows, :] = (jnp.where(own, kn2, 0.0) + augk_ref[u] + k_pos).astype(_BF16)
            else:
                qa_ref[u, 0, rows, :] = jnp.where(own, qn2, 0.0).astype(_BF16)
                ka_ref[u, rows, :] = jnp.where(own, kn2, 0.0).astype(_BF16)

    def store_values(v_ref, hd, vv):
        if fixed_shift:
            pad_rows = lax.broadcasted_iota(jnp.int32, (VT_ROWS - LANES, tm), 0)
            v_ref[hd, 0, :, rows] = jnp.concatenate([vv.T, (pad_rows == 0).astype(_F32)], axis=0).astype(_BF16)
        else:
            v_ref[hd, rows, :] = jnp.concatenate([vv, ones_col], axis=1).astype(_BF16)

    av = proj(_P_AV, _P_AG)
    for hd in range(A_HEADS):
        store_values(va_ref, hd, av[:, hd * LANES:(hd + 1) * LANES])
    ga_ref[rows, :] = _silu(proj(_P_AG, _P_CQ)).astype(_BF16)

    cq = proj(_P_CQ, _P_CKV)
    cqn = cq * lax.rsqrt(jnp.mean(cq * cq, axis=-1, keepdims=True) + EPS) * gcq_ref[...]
    qall = jnp.dot(cqn.astype(_BF16), wuq_ref[...], preferred_element_type=_F32)
    ckv = proj(_P_CKV, _P_KR)
    ckvn = ckv * lax.rsqrt(jnp.mean(ckv * ckv, axis=-1, keepdims=True) + EPS) * gckv_ref[...]
    kv = jnp.dot(ckvn.astype(_BF16), wukv_ref[...], preferred_element_type=_F32)
    kr2 = proj(_P_KR, _P_BG)

    for hd in range(B_HEADS):
        q0 = qall[:, hd * 256:hd * 256 + LANES]
        q1 = qall[:, hd * 256 + LANES:(hd + 1) * 256]
        r = lax.rsqrt(group_sums(qall[:, hd * 256:(hd + 1) * 256], sumb_ref[0]) * (1.0 / B_QK_DIM) + EPS)
        q0n = q0 * r[:, :LANES] * gbq_ref[:, :LANES] * b_qscale
        q1n = q1 * r[:, LANES:] * gbq_ref[:, LANES:]
        q1r = (q1n * ct + pltpu.roll(q1n, B_ROPE, 1) * st) * b_qscale
        if fixed_shift:
            q1r = q1r + augb_ref[0:1, :]
            qb_ref[hd, 0, :, rows] = jnp.concatenate([q0n.T, q1r.T], axis=0).astype(_BF16)
        else:
            qb_ref[hd, 0, rows, :] = jnp.concatenate([q0n, q1r], axis=1).astype(_BF16)

    kr_ss = jnp.sum(jnp.where(low_half, kr2 * kr2, 0.0), axis=-1, keepdims=True)
    krg = kr2 * gbkr_ref[...]
    krr = krg * ct + pltpu.roll(krg, B_ROPE, 1) * st
    for hd in range(B_HEADS):
        kn = kv[:, hd * 256:hd * 256 + LANES]
        vv = kv[:, hd * 256 + LANES:(hd + 1) * 256]
        kn_ss = group_sums(kv[:, hd * 256:(hd + 1) * 256], sumb_ref[1])[:, :LANES]
        r = lax.rsqrt((kn_ss + kr_ss) * (1.0 / B_QK_DIM) + EPS)
        k1 = krr * r
        if fixed_shift:
            k1 = k1 + augb_ref[1:2, :]
        kb_ref[hd, rows, :] = jnp.concatenate([kn * r * gbkn_ref[...], k1], axis=1).astype(_BF16)
        store_values(vb_ref, hd, vv)
    gb_ref[rows, :] = _silu(proj(_P_BG, _P_END)).astype(_BF16)


def _proj_call(x2d, seq_len, lw, tm, fixed_shift):
    t = x2d.shape[0]
    blocks_per_seq = seq_len // tm
    n_var = 2 if fixed_shift else 1
    const = lambda i: (0, 0)
    const3 = lambda i: (0, 0, 0)
    row = lambda i: (i, 0)
    unit_row = lambda i: (0, i, 0)
    var_row = lambda i: (0, 0, i, 0)
    rope_row = lambda i: (i % blocks_per_seq, 0)
    in_specs = [
        pl.BlockSpec((tm, D_MODEL), row),
        pl.BlockSpec((1, D_MODEL), const),
        pl.BlockSpec((D_MODEL, int(_P_END)), const),
        pl.BlockSpec((1, A_UNITS * A_HEAD_DIM), const),
        pl.BlockSpec((1, A_UNITS * A_HEAD_DIM), const),
        pl.BlockSpec((1, Q_LORA), const),
        pl.BlockSpec((Q_LORA, B_HEADS * 256), const),
        pl.BlockSpec((1, KV_LORA), const),
        pl.BlockSpec((KV_LORA, B_HEADS * 256), const),
        pl.BlockSpec((1, 256), const),
        pl.BlockSpec((1, LANES), const),
        pl.BlockSpec((1, LANES), const),
        pl.BlockSpec((tm, LANES), rope_row),
        pl.BlockSpec((tm, LANES), rope_row),
        pl.BlockSpec((A_UNITS, A_HEAD_DIM, LANES), const3),
        pl.BlockSpec((A_UNITS, 1, LANES), const3),
        pl.BlockSpec((2, LANES), const),
        pl.BlockSpec((A_UNITS * A_HEAD_DIM, A_UNITS * A_HEAD_DIM), const),
        pl.BlockSpec((2, 256, 256), const3),
    ]
    if fixed_shift:
        q_shape = lambda units, dk: jax.ShapeDtypeStruct((units, n_var if dk == LANES else 1, dk, t), _BF16)
        q_spec = lambda units, dk: pl.BlockSpec((units, n_var if dk == LANES else 1, dk, tm),
                                                lambda i: (0, 0, 0, i))
        v_shape = lambda heads: jax.ShapeDtypeStruct((heads, t // tm, VT_ROWS, tm), _BF16)
        v_spec = lambda heads: pl.BlockSpec((heads, 1, VT_ROWS, tm), lambda i: (0, i, 0, 0))
    else:
        q_shape = lambda units, dk: jax.ShapeDtypeStruct((units, 1, t, dk), _BF16)
        q_spec = lambda units, dk: pl.BlockSpec((units, 1, tm, dk), var_row)
        v_shape = lambda heads: jax.ShapeDtypeStruct((heads, t, 256), _BF16)
        v_spec = lambda heads: pl.BlockSpec((heads, tm, 256), unit_row)
    out_shape = [
        q_shape(A_UNITS, LANES),
        jax.ShapeDtypeStruct((A_UNITS, t, LANES), _BF16),
        v_shape(A_HEADS),
        jax.ShapeDtypeStruct((t, 512), _BF16),
        q_shape(B_HEADS, 256),
        jax.ShapeDtypeStruct((B_HEADS, t, 256), _BF16),
        v_shape(B_HEADS),
        jax.ShapeDtypeStruct((t, 512), _BF16),
    ]
    out_specs = [
        q_spec(A_UNITS, LANES),
        pl.BlockSpec((A_UNITS, tm, LANES), unit_row),
        v_spec(A_HEADS),
        pl.BlockSpec((tm, 512), row),
        q_spec(B_HEADS, 256),
        pl.BlockSpec((B_HEADS, tm, 256), unit_row),
        v_spec(B_HEADS),
        pl.BlockSpec((tm, 512), row),
    ]
    return pl.pallas_call(
        functools.partial(_proj_kernel, fixed_shift=fixed_shift, seq_len=seq_len),
        grid=(t // tm,),
        in_specs=in_specs,
        out_specs=out_specs,
        out_shape=out_shape,
        compiler_params=pltpu.CompilerParams(
            dimension_semantics=("parallel",), vmem_limit_bytes=_VMEM_LIMIT),
        name="proj",
    )(x2d, lw["norm_w"], lw["w_in"], lw["g_aq"], lw["g_ak"], lw["g_cq"], lw["w_uq"], lw["g_ckv"],
      lw["w_ukv"], lw["g_bq"], lw["g_bkn"], lw["g_bkr"], lw["rope_c"][seq_len], lw["rope_s"][seq_len],
      lw["aug_q"], lw["aug_k"], lw["aug_b"], *_group_sum_matrices())


def _flash_kernel(slope_ref, q_ref, k_ref, v_ref, o_ref, m_sc, acc_sc, *, alibi, heads_per_slope):
    kj = pl.program_id(3)
    tq = q_ref.shape[3]
    tk = k_ref.shape[2]

    @pl.when(kj == 0)
    def _():
        m_sc[...] = jnp.full(m_sc.shape, -jnp.inf, _F32)
        acc_sc[...] = jnp.zeros(acc_sc.shape, _F32)

    s = lax.dot_general(q_ref[0, 0, 0], k_ref[0, 0], (((1,), (1,)), ((), ())),
                        preferred_element_type=_F32)
    if alibi:
        slope = slope_ref[pl.program_id(0) // heads_per_slope]
        qpos = pl.program_id(2) * tq + lax.broadcasted_iota(jnp.int32, (tq, tk), 0)
        kpos = kj * tk + lax.broadcasted_iota(jnp.int32, (tq, tk), 1)
        s = s - slope * jnp.abs(qpos - kpos).astype(_F32)
    m_prev = m_sc[...]
    m_new = jnp.maximum(m_prev, jnp.max(s, axis=-1, keepdims=True))
    alpha = jnp.exp(m_prev - m_new)
    p = jnp.exp(s - m_new)
    acc_sc[...] = alpha * acc_sc[...] + jnp.dot(p.astype(_BF16), v_ref[0, 0],
                                                 preferred_element_type=_F32)
    m_sc[...] = m_new

    @pl.when(kj == pl.num_programs(3) - 1)
    def _():
        acc = acc_sc[...]
        o_ref[0, 0] = acc[:, :LANES] / acc[:, LANES:LANES + 1]


def _flash_call(q, k, v, slopes, *, alibi, units_per_v, tq, tk):
    u, _, b, s, dk = q.shape
    kern = functools.partial(_flash_kernel, alibi=alibi, heads_per_slope=units_per_v)
    grid_spec = pltpu.PrefetchScalarGridSpec(
        num_scalar_prefetch=1,
        grid=(u, b, s // tq, s // tk),
        in_specs=[
            pl.BlockSpec((1, 1, 1, tq, dk), lambda ui, bi, qi, ki, sl: (ui, 0, bi, qi, 0)),
            pl.BlockSpec((1, 1, tk, dk), lambda ui, bi, qi, ki, sl: (ui, bi, ki, 0)),
            pl.BlockSpec((1, 1, tk, 256), lambda ui, bi, qi, ki, sl: (ui // units_per_v, bi, ki, 0)),
        ],
        out_specs=pl.BlockSpec((1, 1, tq, LANES), lambda ui, bi, qi, ki, sl: (ui, bi, qi, 0)),
        scratch_shapes=[pltpu.VMEM((tq, 1), _F32), pltpu.VMEM((tq, 256), _F32)],
    )
    return pl.pallas_call(
        kern,
        grid_spec=grid_spec,
        out_shape=jax.ShapeDtypeStruct((u, b, s, LANES), _F32),
        compiler_params=pltpu.CompilerParams(
            dimension_semantics=("parallel", "parallel", "parallel", "arbitrary"),
            vmem_limit_bytes=_VMEM_LIMIT),
        name="flash_a" if alibi else "flash_b",
    )(slopes, q, k, v)


def _attn_kernel(qt_ref, k_ref, vt_ref, o_ref, *, tq, q_steps, alibi):
    n_k, _, tk = vt_ref.shape[2:]
    q_sub = qt_ref.shape[3] // tq
    n_diag = tq // tk
    if alibi:
        head = pl.program_id(0) // 2
        sigma = jnp.float32(_alibi_slope(A_HEADS - 1) * LOG2E)
        for hd in range(A_HEADS - 1):
            sigma = jnp.where(head == hd, jnp.float32(_alibi_slope(hd) * LOG2E), sigma)
        ahead = (lax.broadcasted_iota(jnp.int32, (tk, tq), 0) - lax.broadcasted_iota(jnp.int32, (tk, tq), 1))

    for sub in range(q_sub):
        qi = sub if q_steps == 1 else pl.program_id(2) * q_sub + sub
        queries = slice(sub * tq, (sub + 1) * tq)

        def block(var, j, correction=None):
            k = k_ref[0, 0, pl.ds(pl.multiple_of(j * tk, tk), tk), :]
            st = jnp.dot(k, qt_ref[0, var, :, queries], preferred_element_type=_F32)
            if correction is not None:
                st = st + correction
            return jnp.dot(vt_ref[0, 0, j], jnp.exp2(st).astype(_BF16), preferred_element_type=_F32)

        def accumulate(acc, part):
            return part if acc is None else acc + part

        acc = None
        if alibi:
            first_diag = qi * n_diag
            for d in range(n_diag):
                corr = (-2.0 * sigma) * jnp.maximum(ahead + d * tk, 0).astype(_F32)
                acc = accumulate(acc, block(0, first_diag + d, corr))
            for x in range(n_k - n_diag):
                after = (x >= first_diag) * 1
                acc = accumulate(acc, block(after, x + n_diag * after))
        else:
            for x in range(n_k):
                acc = accumulate(acc, block(0, x))
        o_ref[0, 0, queries, :] = (acc[:LANES] / acc[LANES:LANES + 1]).T.astype(o_ref.dtype)


def _attn_call(qt, k, vt, *, alibi, units_per_v, tq):
    u, n_var, dk, _ = qt.shape
    _, b, s, _ = k.shape
    n_k, _, tk = vt.shape[2:]
    assert tq % tk == 0 and s % tq == 0, (s, tq, tk)
    q_sub = _largest_group(s // tq, max(1, MAX_BLOCK_PAIRS // n_k))
    q_steps = s // (tq * q_sub)
    kern = functools.partial(_attn_kernel, tq=tq, q_steps=q_steps, alibi=alibi)
    return pl.pallas_call(
        kern,
        grid=(u, b, q_steps),
        in_specs=[
            pl.BlockSpec((1, n_var, dk, tq * q_sub), lambda ui, bi, qi: (ui, 0, 0, bi * q_steps + qi)),
            pl.BlockSpec((1, 1, s, dk), lambda ui, bi, qi: (ui, bi, 0, 0)),
            pl.BlockSpec((1, 1, n_k, VT_ROWS, tk), lambda ui, bi, qi: (ui // units_per_v, bi, 0, 0, 0)),
        ],
        out_specs=pl.BlockSpec((1, 1, tq * q_sub, LANES), lambda ui, bi, qi: (ui, bi, qi, 0)),
        out_shape=jax.ShapeDtypeStruct((u, b, s, LANES), _BF16),
        compiler_params=pltpu.CompilerParams(
            dimension_semantics=("parallel", "parallel", "parallel"),
            vmem_limit_bytes=_VMEM_LIMIT),
        name="attn_a" if alibi else "attn_b",
    )(qt, k, vt)


def _out_kernel(oa_ref, ob_ref, ga_ref, gb_ref, x_ref, lq1_ref, lk1_ref, lq2_ref, lk2_ref, subln_ref,
                wout_ref, y_ref, *, lam_init):
    lam = (jnp.exp(jnp.sum(lq1_ref[...] * lk1_ref[...], axis=-1, keepdims=True))
           - jnp.exp(jnp.sum(lq2_ref[...] * lk2_ref[...], axis=-1, keepdims=True)) + lam_init)
    ga = ga_ref[...].astype(_F32)
    gb = gb_ref[...].astype(_F32)
    pieces = []
    for hd in range(A_HEADS):
        d = oa_ref[2 * hd].astype(_F32) - lam * oa_ref[2 * hd + 1].astype(_F32)
        n = d * lax.rsqrt(jnp.mean(d * d, axis=-1, keepdims=True) + EPS) * subln_ref[...]
        pieces.append(n * (1.0 - lam_init) * ga[:, hd * LANES:(hd + 1) * LANES])
    for hd in range(B_HEADS):
        pieces.append(ob_ref[hd].astype(_F32) * gb[:, hd * LANES:(hd + 1) * LANES])
    y = jnp.concatenate(pieces, axis=1).astype(_BF16)
    y_ref[...] = x_ref[...] + jnp.dot(y, wout_ref[...], preferred_element_type=_F32)


def _out_call(oa, ob, ga, gb, x2d, lw, lam_init, tm):
    t = x2d.shape[0]
    const = lambda i: (0, 0)
    row = lambda i: (i, 0)
    unit_row = lambda i: (0, i, 0)
    return pl.pallas_call(
        functools.partial(_out_kernel, lam_init=lam_init),
        grid=(t // tm,),
        in_specs=[
            pl.BlockSpec((A_UNITS, tm, LANES), unit_row),
            pl.BlockSpec((B_HEADS, tm, LANES), unit_row),
            pl.BlockSpec((tm, 512), row),
            pl.BlockSpec((tm, 512), row),
            pl.BlockSpec((tm, D_MODEL), row),
            pl.BlockSpec((1, A_HEAD_DIM), const),
            pl.BlockSpec((1, A_HEAD_DIM), const),
            pl.BlockSpec((1, A_HEAD_DIM), const),
            pl.BlockSpec((1, A_HEAD_DIM), const),
            pl.BlockSpec((1, A_V_DIM), const),
            pl.BlockSpec((D_MODEL, D_MODEL), const),
        ],
        out_specs=pl.BlockSpec((tm, D_MODEL), row),
        out_shape=jax.ShapeDtypeStruct((t, D_MODEL), _F32),
        compiler_params=pltpu.CompilerParams(
            dimension_semantics=("parallel",), vmem_limit_bytes=_VMEM_LIMIT),
        name="out",
    )(oa, ob, ga, gb, x2d, lw["lq1"], lw["lk1"], lw["lq2"], lw["lk2"], lw["subln"], lw["w_out"])


def _rope_tables(seq_len):
    inv = ROPE_THETA ** (-jnp.arange(0, B_ROPE, 2, dtype=_F32) / B_ROPE)
    ang = jnp.arange(seq_len, dtype=_F32)[:, None] * inv[None, :]
    cos, sin = jnp.cos(ang), jnp.sin(ang)
    zeros = jnp.zeros((seq_len, LANES - B_ROPE), _F32)
    return (jnp.concatenate([cos, cos, zeros], axis=1),
            jnp.concatenate([-sin, sin, zeros], axis=1))


def _score_bounds(l, p):
    amax = lambda v: jnp.max(jnp.abs(v[l].astype(_F32)))
    bound_a = amax(p["a_q_norm"]) * amax(p["a_k_norm"]) * (A_HEAD_DIM ** 0.5)
    bound_b = amax(p["b_q_norm"]) * amax(p["b_k_norm"]) * (B_QK_DIM ** 0.5)
    return bound_a, bound_b


def _aug_rows(bound_a, bound_b):
    aug_q = np.zeros((A_UNITS, A_HEAD_DIM, LANES), np.float32)
    aug_k = np.zeros((A_UNITS, 1, LANES), np.float32)
    for u in range(A_UNITS):
        sig = _sigma_pieces(u // 2)
        base = _aug_base(u % 2)
        aug_k[u, 0, base + _AUG_SHIFT] = 1.0
        for a in range(SIGMA_PIECES):
            aug_k[u, 0, base + _AUG_QHI + a] = -sig[a]
            aug_k[u, 0, base + _AUG_QLO + a] = -sig[a]
            aug_q[u, _AUG_KHI + a, :] = sig[a]
            aug_q[u, _AUG_KLO + a, :] = sig[a]
    shift_row = (np.arange(A_HEAD_DIM) == _AUG_SHIFT).astype(np.float32).reshape(1, A_HEAD_DIM, 1)
    aug_q = jnp.asarray(aug_q) - (bound_a * LOG2E) * shift_row
    b_lane = (np.arange(LANES) == _B_SHIFT).astype(np.float32)
    aug_b = jnp.stack([-(bound_b * LOG2E) * b_lane, jnp.asarray(b_lane)])
    return aug_q, jnp.asarray(aug_k), aug_b


def _layer_weights(l, p, rope_c, rope_s):
    swap = _rope_swap_index()
    w = p["w_in"][l]
    kr = w[:, _SPLITS[6]:_SPLITS[7]]
    w_in = jnp.concatenate([w[:, :_SPLITS[6]], kr, kr[:, swap], w[:, _SPLITS[7]:]], axis=1).astype(_BF16)
    wuq = p["b_w_uq"][l].reshape(Q_LORA, B_HEADS, B_QK_DIM)
    w_uq = jnp.concatenate([wuq, wuq[:, :, B_NOPE:][:, :, swap]], axis=2).reshape(Q_LORA, B_HEADS * 256)
    gq = p["b_q_norm"][l]
    gk = p["b_k_norm"][l]
    row = lambda v: v.reshape(1, -1).astype(_F32)
    bound_a, bound_b = _score_bounds(l, p)
    aug_q, aug_k, aug_b = _aug_rows(bound_a, bound_b)
    return {
        "norm_w": row(p["norm_w"][l]),
        "w_in": w_in,
        "g_aq": row(jnp.tile(p["a_q_norm"][l], A_UNITS)),
        "g_ak": row(jnp.tile(p["a_k_norm"][l], A_UNITS)),
        "g_cq": row(p["b_cq_norm"][l]),
        "w_uq": w_uq.astype(_BF16),
        "g_ckv": row(p["b_ckv_norm"][l]),
        "w_ukv": p["b_w_ukv"][l].astype(_BF16),
        "g_bq": row(jnp.concatenate([gq, gq[B_NOPE:][swap]])),
        "g_bkn": row(gk[:B_NOPE]),
        "g_bkr": row(jnp.concatenate([gk[B_NOPE:], gk[B_NOPE:][swap]])),
        "rope_c": rope_c,
        "rope_s": rope_s,
        "aug_q": aug_q, "aug_k": aug_k, "aug_b": aug_b,
        "lq1": row(p["a_lq1"][l]), "lk1": row(p["a_lk1"][l]),
        "lq2": row(p["a_lq2"][l]), "lk2": row(p["a_lk2"][l]),
        "subln": row(p["a_subln"][l]),
        "w_out": p["w_out"][l].astype(_BF16),
    }


def _tile(n, pref):
    return pref if n % pref == 0 else n


def _largest_group(n, cap):
    return max(g for g in range(1, cap + 1) if n % g == 0) if n > 0 else 1


def _encoder_layer(x, lw, lam_init, slopes, fixed_shift):
    b, s, _ = x.shape
    t = b * s
    x2d = x.reshape(t, D_MODEL)
    tm = _tile(s, FIXED_TK)
    qa, ka, va, ga, qb, kb, vb, gb = _proj_call(x2d, s, lw, tm, fixed_shift)
    split = lambda a: a.reshape(a.shape[:-2] + (b, s, a.shape[-1]))
    tq = tk = _tile(s, 512)
    if fixed_shift:
        assert s <= POS_SPLIT * POS_SPLIT * 2
        ftq = _tile(s, FIXED_TQ)
        split_t = lambda a: a.reshape((a.shape[0], b, s // tm) + a.shape[2:])
        oa = _attn_call(qa, split(ka), split_t(va), alibi=True, units_per_v=2, tq=ftq)
        ob = _attn_call(qb, split(kb), split_t(vb), alibi=False, units_per_v=1, tq=ftq)
    else:
        oa = _flash_call(split(qa), split(ka), split(va), slopes, alibi=True, units_per_v=2, tq=tq, tk=tk)
        ob = _flash_call(split(qb), split(kb), split(vb), slopes, alibi=False, units_per_v=1, tq=tq, tk=tk)
    y = _out_call(oa.reshape(A_UNITS, t, LANES), ob.reshape(B_HEADS, t, LANES), ga, gb, x2d, lw, lam_init, tm)
    return y.reshape(b, s, D_MODEL)


def _forward(x_prompt, x_sample, p, fixed_shift):
    depth = p["norm_w"].shape[0]
    rope_c, rope_s = {}, {}
    for s in {x_prompt.shape[1], x_sample.shape[1]}:
        rope_c[s], rope_s[s] = _rope_tables(s)
    slopes = jnp.asarray([_alibi_slope(hd) for hd in range(A_HEADS)], _F32)
    y_prompt, y_sample = x_prompt, x_sample
    for l in range(depth):
        lw = _layer_weights(l, p, rope_c, rope_s)
        lam_init = _lambda_init(l)
        y_prompt = _encoder_layer(y_prompt, lw, lam_init, slopes, fixed_shift)
        y_sample = _encoder_layer(y_sample, lw, lam_init, slopes, fixed_shift)
    return (y_prompt, y_sample)


def kernel(x_prompt, x_sample, norm_w, w_in, a_q_norm, a_k_norm, a_lq1, a_lk1, a_lq2, a_lk2, a_subln,
           b_cq_norm, b_w_uq, b_ckv_norm, b_w_ukv, b_q_norm, b_k_norm, w_out):
    p = dict(norm_w=norm_w, w_in=w_in, a_q_norm=a_q_norm, a_k_norm=a_k_norm, a_lq1=a_lq1, a_lk1=a_lk1,
             a_lq2=a_lq2, a_lk2=a_lk2, a_subln=a_subln, b_cq_norm=b_cq_norm, b_w_uq=b_w_uq,
             b_ckv_norm=b_ckv_norm, b_w_ukv=b_w_ukv, b_q_norm=b_q_norm, b_k_norm=b_k_norm, w_out=w_out)
    bounds = jnp.stack([jnp.stack(_score_bounds(l, p)) for l in range(norm_w.shape[0])])
    shift_ok = jnp.max(bounds) <= MAX_STATIC_SHIFT
    return lax.cond(shift_ok,
                    lambda xp, xs, pp: _forward(xp, xs, pp, True),
                    lambda xp, xs, pp: _forward(xp, xs, pp, False),
                    x_prompt, x_sample, p)
```

```python
import functools
import math

import numpy as np
import jax
import jax.numpy as jnp
from jax import lax
from jax.experimental import pallas as pl
from jax.experimental.pallas import tpu as pltpu

D_MODEL = 1024
A_HEADS = 4
A_HEAD_DIM = 64
A_V_DIM = 128
A_UNITS = 2 * A_HEADS
B_HEADS = 4
B_NOPE = 128
B_ROPE = 64
B_QK_DIM = B_NOPE + B_ROPE
B_V_DIM = 128
Q_LORA = 256
KV_LORA = 128
ROPE_THETA = 10000.0
EPS = 1e-6
LANES = 128

_SPLITS = np.cumsum([0, 512, 512, 512, 512, Q_LORA, KV_LORA, B_ROPE, 512])
_P_AQ, _P_AK, _P_AV, _P_AG, _P_CQ, _P_CKV, _P_KR, _P_BG, _P_END = np.cumsum(
    [0, 512, 512, 512, 512, Q_LORA, KV_LORA, 2 * B_ROPE, 512])

_VMEM_LIMIT = 56 * 1024 * 1024
LOG2E = math.log2(math.e)
MAX_STATIC_SHIFT = 32.0
POS_SPLIT = 128
SIGMA_PIECES = 3
FIXED_TQ = 1024
FIXED_TK = 512
MAX_BLOCK_PAIRS = 64
VT_ROWS = 144

_AUG_SHIFT = 0
_AUG_QHI = _AUG_SHIFT + 1
_AUG_QLO = _AUG_QHI + SIGMA_PIECES
_AUG_KHI = _AUG_QLO + SIGMA_PIECES
_AUG_KLO = _AUG_KHI + SIGMA_PIECES
_AUG_END = _AUG_KLO + SIGMA_PIECES
_B_SHIFT = B_QK_DIM - LANES


def _aug_base(comp):
    return A_HEAD_DIM if comp == 0 else 0

_F32 = jnp.float32
_BF16 = jnp.bfloat16


def _lambda_init(layer_idx):
    return 0.8 - 0.6 * math.exp(-0.3 * layer_idx)


def _alibi_slope(head):
    return 2.0 ** (-8.0 * (head + 1) / A_HEADS)


def _sigma_pieces(head):
    rest = _alibi_slope(head) * LOG2E
    pieces = []
    for _ in range(SIGMA_PIECES):
        p = float(np.asarray(rest, np.float32).astype(_BF16).astype(np.float32))
        pieces.append(p)
        rest -= p
    return pieces


def _group_sum_matrices():
    unit = np.arange(A_UNITS * A_HEAD_DIM) // A_HEAD_DIM
    m_a = (unit[:, None] == unit[None, :]).astype(np.float32)
    m_b = np.zeros((2, 256, 256), np.float32)
    m_b[0, :B_QK_DIM, :] = 1.0
    m_b[1, :B_NOPE, :] = 1.0
    return jnp.asarray(m_a, _BF16), jnp.asarray(m_b, _BF16)


def _rope_swap_index():
    half = B_ROPE // 2
    return np.concatenate([np.arange(half, B_ROPE), np.arange(0, half)])


def _silu(x):
    return x / (1.0 + jnp.exp(-x))


def _proj_kernel(x_ref, nw_ref, win_ref, gaq_ref, gak_ref, gcq_ref, wuq_ref, gckv_ref, wukv_ref,
                 gbq_ref, gbkn_ref, gbkr_ref, ct_ref, st_ref, augq_ref, augk_ref, augb_ref, suma_ref, sumb_ref,
                 qa_ref, ka_ref, va_ref, ga_ref, qb_ref, kb_ref, vb_ref, gb_ref, *, fixed_shift, seq_len):
    tm = x_ref.shape[0]
    rows = slice(0, tm)
    x = x_ref[rows, :]
    h = x * lax.rsqrt(jnp.mean(x * x, axis=-1, keepdims=True) + EPS) * nw_ref[...]
    hb = h.astype(_BF16)

    def proj(lo, hi):
        return jnp.dot(hb, win_ref[:, lo:hi], preferred_element_type=_F32)

    lane = lax.broadcasted_iota(jnp.int32, (tm, LANES), 1)
    low_half = lane < B_ROPE
    ones_col = (lane == 0).astype(_F32)
    ct = ct_ref[rows, :]
    st = st_ref[rows, :]

    def aug_select(index, base, hi, lo):
        at = lambda off: (index >= base + off) & (index < base + off + SIGMA_PIECES)
        return at(hi), at(lo)

    if fixed_shift:
        a_qscale = (A_HEAD_DIM ** -0.5) * LOG2E
        b_qscale = (B_QK_DIM ** -0.5) * LOG2E
        first = (pl.program_id(0) % (seq_len // tm)) * tm - seq_len // 2
        split_pos = lambda p: (((p >> 7) * POS_SPLIT).astype(_F32), (p & (POS_SPLIT - 1)).astype(_F32))
        k_hi, k_lo = split_pos(first + lax.broadcasted_iota(jnp.int32, (tm, LANES), 0))
        q_hi, q_lo = split_pos(first + lax.broadcasted_iota(jnp.int32, (A_HEAD_DIM, tm), 1))
        aug_row = lax.broadcasted_iota(jnp.int32, (A_HEAD_DIM, tm), 0)
        at_hi, at_lo = aug_select(aug_row, 0, _AUG_QHI, _AUG_QLO)
        q_pos = jnp.where(at_hi, q_hi, jnp.where(at_lo, q_lo, 0.0))
        bias_rows = (aug_row >= _AUG_QHI) & (aug_row < _AUG_END)
    else:
        a_qscale = A_HEAD_DIM ** -0.5
        b_qscale = B_QK_DIM ** -0.5

    def group_sums(a, m):
        return jnp.dot((a * a).astype(_BF16), m, preferred_element_type=_F32)

    aq = proj(_P_AQ, _P_AK)
    ak = proj(_P_AK, _P_AV)
    qn_all = aq * lax.rsqrt(group_sums(aq, suma_ref[...]) * (1.0 / A_HEAD_DIM) + EPS) * gaq_ref[...] * a_qscale
    kn_all = ak * lax.rsqrt(group_sums(ak, suma_ref[...]) * (1.0 / A_HEAD_DIM) + EPS) * gak_ref[...]
    for hd in range(A_HEADS):
        qn2 = qn_all[:, hd * LANES:(hd + 1) * LANES]
        kn2 = kn_all[:, hd * LANES:(hd + 1) * LANES]
        if fixed_shift:
            qn2_t = qn2.T
        for comp in range(2):
            u = 2 * hd + comp
            own = (lane < A_HEAD_DIM) if comp == 0 else (lane >= A_HEAD_DIM)
            if fixed_shift:
                feat = qn2_t[comp * A_HEAD_DIM:(comp + 1) * A_HEAD_DIM]
                aug = jnp.concatenate([augq_ref[u]] * (tm // LANES), axis=1) + q_pos
                aug_after = jnp.where(bias_rows, -aug, aug)
                stack = (lambda a: [feat, a]) if comp == 0 else (lambda a: [a, feat])
                qa_ref[u, 0, :, rows] = jnp.concatenate(stack(aug), axis=0).astype(_BF16)
                qa_ref[u, 1, :, rows] = jnp.concatenate(stack(aug_after), axis=0).astype(_BF16)
                at_hi, at_lo = aug_select(lane, _aug_base(comp), _AUG_KHI, _AUG_KLO)
                k_pos = jnp.where(at_hi, k_hi, jnp.where(at_lo, k_lo, 0.0))
                ka_ref[u, rows, :] = (jnp.where(own, kn2, 0.0) + augk_ref[u] + k_pos).astype(_BF16)
            else:
                qa_ref[u, 0, rows, :] = jnp.where(own, qn2, 0.0).astype(_BF16)
                ka_ref[u, rows, :] = jnp.where(own, kn2, 0.0).astype(_BF16)

    def store_values(v_ref, hd, vv):
        if fixed_shift:
            pad_rows = lax.broadcasted_iota(jnp.int32, (VT_ROWS - LANES, tm), 0)
            v_ref[hd, 0, :, rows] = jnp.concatenate([vv.T, (pad_rows == 0).astype(_F32)], axis=0).astype(_BF16)
        else:
            v_ref[hd, rows, :] = jnp.concatenate([vv, ones_col], axis=1).astype(_BF16)

    av = proj(_P_AV, _P_AG)
    for hd in range(A_HEADS):
        store_values(va_ref, hd, av[:, hd * LANES:(hd + 1) * LANES])
    ga_ref[rows, :] = _silu(proj(_P_AG, _P_CQ)).astype(_BF16)

    cq = proj(_P_CQ, _P_CKV)
    cqn = cq * lax.rsqrt(jnp.mean(cq * cq, axis=-1, keepdims=True) + EPS) * gcq_ref[...]
    qall = jnp.dot(cqn.astype(_BF16), wuq_ref[...], preferred_element_type=_F32)
    ckv = proj(_P_CKV, _P_KR)
    ckvn = ckv * lax.rsqrt(jnp.mean(ckv * ckv, axis=-1, keepdims=True) + EPS) * gckv_ref[...]
    kv = jnp.dot(ckvn.astype(_BF16), wukv_ref[...], preferred_element_type=_F32)
    kr2 = proj(_P_KR, _P_BG)

    for hd in range(B_HEADS):
        q0 = qall[:, hd * 256:hd * 256 + LANES]
        q1 = qall[:, hd * 256 + LANES:(hd + 1) * 256]
        r = lax.rsqrt(group_sums(qall[:, hd * 256:(hd + 1) * 256], sumb_ref[0]) * (1.0 / B_QK_DIM) + EPS)
        q0n = q0 * r[:, :LANES] * gbq_ref[:, :LANES] * b_qscale
        q1n = q1 * r[:, LANES:] * gbq_ref[:, LANES:]
        q1r = (q1n * ct + pltpu.roll(q1n, B_ROPE, 1) * st) * b_qscale
        if fixed_shift:
            q1r = q1r + augb_ref[0:1, :]
            qb_ref[hd, 0, :, rows] = jnp.concatenate([q0n.T, q1r.T], axis=0).astype(_BF16)
        else:
            qb_ref[hd, 0, rows, :] = jnp.concatenate([q0n, q1r], axis=1).astype(_BF16)

    kr_ss = jnp.sum(jnp.where(low_half, kr2 * kr2, 0.0), axis=-1, keepdims=True)
    krg = kr2 * gbkr_ref[...]
    krr = krg * ct + pltpu.roll(krg, B_ROPE, 1) * st
    for hd in range(B_HEADS):
        kn = kv[:, hd * 256:hd * 256 + LANES]
        vv = kv[:, hd * 256 + LANES:(hd + 1) * 256]
        kn_ss = group_sums(kv[:, hd * 256:(hd + 1) * 256], sumb_ref[1])[:, :LANES]
        r = lax.rsqrt((kn_ss + kr_ss) * (1.0 / B_QK_DIM) + EPS)
        k1 = krr * r
        if fixed_shift:
            k1 = k1 + augb_ref[1:2, :]
        kb_ref[hd, rows, :] = jnp.concatenate([kn * r * gbkn_ref[...], k1], axis=1).astype(_BF16)
        store_values(vb_ref, hd, vv)
    gb_ref[rows, :] = _silu(proj(_P_BG, _P_END)).astype(_BF16)


def _proj_call(x2d, seq_len, lw, tm, fixed_shift):
    t = x2d.shape[0]
    blocks_per_seq = seq_len // tm
    n_var = 2 if fixed_shift else 1
    const = lambda i: (0, 0)
    const3 = lambda i: (0, 0, 0)
    row = lambda i: (i, 0)
    unit_row = lambda i: (0, i, 0)
    var_row = lambda i: (0, 0, i, 0)
    rope_row = lambda i: (i % blocks_per_seq, 0)
    in_specs = [
        pl.BlockSpec((tm, D_MODEL), row),
        pl.BlockSpec((1, D_MODEL), const),
        pl.BlockSpec((D_MODEL, int(_P_END)), const),
        pl.BlockSpec((1, A_UNITS * A_HEAD_DIM), const),
        pl.BlockSpec((1, A_UNITS * A_HEAD_DIM), const),
        pl.BlockSpec((1, Q_LORA), const),
        pl.BlockSpec((Q_LORA, B_HEADS * 256), const),
        pl.BlockSpec((1, KV_LORA), const),
        pl.BlockSpec((KV_LORA, B_HEADS * 256), const),
        pl.BlockSpec((1, 256), const),
        pl.BlockSpec((1, LANES), const),
        pl.BlockSpec((1, LANES), const),
        pl.BlockSpec((tm, LANES), rope_row),
        pl.BlockSpec((tm, LANES), rope_row),
        pl.BlockSpec((A_UNITS, A_HEAD_DIM, LANES), const3),
        pl.BlockSpec((A_UNITS, 1, LANES), const3),
        pl.BlockSpec((2, LANES), const),
        pl.BlockSpec((A_UNITS * A_HEAD_DIM, A_UNITS * A_HEAD_DIM), const),
        pl.BlockSpec((2, 256, 256), const3),
    ]
    if fixed_shift:
        q_shape = lambda units, dk: jax.ShapeDtypeStruct((units, n_var if dk == LANES else 1, dk, t), _BF16)
        q_spec = lambda units, dk: pl.BlockSpec((units, n_var if dk == LANES else 1, dk, tm),
                                                lambda i: (0, 0, 0, i))
        v_shape = lambda heads: jax.ShapeDtypeStruct((heads, t // tm, VT_ROWS, tm), _BF16)
        v_spec = lambda heads: pl.BlockSpec((heads, 1, VT_ROWS, tm), lambda i: (0, i, 0, 0))
    else:
        q_shape = lambda units, dk: jax.ShapeDtypeStruct((units, 1, t, dk), _BF16)
        q_spec = lambda units, dk: pl.BlockSpec((units, 1, tm, dk), var_row)
        v_shape = lambda heads: jax.ShapeDtypeStruct((heads, t, 256), _BF16)
        v_spec = lambda heads: pl.BlockSpec((heads, tm, 256), unit_row)
    out_shape = [
        q_shape(A_UNITS, LANES),
        jax.ShapeDtypeStruct((A_UNITS, t, LANES), _BF16),
        v_shape(A_HEADS),
        jax.ShapeDtypeStruct((t, 512), _BF16),
        q_shape(B_HEADS, 256),
        jax.ShapeDtypeStruct((B_HEADS, t, 256), _BF16),
        v_shape(B_HEADS),
        jax.ShapeDtypeStruct((t, 512), _BF16),
    ]
    out_specs = [
        q_spec(A_UNITS, LANES),
        pl.BlockSpec((A_UNITS, tm, LANES), unit_row),
        v_spec(A_HEADS),
        pl.BlockSpec((tm, 512), row),
        q_spec(B_HEADS, 256),
        pl.BlockSpec((B_HEADS, tm, 256), unit_row),
        v_spec(B_HEADS),
        pl.BlockSpec((tm, 512), row),
    ]
    return pl.pallas_call(
        functools.partial(_proj_kernel, fixed_shift=fixed_shift, seq_len=seq_len),
        grid=(t // tm,),
        in_specs=in_specs,
        out_specs=out_specs,
        out_shape=out_shape,
        compiler_params=pltpu.CompilerParams(
            dimension_semantics=("parallel",), vmem_limit_bytes=_VMEM_LIMIT),
        name="proj",
    )(x2d, lw["norm_w"], lw["w_in"], lw["g_aq"], lw["g_ak"], lw["g_cq"], lw["w_uq"], lw["g_ckv"],
      lw["w_ukv"], lw["g_bq"], lw["g_bkn"], lw["g_bkr"], lw["rope_c"][seq_len], lw["rope_s"][seq_len],
      lw["aug_q"], lw["aug_k"], lw["aug_b"], *_group_sum_matrices())


def _flash_kernel(slope_ref, q_ref, k_ref, v_ref, o_ref, m_sc, acc_sc, *, alibi, heads_per_slope):
    kj = pl.program_id(3)
    tq = q_ref.shape[3]
    tk = k_ref.shape[2]

    @pl.when(kj == 0)
    def _():
        m_sc[...] = jnp.full(m_sc.shape, -jnp.inf, _F32)
        acc_sc[...] = jnp.zeros(acc_sc.shape, _F32)

    s = lax.dot_general(q_ref[0, 0, 0], k_ref[0, 0], (((1,), (1,)), ((), ())),
                        preferred_element_type=_F32)
    if alibi:
        slope = slope_ref[pl.program_id(0) // heads_per_slope]
        qpos = pl.program_id(2) * tq + lax.broadcasted_iota(jnp.int32, (tq, tk), 0)
        kpos = kj * tk + lax.broadcasted_iota(jnp.int32, (tq, tk), 1)
        s = s - slope * jnp.abs(qpos - kpos).astype(_F32)
    m_prev = m_sc[...]
    m_new = jnp.maximum(m_prev, jnp.max(s, axis=-1, keepdims=True))
    alpha = jnp.exp(m_prev - m_new)
    p = jnp.exp(s - m_new)
    acc_sc[...] = alpha * acc_sc[...] + jnp.dot(p.astype(_BF16), v_ref[0, 0],
                                                 preferred_element_type=_F32)
    m_sc[...] = m_new

    @pl.when(kj == pl.num_programs(3) - 1)
    def _():
        acc = acc_sc[...]
        o_ref[0, 0] = acc[:, :LANES] / acc[:, LANES:LANES + 1]


def _flash_call(q, k, v, slopes, *, alibi, units_per_v, tq, tk):
    u, _, b, s, dk = q.shape
    kern = functools.partial(_flash_kernel, alibi=alibi, heads_per_slope=units_per_v)
    grid_spec = pltpu.PrefetchScalarGridSpec(
        num_scalar_prefetch=1,
        grid=(u, b, s // tq, s // tk),
        in_specs=[
            pl.BlockSpec((1, 1, 1, tq, dk), lambda ui, bi, qi, ki, sl: (ui, 0, bi, qi, 0)),
            pl.BlockSpec((1, 1, tk, dk), lambda ui, bi, qi, ki, sl: (ui, bi, ki, 0)),
            pl.BlockSpec((1, 1, tk, 256), lambda ui, bi, qi, ki, sl: (ui // units_per_v, bi, ki, 0)),
        ],
        out_specs=pl.BlockSpec((1, 1, tq, LANES), lambda ui, bi, qi, ki, sl: (ui, bi, qi, 0)),
        scratch_shapes=[pltpu.VMEM((tq, 1), _F32), pltpu.VMEM((tq, 256), _F32)],
    )
    return pl.pallas_call(
        kern,
        grid_spec=grid_spec,
        out_shape=jax.ShapeDtypeStruct((u, b, s, LANES), _F32),
        compiler_params=pltpu.CompilerParams(
            dimension_semantics=("parallel", "parallel", "parallel", "arbitrary"),
            vmem_limit_bytes=_VMEM_LIMIT),
        name="flash_a" if alibi else "flash_b",
    )(slopes, q, k, v)


def _attn_kernel(qt_ref, k_ref, vt_ref, o_ref, *, tq, q_steps, alibi):
    n_k, _, tk = vt_ref.shape[2:]
    q_sub = qt_ref.shape[3] // tq
    n_diag = tq // tk
    if alibi:
        head = pl.program_id(0) // 2
        sigma = jnp.float32(_alibi_slope(A_HEADS - 1) * LOG2E)
        for hd in range(A_HEADS - 1):
            sigma = jnp.where(head == hd, jnp.float32(_alibi_slope(hd) * LOG2E), sigma)
        ahead = (lax.broadcasted_iota(jnp.int32, (tk, tq), 0) - lax.broadcasted_iota(jnp.int32, (tk, tq), 1))

    for sub in range(q_sub):
        qi = sub if q_steps == 1 else pl.program_id(2) * q_sub + sub
        queries = slice(sub * tq, (sub + 1) * tq)

        def block(var, j, correction=None):
            k = k_ref[0, 0, pl.ds(pl.multiple_of(j * tk, tk), tk), :]
            st = jnp.dot(k, qt_ref[0, var, :, queries], preferred_element_type=_F32)
            if correction is not None:
                st = st + correction
            return jnp.dot(vt_ref[0, 0, j], jnp.exp2(st).astype(_BF16), preferred_element_type=_F32)

        def accumulate(acc, part):
            return part if acc is None else acc + part

        acc = None
        if alibi:
            first_diag = qi * n_diag
            for d in range(n_diag):
                corr = (-2.0 * sigma) * jnp.maximum(ahead + d * tk, 0).astype(_F32)
                acc = accumulate(acc, block(0, first_diag + d, corr))
            for x in range(n_k - n_diag):
                after = (x >= first_diag) * 1
                acc = accumulate(acc, block(after, x + n_diag * after))
        else:
            for x in range(n_k):
                acc = accumulate(acc, block(0, x))
        o_ref[0, 0, queries, :] = (acc[:LANES] / acc[LANES:LANES + 1]).T.astype(o_ref.dtype)


def _attn_call(qt, k, vt, *, alibi, units_per_v, tq):
    u, n_var, dk, _ = qt.shape
    _, b, s, _ = k.shape
    n_k, _, tk = vt.shape[2:]
    assert tq % tk == 0 and s % tq == 0, (s, tq, tk)
    q_sub = _largest_group(s // tq, max(1, MAX_BLOCK_PAIRS // n_k))
    q_steps = s // (tq * q_sub)
    kern = functools.partial(_attn_kernel, tq=tq, q_steps=q_steps, alibi=alibi)
    return pl.pallas_call(
        kern,
        grid=(u, b, q_steps),
        in_specs=[
            pl.BlockSpec((1, n_var, dk, tq * q_sub), lambda ui, bi, qi: (ui, 0, 0, bi * q_steps + qi)),
            pl.BlockSpec((1, 1, s, dk), lambda ui, bi, qi: (ui, bi, 0, 0)),
            pl.BlockSpec((1, 1, n_k, VT_ROWS, tk), lambda ui, bi, qi: (ui // units_per_v, bi, 0, 0, 0)),
        ],
        out_specs=pl.BlockSpec((1, 1, tq * q_sub, LANES), lambda ui, bi, qi: (ui, bi, qi, 0)),
        out_shape=jax.ShapeDtypeStruct((u, b, s, LANES), _BF16),
        compiler_params=pltpu.CompilerParams(
            dimension_semantics=("parallel", "parallel", "parallel"),
            vmem_limit_bytes=_VMEM_LIMIT),
        name="attn_a" if alibi else "attn_b",
    )(qt, k, vt)


def _out_kernel(oa_ref, ob_ref, ga_ref, gb_ref, x_ref, lq1_ref, lk1_ref, lq2_ref, lk2_ref, subln_ref,
                wout_ref, y_ref, *, lam_init):
    lam = (jnp.exp(jnp.sum(lq1_ref[...] * lk1_ref[...], axis=-1, keepdims=True))
           - jnp.exp(jnp.sum(lq2_ref[...] * lk2_ref[...], axis=-1, keepdims=True)) + lam_init)
    ga = ga_ref[...].astype(_F32)
    gb = gb_ref[...].astype(_F32)
    pieces = []
    for hd in range(A_HEADS):
        d = oa_ref[2 * hd].astype(_F32) - lam * oa_ref[2 * hd + 1].astype(_F32)
        n = d * lax.rsqrt(jnp.mean(d * d, axis=-1, keepdims=True) + EPS) * subln_ref[...]
        pieces.append(n * (1.0 - lam_init) * ga[:, hd * LANES:(hd + 1) * LANES])
    for hd in range(B_HEADS):
        pieces.append(ob_ref[hd].astype(_F32) * gb[:, hd * LANES:(hd + 1) * LANES])
    y = jnp.concatenate(pieces, axis=1).astype(_BF16)
    y_ref[...] = x_ref[...] + jnp.dot(y, wout_ref[...], preferred_element_type=_F32)


def _out_call(oa, ob, ga, gb, x2d, lw, lam_init, tm):
    t = x2d.shape[0]
    const = lambda i: (0, 0)
    row = lambda i: (i, 0)
    unit_row = lambda i: (0, i, 0)
    return pl.pallas_call(
        functools.partial(_out_kernel, lam_init=lam_init),
        grid=(t // tm,),
        in_specs=[
            pl.BlockSpec((A_UNITS, tm, LANES), unit_row),
            pl.BlockSpec((B_HEADS, tm, LANES), unit_row),
            pl.BlockSpec((tm, 512), row),
            pl.BlockSpec((tm, 512), row),
            pl.BlockSpec((tm, D_MODEL), row),
            pl.BlockSpec((1, A_HEAD_DIM), const),
            pl.BlockSpec((1, A_HEAD_DIM), const),
            pl.BlockSpec((1, A_HEAD_DIM), const),
            pl.BlockSpec((1, A_HEAD_DIM), const),
            pl.BlockSpec((1, A_V_DIM), const),
            pl.BlockSpec((D_MODEL, D_MODEL), const),
        ],
        out_specs=pl.BlockSpec((tm, D_MODEL), row),
        out_shape=jax.ShapeDtypeStruct((t, D_MODEL), _F32),
        compiler_params=pltpu.CompilerParams(
            dimension_semantics=("parallel",), vmem_limit_bytes=_VMEM_LIMIT),
        name="out",
    )(oa, ob, ga, gb, x2d, lw["lq1"], lw["lk1"], lw["lq2"], lw["lk2"], lw["subln"], lw["w_out"])


def _rope_tables(seq_len):
    inv = ROPE_THETA ** (-jnp.arange(0, B_ROPE, 2, dtype=_F32) / B_ROPE)
    ang = jnp.arange(seq_len, dtype=_F32)[:, None] * inv[None, :]
    cos, sin = jnp.cos(ang), jnp.sin(ang)
    zeros = jnp.zeros((seq_len, LANES - B_ROPE), _F32)
    return (jnp.concatenate([cos, cos, zeros], axis=1),
            jnp.concatenate([-sin, sin, zeros], axis=1))


def _score_bounds(l, p):
    amax = lambda v: jnp.max(jnp.abs(v[l].astype(_F32)))
    bound_a = amax(p["a_q_norm"]) * amax(p["a_k_norm"]) * (A_HEAD_DIM ** 0.5)
    bound_b = amax(p["b_q_norm"]) * amax(p["b_k_norm"]) * (B_QK_DIM ** 0.5)
    return bound_a, bound_b


def _aug_rows(bound_a, bound_b):
    aug_q = np.zeros((A_UNITS, A_HEAD_DIM, LANES), np.float32)
    aug_k = np.zeros((A_UNITS, 1, LANES), np.float32)
    for u in range(A_UNITS):
        sig = _sigma_pieces(u // 2)
        base = _aug_base(u % 2)
        aug_k[u, 0, base + _AUG_SHIFT] = 1.0
        for a in range(SIGMA_PIECES):
            aug_k[u, 0, base + _AUG_QHI + a] = -sig[a]
            aug_k[u, 0, base + _AUG_QLO + a] = -sig[a]
            aug_q[u, _AUG_KHI + a, :] = sig[a]
            aug_q[u, _AUG_KLO + a, :] = sig[a]
    shift_row = (np.arange(A_HEAD_DIM) == _AUG_SHIFT).astype(np.float32).reshape(1, A_HEAD_DIM, 1)
    aug_q = jnp.asarray(aug_q) - (bound_a * LOG2E) * shift_row
    b_lane = (np.arange(LANES) == _B_SHIFT).astype(np.float32)
    aug_b = jnp.stack([-(bound_b * LOG2E) * b_lane, jnp.asarray(b_lane)])
    return aug_q, jnp.asarray(aug_k), aug_b


def _layer_weights(l, p, rope_c, rope_s):
    swap = _rope_swap_index()
    w = p["w_in"][l]
    kr = w[:, _SPLITS[6]:_SPLITS[7]]
    w_in = jnp.concatenate([w[:, :_SPLITS[6]], kr, kr[:, swap], w[:, _SPLITS[7]:]], axis=1).astype(_BF16)
    wuq = p["b_w_uq"][l].reshape(Q_LORA, B_HEADS, B_QK_DIM)
    w_uq = jnp.concatenate([wuq, wuq[:, :, B_NOPE:][:, :, swap]], axis=2).reshape(Q_LORA, B_HEADS * 256)
    gq = p["b_q_norm"][l]
    gk = p["b_k_norm"][l]
    row = lambda v: v.reshape(1, -1).astype(_F32)
    bound_a, bound_b = _score_bounds(l, p)
    aug_q, aug_k, aug_b = _aug_rows(bound_a, bound_b)
    return {
        "norm_w": row(p["norm_w"][l]),
        "w_in": w_in,
        "g_aq": row(jnp.tile(p["a_q_norm"][l], A_UNITS)),
        "g_ak": row(jnp.tile(p["a_k_norm"][l], A_UNITS)),
        "g_cq": row(p["b_cq_norm"][l]),
        "w_uq": w_uq.astype(_BF16),
        "g_ckv": row(p["b_ckv_norm"][l]),
        "w_ukv": p["b_w_ukv"][l].astype(_BF16),
        "g_bq": row(jnp.concatenate([gq, gq[B_NOPE:][swap]])),
        "g_bkn": row(gk[:B_NOPE]),
        "g_bkr": row(jnp.concatenate([gk[B_NOPE:], gk[B_NOPE:][swap]])),
        "rope_c": rope_c,
        "rope_s": rope_s,
        "aug_q": aug_q, "aug_k": aug_k, "aug_b": aug_b,
        "lq1": row(p["a_lq1"][l]), "lk1": row(p["a_lk1"][l]),
        "lq2": row(p["a_lq2"][l]), "lk2": row(p["a_lk2"][l]),
        "subln": row(p["a_subln"][l]),
        "w_out": p["w_out"][l].astype(_BF16),
    }


def _tile(n, pref):
    return pref if n % pref == 0 else n


def _largest_group(n, cap):
    return max(g for g in range(1, cap + 1) if n % g == 0) if n > 0 else 1


def _encoder_layer(x, lw, lam_init, slopes, fixed_shift):
    b, s, _ = x.shape
    t = b * s
    x2d = x.reshape(t, D_MODEL)
    tm = _tile(s, FIXED_TK)
    qa, ka, va, ga, qb, kb, vb, gb = _proj_call(x2d, s, lw, tm, fixed_shift)
    split = lambda a: a.reshape(a.shape[:-2] + (b, s, a.shape[-1]))
    tq = tk = _tile(s, 512)
    if fixed_shift:
        assert s <= POS_SPLIT * POS_SPLIT * 2
        ftq = _tile(s, FIXED_TQ)
        split_t = lambda a: a.reshape((a.shape[0], b, s // tm) + a.shape[2:])
        oa = _attn_call(qa, split(ka), split_t(va), alibi=True, units_per_v=2, tq=ftq)
        ob = _attn_call(qb, split(kb), split_t(vb), alibi=False, units_per_v=1, tq=ftq)
    else:
        oa = _flash_call(split(qa), split(ka), split(va), slopes, alibi=True, units_per_v=2, tq=tq, tk=tk)
        ob = _flash_call(split(qb), split(kb), split(vb), slopes, alibi=False, units_per_v=1, tq=tq, tk=tk)
    y = _out_call(oa.reshape(A_UNITS, t, LANES), ob.reshape(B_HEADS, t, LANES), ga, gb, x2d, lw, lam_init, tm)
    return y.reshape(b, s, D_MODEL)


def _forward(x_prompt, x_sample, p, fixed_shift):
    depth = p["norm_w"].shape[0]
    rope_c, rope_s = {}, {}
    for s in {x_prompt.shape[1], x_sample.shape[1]}:
        rope_c[s], rope_s[s] = _rope_tables(s)
    slopes = jnp.asarray([_alibi_slope(hd) for hd in range(A_HEADS)], _F32)
    y_prompt, y_sample = x_prompt, x_sample
    for l in range(depth):
        lw = _layer_weights(l, p, rope_c, rope_s)
        lam_init = _lambda_init(l)
        y_prompt = _encoder_layer(y_prompt, lw, lam_init, slopes, fixed_shift)
        y_sample = _encoder_layer(y_sample, lw, lam_init, slopes, fixed_shift)
    return (y_prompt, y_sample)


def kernel(x_prompt, x_sample, norm_w, w_in, a_q_norm, a_k_norm, a_lq1, a_lk1, a_lq2, a_lk2, a_subln,
           b_cq_norm, b_w_uq, b_ckv_norm, b_w_ukv, b_q_norm, b_k_norm, w_out):
    p = dict(norm_w=norm_w, w_in=w_in, a_q_norm=a_q_norm, a_k_norm=a_k_norm, a_lq1=a_lq1, a_lk1=a_lk1,
             a_lq2=a_lq2, a_lk2=a_lk2, a_subln=a_subln, b_cq_norm=b_cq_norm, b_w_uq=b_w_uq,
             b_ckv_norm=b_ckv_norm, b_w_ukv=b_w_ukv, b_q_norm=b_q_norm, b_k_norm=b_k_norm, w_out=w_out)
    bounds = jnp.stack([jnp.stack(_score_bounds(l, p)) for l in range(norm_w.shape[0])])
    shift_ok = jnp.max(bounds) <= MAX_STATIC_SHIFT
    return lax.cond(shift_ok,
                    lambda xp, xs, pp: _forward(xp, xs, pp, True),
                    lambda xp, xs, pp: _forward(xp, xs, pp, False),
                    x_prompt, x_sample, p)
```

```python
import functools
import math

import numpy as np
import jax
import jax.numpy as jnp
from jax import lax
from jax.experimental import pallas as pl
from jax.experimental.pallas import tpu as pltpu

D_MODEL = 1024
A_HEADS = 4
A_HEAD_DIM = 64
A_V_DIM = 128
A_UNITS = 2 * A_HEADS
B_HEADS = 4
B_NOPE = 128
B_ROPE = 64
B_QK_DIM = B_NOPE + B_ROPE
B_V_DIM = 128
Q_LORA = 256
KV_LORA = 128
ROPE_THETA = 10000.0
EPS = 1e-6
LANES = 128

_SPLITS = np.cumsum([0, 512, 512, 512, 512, Q_LORA, KV_LORA, B_ROPE, 512])
_P_AQ, _P_AK, _P_AV, _P_AG, _P_CQ, _P_CKV, _P_KR, _P_BG, _P_END = np.cumsum(
    [0, 512, 512, 512, 512, Q_LORA, KV_LORA, 2 * B_ROPE, 512])

_VMEM_LIMIT = 56 * 1024 * 1024
LOG2E = math.log2(math.e)
MAX_STATIC_SHIFT = 32.0
POS_SPLIT = 128
SIGMA_PIECES = 3
FIXED_TQ = 1024
FIXED_TK = 512
MAX_BLOCK_PAIRS = 32
MAX_LOOPED_QUERY_BLOCKS = 4
VT_ROWS = 144

_AUG_SHIFT = 0
_AUG_QHI = _AUG_SHIFT + 1
_AUG_QLO = _AUG_QHI + SIGMA_PIECES
_AUG_KHI = _AUG_QLO + SIGMA_PIECES
_AUG_KLO = _AUG_KHI + SIGMA_PIECES
_AUG_END = _AUG_KLO + SIGMA_PIECES
_B_SHIFT = B_QK_DIM - LANES


def _aug_base(comp):
    return A_HEAD_DIM if comp == 0 else 0

_F32 = jnp.float32
_BF16 = jnp.bfloat16


def _lambda_init(layer_idx):
    return 0.8 - 0.6 * math.exp(-0.3 * layer_idx)


def _alibi_slope(head):
    return 2.0 ** (-8.0 * (head + 1) / A_HEADS)


def _sigma_pieces(head):
    rest = _alibi_slope(head) * LOG2E
    pieces = []
    for _ in range(SIGMA_PIECES):
        p = float(np.asarray(rest, np.float32).astype(_BF16).astype(np.float32))
        pieces.append(p)
        rest -= p
    return pieces


def _group_sum_matrices():
    unit = np.arange(A_UNITS * A_HEAD_DIM) // A_HEAD_DIM
    m_a = (unit[:, None] == unit[None, :]).astype(np.float32)
    m_b = np.zeros((2, 256, 256), np.float32)
    m_b[0, :B_QK_DIM, :] = 1.0
    m_b[1, :B_NOPE, :] = 1.0
    return jnp.asarray(m_a, _BF16), jnp.asarray(m_b, _BF16)


def _rope_swap_index():
    half = B_ROPE // 2
    return np.concatenate([np.arange(half, B_ROPE), np.arange(0, half)])


def _silu(x):
    return x / (1.0 + jnp.exp(-x))


def _proj_kernel(x_ref, nw_ref, win_ref, gaq_ref, gak_ref, gcq_ref, wuq_ref, gckv_ref, wukv_ref,
                 gbq_ref, gbkn_ref, gbkr_ref, ct_ref, st_ref, augq_ref, augk_ref, augb_ref, suma_ref, sumb_ref,
                 qa_ref, ka_ref, va_ref, ga_ref, qb_ref, kb_ref, vb_ref, gb_ref, *, fixed_shift, seq_len):
    tm = x_ref.shape[0]
    rows = slice(0, tm)
    x = x_ref[rows, :]
    h = x * lax.rsqrt(jnp.mean(x * x, axis=-1, keepdims=True) + EPS) * nw_ref[...]
    hb = h.astype(_BF16)

    def proj(lo, hi):
        return jnp.dot(hb, win_ref[:, lo:hi], preferred_element_type=_F32)

    lane = lax.broadcasted_iota(jnp.int32, (tm, LANES), 1)
    low_half = lane < B_ROPE
    ones_col = (lane == 0).astype(_F32)
    ct = ct_ref[rows, :]
    st = st_ref[rows, :]

    def aug_select(index, base, hi, lo):
        at = lambda off: (index >= base + off) & (index < base + off + SIGMA_PIECES)
        return at(hi), at(lo)

    if fixed_shift:
        a_qscale = (A_HEAD_DIM ** -0.5) * LOG2E
        b_qscale = (B_QK_DIM ** -0.5) * LOG2E
        first = (pl.program_id(0) % (seq_len // tm)) * tm - seq_len // 2
        split_pos = lambda p: (((p >> 7) * POS_SPLIT).astype(_F32), (p & (POS_SPLIT - 1)).astype(_F32))
        k_hi, k_lo = split_pos(first + lax.broadcasted_iota(jnp.int32, (tm, LANES), 0))
        q_hi, q_lo = split_pos(first + lax.broadcasted_iota(jnp.int32, (A_HEAD_DIM, tm), 1))
        aug_row = lax.broadcasted_iota(jnp.int32, (A_HEAD_DIM, tm), 0)
        at_hi, at_lo = aug_select(aug_row, 0, _AUG_QHI, _AUG_QLO)
        q_pos = jnp.where(at_hi, q_hi, jnp.where(at_lo, q_lo, 0.0))
    else:
        a_qscale = A_HEAD_DIM ** -0.5
        b_qscale = B_QK_DIM ** -0.5

    def group_sums(a, m):
        return jnp.dot((a * a).astype(_BF16), m, preferred_element_type=_F32)

    aq = proj(_P_AQ, _P_AK)
    ak = proj(_P_AK, _P_AV)
    qn_all = aq * lax.rsqrt(group_sums(aq, suma_ref[...]) * (1.0 / A_HEAD_DIM) + EPS) * gaq_ref[...] * a_qscale
    kn_all = ak * lax.rsqrt(group_sums(ak, suma_ref[...]) * (1.0 / A_HEAD_DIM) + EPS) * gak_ref[...]
    for hd in range(A_HEADS):
        qn2 = qn_all[:, hd * LANES:(hd + 1) * LANES]
        kn2 = kn_all[:, hd * LANES:(hd + 1) * LANES]
        if fixed_shift:
            qn2_t = qn2.T
        for comp in range(2):
            u = 2 * hd + comp
            own = (lane < A_HEAD_DIM) if comp == 0 else (lane >= A_HEAD_DIM)
            if fixed_shift:
                feat = qn2_t[comp * A_HEAD_DIM:(comp + 1) * A_HEAD_DIM]
                aug = jnp.concatenate([augq_ref[u]] * (tm // LANES), axis=1) + q_pos
                stacked = [feat, aug] if comp == 0 else [aug, feat]
                qa_ref[u, 0, :, rows] = jnp.concatenate(stacked, axis=0).astype(_BF16)
                at_hi, at_lo = aug_select(lane, _aug_base(comp), _AUG_KHI, _AUG_KLO)
                k_pos = jnp.where(at_hi, k_hi, jnp.where(at_lo, k_lo, 0.0))
                ka_ref[u, rows, :] = (jnp.where(own, kn2, 0.0) + augk_ref[u] + k_pos).astype(_BF16)
            else:
                qa_ref[u, 0, rows, :] = jnp.where(own, qn2, 0.0).astype(_BF16)
                ka_ref[u, rows, :] = jnp.where(own, kn2, 0.0).astype(_BF16)

    def store_values(v_ref, hd, vv):
        if fixed_shift:
            pad_rows = lax.broadcasted_iota(jnp.int32, (VT_ROWS - LANES, tm), 0)
            v_ref[hd, 0, :, rows] = jnp.concatenate([vv.T, (pad_rows == 0).astype(_F32)], axis=0).astype(_BF16)
        else:
            v_ref[hd, rows, :] = jnp.concatenate([vv, ones_col], axis=1).astype(_BF16)

    av = proj(_P_AV, _P_AG)
    for hd in range(A_HEADS):
        store_values(va_ref, hd, av[:, hd * LANES:(hd + 1) * LANES])
    ga_ref[rows, :] = _silu(proj(_P_AG, _P_CQ)).astype(_BF16)

    cq = proj(_P_CQ, _P_CKV)
    cqn = cq * lax.rsqrt(jnp.mean(cq * cq, axis=-1, keepdims=True) + EPS) * gcq_ref[...]
    qall = jnp.dot(cqn.astype(_BF16), wuq_ref[...], preferred_element_type=_F32)
    ckv = proj(_P_CKV, _P_KR)
    ckvn = ckv * lax.rsqrt(jnp.mean(ckv * ckv, axis=-1, keepdims=True) + EPS) * gckv_ref[...]
    kv = jnp.dot(ckvn.astype(_BF16), wukv_ref[...], preferred_element_type=_F32)
    kr2 = proj(_P_KR, _P_BG)

    for hd in range(B_HEADS):
        q0 = qall[:, hd * 256:hd * 256 + LANES]
        q1 = qall[:, hd * 256 + LANES:(hd + 1) * 256]
        r = lax.rsqrt(group_sums(qall[:, hd * 256:(hd + 1) * 256], sumb_ref[0]) * (1.0 / B_QK_DIM) + EPS)
        q0n = q0 * r[:, :LANES] * gbq_ref[:, :LANES] * b_qscale
        q1n = q1 * r[:, LANES:] * gbq_ref[:, LANES:]
        q1r = (q1n * ct + pltpu.roll(q1n, B_ROPE, 1) * st) * b_qscale
        if fixed_shift:
            q1r = q1r + augb_ref[0:1, :]
            qb_ref[hd, 0, :, rows] = jnp.concatenate([q0n.T, q1r.T], axis=0).astype(_BF16)
        else:
            qb_ref[hd, 0, rows, :] = jnp.concatenate([q0n, q1r], axis=1).astype(_BF16)

    kr_ss = jnp.sum(jnp.where(low_half, kr2 * kr2, 0.0), axis=-1, keepdims=True)
    krg = kr2 * gbkr_ref[...]
    krr = krg * ct + pltpu.roll(krg, B_ROPE, 1) * st
    for hd in range(B_HEADS):
        kn = kv[:, hd * 256:hd * 256 + LANES]
        vv = kv[:, hd * 256 + LANES:(hd + 1) * 256]
        kn_ss = group_sums(kv[:, hd * 256:(hd + 1) * 256], sumb_ref[1])[:, :LANES]
        r = lax.rsqrt((kn_ss + kr_ss) * (1.0 / B_QK_DIM) + EPS)
        k1 = krr * r
        if fixed_shift:
            k1 = k1 + augb_ref[1:2, :]
        kb_ref[hd, rows, :] = jnp.concatenate([kn * r * gbkn_ref[...], k1], axis=1).astype(_BF16)
        store_values(vb_ref, hd, vv)
    gb_ref[rows, :] = _silu(proj(_P_BG, _P_END)).astype(_BF16)


def _proj_call(x2d, seq_len, lw, tm, fixed_shift):
    t = x2d.shape[0]
    blocks_per_seq = seq_len // tm
    const = lambda i: (0, 0)
    const3 = lambda i: (0, 0, 0)
    row = lambda i: (i, 0)
    unit_row = lambda i: (0, i, 0)
    var_row = lambda i: (0, 0, i, 0)
    rope_row = lambda i: (i % blocks_per_seq, 0)
    in_specs = [
        pl.BlockSpec((tm, D_MODEL), row),
        pl.BlockSpec((1, D_MODEL), const),
        pl.BlockSpec((D_MODEL, int(_P_END)), const),
        pl.BlockSpec((1, A_UNITS * A_HEAD_DIM), const),
        pl.BlockSpec((1, A_UNITS * A_HEAD_DIM), const),
        pl.BlockSpec((1, Q_LORA), const),
        pl.BlockSpec((Q_LORA, B_HEADS * 256), const),
        pl.BlockSpec((1, KV_LORA), const),
        pl.BlockSpec((KV_LORA, B_HEADS * 256), const),
        pl.BlockSpec((1, 256), const),
        pl.BlockSpec((1, LANES), const),
        pl.BlockSpec((1, LANES), const),
        pl.BlockSpec((tm, LANES), rope_row),
        pl.BlockSpec((tm, LANES), rope_row),
        pl.BlockSpec((A_UNITS, A_HEAD_DIM, LANES), const3),
        pl.BlockSpec((A_UNITS, 1, LANES), const3),
        pl.BlockSpec((2, LANES), const),
        pl.BlockSpec((A_UNITS * A_HEAD_DIM, A_UNITS * A_HEAD_DIM), const),
        pl.BlockSpec((2, 256, 256), const3),
    ]
    if fixed_shift:
        q_shape = lambda units, dk: jax.ShapeDtypeStruct((units, t // tm, dk, tm), _BF16)
        q_spec = lambda units, dk: pl.BlockSpec((units, 1, dk, tm), lambda i: (0, i, 0, 0))
        v_shape = lambda heads: jax.ShapeDtypeStruct((heads, t // tm, VT_ROWS, tm), _BF16)
        v_spec = lambda heads: pl.BlockSpec((heads, 1, VT_ROWS, tm), lambda i: (0, i, 0, 0))
    else:
        q_shape = lambda units, dk: jax.ShapeDtypeStruct((units, 1, t, dk), _BF16)
        q_spec = lambda units, dk: pl.BlockSpec((units, 1, tm, dk), var_row)
        v_shape = lambda heads: jax.ShapeDtypeStruct((heads, t, 256), _BF16)
        v_spec = lambda heads: pl.BlockSpec((heads, tm, 256), unit_row)
    out_shape = [
        q_shape(A_UNITS, LANES),
        jax.ShapeDtypeStruct((A_UNITS, t, LANES), _BF16),
        v_shape(A_HEADS),
        jax.ShapeDtypeStruct((t, 512), _BF16),
        q_shape(B_HEADS, 256),
        jax.ShapeDtypeStruct((B_HEADS, t, 256), _BF16),
        v_shape(B_HEADS),
        jax.ShapeDtypeStruct((t, 512), _BF16),
    ]
    out_specs = [
        q_spec(A_UNITS, LANES),
        pl.BlockSpec((A_UNITS, tm, LANES), unit_row),
        v_spec(A_HEADS),
        pl.BlockSpec((tm, 512), row),
        q_spec(B_HEADS, 256),
        pl.BlockSpec((B_HEADS, tm, 256), unit_row),
        v_spec(B_HEADS),
        pl.BlockSpec((tm, 512), row),
    ]
    return pl.pallas_call(
        functools.partial(_proj_kernel, fixed_shift=fixed_shift, seq_len=seq_len),
        grid=(t // tm,),
        in_specs=in_specs,
        out_specs=out_specs,
        out_shape=out_shape,
        compiler_params=pltpu.CompilerParams(
            dimension_semantics=("parallel",), vmem_limit_bytes=_VMEM_LIMIT),
        name="proj",
    )(x2d, lw["norm_w"], lw["w_in"], lw["g_aq"], lw["g_ak"], lw["g_cq"], lw["w_uq"], lw["g_ckv"],
      lw["w_ukv"], lw["g_bq"], lw["g_bkn"], lw["g_bkr"], lw["rope_c"][seq_len], lw["rope_s"][seq_len],
      lw["aug_q"], lw["aug_k"], lw["aug_b"], *_group_sum_matrices())


def _flash_kernel(slope_ref, q_ref, k_ref, v_ref, o_ref, m_sc, acc_sc, *, alibi, heads_per_slope):
    kj = pl.program_id(3)
    tq = q_ref.shape[3]
    tk = k_ref.shape[2]

    @pl.when(kj == 0)
    def _():
        m_sc[...] = jnp.full(m_sc.shape, -jnp.inf, _F32)
        acc_sc[...] = jnp.zeros(acc_sc.shape, _F32)

    s = lax.dot_general(q_ref[0, 0, 0], k_ref[0, 0], (((1,), (1,)), ((), ())),
                        preferred_element_type=_F32)
    if alibi:
        slope = slope_ref[pl.program_id(0) // heads_per_slope]
        qpos = pl.program_id(2) * tq + lax.broadcasted_iota(jnp.int32, (tq, tk), 0)
        kpos = kj * tk + lax.broadcasted_iota(jnp.int32, (tq, tk), 1)
        s = s - slope * jnp.abs(qpos - kpos).astype(_F32)
    m_prev = m_sc[...]
    m_new = jnp.maximum(m_prev, jnp.max(s, axis=-1, keepdims=True))
    alpha = jnp.exp(m_prev - m_new)
    p = jnp.exp(s - m_new)
    acc_sc[...] = alpha * acc_sc[...] + jnp.dot(p.astype(_BF16), v_ref[0, 0],
                                                 preferred_element_type=_F32)
    m_sc[...] = m_new

    @pl.when(kj == pl.num_programs(3) - 1)
    def _():
        acc = acc_sc[...]
        o_ref[0, 0] = acc[:, :LANES] / acc[:, LANES:LANES + 1]


def _flash_call(q, k, v, slopes, *, alibi, units_per_v, tq, tk):
    u, _, b, s, dk = q.shape
    kern = functools.partial(_flash_kernel, alibi=alibi, heads_per_slope=units_per_v)
    grid_spec = pltpu.PrefetchScalarGridSpec(
        num_scalar_prefetch=1,
        grid=(u, b, s // tq, s // tk),
        in_specs=[
            pl.BlockSpec((1, 1, 1, tq, dk), lambda ui, bi, qi, ki, sl: (ui, 0, bi, qi, 0)),
            pl.BlockSpec((1, 1, tk, dk), lambda ui, bi, qi, ki, sl: (ui, bi, ki, 0)),
            pl.BlockSpec((1, 1, tk, 256), lambda ui, bi, qi, ki, sl: (ui // units_per_v, bi, ki, 0)),
        ],
        out_specs=pl.BlockSpec((1, 1, tq, LANES), lambda ui, bi, qi, ki, sl: (ui, bi, qi, 0)),
        scratch_shapes=[pltpu.VMEM((tq, 1), _F32), pltpu.VMEM((tq, 256), _F32)],
    )
    return pl.pallas_call(
        kern,
        grid_spec=grid_spec,
        out_shape=jax.ShapeDtypeStruct((u, b, s, LANES), _F32),
        compiler_params=pltpu.CompilerParams(
            dimension_semantics=("parallel", "parallel", "parallel", "arbitrary"),
            vmem_limit_bytes=_VMEM_LIMIT),
        name="flash_a" if alibi else "flash_b",
    )(slopes, q, k, v)


def _attn_kernel(qt_ref, k_ref, vt_ref, o_ref, qv_sc, *, tq, q_steps, alibi):
    n_k, _, tk = vt_ref.shape[2:]
    q_sub = qt_ref.shape[1] * qt_ref.shape[3] // tq
    n_diag = tq // tk
    if alibi:
        head = pl.program_id(0) // 2
        sigma = jnp.float32(_alibi_slope(A_HEADS - 1) * LOG2E)
        for hd in range(A_HEADS - 1):
            sigma = jnp.where(head == hd, jnp.float32(_alibi_slope(hd) * LOG2E), sigma)
        ahead = (lax.broadcasted_iota(jnp.int32, (tk, tq), 0) - lax.broadcasted_iota(jnp.int32, (tk, tq), 1))
        aug_row = lax.broadcasted_iota(jnp.int32, (qt_ref.shape[2], tq), 0)
        aug_row = aug_row - jnp.where(pl.program_id(0) % 2 == 0, A_HEAD_DIM, 0)
        bias_rows = (aug_row >= _AUG_QHI) & (aug_row < _AUG_END)

    def query_block(sub, qi, slot):
        pieces = tq // qt_ref.shape[3]
        qt = jnp.concatenate([qt_ref[0, sub * pieces + piece] for piece in range(pieces)], axis=1)
        qv_sc[slot, 0] = qt
        if alibi:
            qv_sc[slot, 1] = jnp.where(bias_rows, -qt, qt)

        def block(var, j, correction=None):
            k = k_ref[0, 0, pl.ds(pl.multiple_of(j * tk, tk), tk), :]
            st = jnp.dot(k, qv_sc[slot, var], preferred_element_type=_F32)
            if correction is not None:
                st = st + correction
            return jnp.dot(vt_ref[0, 0, j], jnp.exp2(st).astype(_BF16), preferred_element_type=_F32)

        def accumulate(acc, part):
            return part if acc is None else acc + part

        acc = None
        if alibi:
            first_diag = qi * n_diag
            for d in range(n_diag):
                corr = (-2.0 * sigma) * jnp.maximum(ahead + d * tk, 0).astype(_F32)
                acc = accumulate(acc, block(0, first_diag + d, corr))
            for x in range(n_k - n_diag):
                after = (x >= first_diag) * 1
                acc = accumulate(acc, block(after, x + n_diag * after))
        else:
            for x in range(n_k):
                acc = accumulate(acc, block(0, x))
        rows = pl.ds(pl.multiple_of(sub * tq, tq), tq)
        o_ref[0, 0, rows, :] = (acc[:LANES] / acc[LANES:LANES + 1]).T.astype(o_ref.dtype)

    if q_steps == 1 or q_sub == 1:
        for sub in range(q_sub):
            query_block(sub, sub if q_steps == 1 else pl.program_id(2), sub)
    else:
        def body(sub, carry):
            query_block(sub, pl.program_id(2) * q_sub + sub, 0)
            return carry

        lax.fori_loop(0, q_sub, body, 0)


def _attn_call(qt, k, vt, *, alibi, units_per_v, tq):
    u, _, dk, tm = qt.shape
    _, b, s, _ = k.shape
    n_k, _, tk = vt.shape[2:]
    assert tq % tk == 0 and s % tq == 0 and tq % tm == 0, (s, tq, tk, tm)
    n_q = s // tq
    if n_q * n_k <= MAX_BLOCK_PAIRS:
        q_sub, slots = n_q, n_q
    else:
        q_sub, slots = _largest_group(n_q, MAX_LOOPED_QUERY_BLOCKS), 1
    q_steps = n_q // q_sub
    pieces = tq * q_sub // tm
    kern = functools.partial(_attn_kernel, tq=tq, q_steps=q_steps, alibi=alibi)
    return pl.pallas_call(
        kern,
        grid=(u, b, q_steps),
        in_specs=[
            pl.BlockSpec((1, pieces, dk, tm), lambda ui, bi, qi: (ui, bi * q_steps + qi, 0, 0)),
            pl.BlockSpec((1, 1, s, dk), lambda ui, bi, qi: (ui, bi, 0, 0)),
            pl.BlockSpec((1, 1, n_k, VT_ROWS, tk), lambda ui, bi, qi: (ui // units_per_v, bi, 0, 0, 0)),
        ],
        out_specs=pl.BlockSpec((1, 1, tq * q_sub, LANES), lambda ui, bi, qi: (ui, bi, qi, 0)),
        out_shape=jax.ShapeDtypeStruct((u, b, s, LANES), _BF16),
        scratch_shapes=[pltpu.VMEM((slots, 2 if alibi else 1, dk, tq), _BF16)],
        compiler_params=pltpu.CompilerParams(
            dimension_semantics=("parallel", "parallel", "parallel"),
            vmem_limit_bytes=_VMEM_LIMIT),
        name="attn_a" if alibi else "attn_b",
    )(qt, k, vt)


def _out_kernel(oa_ref, ob_ref, ga_ref, gb_ref, x_ref, lq1_ref, lk1_ref, lq2_ref, lk2_ref, subln_ref,
                wout_ref, y_ref, *, lam_init):
    lam = (jnp.exp(jnp.sum(lq1_ref[...] * lk1_ref[...], axis=-1, keepdims=True))
           - jnp.exp(jnp.sum(lq2_ref[...] * lk2_ref[...], axis=-1, keepdims=True)) + lam_init)
    ga = ga_ref[...].astype(_F32)
    gb = gb_ref[...].astype(_F32)
    pieces = []
    for hd in range(A_HEADS):
        d = oa_ref[2 * hd].astype(_F32) - lam * oa_ref[2 * hd + 1].astype(_F32)
        n = d * lax.rsqrt(jnp.mean(d * d, axis=-1, keepdims=True) + EPS) * subln_ref[...]
        pieces.append(n * (1.0 - lam_init) * ga[:, hd * LANES:(hd + 1) * LANES])
    for hd in range(B_HEADS):
        pieces.append(ob_ref[hd].astype(_F32) * gb[:, hd * LANES:(hd + 1) * LANES])
    y = jnp.concatenate(pieces, axis=1).astype(_BF16)
    y_ref[...] = x_ref[...] + jnp.dot(y, wout_ref[...], preferred_element_type=_F32)


def _out_call(oa, ob, ga, gb, x2d, lw, lam_init, tm):
    t = x2d.shape[0]
    const = lambda i: (0, 0)
    row = lambda i: (i, 0)
    unit_row = lambda i: (0, i, 0)
    return pl.pallas_call(
        functools.partial(_out_kernel, lam_init=lam_init),
        grid=(t // tm,),
        in_specs=[
            pl.BlockSpec((A_UNITS, tm, LANES), unit_row),
            pl.BlockSpec((B_HEADS, tm, LANES), unit_row),
            pl.BlockSpec((tm, 512), row),
            pl.BlockSpec((tm, 512), row),
            pl.BlockSpec((tm, D_MODEL), row),
            pl.BlockSpec((1, A_HEAD_DIM), const),
            pl.BlockSpec((1, A_HEAD_DIM), const),
            pl.BlockSpec((1, A_HEAD_DIM), const),
            pl.BlockSpec((1, A_HEAD_DIM), const),
            pl.BlockSpec((1, A_V_DIM), const),
            pl.BlockSpec((D_MODEL, D_MODEL), const),
        ],
        out_specs=pl.BlockSpec((tm, D_MODEL), row),
        out_shape=jax.ShapeDtypeStruct((t, D_MODEL), _F32),
        compiler_params=pltpu.CompilerParams(
            dimension_semantics=("parallel",), vmem_limit_bytes=_VMEM_LIMIT),
        name="out",
    )(oa, ob, ga, gb, x2d, lw["lq1"], lw["lk1"], lw["lq2"], lw["lk2"], lw["subln"], lw["w_out"])


def _rope_tables(seq_len):
    inv = ROPE_THETA ** (-jnp.arange(0, B_ROPE, 2, dtype=_F32) / B_ROPE)
    ang = jnp.arange(seq_len, dtype=_F32)[:, None] * inv[None, :]
    cos, sin = jnp.cos(ang), jnp.sin(ang)
    zeros = jnp.zeros((seq_len, LANES - B_ROPE), _F32)
    return (jnp.concatenate([cos, cos, zeros], axis=1),
            jnp.concatenate([-sin, sin, zeros], axis=1))


def _score_bounds(l, p):
    amax = lambda v: jnp.max(jnp.abs(v[l].astype(_F32)))
    bound_a = amax(p["a_q_norm"]) * amax(p["a_k_norm"]) * (A_HEAD_DIM ** 0.5)
    bound_b = amax(p["b_q_norm"]) * amax(p["b_k_norm"]) * (B_QK_DIM ** 0.5)
    return bound_a, bound_b


def _aug_rows(bound_a, bound_b):
    aug_q = np.zeros((A_UNITS, A_HEAD_DIM, LANES), np.float32)
    aug_k = np.zeros((A_UNITS, 1, LANES), np.float32)
    for u in range(A_UNITS):
        sig = _sigma_pieces(u // 2)
        base = _aug_base(u % 2)
        aug_k[u, 0, base + _AUG_SHIFT] = 1.0
        for a in range(SIGMA_PIECES):
            aug_k[u, 0, base + _AUG_QHI + a] = -sig[a]
            aug_k[u, 0, base + _AUG_QLO + a] = -sig[a]
            aug_q[u, _AUG_KHI + a, :] = sig[a]
            aug_q[u, _AUG_KLO + a, :] = sig[a]
    shift_row = (np.arange(A_HEAD_DIM) == _AUG_SHIFT).astype(np.float32).reshape(1, A_HEAD_DIM, 1)
    aug_q = jnp.asarray(aug_q) - (bound_a * LOG2E) * shift_row
    b_lane = (np.arange(LANES) == _B_SHIFT).astype(np.float32)
    aug_b = jnp.stack([-(bound_b * LOG2E) * b_lane, jnp.asarray(b_lane)])
    return aug_q, jnp.asarray(aug_k), aug_b


def _layer_weights(l, p, rope_c, rope_s):
    swap = _rope_swap_index()
    w = p["w_in"][l]
    kr = w[:, _SPLITS[6]:_SPLITS[7]]
    w_in = jnp.concatenate([w[:, :_SPLITS[6]], kr, kr[:, swap], w[:, _SPLITS[7]:]], axis=1).astype(_BF16)
    wuq = p["b_w_uq"][l].reshape(Q_LORA, B_HEADS, B_QK_DIM)
    w_uq = jnp.concatenate([wuq, wuq[:, :, B_NOPE:][:, :, swap]], axis=2).reshape(Q_LORA, B_HEADS * 256)
    gq = p["b_q_norm"][l]
    gk = p["b_k_norm"][l]
    row = lambda v: v.reshape(1, -1).astype(_F32)
    bound_a, bound_b = _score_bounds(l, p)
    aug_q, aug_k, aug_b = _aug_rows(bound_a, bound_b)
    return {
        "norm_w": row(p["norm_w"][l]),
        "w_in": w_in,
        "g_aq": row(jnp.tile(p["a_q_norm"][l], A_UNITS)),
        "g_ak": row(jnp.tile(p["a_k_norm"][l], A_UNITS)),
        "g_cq": row(p["b_cq_norm"][l]),
        "w_uq": w_uq.astype(_BF16),
        "g_ckv": row(p["b_ckv_norm"][l]),
        "w_ukv": p["b_w_ukv"][l].astype(_BF16),
        "g_bq": row(jnp.concatenate([gq, gq[B_NOPE:][swap]])),
        "g_bkn": row(gk[:B_NOPE]),
        "g_bkr": row(jnp.concatenate([gk[B_NOPE:], gk[B_NOPE:][swap]])),
        "rope_c": rope_c,
        "rope_s": rope_s,
        "aug_q": aug_q, "aug_k": aug_k, "aug_b": aug_b,
        "lq1": row(p["a_lq1"][l]), "lk1": row(p["a_lk1"][l]),
        "lq2": row(p["a_lq2"][l]), "lk2": row(p["a_lk2"][l]),
        "subln": row(p["a_subln"][l]),
        "w_out": p["w_out"][l].astype(_BF16),
    }


def _tile(n, pref):
    return pref if n % pref == 0 else n


def _largest_group(n, cap):
    return max(g for g in range(1, cap + 1) if n % g == 0) if n > 0 else 1


def _encoder_layer(x, lw, lam_init, slopes, fixed_shift):
    b, s, _ = x.shape
    t = b * s
    x2d = x.reshape(t, D_MODEL)
    tm = _tile(s, FIXED_TK)
    qa, ka, va, ga, qb, kb, vb, gb = _proj_call(x2d, s, lw, tm, fixed_shift)
    split = lambda a: a.reshape(a.shape[:-2] + (b, s, a.shape[-1]))
    tq = tk = _tile(s, 512)
    if fixed_shift:
        assert s <= POS_SPLIT * POS_SPLIT * 2
        ftq = _tile(s, FIXED_TQ)
        split_t = lambda a: a.reshape((a.shape[0], b, s // tm) + a.shape[2:])
        oa = _attn_call(qa, split(ka), split_t(va), alibi=True, units_per_v=2, tq=ftq)
        ob = _attn_call(qb, split(kb), split_t(vb), alibi=False, units_per_v=1, tq=ftq)
    else:
        oa = _flash_call(split(qa), split(ka), split(va), slopes, alibi=True, units_per_v=2, tq=tq, tk=tk)
        ob = _flash_call(split(qb), split(kb), split(vb), slopes, alibi=False, units_per_v=1, tq=tq, tk=tk)
    y = _out_call(oa.reshape(A_UNITS, t, LANES), ob.reshape(B_HEADS, t, LANES), ga, gb, x2d, lw, lam_init, tm)
    return y.reshape(b, s, D_MODEL)


def _forward(x_prompt, x_sample, p, fixed_shift):
    depth = p["norm_w"].shape[0]
    rope_c, rope_s = {}, {}
    for s in {x_prompt.shape[1], x_sample.shape[1]}:
        rope_c[s], rope_s[s] = _rope_tables(s)
    slopes = jnp.asarray([_alibi_slope(hd) for hd in range(A_HEADS)], _F32)
    y_prompt, y_sample = x_prompt, x_sample
    for l in range(depth):
        lw = _layer_weights(l, p, rope_c, rope_s)
        lam_init = _lambda_init(l)
        y_prompt = _encoder_layer(y_prompt, lw, lam_init, slopes, fixed_shift)
        y_sample = _encoder_layer(y_sample, lw, lam_init, slopes, fixed_shift)
    return (y_prompt, y_sample)


def kernel(x_prompt, x_sample, norm_w, w_in, a_q_norm, a_k_norm, a_lq1, a_lk1, a_lq2, a_lk2, a_subln,
           b_cq_norm, b_w_uq, b_ckv_norm, b_w_ukv, b_q_norm, b_k_norm, w_out):
    p = dict(norm_w=norm_w, w_in=w_in, a_q_norm=a_q_norm, a_k_norm=a_k_norm, a_lq1=a_lq1, a_lk1=a_lk1,
             a_lq2=a_lq2, a_lk2=a_lk2, a_subln=a_subln, b_cq_norm=b_cq_norm, b_w_uq=b_w_uq,
             b_ckv_norm=b_ckv_norm, b_w_ukv=b_w_ukv, b_q_norm=b_q_norm, b_k_norm=b_k_norm, w_out=w_out)
    bounds = jnp.stack([jnp.stack(_score_bounds(l, p)) for l in range(norm_w.shape[0])])
    shift_ok = jnp.max(bounds) <= MAX_STATIC_SHIFT
    return lax.cond(shift_ok,
                    lambda xp, xs, pp: _forward(xp, xs, pp, True),
                    lambda xp, xs, pp: _forward(xp, xs, pp, False),
                    x_prompt, x_sample, p)
```

```python
import functools
import math

import numpy as np
import jax
import jax.numpy as jnp
from jax import lax
from jax.experimental import pallas as pl
from jax.experimental.pallas import tpu as pltpu

D_MODEL = 1024
A_HEADS = 4
A_HEAD_DIM = 64
A_V_DIM = 128
A_UNITS = 2 * A_HEADS
B_HEADS = 4
B_NOPE = 128
B_ROPE = 64
B_QK_DIM = B_NOPE + B_ROPE
B_V_DIM = 128
Q_LORA = 256
KV_LORA = 128
ROPE_THETA = 10000.0
EPS = 1e-6
LANES = 128

_SPLITS = np.cumsum([0, 512, 512, 512, 512, Q_LORA, KV_LORA, B_ROPE, 512])
_P_AQ, _P_AK, _P_AV, _P_AG, _P_CQ, _P_CKV, _P_KR, _P_BG, _P_END = np.cumsum(
    [0, 512, 512, 512, 512, Q_LORA, KV_LORA, 2 * B_ROPE, 512])

_VMEM_LIMIT = 56 * 1024 * 1024
LOG2E = math.log2(math.e)
MAX_STATIC_SHIFT = 32.0
POS_SPLIT = 128
SIGMA_PIECES = 3
FIXED_TQ = 1024
FIXED_TK = 512
MAX_BLOCK_PAIRS = 32
MAX_LOOPED_QUERY_BLOCKS = 4
OUT_ROWS = 1024
VT_ROWS = 144

_AUG_SHIFT = 0
_AUG_QHI = _AUG_SHIFT + 1
_AUG_QLO = _AUG_QHI + SIGMA_PIECES
_AUG_KHI = _AUG_QLO + SIGMA_PIECES
_AUG_KLO = _AUG_KHI + SIGMA_PIECES
_AUG_END = _AUG_KLO + SIGMA_PIECES
_B_SHIFT = B_QK_DIM - LANES


def _aug_base(comp):
    return A_HEAD_DIM if comp == 0 else 0


_F32 = jnp.float32
_BF16 = jnp.bfloat16


def _lambda_init(layer_idx):
    return 0.8 - 0.6 * math.exp(-0.3 * layer_idx)


def _alibi_slope(head):
    return 2.0 ** (-8.0 * (head + 1) / A_HEADS)


def _sigma_pieces(head):
    rest = _alibi_slope(head) * LOG2E
    pieces = []
    for _ in range(SIGMA_PIECES):
        p = float(np.asarray(rest, np.float32).astype(_BF16).astype(np.float32))
        pieces.append(p)
        rest -= p
    return pieces


def _group_sum_matrices():
    unit = np.arange(256) // A_HEAD_DIM
    m_a = (unit[:, None] == unit[None, :]).astype(np.float32)
    m_b = np.zeros((2, 256, 256), np.float32)
    m_b[0, :B_QK_DIM, :] = 1.0
    m_b[1, :B_NOPE, :] = 1.0
    return jnp.asarray(m_a, _BF16), jnp.asarray(m_b, _BF16)


def _rope_swap_index():
    half = B_ROPE // 2
    return np.concatenate([np.arange(half, B_ROPE), np.arange(0, half)])


def _silu(x):
    return x / (1.0 + jnp.exp(-x))


def _proj_kernel(x_ref, nw_ref, win_ref, gaq_ref, gak_ref, gcq_ref, wuq_ref, gckv_ref, wukv_ref,
                 gbq_ref, gbkn_ref, gbkr_ref, ct_ref, st_ref, augq_ref, augk_ref, augb_ref, suma_ref, sumb_ref,
                 qa_ref, ka_ref, va_ref, ga_ref, qb_ref, kb_ref, vb_ref, gb_ref, *, fixed_shift, seq_len):
    tm = x_ref.shape[0]
    rows = slice(0, tm)
    x = x_ref[rows, :]
    h = x * lax.rsqrt(jnp.mean(x * x, axis=-1, keepdims=True) + EPS) * nw_ref[...]
    hb = h.astype(_BF16)

    def proj(lo, hi):
        return jnp.dot(hb, win_ref[:, lo:hi], preferred_element_type=_F32)

    lane = lax.broadcasted_iota(jnp.int32, (tm, LANES), 1)
    low_half = lane < B_ROPE
    ones_col = (lane == 0).astype(_F32)
    ct = ct_ref[rows, :]
    st = st_ref[rows, :]

    def aug_select(index, base, hi, lo):
        at = lambda off: (index >= base + off) & (index < base + off + SIGMA_PIECES)
        return at(hi), at(lo)

    if fixed_shift:
        a_qscale = (A_HEAD_DIM ** -0.5) * LOG2E
        b_qscale = (B_QK_DIM ** -0.5) * LOG2E
        first = (pl.program_id(0) % (seq_len // tm)) * tm - seq_len // 2
        split_pos = lambda p: (((p >> 7) * POS_SPLIT).astype(_F32), (p & (POS_SPLIT - 1)).astype(_F32))
        k_hi, k_lo = split_pos(first + lax.broadcasted_iota(jnp.int32, (tm, LANES), 0))
        q_hi, q_lo = split_pos(first + lax.broadcasted_iota(jnp.int32, (A_HEAD_DIM, tm), 1))
        aug_row = lax.broadcasted_iota(jnp.int32, (A_HEAD_DIM, tm), 0)
        at_hi, at_lo = aug_select(aug_row, 0, _AUG_QHI, _AUG_QLO)
        q_pos = jnp.where(at_hi, q_hi, jnp.where(at_lo, q_lo, 0.0))
    else:
        a_qscale = A_HEAD_DIM ** -0.5
        b_qscale = B_QK_DIM ** -0.5

    def group_sums(a, m):
        return jnp.dot((a * a).astype(_BF16), m, preferred_element_type=_F32)

    aq = proj(_P_AQ, _P_AK)
    ak = proj(_P_AK, _P_AV)
    def unit_sums(a):
        return jnp.concatenate([group_sums(a[:, lo:lo + 256], suma_ref[...]) for lo in range(0, a.shape[1], 256)],
                               axis=1)

    qn_all = aq * lax.rsqrt(unit_sums(aq) * (1.0 / A_HEAD_DIM) + EPS) * gaq_ref[...] * a_qscale
    kn_all = ak * lax.rsqrt(unit_sums(ak) * (1.0 / A_HEAD_DIM) + EPS) * gak_ref[...]
    for hd in range(A_HEADS):
        qn2 = qn_all[:, hd * LANES:(hd + 1) * LANES]
        kn2 = kn_all[:, hd * LANES:(hd + 1) * LANES]
        if fixed_shift:
            qn2_t = qn2.T
        for comp in range(2):
            u = 2 * hd + comp
            own = (lane < A_HEAD_DIM) if comp == 0 else (lane >= A_HEAD_DIM)
            if fixed_shift:
                feat = qn2_t[comp * A_HEAD_DIM:(comp + 1) * A_HEAD_DIM]
                aug = jnp.concatenate([augq_ref[u]] * (tm // LANES), axis=1) + q_pos
                stacked = [feat, aug] if comp == 0 else [aug, feat]
                qa_ref[u, 0, :, rows] = jnp.concatenate(stacked, axis=0).astype(_BF16)
                at_hi, at_lo = aug_select(lane, _aug_base(comp), _AUG_KHI, _AUG_KLO)
                k_pos = jnp.where(at_hi, k_hi, jnp.where(at_lo, k_lo, 0.0))
                ka_ref[u, rows, :] = (jnp.where(own, kn2, 0.0) + augk_ref[u] + k_pos).astype(_BF16)
            else:
                qa_ref[u, 0, rows, :] = jnp.where(own, qn2, 0.0).astype(_BF16)
                ka_ref[u, rows, :] = jnp.where(own, kn2, 0.0).astype(_BF16)

    def store_values(v_ref, hd, vv):
        if fixed_shift:
            pad_rows = lax.broadcasted_iota(jnp.int32, (VT_ROWS - LANES, tm), 0)
            v_ref[hd, 0, :, rows] = jnp.concatenate([vv.T, (pad_rows == 0).astype(_F32)], axis=0).astype(_BF16)
        else:
            v_ref[hd, rows, :] = jnp.concatenate([vv, ones_col], axis=1).astype(_BF16)

    av = proj(_P_AV, _P_AG)
    for hd in range(A_HEADS):
        store_values(va_ref, hd, av[:, hd * LANES:(hd + 1) * LANES])
    ga_ref[rows, :] = _silu(proj(_P_AG, _P_CQ)).astype(_BF16)

    cq = proj(_P_CQ, _P_CKV)
    cqn = cq * lax.rsqrt(jnp.mean(cq * cq, axis=-1, keepdims=True) + EPS) * gcq_ref[...]
    qall = jnp.dot(cqn.astype(_BF16), wuq_ref[...], preferred_element_type=_F32)
    ckv = proj(_P_CKV, _P_KR)
    ckvn = ckv * lax.rsqrt(jnp.mean(ckv * ckv, axis=-1, keepdims=True) + EPS) * gckv_ref[...]
    kv = jnp.dot(ckvn.astype(_BF16), wukv_ref[...], preferred_element_type=_F32)
    kr2 = proj(_P_KR, _P_BG)

    for hd in range(B_HEADS):
        q0 = qall[:, hd * 256:hd * 256 + LANES]
        q1 = qall[:, hd * 256 + LANES:(hd + 1) * 256]
        r = lax.rsqrt(group_sums(qall[:, hd * 256:(hd + 1) * 256], sumb_ref[0]) * (1.0 / B_QK_DIM) + EPS)
        q0n = q0 * r[:, :LANES] * gbq_ref[:, :LANES] * b_qscale
        q1n = q1 * r[:, LANES:] * gbq_ref[:, LANES:]
        q1r = (q1n * ct + pltpu.roll(q1n, B_ROPE, 1) * st) * b_qscale
        if fixed_shift:
            q1r = q1r + augb_ref[0:1, :]
            qb_ref[hd, 0, :, rows] = jnp.concatenate([q0n.T, q1r.T], axis=0).astype(_BF16)
        else:
            qb_ref[hd, 0, rows, :] = jnp.concatenate([q0n, q1r], axis=1).astype(_BF16)

    kr_ss = jnp.sum(jnp.where(low_half, kr2 * kr2, 0.0), axis=-1, keepdims=True)
    krg = kr2 * gbkr_ref[...]
    krr = krg * ct + pltpu.roll(krg, B_ROPE, 1) * st
    for hd in range(B_HEADS):
        kn = kv[:, hd * 256:hd * 256 + LANES]
        vv = kv[:, hd * 256 + LANES:(hd + 1) * 256]
        kn_ss = group_sums(kv[:, hd * 256:(hd + 1) * 256], sumb_ref[1])[:, :LANES]
        r = lax.rsqrt((kn_ss + kr_ss) * (1.0 / B_QK_DIM) + EPS)
        k1 = krr * r
        if fixed_shift:
            k1 = k1 + augb_ref[1:2, :]
        kb_ref[hd, rows, :] = jnp.concatenate([kn * r * gbkn_ref[...], k1], axis=1).astype(_BF16)
        store_values(vb_ref, hd, vv)
    gb_ref[rows, :] = _silu(proj(_P_BG, _P_END)).astype(_BF16)


def _proj_call(x2d, seq_len, lw, tm, fixed_shift):
    t = x2d.shape[0]
    blocks_per_seq = seq_len // tm
    const = lambda i: (0, 0)
    const3 = lambda i: (0, 0, 0)
    row = lambda i: (i, 0)
    unit_row = lambda i: (0, i, 0)
    var_row = lambda i: (0, 0, i, 0)
    rope_row = lambda i: (i % blocks_per_seq, 0)
    in_specs = [
        pl.BlockSpec((tm, D_MODEL), row),
        pl.BlockSpec((1, D_MODEL), const),
        pl.BlockSpec((D_MODEL, int(_P_END)), const),
        pl.BlockSpec((1, A_UNITS * A_HEAD_DIM), const),
        pl.BlockSpec((1, A_UNITS * A_HEAD_DIM), const),
        pl.BlockSpec((1, Q_LORA), const),
        pl.BlockSpec((Q_LORA, B_HEADS * 256), const),
        pl.BlockSpec((1, KV_LORA), const),
        pl.BlockSpec((KV_LORA, B_HEADS * 256), const),
        pl.BlockSpec((1, 256), const),
        pl.BlockSpec((1, LANES), const),
        pl.BlockSpec((1, LANES), const),
        pl.BlockSpec((tm, LANES), rope_row),
        pl.BlockSpec((tm, LANES), rope_row),
        pl.BlockSpec((A_UNITS, A_HEAD_DIM, LANES), const3),
        pl.BlockSpec((A_UNITS, 1, LANES), const3),
        pl.BlockSpec((2, LANES), const),
        pl.BlockSpec((256, 256), const),
        pl.BlockSpec((2, 256, 256), const3),
    ]
    if fixed_shift:
        q_shape = lambda units, dk: jax.ShapeDtypeStruct((units, t // tm, dk, tm), _BF16)
        q_spec = lambda units, dk: pl.BlockSpec((units, 1, dk, tm), lambda i: (0, i, 0, 0))
        v_shape = lambda heads: jax.ShapeDtypeStruct((heads, t // tm, VT_ROWS, tm), _BF16)
        v_spec = lambda heads: pl.BlockSpec((heads, 1, VT_ROWS, tm), lambda i: (0, i, 0, 0))
    else:
        q_shape = lambda units, dk: jax.ShapeDtypeStruct((units, 1, t, dk), _BF16)
        q_spec = lambda units, dk: pl.BlockSpec((units, 1, tm, dk), var_row)
        v_shape = lambda heads: jax.ShapeDtypeStruct((heads, t, 256), _BF16)
        v_spec = lambda heads: pl.BlockSpec((heads, tm, 256), unit_row)
    out_shape = [
        q_shape(A_UNITS, LANES),
        jax.ShapeDtypeStruct((A_UNITS, t, LANES), _BF16),
        v_shape(A_HEADS),
        jax.ShapeDtypeStruct((t, 512), _BF16),
        q_shape(B_HEADS, 256),
        jax.ShapeDtypeStruct((B_HEADS, t, 256), _BF16),
        v_shape(B_HEADS),
        jax.ShapeDtypeStruct((t, 512), _BF16),
    ]
    out_specs = [
        q_spec(A_UNITS, LANES),
        pl.BlockSpec((A_UNITS, tm, LANES), unit_row),
        v_spec(A_HEADS),
        pl.BlockSpec((tm, 512), row),
        q_spec(B_HEADS, 256),
        pl.BlockSpec((B_HEADS, tm, 256), unit_row),
        v_spec(B_HEADS),
        pl.BlockSpec((tm, 512), row),
    ]
    return pl.pallas_call(
        functools.partial(_proj_kernel, fixed_shift=fixed_shift, seq_len=seq_len),
        grid=(t // tm,),
        in_specs=in_specs,
        out_specs=out_specs,
        out_shape=out_shape,
        compiler_params=pltpu.CompilerParams(
            dimension_semantics=("parallel",), vmem_limit_bytes=_VMEM_LIMIT),
        name="proj",
    )(x2d, lw["norm_w"], lw["w_in"], lw["g_aq"], lw["g_ak"], lw["g_cq"], lw["w_uq"], lw["g_ckv"],
      lw["w_ukv"], lw["g_bq"], lw["g_bkn"], lw["g_bkr"], lw["rope_c"][seq_len], lw["rope_s"][seq_len],
      lw["aug_q"], lw["aug_k"], lw["aug_b"], *_group_sum_matrices())


def _flash_kernel(slope_ref, q_ref, k_ref, v_ref, o_ref, m_sc, acc_sc, *, alibi, heads_per_slope):
    kj = pl.program_id(3)
    tq = q_ref.shape[3]
    tk = k_ref.shape[2]

    @pl.when(kj == 0)
    def _():
        m_sc[...] = jnp.full(m_sc.shape, -jnp.inf, _F32)
        acc_sc[...] = jnp.zeros(acc_sc.shape, _F32)

    s = lax.dot_general(q_ref[0, 0, 0], k_ref[0, 0], (((1,), (1,)), ((), ())),
                        preferred_element_type=_F32)
    if alibi:
        slope = slope_ref[pl.program_id(0) // heads_per_slope]
        qpos = pl.program_id(2) * tq + lax.broadcasted_iota(jnp.int32, (tq, tk), 0)
        kpos = kj * tk + lax.broadcasted_iota(jnp.int32, (tq, tk), 1)
        s = s - slope * jnp.abs(qpos - kpos).astype(_F32)
    m_prev = m_sc[...]
    m_new = jnp.maximum(m_prev, jnp.max(s, axis=-1, keepdims=True))
    alpha = jnp.exp(m_prev - m_new)
    p = jnp.exp(s - m_new)
    acc_sc[...] = alpha * acc_sc[...] + jnp.dot(p.astype(_BF16), v_ref[0, 0],
                                                 preferred_element_type=_F32)
    m_sc[...] = m_new

    @pl.when(kj == pl.num_programs(3) - 1)
    def _():
        acc = acc_sc[...]
        o_ref[0, 0] = acc[:, :LANES] / acc[:, LANES:LANES + 1]


def _flash_call(q, k, v, slopes, *, alibi, units_per_v, tq, tk):
    u, _, b, s, dk = q.shape
    kern = functools.partial(_flash_kernel, alibi=alibi, heads_per_slope=units_per_v)
    grid_spec = pltpu.PrefetchScalarGridSpec(
        num_scalar_prefetch=1,
        grid=(u, b, s // tq, s // tk),
        in_specs=[
            pl.BlockSpec((1, 1, 1, tq, dk), lambda ui, bi, qi, ki, sl: (ui, 0, bi, qi, 0)),
            pl.BlockSpec((1, 1, tk, dk), lambda ui, bi, qi, ki, sl: (ui, bi, ki, 0)),
            pl.BlockSpec((1, 1, tk, 256), lambda ui, bi, qi, ki, sl: (ui // units_per_v, bi, ki, 0)),
        ],
        out_specs=pl.BlockSpec((1, 1, tq, LANES), lambda ui, bi, qi, ki, sl: (ui, bi, qi, 0)),
        scratch_shapes=[pltpu.VMEM((tq, 1), _F32), pltpu.VMEM((tq, 256), _F32)],
    )
    return pl.pallas_call(
        kern,
        grid_spec=grid_spec,
        out_shape=jax.ShapeDtypeStruct((u, b, s, LANES), _F32),
        compiler_params=pltpu.CompilerParams(
            dimension_semantics=("parallel", "parallel", "parallel", "arbitrary"),
            vmem_limit_bytes=_VMEM_LIMIT),
        name="flash_a" if alibi else "flash_b",
    )(slopes, q, k, v)


def _attn_kernel(qt_ref, k_ref, vt_ref, o_ref, qv_sc, *, tq, q_steps, alibi):
    n_k, _, tk = vt_ref.shape[2:]
    q_sub = qt_ref.shape[1] * qt_ref.shape[3] // tq
    n_diag = tq // tk
    if alibi:
        head = pl.program_id(0) // 2
        sigma = jnp.float32(_alibi_slope(A_HEADS - 1) * LOG2E)
        for hd in range(A_HEADS - 1):
            sigma = jnp.where(head == hd, jnp.float32(_alibi_slope(hd) * LOG2E), sigma)
        ahead = (lax.broadcasted_iota(jnp.int32, (tk, tq), 0) - lax.broadcasted_iota(jnp.int32, (tk, tq), 1))
        aug_row = lax.broadcasted_iota(jnp.int32, (qt_ref.shape[2], tq), 0)
        aug_row = aug_row - jnp.where(pl.program_id(0) % 2 == 0, A_HEAD_DIM, 0)
        bias_rows = (aug_row >= _AUG_QHI) & (aug_row < _AUG_END)

    def query_block(sub, qi, slot):
        pieces = tq // qt_ref.shape[3]
        qt = jnp.concatenate([qt_ref[0, sub * pieces + piece] for piece in range(pieces)], axis=1)
        qv_sc[slot, 0] = qt
        if alibi:
            qv_sc[slot, 1] = jnp.where(bias_rows, -qt, qt)

        def block(var, j, correction=None):
            k = k_ref[0, 0, pl.ds(pl.multiple_of(j * tk, tk), tk), :]
            st = jnp.dot(k, qv_sc[slot, var], preferred_element_type=_F32)
            if correction is not None:
                st = st + correction
            return jnp.dot(vt_ref[0, 0, j], jnp.exp2(st).astype(_BF16), preferred_element_type=_F32)

        def accumulate(acc, part):
            return part if acc is None else acc + part

        acc = None
        if alibi:
            first_diag = qi * n_diag
            for d in range(n_diag):
                corr = (-2.0 * sigma) * jnp.maximum(ahead + d * tk, 0).astype(_F32)
                acc = accumulate(acc, block(0, first_diag + d, corr))
            for x in range(n_k - n_diag):
                after = (x >= first_diag) * 1
                acc = accumulate(acc, block(after, x + n_diag * after))
        else:
            for x in range(n_k):
                acc = accumulate(acc, block(0, x))
        rows = pl.ds(pl.multiple_of(sub * tq, tq), tq)
        o_ref[0, 0, rows, :] = (acc[:LANES] / acc[LANES:LANES + 1]).T.astype(o_ref.dtype)

    if q_steps == 1 or q_sub == 1:
        for sub in range(q_sub):
            query_block(sub, sub if q_steps == 1 else pl.program_id(2), sub)
    else:
        def body(sub, carry):
            query_block(sub, pl.program_id(2) * q_sub + sub, 0)
            return carry

        lax.fori_loop(0, q_sub, body, 0)


def _attn_call(qt, k, vt, *, alibi, units_per_v, tq):
    u, _, dk, tm = qt.shape
    _, b, s, _ = k.shape
    n_k, _, tk = vt.shape[2:]
    assert tq % tk == 0 and s % tq == 0 and tq % tm == 0, (s, tq, tk, tm)
    n_q = s // tq
    if n_q * n_k <= MAX_BLOCK_PAIRS:
        q_sub, slots = n_q, n_q
    else:
        q_sub, slots = _largest_group(n_q, MAX_LOOPED_QUERY_BLOCKS), 1
    q_steps = n_q // q_sub
    pieces = tq * q_sub // tm
    kern = functools.partial(_attn_kernel, tq=tq, q_steps=q_steps, alibi=alibi)
    return pl.pallas_call(
        kern,
        grid=(u, b, q_steps),
        in_specs=[
            pl.BlockSpec((1, pieces, dk, tm), lambda ui, bi, qi: (ui, bi * q_steps + qi, 0, 0)),
            pl.BlockSpec((1, 1, s, dk), lambda ui, bi, qi: (ui, bi, 0, 0)),
            pl.BlockSpec((1, 1, n_k, VT_ROWS, tk), lambda ui, bi, qi: (ui // units_per_v, bi, 0, 0, 0)),
        ],
        out_specs=pl.BlockSpec((1, 1, tq * q_sub, LANES), lambda ui, bi, qi: (ui, bi, qi, 0)),
        out_shape=jax.ShapeDtypeStruct((u, b, s, LANES), _BF16),
        scratch_shapes=[pltpu.VMEM((slots, 2 if alibi else 1, dk, tq), _BF16)],
        compiler_params=pltpu.CompilerParams(
            dimension_semantics=("parallel", "parallel", "parallel"),
            vmem_limit_bytes=_VMEM_LIMIT),
        name="attn_a" if alibi else "attn_b",
    )(qt, k, vt)


def _out_kernel(oa_ref, ob_ref, ga_ref, gb_ref, x_ref, lq1_ref, lk1_ref, lq2_ref, lk2_ref, subln_ref,
                wout_ref, y_ref, *, lam_init):
    lam = (jnp.exp(jnp.sum(lq1_ref[...] * lk1_ref[...], axis=-1, keepdims=True))
           - jnp.exp(jnp.sum(lq2_ref[...] * lk2_ref[...], axis=-1, keepdims=True)) + lam_init)
    ga = ga_ref[...].astype(_F32)
    gb = gb_ref[...].astype(_F32)
    pieces = []
    for hd in range(A_HEADS):
        d = oa_ref[2 * hd].astype(_F32) - lam * oa_ref[2 * hd + 1].astype(_F32)
        n = d * lax.rsqrt(jnp.mean(d * d, axis=-1, keepdims=True) + EPS) * subln_ref[...]
        pieces.append(n * (1.0 - lam_init) * ga[:, hd * LANES:(hd + 1) * LANES])
    for hd in range(B_HEADS):
        pieces.append(ob_ref[hd].astype(_F32) * gb[:, hd * LANES:(hd + 1) * LANES])
    y = jnp.concatenate(pieces, axis=1).astype(_BF16)
    y_ref[...] = x_ref[...] + jnp.dot(y, wout_ref[...], preferred_element_type=_F32)


def _out_call(oa, ob, ga, gb, x2d, lw, lam_init, tm):
    t = x2d.shape[0]
    const = lambda i: (0, 0)
    row = lambda i: (i, 0)
    unit_row = lambda i: (0, i, 0)
    return pl.pallas_call(
        functools.partial(_out_kernel, lam_init=lam_init),
        grid=(t // tm,),
        in_specs=[
            pl.BlockSpec((A_UNITS, tm, LANES), unit_row),
            pl.BlockSpec((B_HEADS, tm, LANES), unit_row),
            pl.BlockSpec((tm, 512), row),
            pl.BlockSpec((tm, 512), row),
            pl.BlockSpec((tm, D_MODEL), row),
            pl.BlockSpec((1, A_HEAD_DIM), const),
            pl.BlockSpec((1, A_HEAD_DIM), const),
            pl.BlockSpec((1, A_HEAD_DIM), const),
            pl.BlockSpec((1, A_HEAD_DIM), const),
            pl.BlockSpec((1, A_V_DIM), const),
            pl.BlockSpec((D_MODEL, D_MODEL), const),
        ],
        out_specs=pl.BlockSpec((tm, D_MODEL), row),
        out_shape=jax.ShapeDtypeStruct((t, D_MODEL), _F32),
        compiler_params=pltpu.CompilerParams(
            dimension_semantics=("parallel",), vmem_limit_bytes=_VMEM_LIMIT),
        name="out",
    )(oa, ob, ga, gb, x2d, lw["lq1"], lw["lk1"], lw["lq2"], lw["lk2"], lw["subln"], lw["w_out"])


def _rope_tables(seq_len):
    inv = ROPE_THETA ** (-jnp.arange(0, B_ROPE, 2, dtype=_F32) / B_ROPE)
    ang = jnp.arange(seq_len, dtype=_F32)[:, None] * inv[None, :]
    cos, sin = jnp.cos(ang), jnp.sin(ang)
    zeros = jnp.zeros((seq_len, LANES - B_ROPE), _F32)
    return (jnp.concatenate([cos, cos, zeros], axis=1),
            jnp.concatenate([-sin, sin, zeros], axis=1))


def _score_bounds(l, p):
    amax = lambda v: jnp.max(jnp.abs(v[l].astype(_F32)))
    bound_a = amax(p["a_q_norm"]) * amax(p["a_k_norm"]) * (A_HEAD_DIM ** 0.5)
    bound_b = amax(p["b_q_norm"]) * amax(p["b_k_norm"]) * (B_QK_DIM ** 0.5)
    return bound_a, bound_b


def _aug_rows(bound_a, bound_b):
    aug_q = np.zeros((A_UNITS, A_HEAD_DIM, LANES), np.float32)
    aug_k = np.zeros((A_UNITS, 1, LANES), np.float32)
    for u in range(A_UNITS):
        sig = _sigma_pieces(u // 2)
        base = _aug_base(u % 2)
        aug_k[u, 0, base + _AUG_SHIFT] = 1.0
        for a in range(SIGMA_PIECES):
            aug_k[u, 0, base + _AUG_QHI + a] = -sig[a]
            aug_k[u, 0, base + _AUG_QLO + a] = -sig[a]
            aug_q[u, _AUG_KHI + a, :] = sig[a]
            aug_q[u, _AUG_KLO + a, :] = sig[a]
    shift_row = (np.arange(A_HEAD_DIM) == _AUG_SHIFT).astype(np.float32).reshape(1, A_HEAD_DIM, 1)
    aug_q = jnp.asarray(aug_q) - (bound_a * LOG2E) * shift_row
    b_lane = (np.arange(LANES) == _B_SHIFT).astype(np.float32)
    aug_b = jnp.stack([-(bound_b * LOG2E) * b_lane, jnp.asarray(b_lane)])
    return aug_q, jnp.asarray(aug_k), aug_b


def _layer_weights(l, p, rope_c, rope_s):
    swap = _rope_swap_index()
    w = p["w_in"][l]
    kr = w[:, _SPLITS[6]:_SPLITS[7]]
    w_in = jnp.concatenate([w[:, :_SPLITS[6]], kr, kr[:, swap], w[:, _SPLITS[7]:]], axis=1).astype(_BF16)
    wuq = p["b_w_uq"][l].reshape(Q_LORA, B_HEADS, B_QK_DIM)
    w_uq = jnp.concatenate([wuq, wuq[:, :, B_NOPE:][:, :, swap]], axis=2).reshape(Q_LORA, B_HEADS * 256)
    gq = p["b_q_norm"][l]
    gk = p["b_k_norm"][l]
    row = lambda v: v.reshape(1, -1).astype(_F32)
    bound_a, bound_b = _score_bounds(l, p)
    aug_q, aug_k, aug_b = _aug_rows(bound_a, bound_b)
    return {
        "norm_w": row(p["norm_w"][l]),
        "w_in": w_in,
        "g_aq": row(jnp.tile(p["a_q_norm"][l], A_UNITS)),
        "g_ak": row(jnp.tile(p["a_k_norm"][l], A_UNITS)),
        "g_cq": row(p["b_cq_norm"][l]),
        "w_uq": w_uq.astype(_BF16),
        "g_ckv": row(p["b_ckv_norm"][l]),
        "w_ukv": p["b_w_ukv"][l].astype(_BF16),
        "g_bq": row(jnp.concatenate([gq, gq[B_NOPE:][swap]])),
        "g_bkn": row(gk[:B_NOPE]),
        "g_bkr": row(jnp.concatenate([gk[B_NOPE:], gk[B_NOPE:][swap]])),
        "rope_c": rope_c,
        "rope_s": rope_s,
        "aug_q": aug_q, "aug_k": aug_k, "aug_b": aug_b,
        "lq1": row(p["a_lq1"][l]), "lk1": row(p["a_lk1"][l]),
        "lq2": row(p["a_lq2"][l]), "lk2": row(p["a_lk2"][l]),
        "subln": row(p["a_subln"][l]),
        "w_out": p["w_out"][l].astype(_BF16),
    }


def _tile(n, pref):
    return pref if n % pref == 0 else n


def _largest_group(n, cap):
    return max(g for g in range(1, cap + 1) if n % g == 0) if n > 0 else 1


def _encoder_layer(x, lw, lam_init, slopes, fixed_shift):
    b, s, _ = x.shape
    t = b * s
    x2d = x.reshape(t, D_MODEL)
    tm = _tile(s, FIXED_TK)
    qa, ka, va, ga, qb, kb, vb, gb = _proj_call(x2d, s, lw, tm, fixed_shift)
    split = lambda a: a.reshape(a.shape[:-2] + (b, s, a.shape[-1]))
    tq = tk = _tile(s, 512)
    if fixed_shift:
        assert s <= POS_SPLIT * POS_SPLIT * 2
        ftq = _tile(s, FIXED_TQ)
        split_t = lambda a: a.reshape((a.shape[0], b, s // tm) + a.shape[2:])
        oa = _attn_call(qa, split(ka), split_t(va), alibi=True, units_per_v=2, tq=ftq)
        ob = _attn_call(qb, split(kb), split_t(vb), alibi=False, units_per_v=1, tq=ftq)
    else:
        oa = _flash_call(split(qa), split(ka), split(va), slopes, alibi=True, units_per_v=2, tq=tq, tk=tk)
        ob = _flash_call(split(qb), split(kb), split(vb), slopes, alibi=False, units_per_v=1, tq=tq, tk=tk)
    y = _out_call(oa.reshape(A_UNITS, t, LANES), ob.reshape(B_HEADS, t, LANES), ga, gb, x2d, lw, lam_init,
                  _tile(s, OUT_ROWS))
    return y.reshape(b, s, D_MODEL)


def _forward(x_prompt, x_sample, p, fixed_shift):
    depth = p["norm_w"].shape[0]
    rope_c, rope_s = {}, {}
    for s in {x_prompt.shape[1], x_sample.shape[1]}:
        rope_c[s], rope_s[s] = _rope_tables(s)
    slopes = jnp.asarray([_alibi_slope(hd) for hd in range(A_HEADS)], _F32)
    y_prompt, y_sample = x_prompt, x_sample
    for l in range(depth):
        lw = _layer_weights(l, p, rope_c, rope_s)
        lam_init = _lambda_init(l)
        y_prompt = _encoder_layer(y_prompt, lw, lam_init, slopes, fixed_shift)
        y_sample = _encoder_layer(y_sample, lw, lam_init, slopes, fixed_shift)
    return (y_prompt, y_sample)


def kernel(x_prompt, x_sample, norm_w, w_in, a_q_norm, a_k_norm, a_lq1, a_lk1, a_lq2, a_lk2, a_subln,
           b_cq_norm, b_w_uq, b_ckv_norm, b_w_ukv, b_q_norm, b_k_norm, w_out):
    p = dict(norm_w=norm_w, w_in=w_in, a_q_norm=a_q_norm, a_k_norm=a_k_norm, a_lq1=a_lq1, a_lk1=a_lk1,
             a_lq2=a_lq2, a_lk2=a_lk2, a_subln=a_subln, b_cq_norm=b_cq_norm, b_w_uq=b_w_uq,
             b_ckv_norm=b_ckv_norm, b_w_ukv=b_w_ukv, b_q_norm=b_q_norm, b_k_norm=b_k_norm, w_out=w_out)
    bounds = jnp.stack([jnp.stack(_score_bounds(l, p)) for l in range(norm_w.shape[0])])
    shift_ok = jnp.max(bounds) <= MAX_STATIC_SHIFT
    return lax.cond(shift_ok,
                    lambda xp, xs, pp: _forward(xp, xs, pp, True),
                    lambda xp, xs, pp: _forward(xp, xs, pp, False),
                    x_prompt, x_sample, p)
```

```python
import functools
import math

import numpy as np
import jax
import jax.numpy as jnp
from jax import lax
from jax.experimental import pallas as pl
from jax.experimental.pallas import tpu as pltpu

D_MODEL = 1024
A_HEADS = 4
A_HEAD_DIM = 64
A_V_DIM = 128
A_UNITS = 2 * A_HEADS
B_HEADS = 4
B_NOPE = 128
B_ROPE = 64
B_QK_DIM = B_NOPE + B_ROPE
B_V_DIM = 128
Q_LORA = 256
KV_LORA = 128
ROPE_THETA = 10000.0
EPS = 1e-6
LANES = 128

_SPLITS = np.cumsum([0, 512, 512, 512, 512, Q_LORA, KV_LORA, B_ROPE, 512])
_P_AQ, _P_AK, _P_AV, _P_AG, _P_CQ, _P_CKV, _P_KR, _P_BG, _P_END = np.cumsum(
    [0, 512, 512, 512, 512, Q_LORA, KV_LORA, 2 * B_ROPE, 512])

_VMEM_LIMIT = 56 * 1024 * 1024
LOG2E = math.log2(math.e)
MAX_STATIC_SHIFT = 32.0
POS_SPLIT = 128
SIGMA_PIECES = 3
FIXED_TQ = 1024
FIXED_TK = 512
MAX_BLOCK_PAIRS = 32
MAX_LOOPED_QUERY_BLOCKS = 4
OUT_ROWS = 1024
VT_ROWS = 144

_AUG_SHIFT = 0
_AUG_QHI = _AUG_SHIFT + 1
_AUG_QLO = _AUG_QHI + SIGMA_PIECES
_AUG_KHI = _AUG_QLO + SIGMA_PIECES
_AUG_KLO = _AUG_KHI + SIGMA_PIECES
_AUG_END = _AUG_KLO + SIGMA_PIECES
_B_SHIFT = B_QK_DIM - LANES


def _aug_base(comp):
    return A_HEAD_DIM if comp == 0 else 0


_F32 = jnp.float32
_BF16 = jnp.bfloat16


def _lambda_init(layer_idx):
    return 0.8 - 0.6 * math.exp(-0.3 * layer_idx)


def _alibi_slope(head):
    return 2.0 ** (-8.0 * (head + 1) / A_HEADS)


def _sigma_pieces(head):
    rest = _alibi_slope(head) * LOG2E
    pieces = []
    for _ in range(SIGMA_PIECES):
        p = float(np.asarray(rest, np.float32).astype(_BF16).astype(np.float32))
        pieces.append(p)
        rest -= p
    return pieces


def _group_sum_matrices():
    unit = np.arange(256) // A_HEAD_DIM
    m_a = (unit[:, None] == unit[None, :]).astype(np.float32)
    m_b = np.zeros((2, 256, 256), np.float32)
    m_b[0, :B_QK_DIM, :] = 1.0
    m_b[1, :B_NOPE, :] = 1.0
    return jnp.asarray(m_a, _BF16), jnp.asarray(m_b, _BF16)


def _rope_swap_index():
    half = B_ROPE // 2
    return np.concatenate([np.arange(half, B_ROPE), np.arange(0, half)])


def _silu(x):
    return x / (1.0 + jnp.exp(-x))


def _proj_kernel(x_ref, nw_ref, win_ref, gaq_ref, gak_ref, gcq_ref, wuq_ref, gckv_ref, wukv_ref,
                 gbq_ref, gbkn_ref, gbkr_ref, ct_ref, st_ref, augq_ref, augk_ref, augb_ref, suma_ref, sumb_ref,
                 qa_ref, ka_ref, va_ref, ga_ref, qb_ref, kb_ref, vb_ref, gb_ref, *, fixed_shift, seq_len):
    tm = x_ref.shape[0]
    rows = slice(0, tm)
    x = x_ref[rows, :]
    h = x * lax.rsqrt(jnp.mean(x * x, axis=-1, keepdims=True) + EPS) * nw_ref[...]
    hb = h.astype(_BF16)

    def proj(lo, hi):
        return jnp.dot(hb, win_ref[:, lo:hi], preferred_element_type=_F32)

    lane = lax.broadcasted_iota(jnp.int32, (tm, LANES), 1)
    low_half = lane < B_ROPE
    ones_col = (lane == 0).astype(_F32)
    ct = ct_ref[rows, :]
    st = st_ref[rows, :]

    def aug_select(index, base, hi, lo):
        at = lambda off: (index >= base + off) & (index < base + off + SIGMA_PIECES)
        return at(hi), at(lo)

    if fixed_shift:
        a_qscale = (A_HEAD_DIM ** -0.5) * LOG2E
        b_qscale = (B_QK_DIM ** -0.5) * LOG2E
        first = (pl.program_id(0) % (seq_len // tm)) * tm - seq_len // 2
        split_pos = lambda p: (((p >> 7) * POS_SPLIT).astype(_F32), (p & (POS_SPLIT - 1)).astype(_F32))
        k_hi, k_lo = split_pos(first + lax.broadcasted_iota(jnp.int32, (tm, LANES), 0))
        q_hi, q_lo = split_pos(first + lax.broadcasted_iota(jnp.int32, (A_HEAD_DIM, tm), 1))
        aug_row = lax.broadcasted_iota(jnp.int32, (A_HEAD_DIM, tm), 0)
        at_hi, at_lo = aug_select(aug_row, 0, _AUG_QHI, _AUG_QLO)
        q_pos = jnp.where(at_hi, q_hi, jnp.where(at_lo, q_lo, 0.0))
    else:
        a_qscale = A_HEAD_DIM ** -0.5
        b_qscale = B_QK_DIM ** -0.5

    def group_sums(a, m):
        return jnp.dot((a * a).astype(_BF16), m, preferred_element_type=_F32)

    aq = proj(_P_AQ, _P_AK)
    ak = proj(_P_AK, _P_AV)
    def unit_sums(a):
        return jnp.concatenate([group_sums(a[:, lo:lo + 256], suma_ref[...]) for lo in range(0, a.shape[1], 256)],
                               axis=1)

    qn_all = aq * lax.rsqrt(unit_sums(aq) * (1.0 / A_HEAD_DIM) + EPS) * gaq_ref[...] * a_qscale
    kn_all = ak * lax.rsqrt(unit_sums(ak) * (1.0 / A_HEAD_DIM) + EPS) * gak_ref[...]
    for hd in range(A_HEADS):
        qn2 = qn_all[:, hd * LANES:(hd + 1) * LANES]
        kn2 = kn_all[:, hd * LANES:(hd + 1) * LANES]
        if fixed_shift:
            qn2_t = qn2.T
        for comp in range(2):
            u = 2 * hd + comp
            own = (lane < A_HEAD_DIM) if comp == 0 else (lane >= A_HEAD_DIM)
            if fixed_shift:
                feat = qn2_t[comp * A_HEAD_DIM:(comp + 1) * A_HEAD_DIM]
                aug = jnp.concatenate([augq_ref[u]] * (tm // LANES), axis=1) + q_pos
                stacked = [feat, aug] if comp == 0 else [aug, feat]
                qa_ref[u, 0, :, rows] = jnp.concatenate(stacked, axis=0).astype(_BF16)
                at_hi, at_lo = aug_select(lane, _aug_base(comp), _AUG_KHI, _AUG_KLO)
                k_pos = jnp.where(at_hi, k_hi, jnp.where(at_lo, k_lo, 0.0))
                ka_ref[u, rows, :] = (jnp.where(own, kn2, 0.0) + augk_ref[u] + k_pos).astype(_BF16)
            else:
                qa_ref[u, 0, rows, :] = jnp.where(own, qn2, 0.0).astype(_BF16)
                ka_ref[u, rows, :] = jnp.where(own, kn2, 0.0).astype(_BF16)

    def store_values(v_ref, hd, vv):
        if fixed_shift:
            pad_rows = lax.broadcasted_iota(jnp.int32, (VT_ROWS - LANES, tm), 0)
            v_ref[hd, 0, :, rows] = jnp.concatenate([vv.T, (pad_rows == 0).astype(_F32)], axis=0).astype(_BF16)
        else:
            v_ref[hd, rows, :] = jnp.concatenate([vv, ones_col], axis=1).astype(_BF16)

    av = proj(_P_AV, _P_AG)
    for hd in range(A_HEADS):
        store_values(va_ref, hd, av[:, hd * LANES:(hd + 1) * LANES])
    ga_ref[rows, :] = _silu(proj(_P_AG, _P_CQ)).astype(_BF16)

    cq = proj(_P_CQ, _P_CKV)
    cqn = cq * lax.rsqrt(jnp.mean(cq * cq, axis=-1, keepdims=True) + EPS) * gcq_ref[...]
    qall = jnp.dot(cqn.astype(_BF16), wuq_ref[...], preferred_element_type=_F32)
    ckv = proj(_P_CKV, _P_KR)
    ckvn = ckv * lax.rsqrt(jnp.mean(ckv * ckv, axis=-1, keepdims=True) + EPS) * gckv_ref[...]
    kv = jnp.dot(ckvn.astype(_BF16), wukv_ref[...], preferred_element_type=_F32)
    kr2 = proj(_P_KR, _P_BG)

    for hd in range(B_HEADS):
        q0 = qall[:, hd * 256:hd * 256 + LANES]
        q1 = qall[:, hd * 256 + LANES:(hd + 1) * 256]
        r = lax.rsqrt(group_sums(qall[:, hd * 256:(hd + 1) * 256], sumb_ref[0]) * (1.0 / B_QK_DIM) + EPS)
        q0n = q0 * r[:, :LANES] * gbq_ref[:, :LANES] * b_qscale
        q1n = q1 * r[:, LANES:] * gbq_ref[:, LANES:]
        q1r = (q1n * ct + pltpu.roll(q1n, B_ROPE, 1) * st) * b_qscale
        if fixed_shift:
            q1r = q1r + augb_ref[0:1, :]
            qb_ref[hd, 0, :, rows] = jnp.concatenate([q0n.T, q1r.T], axis=0).astype(_BF16)
        else:
            qb_ref[hd, 0, rows, :] = jnp.concatenate([q0n, q1r], axis=1).astype(_BF16)

    kr_ss = jnp.sum(jnp.where(low_half, kr2 * kr2, 0.0), axis=-1, keepdims=True)
    krg = kr2 * gbkr_ref[...]
    krr = krg * ct + pltpu.roll(krg, B_ROPE, 1) * st
    for hd in range(B_HEADS):
        kn = kv[:, hd * 256:hd * 256 + LANES]
        vv = kv[:, hd * 256 + LANES:(hd + 1) * 256]
        kn_ss = group_sums(kv[:, hd * 256:(hd + 1) * 256], sumb_ref[1])[:, :LANES]
        r = lax.rsqrt((kn_ss + kr_ss) * (1.0 / B_QK_DIM) + EPS)
        k1 = krr * r
        if fixed_shift:
            k1 = k1 + augb_ref[1:2, :]
        kb_ref[hd, rows, :] = jnp.concatenate([kn * r * gbkn_ref[...], k1], axis=1).astype(_BF16)
        store_values(vb_ref, hd, vv)
    gb_ref[rows, :] = _silu(proj(_P_BG, _P_END)).astype(_BF16)


def _proj_call(x2d, seq_len, lw, tm, fixed_shift):
    t = x2d.shape[0]
    blocks_per_seq = seq_len // tm
    const = lambda i: (0, 0)
    const3 = lambda i: (0, 0, 0)
    row = lambda i: (i, 0)
    unit_row = lambda i: (0, i, 0)
    var_row = lambda i: (0, 0, i, 0)
    rope_row = lambda i: (i % blocks_per_seq, 0)
    in_specs = [
        pl.BlockSpec((tm, D_MODEL), row),
        pl.BlockSpec((1, D_MODEL), const),
        pl.BlockSpec((D_MODEL, int(_P_END)), const),
        pl.BlockSpec((1, A_UNITS * A_HEAD_DIM), const),
        pl.BlockSpec((1, A_UNITS * A_HEAD_DIM), const),
        pl.BlockSpec((1, Q_LORA), const),
        pl.BlockSpec((Q_LORA, B_HEADS * 256), const),
        pl.BlockSpec((1, KV_LORA), const),
        pl.BlockSpec((KV_LORA, B_HEADS * 256), const),
        pl.BlockSpec((1, 256), const),
        pl.BlockSpec((1, LANES), const),
        pl.BlockSpec((1, LANES), const),
        pl.BlockSpec((tm, LANES), rope_row),
        pl.BlockSpec((tm, LANES), rope_row),
        pl.BlockSpec((A_UNITS, A_HEAD_DIM, LANES), const3),
        pl.BlockSpec((A_UNITS, 1, LANES), const3),
        pl.BlockSpec((2, LANES), const),
        pl.BlockSpec((256, 256), const),
        pl.BlockSpec((2, 256, 256), const3),
    ]
    if fixed_shift:
        q_shape = lambda units, dk: jax.ShapeDtypeStruct((units, t // tm, dk, tm), _BF16)
        q_spec = lambda units, dk: pl.BlockSpec((units, 1, dk, tm), lambda i: (0, i, 0, 0))
        v_shape = lambda heads: jax.ShapeDtypeStruct((heads, t // tm, VT_ROWS, tm), _BF16)
        v_spec = lambda heads: pl.BlockSpec((heads, 1, VT_ROWS, tm), lambda i: (0, i, 0, 0))
    else:
        q_shape = lambda units, dk: jax.ShapeDtypeStruct((units, 1, t, dk), _BF16)
        q_spec = lambda units, dk: pl.BlockSpec((units, 1, tm, dk), var_row)
        v_shape = lambda heads: jax.ShapeDtypeStruct((heads, t, 256), _BF16)
        v_spec = lambda heads: pl.BlockSpec((heads, tm, 256), unit_row)
    out_shape = [
        q_shape(A_UNITS, LANES),
        jax.ShapeDtypeStruct((A_UNITS, t, LANES), _BF16),
        v_shape(A_HEADS),
        jax.ShapeDtypeStruct((t, 512), _BF16),
        q_shape(B_HEADS, 256),
        jax.ShapeDtypeStruct((B_HEADS, t, 256), _BF16),
        v_shape(B_HEADS),
        jax.ShapeDtypeStruct((t, 512), _BF16),
    ]
    out_specs = [
        q_spec(A_UNITS, LANES),
        pl.BlockSpec((A_UNITS, tm, LANES), unit_row),
        v_spec(A_HEADS),
        pl.BlockSpec((tm, 512), row),
        q_spec(B_HEADS, 256),
        pl.BlockSpec((B_HEADS, tm, 256), unit_row),
        v_spec(B_HEADS),
        pl.BlockSpec((tm, 512), row),
    ]
    return pl.pallas_call(
        functools.partial(_proj_kernel, fixed_shift=fixed_shift, seq_len=seq_len),
        grid=(t // tm,),
        in_specs=in_specs,
        out_specs=out_specs,
        out_shape=out_shape,
        compiler_params=pltpu.CompilerParams(
            dimension_semantics=("parallel",), vmem_limit_bytes=_VMEM_LIMIT),
        name="proj",
    )(x2d, lw["norm_w"], lw["w_in"], lw["g_aq"], lw["g_ak"], lw["g_cq"], lw["w_uq"], lw["g_ckv"],
      lw["w_ukv"], lw["g_bq"], lw["g_bkn"], lw["g_bkr"], lw["rope_c"][seq_len], lw["rope_s"][seq_len],
      lw["aug_q"], lw["aug_k"], lw["aug_b"], *_group_sum_matrices())


def _flash_kernel(slope_ref, q_ref, k_ref, v_ref, o_ref, m_sc, acc_sc, *, alibi, heads_per_slope):
    kj = pl.program_id(3)
    tq = q_ref.shape[3]
    tk = k_ref.shape[2]

    @pl.when(kj == 0)
    def _():
        m_sc[...] = jnp.full(m_sc.shape, -jnp.inf, _F32)
        acc_sc[...] = jnp.zeros(acc_sc.shape, _F32)

    s = lax.dot_general(q_ref[0, 0, 0], k_ref[0, 0], (((1,), (1,)), ((), ())),
                        preferred_element_type=_F32)
    if alibi:
        slope = slope_ref[pl.program_id(0) // heads_per_slope]
        qpos = pl.program_id(2) * tq + lax.broadcasted_iota(jnp.int32, (tq, tk), 0)
        kpos = kj * tk + lax.broadcasted_iota(jnp.int32, (tq, tk), 1)
        s = s - slope * jnp.abs(qpos - kpos).astype(_F32)
    m_prev = m_sc[...]
    m_new = jnp.maximum(m_prev, jnp.max(s, axis=-1, keepdims=True))
    alpha = jnp.exp(m_prev - m_new)
    p = jnp.exp(s - m_new)
    acc_sc[...] = alpha * acc_sc[...] + jnp.dot(p.astype(_BF16), v_ref[0, 0],
                                                 preferred_element_type=_F32)
    m_sc[...] = m_new

    @pl.when(kj == pl.num_programs(3) - 1)
    def _():
        acc = acc_sc[...]
        o_ref[0, 0] = acc[:, :LANES] / acc[:, LANES:LANES + 1]


def _flash_call(q, k, v, slopes, *, alibi, units_per_v, tq, tk):
    u, _, b, s, dk = q.shape
    kern = functools.partial(_flash_kernel, alibi=alibi, heads_per_slope=units_per_v)
    grid_spec = pltpu.PrefetchScalarGridSpec(
        num_scalar_prefetch=1,
        grid=(u, b, s // tq, s // tk),
        in_specs=[
            pl.BlockSpec((1, 1, 1, tq, dk), lambda ui, bi, qi, ki, sl: (ui, 0, bi, qi, 0)),
            pl.BlockSpec((1, 1, tk, dk), lambda ui, bi, qi, ki, sl: (ui, bi, ki, 0)),
            pl.BlockSpec((1, 1, tk, 256), lambda ui, bi, qi, ki, sl: (ui // units_per_v, bi, ki, 0)),
        ],
        out_specs=pl.BlockSpec((1, 1, tq, LANES), lambda ui, bi, qi, ki, sl: (ui, bi, qi, 0)),
        scratch_shapes=[pltpu.VMEM((tq, 1), _F32), pltpu.VMEM((tq, 256), _F32)],
    )
    return pl.pallas_call(
        kern,
        grid_spec=grid_spec,
        out_shape=jax.ShapeDtypeStruct((u, b, s, LANES), _F32),
        compiler_params=pltpu.CompilerParams(
            dimension_semantics=("parallel", "parallel", "parallel", "arbitrary"),
            vmem_limit_bytes=_VMEM_LIMIT),
        name="flash_a" if alibi else "flash_b",
    )(slopes, q, k, v)


def _attn_kernel(qt_ref, k_ref, vt_ref, o_ref, qv_sc, *, tq, q_steps, alibi):
    n_k, _, tk = vt_ref.shape[2:]
    q_sub = qt_ref.shape[1] * qt_ref.shape[3] // tq
    n_diag = tq // tk
    if alibi:
        head = pl.program_id(0) // 2
        sigma = jnp.float32(_alibi_slope(A_HEADS - 1) * LOG2E)
        for hd in range(A_HEADS - 1):
            sigma = jnp.where(head == hd, jnp.float32(_alibi_slope(hd) * LOG2E), sigma)
        ahead = (lax.broadcasted_iota(jnp.int32, (tk, tq), 0) - lax.broadcasted_iota(jnp.int32, (tk, tq), 1))
        aug_row = lax.broadcasted_iota(jnp.int32, (qt_ref.shape[2], tq), 0)
        aug_row = aug_row - jnp.where(pl.program_id(0) % 2 == 0, A_HEAD_DIM, 0)
        bias_rows = (aug_row >= _AUG_QHI) & (aug_row < _AUG_END)

    def query_block(sub, qi, slot):
        pieces = tq // qt_ref.shape[3]
        qt = jnp.concatenate([qt_ref[0, sub * pieces + piece] for piece in range(pieces)], axis=1)
        qv_sc[slot, 0] = qt
        if alibi:
            qv_sc[slot, 1] = jnp.where(bias_rows, -qt, qt)

        def block(var, j, correction=None):
            k = k_ref[0, 0, pl.ds(pl.multiple_of(j * tk, tk), tk), :]
            st = jnp.dot(k, qv_sc[slot, var], preferred_element_type=_F32)
            if correction is not None:
                st = st + correction
            return jnp.dot(vt_ref[0, 0, j], jnp.exp2(st).astype(_BF16), preferred_element_type=_F32)

        def accumulate(acc, part):
            return part if acc is None else acc + part

        acc = None
        if alibi:
            first_diag = qi * n_diag
            for d in range(n_diag):
                corr = (-2.0 * sigma) * jnp.maximum(ahead + d * tk, 0).astype(_F32)
                acc = accumulate(acc, block(0, first_diag + d, corr))
            for x in range(n_k - n_diag):
                after = (x >= first_diag) * 1
                acc = accumulate(acc, block(after, x + n_diag * after))
        else:
            for x in range(n_k):
                acc = accumulate(acc, block(0, x))
        rows = pl.ds(pl.multiple_of(sub * tq, tq), tq)
        o_ref[0, 0, rows, :] = (acc[:LANES] / acc[LANES:LANES + 1]).T.astype(o_ref.dtype)

    if q_steps == 1 or q_sub == 1:
        for sub in range(q_sub):
            query_block(sub, sub if q_steps == 1 else pl.program_id(2), sub)
    else:
        def body(sub, carry):
            query_block(sub, pl.program_id(2) * q_sub + sub, 0)
            return carry

        lax.fori_loop(0, q_sub, body, 0)


def _attn_call(qt, k, vt, *, alibi, units_per_v, tq):
    u, _, dk, tm = qt.shape
    _, b, s, _ = k.shape
    n_k, _, tk = vt.shape[2:]
    assert tq % tk == 0 and s % tq == 0 and tq % tm == 0, (s, tq, tk, tm)
    n_q = s // tq
    if n_q * n_k <= MAX_BLOCK_PAIRS:
        q_sub, slots = n_q, n_q
    else:
        q_sub, slots = _largest_group(n_q, MAX_LOOPED_QUERY_BLOCKS), 1
    q_steps = n_q // q_sub
    pieces = tq * q_sub // tm
    kern = functools.partial(_attn_kernel, tq=tq, q_steps=q_steps, alibi=alibi)
    return pl.pallas_call(
        kern,
        grid=(u, b, q_steps),
        in_specs=[
            pl.BlockSpec((1, pieces, dk, tm), lambda ui, bi, qi: (ui, bi * q_steps + qi, 0, 0)),
            pl.BlockSpec((1, 1, s, dk), lambda ui, bi, qi: (ui, bi, 0, 0)),
            pl.BlockSpec((1, 1, n_k, VT_ROWS, tk), lambda ui, bi, qi: (ui // units_per_v, bi, 0, 0, 0)),
        ],
        out_specs=pl.BlockSpec((1, 1, tq * q_sub, LANES), lambda ui, bi, qi: (ui, bi, qi, 0)),
        out_shape=jax.ShapeDtypeStruct((u, b, s, LANES), _BF16),
        scratch_shapes=[pltpu.VMEM((slots, 2 if alibi else 1, dk, tq), _BF16)],
        compiler_params=pltpu.CompilerParams(
            dimension_semantics=("parallel", "parallel", "parallel"),
            vmem_limit_bytes=_VMEM_LIMIT),
        name="attn_a" if alibi else "attn_b",
    )(qt, k, vt)


def _out_kernel(oa_ref, ob_ref, ga_ref, gb_ref, x_ref, lq1_ref, lk1_ref, lq2_ref, lk2_ref, subln_ref,
                wout_ref, y_ref, *, lam_init):
    lam = (jnp.exp(jnp.sum(lq1_ref[...] * lk1_ref[...], axis=-1, keepdims=True))
           - jnp.exp(jnp.sum(lq2_ref[...] * lk2_ref[...], axis=-1, keepdims=True)) + lam_init)
    ga = ga_ref[...].astype(_F32)
    gb = gb_ref[...].astype(_F32)
    pieces = []
    for hd in range(A_HEADS):
        d = oa_ref[2 * hd].astype(_F32) - lam * oa_ref[2 * hd + 1].astype(_F32)
        n = d * lax.rsqrt(jnp.mean(d * d, axis=-1, keepdims=True) + EPS) * subln_ref[...]
        pieces.append(n * (1.0 - lam_init) * ga[:, hd * LANES:(hd + 1) * LANES])
    for hd in range(B_HEADS):
        pieces.append(ob_ref[hd].astype(_F32) * gb[:, hd * LANES:(hd + 1) * LANES])
    y = jnp.concatenate(pieces, axis=1).astype(_BF16)
    y_ref[...] = x_ref[...] + jnp.dot(y, wout_ref[...], preferred_element_type=_F32)


def _out_call(oa, ob, ga, gb, x2d, lw, lam_init, tm):
    t = x2d.shape[0]
    const = lambda i: (0, 0)
    row = lambda i: (i, 0)
    unit_row = lambda i: (0, i, 0)
    return pl.pallas_call(
        functools.partial(_out_kernel, lam_init=lam_init),
        grid=(t // tm,),
        in_specs=[
            pl.BlockSpec((A_UNITS, tm, LANES), unit_row),
            pl.BlockSpec((B_HEADS, tm, LANES), unit_row),
            pl.BlockSpec((tm, 512), row),
            pl.BlockSpec((tm, 512), row),
            pl.BlockSpec((tm, D_MODEL), row),
            pl.BlockSpec((1, A_HEAD_DIM), const),
            pl.BlockSpec((1, A_HEAD_DIM), const),
            pl.BlockSpec((1, A_HEAD_DIM), const),
            pl.BlockSpec((1, A_HEAD_DIM), const),
            pl.BlockSpec((1, A_V_DIM), const),
            pl.BlockSpec((D_MODEL, D_MODEL), const),
        ],
        out_specs=pl.BlockSpec((tm, D_MODEL), row),
        out_shape=jax.ShapeDtypeStruct((t, D_MODEL), _F32),
        compiler_params=pltpu.CompilerParams(
            dimension_semantics=("parallel",), vmem_limit_bytes=_VMEM_LIMIT),
        name="out",
    )(oa, ob, ga, gb, x2d, lw["lq1"], lw["lk1"], lw["lq2"], lw["lk2"], lw["subln"], lw["w_out"])


def _rope_tables(seq_len):
    inv = ROPE_THETA ** (-np.arange(0, B_ROPE, 2, dtype=np.float64) / B_ROPE)
    n_hi = -(-seq_len // POS_SPLIT)
    ang_hi = (np.arange(n_hi, dtype=np.float64) * POS_SPLIT)[:, None, None] * inv
    ang_lo = np.arange(POS_SPLIT, dtype=np.float64)[None, :, None] * inv
    cos_hi, sin_hi, cos_lo, sin_lo = (jnp.asarray(f(a), _F32) for a in (ang_hi, ang_lo) for f in (np.cos, np.sin))
    cos = (cos_hi * cos_lo - sin_hi * sin_lo).reshape(n_hi * POS_SPLIT, B_ROPE // 2)[:seq_len]
    sin = (sin_hi * cos_lo + cos_hi * sin_lo).reshape(n_hi * POS_SPLIT, B_ROPE // 2)[:seq_len]
    zeros = jnp.zeros((seq_len, LANES - B_ROPE), _F32)
    return (jnp.concatenate([cos, cos, zeros], axis=1),
            jnp.concatenate([-sin, sin, zeros], axis=1))


def _score_bounds(l, p):
    amax = lambda v: jnp.max(jnp.abs(v[l].astype(_F32)))
    bound_a = amax(p["a_q_norm"]) * amax(p["a_k_norm"]) * (A_HEAD_DIM ** 0.5)
    bound_b = amax(p["b_q_norm"]) * amax(p["b_k_norm"]) * (B_QK_DIM ** 0.5)
    return bound_a, bound_b


def _aug_rows(bound_a, bound_b):
    aug_q = np.zeros((A_UNITS, A_HEAD_DIM, LANES), np.float32)
    aug_k = np.zeros((A_UNITS, 1, LANES), np.float32)
    for u in range(A_UNITS):
        sig = _sigma_pieces(u // 2)
        base = _aug_base(u % 2)
        aug_k[u, 0, base + _AUG_SHIFT] = 1.0
        for a in range(SIGMA_PIECES):
            aug_k[u, 0, base + _AUG_QHI + a] = -sig[a]
            aug_k[u, 0, base + _AUG_QLO + a] = -sig[a]
            aug_q[u, _AUG_KHI + a, :] = sig[a]
            aug_q[u, _AUG_KLO + a, :] = sig[a]
    shift_row = (np.arange(A_HEAD_DIM) == _AUG_SHIFT).astype(np.float32).reshape(1, A_HEAD_DIM, 1)
    aug_q = jnp.asarray(aug_q) - (bound_a * LOG2E) * shift_row
    b_lane = (np.arange(LANES) == _B_SHIFT).astype(np.float32)
    aug_b = jnp.stack([-(bound_b * LOG2E) * b_lane, jnp.asarray(b_lane)])
    return aug_q, jnp.asarray(aug_k), aug_b


def _layer_weights(l, p, rope_c, rope_s):
    swap = _rope_swap_index()
    w = p["w_in"][l]
    kr = w[:, _SPLITS[6]:_SPLITS[7]]
    w_in = jnp.concatenate([w[:, :_SPLITS[6]], kr, kr[:, swap], w[:, _SPLITS[7]:]], axis=1).astype(_BF16)
    wuq = p["b_w_uq"][l].reshape(Q_LORA, B_HEADS, B_QK_DIM)
    w_uq = jnp.concatenate([wuq, wuq[:, :, B_NOPE:][:, :, swap]], axis=2).reshape(Q_LORA, B_HEADS * 256)
    gq = p["b_q_norm"][l]
    gk = p["b_k_norm"][l]
    row = lambda v: v.reshape(1, -1).astype(_F32)
    bound_a, bound_b = _score_bounds(l, p)
    aug_q, aug_k, aug_b = _aug_rows(bound_a, bound_b)
    return {
        "norm_w": row(p["norm_w"][l]),
        "w_in": w_in,
        "g_aq": row(jnp.tile(p["a_q_norm"][l], A_UNITS)),
        "g_ak": row(jnp.tile(p["a_k_norm"][l], A_UNITS)),
        "g_cq": row(p["b_cq_norm"][l]),
        "w_uq": w_uq.astype(_BF16),
        "g_ckv": row(p["b_ckv_norm"][l]),
        "w_ukv": p["b_w_ukv"][l].astype(_BF16),
        "g_bq": row(jnp.concatenate([gq, gq[B_NOPE:][swap]])),
        "g_bkn": row(gk[:B_NOPE]),
        "g_bkr": row(jnp.concatenate([gk[B_NOPE:], gk[B_NOPE:][swap]])),
        "rope_c": rope_c,
        "rope_s": rope_s,
        "aug_q": aug_q, "aug_k": aug_k, "aug_b": aug_b,
        "lq1": row(p["a_lq1"][l]), "lk1": row(p["a_lk1"][l]),
        "lq2": row(p["a_lq2"][l]), "lk2": row(p["a_lk2"][l]),
        "subln": row(p["a_subln"][l]),
        "w_out": p["w_out"][l].astype(_BF16),
    }


def _tile(n, pref):
    return pref if n % pref == 0 else n


def _largest_group(n, cap):
    return max(g for g in range(1, cap + 1) if n % g == 0) if n > 0 else 1


def _encoder_layer(x, lw, lam_init, slopes, fixed_shift):
    b, s, _ = x.shape
    t = b * s
    x2d = x.reshape(t, D_MODEL)
    tm = _tile(s, FIXED_TK)
    qa, ka, va, ga, qb, kb, vb, gb = _proj_call(x2d, s, lw, tm, fixed_shift)
    split = lambda a: a.reshape(a.shape[:-2] + (b, s, a.shape[-1]))
    tq = tk = _tile(s, 512)
    if fixed_shift:
        assert s <= POS_SPLIT * POS_SPLIT * 2
        ftq = _tile(s, FIXED_TQ)
        split_t = lambda a: a.reshape((a.shape[0], b, s // tm) + a.shape[2:])
        oa = _attn_call(qa, split(ka), split_t(va), alibi=True, units_per_v=2, tq=ftq)
        ob = _attn_call(qb, split(kb), split_t(vb), alibi=False, units_per_v=1, tq=ftq)
    else:
        oa = _flash_call(split(qa), split(ka), split(va), slopes, alibi=True, units_per_v=2, tq=tq, tk=tk)
        ob = _flash_call(split(qb), split(kb), split(vb), slopes, alibi=False, units_per_v=1, tq=tq, tk=tk)
    y = _out_call(oa.reshape(A_UNITS, t, LANES), ob.reshape(B_HEADS, t, LANES), ga, gb, x2d, lw, lam_init,
                  _tile(s, OUT_ROWS))
    return y.reshape(b, s, D_MODEL)


def _forward(x_prompt, x_sample, p, fixed_shift):
    depth = p["norm_w"].shape[0]
    rope_c, rope_s = {}, {}
    for s in {x_prompt.shape[1], x_sample.shape[1]}:
        rope_c[s], rope_s[s] = _rope_tables(s)
    slopes = jnp.asarray([_alibi_slope(hd) for hd in range(A_HEADS)], _F32)
    y_prompt, y_sample = x_prompt, x_sample
    for l in range(depth):
        lw = _layer_weights(l, p, rope_c, rope_s)
        lam_init = _lambda_init(l)
        y_prompt = _encoder_layer(y_prompt, lw, lam_init, slopes, fixed_shift)
        y_sample = _encoder_layer(y_sample, lw, lam_init, slopes, fixed_shift)
    return (y_prompt, y_sample)


def kernel(x_prompt, x_sample, norm_w, w_in, a_q_norm, a_k_norm, a_lq1, a_lk1, a_lq2, a_lk2, a_subln,
           b_cq_norm, b_w_uq, b_ckv_norm, b_w_ukv, b_q_norm, b_k_norm, w_out):
    p = dict(norm_w=norm_w, w_in=w_in, a_q_norm=a_q_norm, a_k_norm=a_k_norm, a_lq1=a_lq1, a_lk1=a_lk1,
             a_lq2=a_lq2, a_lk2=a_lk2, a_subln=a_subln, b_cq_norm=b_cq_norm, b_w_uq=b_w_uq,
             b_ckv_norm=b_ckv_norm, b_w_ukv=b_w_ukv, b_q_norm=b_q_norm, b_k_norm=b_k_norm, w_out=w_out)
    bounds = jnp.stack([jnp.stack(_score_bounds(l, p)) for l in range(norm_w.shape[0])])
    shift_ok = jnp.max(bounds) <= MAX_STATIC_SHIFT
    return lax.cond(shift_ok,
                    lambda xp, xs, pp: _forward(xp, xs, pp, True),
                    lambda xp, xs, pp: _forward(xp, xs, pp, False),
                    x_prompt, x_sample, p)
```

```python
import functools
import math

import numpy as np
import jax
import jax.numpy as jnp
from jax import lax
from jax.experimental import pallas as pl
from jax.experimental.pallas import tpu as pltpu

D_MODEL = 1024
A_HEADS = 4
A_HEAD_DIM = 64
A_V_DIM = 128
A_UNITS = 2 * A_HEADS
B_HEADS = 4
B_NOPE = 128
B_ROPE = 64
B_QK_DIM = B_NOPE + B_ROPE
B_V_DIM = 128
Q_LORA = 256
KV_LORA = 128
ROPE_THETA = 10000.0
EPS = 1e-6
LANES = 128

_SPLITS = np.cumsum([0, 512, 512, 512, 512, Q_LORA, KV_LORA, B_ROPE, 512])
_P_AQ, _P_AK, _P_AV, _P_AG, _P_CQ, _P_CKV, _P_KR, _P_BG, _P_END = np.cumsum(
    [0, 512, 512, 512, 512, Q_LORA, KV_LORA, 2 * B_ROPE, 512])

_VMEM_LIMIT = 56 * 1024 * 1024
LOG2E = math.log2(math.e)
MAX_STATIC_SHIFT = 32.0
POS_SPLIT = 128
SIGMA_PIECES = 3
FIXED_TQ = 1024
FIXED_TK = 512
MAX_BLOCK_PAIRS = 32
MAX_LOOPED_QUERY_BLOCKS = 4
OUT_ROWS = 1024
VT_ROWS = 144

_AUG_SHIFT = 0
_AUG_QHI = _AUG_SHIFT + 1
_AUG_QLO = _AUG_QHI + SIGMA_PIECES
_AUG_KHI = _AUG_QLO + SIGMA_PIECES
_AUG_KLO = _AUG_KHI + SIGMA_PIECES
_AUG_END = _AUG_KLO + SIGMA_PIECES
_B_SHIFT = B_QK_DIM - LANES


def _aug_base(comp):
    return A_HEAD_DIM if comp == 0 else 0


_F32 = jnp.float32
_BF16 = jnp.bfloat16


def _lambda_init(layer_idx):
    return 0.8 - 0.6 * math.exp(-0.3 * layer_idx)


def _alibi_slope(head):
    return 2.0 ** (-8.0 * (head + 1) / A_HEADS)


def _sigma_pieces(head):
    rest = _alibi_slope(head) * LOG2E
    pieces = []
    for _ in range(SIGMA_PIECES):
        p = float(np.asarray(rest, np.float32).astype(_BF16).astype(np.float32))
        pieces.append(p)
        rest -= p
    return pieces


def _group_sum_matrices():
    unit = np.arange(256) // A_HEAD_DIM
    m_a = (unit[:, None] == unit[None, :]).astype(np.float32)
    m_b = np.zeros((2, 256, 256), np.float32)
    m_b[0, :B_QK_DIM, :] = 1.0
    m_b[1, :B_NOPE, :] = 1.0
    return jnp.asarray(m_a, _BF16), jnp.asarray(m_b, _BF16)


def _swap_rope_halves(a):
    half = B_ROPE // 2
    return jnp.concatenate([a[..., half:], a[..., :half]], axis=-1)


def _silu(x):
    return x / (1.0 + jnp.exp(-x))


def _proj_kernel(x_ref, nw_ref, win_ref, gaq_ref, gak_ref, gcq_ref, wuq_ref, gckv_ref, wukv_ref,
                 gbq_ref, gbkn_ref, gbkr_ref, ct_ref, st_ref, augq_ref, augk_ref, augb_ref, suma_ref, sumb_ref,
                 qa_ref, ka_ref, va_ref, ga_ref, qb_ref, kb_ref, vb_ref, gb_ref, *, fixed_shift, seq_len):
    tm = x_ref.shape[0]
    rows = slice(0, tm)
    x = x_ref[rows, :]
    h = x * lax.rsqrt(jnp.mean(x * x, axis=-1, keepdims=True) + EPS) * nw_ref[...]
    hb = h.astype(_BF16)

    def proj(lo, hi):
        return jnp.dot(hb, win_ref[:, lo:hi], preferred_element_type=_F32)

    lane = lax.broadcasted_iota(jnp.int32, (tm, LANES), 1)
    low_half = lane < B_ROPE
    ones_col = (lane == 0).astype(_F32)
    ct = ct_ref[rows, :]
    st = st_ref[rows, :]

    def aug_select(index, base, hi, lo):
        at = lambda off: (index >= base + off) & (index < base + off + SIGMA_PIECES)
        return at(hi), at(lo)

    if fixed_shift:
        a_qscale = (A_HEAD_DIM ** -0.5) * LOG2E
        b_qscale = (B_QK_DIM ** -0.5) * LOG2E
        first = (pl.program_id(0) % (seq_len // tm)) * tm - seq_len // 2
        split_pos = lambda p: (((p >> 7) * POS_SPLIT).astype(_F32), (p & (POS_SPLIT - 1)).astype(_F32))
        k_hi, k_lo = split_pos(first + lax.broadcasted_iota(jnp.int32, (tm, LANES), 0))
        q_hi, q_lo = split_pos(first + lax.broadcasted_iota(jnp.int32, (A_HEAD_DIM, tm), 1))
        aug_row = lax.broadcasted_iota(jnp.int32, (A_HEAD_DIM, tm), 0)
        at_hi, at_lo = aug_select(aug_row, 0, _AUG_QHI, _AUG_QLO)
        q_pos = jnp.where(at_hi, q_hi, jnp.where(at_lo, q_lo, 0.0))
    else:
        a_qscale = A_HEAD_DIM ** -0.5
        b_qscale = B_QK_DIM ** -0.5

    def group_sums(a, m):
        return jnp.dot((a * a).astype(_BF16), m, preferred_element_type=_F32)

    aq = proj(_P_AQ, _P_AK)
    ak = proj(_P_AK, _P_AV)
    def unit_sums(a):
        return jnp.concatenate([group_sums(a[:, lo:lo + 256], suma_ref[...]) for lo in range(0, a.shape[1], 256)],
                               axis=1)

    qn_all = aq * lax.rsqrt(unit_sums(aq) * (1.0 / A_HEAD_DIM) + EPS) * gaq_ref[...] * a_qscale
    kn_all = ak * lax.rsqrt(unit_sums(ak) * (1.0 / A_HEAD_DIM) + EPS) * gak_ref[...]
    for hd in range(A_HEADS):
        qn2 = qn_all[:, hd * LANES:(hd + 1) * LANES]
        kn2 = kn_all[:, hd * LANES:(hd + 1) * LANES]
        if fixed_shift:
            qn2_t = qn2.T
        for comp in range(2):
            u = 2 * hd + comp
            own = (lane < A_HEAD_DIM) if comp == 0 else (lane >= A_HEAD_DIM)
            if fixed_shift:
                feat = qn2_t[comp * A_HEAD_DIM:(comp + 1) * A_HEAD_DIM]
                aug = jnp.concatenate([augq_ref[u]] * (tm // LANES), axis=1) + q_pos
                stacked = [feat, aug] if comp == 0 else [aug, feat]
                qa_ref[u, 0, :, rows] = jnp.concatenate(stacked, axis=0).astype(_BF16)
                at_hi, at_lo = aug_select(lane, _aug_base(comp), _AUG_KHI, _AUG_KLO)
                k_pos = jnp.where(at_hi, k_hi, jnp.where(at_lo, k_lo, 0.0))
                ka_ref[u, rows, :] = (jnp.where(own, kn2, 0.0) + augk_ref[u] + k_pos).astype(_BF16)
            else:
                qa_ref[u, 0, rows, :] = jnp.where(own, qn2, 0.0).astype(_BF16)
                ka_ref[u, rows, :] = jnp.where(own, kn2, 0.0).astype(_BF16)

    def store_values(v_ref, hd, vv):
        if fixed_shift:
            pad_rows = lax.broadcasted_iota(jnp.int32, (VT_ROWS - LANES, tm), 0)
            v_ref[hd, 0, :, rows] = jnp.concatenate([vv.T, (pad_rows == 0).astype(_F32)], axis=0).astype(_BF16)
        else:
            v_ref[hd, rows, :] = jnp.concatenate([vv, ones_col], axis=1).astype(_BF16)

    av = proj(_P_AV, _P_AG)
    for hd in range(A_HEADS):
        store_values(va_ref, hd, av[:, hd * LANES:(hd + 1) * LANES])
    ga_ref[rows, :] = _silu(proj(_P_AG, _P_CQ)).astype(_BF16)

    cq = proj(_P_CQ, _P_CKV)
    cqn = cq * lax.rsqrt(jnp.mean(cq * cq, axis=-1, keepdims=True) + EPS) * gcq_ref[...]
    qall = jnp.dot(cqn.astype(_BF16), wuq_ref[...], preferred_element_type=_F32)
    ckv = proj(_P_CKV, _P_KR)
    ckvn = ckv * lax.rsqrt(jnp.mean(ckv * ckv, axis=-1, keepdims=True) + EPS) * gckv_ref[...]
    kv = jnp.dot(ckvn.astype(_BF16), wukv_ref[...], preferred_element_type=_F32)
    kr2 = proj(_P_KR, _P_BG)

    for hd in range(B_HEADS):
        q0 = qall[:, hd * 256:hd * 256 + LANES]
        q1 = qall[:, hd * 256 + LANES:(hd + 1) * 256]
        r = lax.rsqrt(group_sums(qall[:, hd * 256:(hd + 1) * 256], sumb_ref[0]) * (1.0 / B_QK_DIM) + EPS)
        q0n = q0 * r[:, :LANES] * gbq_ref[:, :LANES] * b_qscale
        q1n = q1 * r[:, LANES:] * gbq_ref[:, LANES:]
        q1r = (q1n * ct + pltpu.roll(q1n, B_ROPE, 1) * st) * b_qscale
        if fixed_shift:
            q1r = q1r + augb_ref[0:1, :]
            qb_ref[hd, 0, :, rows] = jnp.concatenate([q0n.T, q1r.T], axis=0).astype(_BF16)
        else:
            qb_ref[hd, 0, rows, :] = jnp.concatenate([q0n, q1r], axis=1).astype(_BF16)

    kr_ss = jnp.sum(jnp.where(low_half, kr2 * kr2, 0.0), axis=-1, keepdims=True)
    krg = kr2 * gbkr_ref[...]
    krr = krg * ct + pltpu.roll(krg, B_ROPE, 1) * st
    for hd in range(B_HEADS):
        kn = kv[:, hd * 256:hd * 256 + LANES]
        vv = kv[:, hd * 256 + LANES:(hd + 1) * 256]
        kn_ss = group_sums(kv[:, hd * 256:(hd + 1) * 256], sumb_ref[1])[:, :LANES]
        r = lax.rsqrt((kn_ss + kr_ss) * (1.0 / B_QK_DIM) + EPS)
        k1 = krr * r
        if fixed_shift:
            k1 = k1 + augb_ref[1:2, :]
        kb_ref[hd, rows, :] = jnp.concatenate([kn * r * gbkn_ref[...], k1], axis=1).astype(_BF16)
        store_values(vb_ref, hd, vv)
    gb_ref[rows, :] = _silu(proj(_P_BG, _P_END)).astype(_BF16)


def _proj_call(x2d, seq_len, lw, tm, fixed_shift):
    t = x2d.shape[0]
    blocks_per_seq = seq_len // tm
    const = lambda i: (0, 0)
    const3 = lambda i: (0, 0, 0)
    row = lambda i: (i, 0)
    unit_row = lambda i: (0, i, 0)
    var_row = lambda i: (0, 0, i, 0)
    rope_row = lambda i: (i % blocks_per_seq, 0)
    in_specs = [
        pl.BlockSpec((tm, D_MODEL), row),
        pl.BlockSpec((1, D_MODEL), const),
        pl.BlockSpec((None, D_MODEL, int(_P_END)), lambda i: (lw["layer"], 0, 0)),
        pl.BlockSpec((1, A_UNITS * A_HEAD_DIM), const),
        pl.BlockSpec((1, A_UNITS * A_HEAD_DIM), const),
        pl.BlockSpec((1, Q_LORA), const),
        pl.BlockSpec((Q_LORA, B_HEADS * 256), const),
        pl.BlockSpec((1, KV_LORA), const),
        pl.BlockSpec((KV_LORA, B_HEADS * 256), const),
        pl.BlockSpec((1, 256), const),
        pl.BlockSpec((1, LANES), const),
        pl.BlockSpec((1, LANES), const),
        pl.BlockSpec((tm, LANES), rope_row),
        pl.BlockSpec((tm, LANES), rope_row),
        pl.BlockSpec((A_UNITS, A_HEAD_DIM, LANES), const3),
        pl.BlockSpec((A_UNITS, 1, LANES), const3),
        pl.BlockSpec((2, LANES), const),
        pl.BlockSpec((256, 256), const),
        pl.BlockSpec((2, 256, 256), const3),
    ]
    if fixed_shift:
        q_shape = lambda units, dk: jax.ShapeDtypeStruct((units, t // tm, dk, tm), _BF16)
        q_spec = lambda units, dk: pl.BlockSpec((units, 1, dk, tm), lambda i: (0, i, 0, 0))
        v_shape = lambda heads: jax.ShapeDtypeStruct((heads, t // tm, VT_ROWS, tm), _BF16)
        v_spec = lambda heads: pl.BlockSpec((heads, 1, VT_ROWS, tm), lambda i: (0, i, 0, 0))
    else:
        q_shape = lambda units, dk: jax.ShapeDtypeStruct((units, 1, t, dk), _BF16)
        q_spec = lambda units, dk: pl.BlockSpec((units, 1, tm, dk), var_row)
        v_shape = lambda heads: jax.ShapeDtypeStruct((heads, t, 256), _BF16)
        v_spec = lambda heads: pl.BlockSpec((heads, tm, 256), unit_row)
    out_shape = [
        q_shape(A_UNITS, LANES),
        jax.ShapeDtypeStruct((A_UNITS, t, LANES), _BF16),
        v_shape(A_HEADS),
        jax.ShapeDtypeStruct((t, 512), _BF16),
        q_shape(B_HEADS, 256),
        jax.ShapeDtypeStruct((B_HEADS, t, 256), _BF16),
        v_shape(B_HEADS),
        jax.ShapeDtypeStruct((t, 512), _BF16),
    ]
    out_specs = [
        q_spec(A_UNITS, LANES),
        pl.BlockSpec((A_UNITS, tm, LANES), unit_row),
        v_spec(A_HEADS),
        pl.BlockSpec((tm, 512), row),
        q_spec(B_HEADS, 256),
        pl.BlockSpec((B_HEADS, tm, 256), unit_row),
        v_spec(B_HEADS),
        pl.BlockSpec((tm, 512), row),
    ]
    return pl.pallas_call(
        functools.partial(_proj_kernel, fixed_shift=fixed_shift, seq_len=seq_len),
        grid=(t // tm,),
        in_specs=in_specs,
        out_specs=out_specs,
        out_shape=out_shape,
        compiler_params=pltpu.CompilerParams(
            dimension_semantics=("parallel",), vmem_limit_bytes=_VMEM_LIMIT),
        name="proj",
    )(x2d, lw["norm_w"], lw["w_in"], lw["g_aq"], lw["g_ak"], lw["g_cq"], lw["w_uq"], lw["g_ckv"],
      lw["w_ukv"], lw["g_bq"], lw["g_bkn"], lw["g_bkr"], lw["rope_c"][seq_len], lw["rope_s"][seq_len],
      lw["aug_q"], lw["aug_k"], lw["aug_b"], *_group_sum_matrices())


def _flash_kernel(slope_ref, q_ref, k_ref, v_ref, o_ref, m_sc, acc_sc, *, alibi, heads_per_slope):
    kj = pl.program_id(3)
    tq = q_ref.shape[3]
    tk = k_ref.shape[2]

    @pl.when(kj == 0)
    def _():
        m_sc[...] = jnp.full(m_sc.shape, -jnp.inf, _F32)
        acc_sc[...] = jnp.zeros(acc_sc.shape, _F32)

    s = lax.dot_general(q_ref[0, 0, 0], k_ref[0, 0], (((1,), (1,)), ((), ())),
                        preferred_element_type=_F32)
    if alibi:
        slope = slope_ref[pl.program_id(0) // heads_per_slope]
        qpos = pl.program_id(2) * tq + lax.broadcasted_iota(jnp.int32, (tq, tk), 0)
        kpos = kj * tk + lax.broadcasted_iota(jnp.int32, (tq, tk), 1)
        s = s - slope * jnp.abs(qpos - kpos).astype(_F32)
    m_prev = m_sc[...]
    m_new = jnp.maximum(m_prev, jnp.max(s, axis=-1, keepdims=True))
    alpha = jnp.exp(m_prev - m_new)
    p = jnp.exp(s - m_new)
    acc_sc[...] = alpha * acc_sc[...] + jnp.dot(p.astype(_BF16), v_ref[0, 0],
                                                 preferred_element_type=_F32)
    m_sc[...] = m_new

    @pl.when(kj == pl.num_programs(3) - 1)
    def _():
        acc = acc_sc[...]
        o_ref[0, 0] = acc[:, :LANES] / acc[:, LANES:LANES + 1]


def _flash_call(q, k, v, slopes, *, alibi, units_per_v, tq, tk):
    u, _, b, s, dk = q.shape
    kern = functools.partial(_flash_kernel, alibi=alibi, heads_per_slope=units_per_v)
    grid_spec = pltpu.PrefetchScalarGridSpec(
        num_scalar_prefetch=1,
        grid=(u, b, s // tq, s // tk),
        in_specs=[
            pl.BlockSpec((1, 1, 1, tq, dk), lambda ui, bi, qi, ki, sl: (ui, 0, bi, qi, 0)),
            pl.BlockSpec((1, 1, tk, dk), lambda ui, bi, qi, ki, sl: (ui, bi, ki, 0)),
            pl.BlockSpec((1, 1, tk, 256), lambda ui, bi, qi, ki, sl: (ui // units_per_v, bi, ki, 0)),
        ],
        out_specs=pl.BlockSpec((1, 1, tq, LANES), lambda ui, bi, qi, ki, sl: (ui, bi, qi, 0)),
        scratch_shapes=[pltpu.VMEM((tq, 1), _F32), pltpu.VMEM((tq, 256), _F32)],
    )
    return pl.pallas_call(
        kern,
        grid_spec=grid_spec,
        out_shape=jax.ShapeDtypeStruct((u, b, s, LANES), _F32),
        compiler_params=pltpu.CompilerParams(
            dimension_semantics=("parallel", "parallel", "parallel", "arbitrary"),
            vmem_limit_bytes=_VMEM_LIMIT),
        name="flash_a" if alibi else "flash_b",
    )(slopes, q, k, v)


def _attn_kernel(qt_ref, k_ref, vt_ref, o_ref, qv_sc, *, tq, q_steps, alibi):
    n_k, _, tk = vt_ref.shape[2:]
    q_sub = qt_ref.shape[1] * qt_ref.shape[3] // tq
    n_diag = tq // tk
    if alibi:
        head = pl.program_id(0) // 2
        sigma = jnp.float32(_alibi_slope(A_HEADS - 1) * LOG2E)
        for hd in range(A_HEADS - 1):
            sigma = jnp.where(head == hd, jnp.float32(_alibi_slope(hd) * LOG2E), sigma)
        ahead = (lax.broadcasted_iota(jnp.int32, (tk, tq), 0) - lax.broadcasted_iota(jnp.int32, (tk, tq), 1))
        aug_row = lax.broadcasted_iota(jnp.int32, (qt_ref.shape[2], tq), 0)
        aug_row = aug_row - jnp.where(pl.program_id(0) % 2 == 0, A_HEAD_DIM, 0)
        bias_rows = (aug_row >= _AUG_QHI) & (aug_row < _AUG_END)

    def query_block(sub, qi, slot):
        pieces = tq // qt_ref.shape[3]
        qt = jnp.concatenate([qt_ref[0, sub * pieces + piece] for piece in range(pieces)], axis=1)
        qv_sc[slot, 0] = qt
        if alibi:
            qv_sc[slot, 1] = jnp.where(bias_rows, -qt, qt)

        def block(var, j, correction=None):
            k = k_ref[0, 0, pl.ds(pl.multiple_of(j * tk, tk), tk), :]
            st = jnp.dot(k, qv_sc[slot, var], preferred_element_type=_F32)
            if correction is not None:
                st = st + correction
            return jnp.dot(vt_ref[0, 0, j], jnp.exp2(st).astype(_BF16), preferred_element_type=_F32)

        def accumulate(acc, part):
            return part if acc is None else acc + part

        acc = None
        if alibi:
            first_diag = qi * n_diag
            for d in range(n_diag):
                corr = (-2.0 * sigma) * jnp.maximum(ahead + d * tk, 0).astype(_F32)
                acc = accumulate(acc, block(0, first_diag + d, corr))
            for x in range(n_k - n_diag):
                after = (x >= first_diag) * 1
                acc = accumulate(acc, block(after, x + n_diag * after))
        else:
            for x in range(n_k):
                acc = accumulate(acc, block(0, x))
        rows = pl.ds(pl.multiple_of(sub * tq, tq), tq)
        o_ref[0, 0, rows, :] = (acc[:LANES] / acc[LANES:LANES + 1]).T.astype(o_ref.dtype)

    if q_steps == 1 or q_sub == 1:
        for sub in range(q_sub):
            query_block(sub, sub if q_steps == 1 else pl.program_id(2), sub)
    else:
        def body(sub, carry):
            query_block(sub, pl.program_id(2) * q_sub + sub, 0)
            return carry

        lax.fori_loop(0, q_sub, body, 0)


def _attn_call(qt, k, vt, *, alibi, units_per_v, tq):
    u, _, dk, tm = qt.shape
    _, b, s, _ = k.shape
    n_k, _, tk = vt.shape[2:]
    assert tq % tk == 0 and s % tq == 0 and tq % tm == 0, (s, tq, tk, tm)
    n_q = s // tq
    if n_q * n_k <= MAX_BLOCK_PAIRS:
        q_sub, slots = n_q, n_q
    else:
        q_sub, slots = _largest_group(n_q, MAX_LOOPED_QUERY_BLOCKS), 1
    q_steps = n_q // q_sub
    pieces = tq * q_sub // tm
    kern = functools.partial(_attn_kernel, tq=tq, q_steps=q_steps, alibi=alibi)
    return pl.pallas_call(
        kern,
        grid=(u, b, q_steps),
        in_specs=[
            pl.BlockSpec((1, pieces, dk, tm), lambda ui, bi, qi: (ui, bi * q_steps + qi, 0, 0)),
            pl.BlockSpec((1, 1, s, dk), lambda ui, bi, qi: (ui, bi, 0, 0)),
            pl.BlockSpec((1, 1, n_k, VT_ROWS, tk), lambda ui, bi, qi: (ui // units_per_v, bi, 0, 0, 0)),
        ],
        out_specs=pl.BlockSpec((1, 1, tq * q_sub, LANES), lambda ui, bi, qi: (ui, bi, qi, 0)),
        out_shape=jax.ShapeDtypeStruct((u, b, s, LANES), _BF16),
        scratch_shapes=[pltpu.VMEM((slots, 2 if alibi else 1, dk, tq), _BF16)],
        compiler_params=pltpu.CompilerParams(
            dimension_semantics=("parallel", "parallel", "parallel"),
            vmem_limit_bytes=_VMEM_LIMIT),
        name="attn_a" if alibi else "attn_b",
    )(qt, k, vt)


def _out_kernel(oa_ref, ob_ref, ga_ref, gb_ref, x_ref, lq1_ref, lk1_ref, lq2_ref, lk2_ref, subln_ref,
                wout_ref, y_ref, *, lam_init):
    lam = (jnp.exp(jnp.sum(lq1_ref[...] * lk1_ref[...], axis=-1, keepdims=True))
           - jnp.exp(jnp.sum(lq2_ref[...] * lk2_ref[...], axis=-1, keepdims=True)) + lam_init)
    ga = ga_ref[...].astype(_F32)
    gb = gb_ref[...].astype(_F32)
    pieces = []
    for hd in range(A_HEADS):
        d = oa_ref[2 * hd].astype(_F32) - lam * oa_ref[2 * hd + 1].astype(_F32)
        n = d * lax.rsqrt(jnp.mean(d * d, axis=-1, keepdims=True) + EPS) * subln_ref[...]
        pieces.append(n * (1.0 - lam_init) * ga[:, hd * LANES:(hd + 1) * LANES])
    for hd in range(B_HEADS):
        pieces.append(ob_ref[hd].astype(_F32) * gb[:, hd * LANES:(hd + 1) * LANES])
    y = jnp.concatenate(pieces, axis=1).astype(_BF16)
    y_ref[...] = x_ref[...] + jnp.dot(y, wout_ref[...], preferred_element_type=_F32)


def _out_call(oa, ob, ga, gb, x2d, lw, lam_init, tm):
    t = x2d.shape[0]
    const = lambda i: (0, 0)
    row = lambda i: (i, 0)
    unit_row = lambda i: (0, i, 0)
    return pl.pallas_call(
        functools.partial(_out_kernel, lam_init=lam_init),
        grid=(t // tm,),
        in_specs=[
            pl.BlockSpec((A_UNITS, tm, LANES), unit_row),
            pl.BlockSpec((B_HEADS, tm, LANES), unit_row),
            pl.BlockSpec((tm, 512), row),
            pl.BlockSpec((tm, 512), row),
            pl.BlockSpec((tm, D_MODEL), row),
            pl.BlockSpec((1, A_HEAD_DIM), const),
            pl.BlockSpec((1, A_HEAD_DIM), const),
            pl.BlockSpec((1, A_HEAD_DIM), const),
            pl.BlockSpec((1, A_HEAD_DIM), const),
            pl.BlockSpec((1, A_V_DIM), const),
            pl.BlockSpec((None, D_MODEL, D_MODEL), lambda i: (lw["layer"], 0, 0)),
        ],
        out_specs=pl.BlockSpec((tm, D_MODEL), row),
        out_shape=jax.ShapeDtypeStruct((t, D_MODEL), _F32),
        compiler_params=pltpu.CompilerParams(
            dimension_semantics=("parallel",), vmem_limit_bytes=_VMEM_LIMIT),
        name="out",
    )(oa, ob, ga, gb, x2d, lw["lq1"], lw["lk1"], lw["lq2"], lw["lk2"], lw["subln"], lw["w_out"])


def _rope_tables(seq_len):
    inv = ROPE_THETA ** (-np.arange(0, B_ROPE, 2, dtype=np.float64) / B_ROPE)
    n_hi = -(-seq_len // POS_SPLIT)
    ang_hi = (np.arange(n_hi, dtype=np.float64) * POS_SPLIT)[:, None, None] * inv
    ang_lo = np.arange(POS_SPLIT, dtype=np.float64)[None, :, None] * inv

    def lanes(first, second):
        pad = np.zeros(first.shape[:-1] + (LANES - B_ROPE,))
        return jnp.asarray(np.concatenate([first, second, pad], axis=-1), _F32)

    cos_hi, sin_hi, cos_lo, sin_lo = np.cos(ang_hi), np.sin(ang_hi), np.cos(ang_lo), np.sin(ang_lo)
    cos_lo, sin_lo = lanes(cos_lo, cos_lo), lanes(sin_lo, sin_lo)
    cos = lanes(cos_hi, cos_hi) * cos_lo - lanes(sin_hi, sin_hi) * sin_lo
    neg_sin = lanes(-sin_hi, sin_hi) * cos_lo + lanes(-cos_hi, cos_hi) * sin_lo
    return (cos.reshape(n_hi * POS_SPLIT, LANES)[:seq_len], neg_sin.reshape(n_hi * POS_SPLIT, LANES)[:seq_len])


def _score_bounds(l, p):
    amax = lambda v: jnp.max(jnp.abs(v[l].astype(_F32)))
    bound_a = amax(p["a_q_norm"]) * amax(p["a_k_norm"]) * (A_HEAD_DIM ** 0.5)
    bound_b = amax(p["b_q_norm"]) * amax(p["b_k_norm"]) * (B_QK_DIM ** 0.5)
    return bound_a, bound_b


def _aug_rows(bound_a, bound_b):
    aug_q = np.zeros((A_UNITS, A_HEAD_DIM, LANES), np.float32)
    aug_k = np.zeros((A_UNITS, 1, LANES), np.float32)
    for u in range(A_UNITS):
        sig = _sigma_pieces(u // 2)
        base = _aug_base(u % 2)
        aug_k[u, 0, base + _AUG_SHIFT] = 1.0
        for a in range(SIGMA_PIECES):
            aug_k[u, 0, base + _AUG_QHI + a] = -sig[a]
            aug_k[u, 0, base + _AUG_QLO + a] = -sig[a]
            aug_q[u, _AUG_KHI + a, :] = sig[a]
            aug_q[u, _AUG_KLO + a, :] = sig[a]
    shift_row = (np.arange(A_HEAD_DIM) == _AUG_SHIFT).astype(np.float32).reshape(1, A_HEAD_DIM, 1)
    aug_q = jnp.asarray(aug_q) - (bound_a * LOG2E) * shift_row
    b_lane = (np.arange(LANES) == _B_SHIFT).astype(np.float32)
    aug_b = jnp.stack([-(bound_b * LOG2E) * b_lane, jnp.asarray(b_lane)])
    return aug_q, jnp.asarray(aug_k), aug_b


def _stacked_weights(p):
    w = p["w_in"]
    lo, hi = _SPLITS[6], _SPLITS[7]
    w_in = jnp.concatenate([w[:, :, :hi], _swap_rope_halves(w[:, :, lo:hi]), w[:, :, hi:]], axis=2).astype(_BF16)
    return w_in, p["w_out"].astype(_BF16)


def _layer_weights(l, p, stacked, rope_c, rope_s):
    w_in, w_out = stacked
    wuq = p["b_w_uq"][l].reshape(Q_LORA, B_HEADS, B_QK_DIM)
    w_uq = jnp.concatenate([wuq, _swap_rope_halves(wuq[:, :, B_NOPE:])], axis=2).reshape(Q_LORA, B_HEADS * 256)
    gq = p["b_q_norm"][l]
    gk = p["b_k_norm"][l]
    row = lambda v: v.reshape(1, -1).astype(_F32)
    bound_a, bound_b = _score_bounds(l, p)
    aug_q, aug_k, aug_b = _aug_rows(bound_a, bound_b)
    return {
        "layer": l,
        "norm_w": row(p["norm_w"][l]),
        "w_in": w_in,
        "g_aq": row(jnp.tile(p["a_q_norm"][l], A_UNITS)),
        "g_ak": row(jnp.tile(p["a_k_norm"][l], A_UNITS)),
        "g_cq": row(p["b_cq_norm"][l]),
        "w_uq": w_uq.astype(_BF16),
        "g_ckv": row(p["b_ckv_norm"][l]),
        "w_ukv": p["b_w_ukv"][l].astype(_BF16),
        "g_bq": row(jnp.concatenate([gq, _swap_rope_halves(gq[B_NOPE:])])),
        "g_bkn": row(gk[:B_NOPE]),
        "g_bkr": row(jnp.concatenate([gk[B_NOPE:], _swap_rope_halves(gk[B_NOPE:])])),
        "rope_c": rope_c,
        "rope_s": rope_s,
        "aug_q": aug_q, "aug_k": aug_k, "aug_b": aug_b,
        "lq1": row(p["a_lq1"][l]), "lk1": row(p["a_lk1"][l]),
        "lq2": row(p["a_lq2"][l]), "lk2": row(p["a_lk2"][l]),
        "subln": row(p["a_subln"][l]),
        "w_out": w_out,
    }


def _tile(n, pref):
    return pref if n % pref == 0 else n


def _largest_group(n, cap):
    return max(g for g in range(1, cap + 1) if n % g == 0) if n > 0 else 1


def _encoder_layer(x, lw, lam_init, slopes, fixed_shift):
    b, s, _ = x.shape
    t = b * s
    x2d = x.reshape(t, D_MODEL)
    tm = _tile(s, FIXED_TK)
    qa, ka, va, ga, qb, kb, vb, gb = _proj_call(x2d, s, lw, tm, fixed_shift)
    split = lambda a: a.reshape(a.shape[:-2] + (b, s, a.shape[-1]))
    tq = tk = _tile(s, 512)
    if fixed_shift:
        assert s <= POS_SPLIT * POS_SPLIT * 2
        ftq = _tile(s, FIXED_TQ)
        split_t = lambda a: a.reshape((a.shape[0], b, s // tm) + a.shape[2:])
        oa = _attn_call(qa, split(ka), split_t(va), alibi=True, units_per_v=2, tq=ftq)
        ob = _attn_call(qb, split(kb), split_t(vb), alibi=False, units_per_v=1, tq=ftq)
    else:
        oa = _flash_call(split(qa), split(ka), split(va), slopes, alibi=True, units_per_v=2, tq=tq, tk=tk)
        ob = _flash_call(split(qb), split(kb), split(vb), slopes, alibi=False, units_per_v=1, tq=tq, tk=tk)
    y = _out_call(oa.reshape(A_UNITS, t, LANES), ob.reshape(B_HEADS, t, LANES), ga, gb, x2d, lw, lam_init,
                  _tile(s, OUT_ROWS))
    return y.reshape(b, s, D_MODEL)


def _forward(x_prompt, x_sample, p, fixed_shift):
    depth = p["norm_w"].shape[0]
    rope_c, rope_s = {}, {}
    for s in {x_prompt.shape[1], x_sample.shape[1]}:
        rope_c[s], rope_s[s] = _rope_tables(s)
    slopes = jnp.asarray([_alibi_slope(hd) for hd in range(A_HEADS)], _F32)
    stacked = _stacked_weights(p)
    y_prompt, y_sample = x_prompt, x_sample
    for l in range(depth):
        lw = _layer_weights(l, p, stacked, rope_c, rope_s)
        lam_init = _lambda_init(l)
        y_prompt = _encoder_layer(y_prompt, lw, lam_init, slopes, fixed_shift)
        y_sample = _encoder_layer(y_sample, lw, lam_init, slopes, fixed_shift)
    return (y_prompt, y_sample)


def kernel(x_prompt, x_sample, norm_w, w_in, a_q_norm, a_k_norm, a_lq1, a_lk1, a_lq2, a_lk2, a_subln,
           b_cq_norm, b_w_uq, b_ckv_norm, b_w_ukv, b_q_norm, b_k_norm, w_out):
    p = dict(norm_w=norm_w, w_in=w_in, a_q_norm=a_q_norm, a_k_norm=a_k_norm, a_lq1=a_lq1, a_lk1=a_lk1,
             a_lq2=a_lq2, a_lk2=a_lk2, a_subln=a_subln, b_cq_norm=b_cq_norm, b_w_uq=b_w_uq,
             b_ckv_norm=b_ckv_norm, b_w_ukv=b_w_ukv, b_q_norm=b_q_norm, b_k_norm=b_k_norm, w_out=w_out)
    bounds = jnp.stack([jnp.stack(_score_bounds(l, p)) for l in range(norm_w.shape[0])])
    shift_ok = jnp.max(bounds) <= MAX_STATIC_SHIFT
    return lax.cond(shift_ok,
                    lambda xp, xs, pp: _forward(xp, xs, pp, True),
                    lambda xp, xs, pp: _forward(xp, xs, pp, False),
                    x_prompt, x_sample, p)
```

```python
import functools
import math

import numpy as np
import jax
import jax.numpy as jnp
from jax import lax
from jax.experimental import pallas as pl
from jax.experimental.pallas import tpu as pltpu

D_MODEL = 1024
A_HEADS = 4
A_HEAD_DIM = 64
A_V_DIM = 128
A_UNITS = 2 * A_HEADS
B_HEADS = 4
B_NOPE = 128
B_ROPE = 64
B_QK_DIM = B_NOPE + B_ROPE
B_V_DIM = 128
Q_LORA = 256
KV_LORA = 128
ROPE_THETA = 10000.0
EPS = 1e-6
LANES = 128

_SPLITS = np.cumsum([0, 512, 512, 512, 512, Q_LORA, KV_LORA, B_ROPE, 512])
_P_AQ, _P_AK, _P_AV, _P_AG, _P_CQ, _P_CKV, _P_KR, _P_BG, _P_END = np.cumsum(
    [0, 512, 512, 512, 512, Q_LORA, KV_LORA, 2 * B_ROPE, 512])

_VMEM_LIMIT = 56 * 1024 * 1024
LOG2E = math.log2(math.e)
MAX_STATIC_SHIFT = 32.0
POS_SPLIT = 128
SIGMA_PIECES = 3
FIXED_TQ = 1024
FIXED_TK = 512
MAX_BLOCK_PAIRS = 32
MAX_LOOPED_QUERY_BLOCKS = 4
OUT_ROWS = 1024
VT_ROWS = 144

_AUG_SHIFT = 0
_AUG_QHI = _AUG_SHIFT + 1
_AUG_QLO = _AUG_QHI + SIGMA_PIECES
_AUG_KHI = _AUG_QLO + SIGMA_PIECES
_AUG_KLO = _AUG_KHI + SIGMA_PIECES
_AUG_END = _AUG_KLO + SIGMA_PIECES
_B_SHIFT = B_QK_DIM - LANES


def _aug_base(comp):
    return A_HEAD_DIM if comp == 0 else 0


_F32 = jnp.float32
_BF16 = jnp.bfloat16


def _lambda_init(layer_idx):
    return 0.8 - 0.6 * math.exp(-0.3 * layer_idx)


def _alibi_slope(head):
    return 2.0 ** (-8.0 * (head + 1) / A_HEADS)


def _sigma_pieces(head):
    rest = _alibi_slope(head) * LOG2E
    pieces = []
    for _ in range(SIGMA_PIECES):
        p = float(np.asarray(rest, np.float32).astype(_BF16).astype(np.float32))
        pieces.append(p)
        rest -= p
    return pieces


def _group_sum_matrices():
    unit = np.arange(256) // A_HEAD_DIM
    m_a = (unit[:, None] == unit[None, :]).astype(np.float32)
    m_b = np.zeros((2, 256, 256), np.float32)
    m_b[0, :B_QK_DIM, :] = 1.0
    m_b[1, :B_NOPE, :] = 1.0
    return jnp.asarray(m_a, _BF16), jnp.asarray(m_b, _BF16)


def _swap_rope_halves(a):
    half = B_ROPE // 2
    return jnp.concatenate([a[..., half:], a[..., :half]], axis=-1)


def _silu(x):
    return x / (1.0 + jnp.exp(-x))


def _proj_kernel(x_ref, nw_ref, win_ref, wtail_ref, gaq_ref, gak_ref, gcq_ref, wuq_ref, gckv_ref, wukv_ref,
                 gbq_ref, gbkn_ref, gbkr_ref, ct_ref, st_ref, augq_ref, augk_ref, augb_ref, suma_ref, sumb_ref,
                 qa_ref, ka_ref, va_ref, ga_ref, qb_ref, kb_ref, vb_ref, gb_ref, *, fixed_shift, seq_len):
    tm = x_ref.shape[0]
    rows = slice(0, tm)
    x = x_ref[rows, :]
    h = x * lax.rsqrt(jnp.mean(x * x, axis=-1, keepdims=True) + EPS) * nw_ref[...]
    hb = h.astype(_BF16)

    def proj(lo, hi):
        if lo >= _P_KR:
            return jnp.dot(hb, wtail_ref[:, lo - _P_KR:hi - _P_KR], preferred_element_type=_F32)
        return jnp.dot(hb, win_ref[:, lo:hi], preferred_element_type=_F32)

    lane = lax.broadcasted_iota(jnp.int32, (tm, LANES), 1)
    low_half = lane < B_ROPE
    ones_col = (lane == 0).astype(_F32)
    ct = ct_ref[rows, :]
    st = st_ref[rows, :]

    def aug_select(index, base, hi, lo):
        at = lambda off: (index >= base + off) & (index < base + off + SIGMA_PIECES)
        return at(hi), at(lo)

    if fixed_shift:
        a_qscale = (A_HEAD_DIM ** -0.5) * LOG2E
        b_qscale = (B_QK_DIM ** -0.5) * LOG2E
        first = (pl.program_id(0) % (seq_len // tm)) * tm - seq_len // 2
        split_pos = lambda p: (((p >> 7) * POS_SPLIT).astype(_F32), (p & (POS_SPLIT - 1)).astype(_F32))
        k_hi, k_lo = split_pos(first + lax.broadcasted_iota(jnp.int32, (tm, LANES), 0))
        q_hi, q_lo = split_pos(first + lax.broadcasted_iota(jnp.int32, (A_HEAD_DIM, tm), 1))
        aug_row = lax.broadcasted_iota(jnp.int32, (A_HEAD_DIM, tm), 0)
        at_hi, at_lo = aug_select(aug_row, 0, _AUG_QHI, _AUG_QLO)
        q_pos = jnp.where(at_hi, q_hi, jnp.where(at_lo, q_lo, 0.0))
    else:
        a_qscale = A_HEAD_DIM ** -0.5
        b_qscale = B_QK_DIM ** -0.5

    def group_sums(a, m):
        return jnp.dot((a * a).astype(_BF16), m, preferred_element_type=_F32)

    aq = proj(_P_AQ, _P_AK)
    ak = proj(_P_AK, _P_AV)
    def unit_sums(a):
        return jnp.concatenate([group_sums(a[:, lo:lo + 256], suma_ref[...]) for lo in range(0, a.shape[1], 256)],
                               axis=1)

    qn_all = aq * lax.rsqrt(unit_sums(aq) * (1.0 / A_HEAD_DIM) + EPS) * gaq_ref[...] * a_qscale
    kn_all = ak * lax.rsqrt(unit_sums(ak) * (1.0 / A_HEAD_DIM) + EPS) * gak_ref[...]
    for hd in range(A_HEADS):
        qn2 = qn_all[:, hd * LANES:(hd + 1) * LANES]
        kn2 = kn_all[:, hd * LANES:(hd + 1) * LANES]
        if fixed_shift:
            qn2_t = qn2.T
        for comp in range(2):
            u = 2 * hd + comp
            own = (lane < A_HEAD_DIM) if comp == 0 else (lane >= A_HEAD_DIM)
            if fixed_shift:
                feat = qn2_t[comp * A_HEAD_DIM:(comp + 1) * A_HEAD_DIM]
                aug = jnp.concatenate([augq_ref[u]] * (tm // LANES), axis=1) + q_pos
                stacked = [feat, aug] if comp == 0 else [aug, feat]
                qa_ref[u, 0, :, rows] = jnp.concatenate(stacked, axis=0).astype(_BF16)
                at_hi, at_lo = aug_select(lane, _aug_base(comp), _AUG_KHI, _AUG_KLO)
                k_pos = jnp.where(at_hi, k_hi, jnp.where(at_lo, k_lo, 0.0))
                ka_ref[u, rows, :] = (jnp.where(own, kn2, 0.0) + augk_ref[u] + k_pos).astype(_BF16)
            else:
                qa_ref[u, 0, rows, :] = jnp.where(own, qn2, 0.0).astype(_BF16)
                ka_ref[u, rows, :] = jnp.where(own, kn2, 0.0).astype(_BF16)

    def store_values(v_ref, hd, vv):
        if fixed_shift:
            pad_rows = lax.broadcasted_iota(jnp.int32, (VT_ROWS - LANES, tm), 0)
            v_ref[hd, 0, :, rows] = jnp.concatenate([vv.T, (pad_rows == 0).astype(_F32)], axis=0).astype(_BF16)
        else:
            v_ref[hd, rows, :] = jnp.concatenate([vv, ones_col], axis=1).astype(_BF16)

    av = proj(_P_AV, _P_AG)
    for hd in range(A_HEADS):
        store_values(va_ref, hd, av[:, hd * LANES:(hd + 1) * LANES])
    ga_ref[rows, :] = _silu(proj(_P_AG, _P_CQ)).astype(_BF16)

    cq = proj(_P_CQ, _P_CKV)
    cqn = cq * lax.rsqrt(jnp.mean(cq * cq, axis=-1, keepdims=True) + EPS) * gcq_ref[...]
    qall = jnp.dot(cqn.astype(_BF16), wuq_ref[...], preferred_element_type=_F32)
    ckv = proj(_P_CKV, _P_KR)
    ckvn = ckv * lax.rsqrt(jnp.mean(ckv * ckv, axis=-1, keepdims=True) + EPS) * gckv_ref[...]
    kv = jnp.dot(ckvn.astype(_BF16), wukv_ref[...], preferred_element_type=_F32)
    kr2 = proj(_P_KR, _P_BG)

    for hd in range(B_HEADS):
        q0 = qall[:, hd * 256:hd * 256 + LANES]
        q1 = qall[:, hd * 256 + LANES:(hd + 1) * 256]
        r = lax.rsqrt(group_sums(qall[:, hd * 256:(hd + 1) * 256], sumb_ref[0]) * (1.0 / B_QK_DIM) + EPS)
        q0n = q0 * r[:, :LANES] * gbq_ref[:, :LANES] * b_qscale
        q1n = q1 * r[:, LANES:] * gbq_ref[:, LANES:]
        q1r = (q1n * ct + pltpu.roll(q1n, B_ROPE, 1) * st) * b_qscale
        if fixed_shift:
            q1r = q1r + augb_ref[0:1, :]
            qb_ref[hd, 0, :, rows] = jnp.concatenate([q0n.T, q1r.T], axis=0).astype(_BF16)
        else:
            qb_ref[hd, 0, rows, :] = jnp.concatenate([q0n, q1r], axis=1).astype(_BF16)

    kr_ss = jnp.sum(jnp.where(low_half, kr2 * kr2, 0.0), axis=-1, keepdims=True)
    krg = kr2 * gbkr_ref[...]
    krr = krg * ct + pltpu.roll(krg, B_ROPE, 1) * st
    for hd in range(B_HEADS):
        kn = kv[:, hd * 256:hd * 256 + LANES]
        vv = kv[:, hd * 256 + LANES:(hd + 1) * 256]
        kn_ss = group_sums(kv[:, hd * 256:(hd + 1) * 256], sumb_ref[1])[:, :LANES]
        r = lax.rsqrt((kn_ss + kr_ss) * (1.0 / B_QK_DIM) + EPS)
        k1 = krr * r
        if fixed_shift:
            k1 = k1 + augb_ref[1:2, :]
        kb_ref[hd, rows, :] = jnp.concatenate([kn * r * gbkn_ref[...], k1], axis=1).astype(_BF16)
        store_values(vb_ref, hd, vv)
    gb_ref[rows, :] = _silu(proj(_P_BG, _P_END)).astype(_BF16)


def _proj_call(x2d, seq_len, lw, tm, fixed_shift):
    t = x2d.shape[0]
    blocks_per_seq = seq_len // tm
    const = lambda i: (0, 0)
    const3 = lambda i: (0, 0, 0)
    row = lambda i: (i, 0)
    unit_row = lambda i: (0, i, 0)
    var_row = lambda i: (0, 0, i, 0)
    rope_row = lambda i: (i % blocks_per_seq, 0)
    in_specs = [
        pl.BlockSpec((tm, D_MODEL), row),
        pl.BlockSpec((1, D_MODEL), const),
        pl.BlockSpec((None, D_MODEL, int(_SPLITS[-1])), lambda i: (lw["layer"], 0, 0)),
        pl.BlockSpec((None, D_MODEL, int(_P_END - _P_KR)), lambda i: (lw["layer"], 0, 0)),
        pl.BlockSpec((1, A_UNITS * A_HEAD_DIM), const),
        pl.BlockSpec((1, A_UNITS * A_HEAD_DIM), const),
        pl.BlockSpec((1, Q_LORA), const),
        pl.BlockSpec((Q_LORA, B_HEADS * 256), const),
        pl.BlockSpec((1, KV_LORA), const),
        pl.BlockSpec((KV_LORA, B_HEADS * 256), const),
        pl.BlockSpec((1, 256), const),
        pl.BlockSpec((1, LANES), const),
        pl.BlockSpec((1, LANES), const),
        pl.BlockSpec((tm, LANES), rope_row),
        pl.BlockSpec((tm, LANES), rope_row),
        pl.BlockSpec((A_UNITS, A_HEAD_DIM, LANES), const3),
        pl.BlockSpec((A_UNITS, 1, LANES), const3),
        pl.BlockSpec((2, LANES), const),
        pl.BlockSpec((256, 256), const),
        pl.BlockSpec((2, 256, 256), const3),
    ]
    if fixed_shift:
        q_shape = lambda units, dk: jax.ShapeDtypeStruct((units, t // tm, dk, tm), _BF16)
        q_spec = lambda units, dk: pl.BlockSpec((units, 1, dk, tm), lambda i: (0, i, 0, 0))
        v_shape = lambda heads: jax.ShapeDtypeStruct((heads, t // tm, VT_ROWS, tm), _BF16)
        v_spec = lambda heads: pl.BlockSpec((heads, 1, VT_ROWS, tm), lambda i: (0, i, 0, 0))
    else:
        q_shape = lambda units, dk: jax.ShapeDtypeStruct((units, 1, t, dk), _BF16)
        q_spec = lambda units, dk: pl.BlockSpec((units, 1, tm, dk), var_row)
        v_shape = lambda heads: jax.ShapeDtypeStruct((heads, t, 256), _BF16)
        v_spec = lambda heads: pl.BlockSpec((heads, tm, 256), unit_row)
    out_shape = [
        q_shape(A_UNITS, LANES),
        jax.ShapeDtypeStruct((A_UNITS, t, LANES), _BF16),
        v_shape(A_HEADS),
        jax.ShapeDtypeStruct((t, 512), _BF16),
        q_shape(B_HEADS, 256),
        jax.ShapeDtypeStruct((B_HEADS, t, 256), _BF16),
        v_shape(B_HEADS),
        jax.ShapeDtypeStruct((t, 512), _BF16),
    ]
    out_specs = [
        q_spec(A_UNITS, LANES),
        pl.BlockSpec((A_UNITS, tm, LANES), unit_row),
        v_spec(A_HEADS),
        pl.BlockSpec((tm, 512), row),
        q_spec(B_HEADS, 256),
        pl.BlockSpec((B_HEADS, tm, 256), unit_row),
        v_spec(B_HEADS),
        pl.BlockSpec((tm, 512), row),
    ]
    return pl.pallas_call(
        functools.partial(_proj_kernel, fixed_shift=fixed_shift, seq_len=seq_len),
        grid=(t // tm,),
        in_specs=in_specs,
        out_specs=out_specs,
        out_shape=out_shape,
        compiler_params=pltpu.CompilerParams(
            dimension_semantics=("parallel",), vmem_limit_bytes=_VMEM_LIMIT),
        name="proj",
    )(x2d, lw["norm_w"], lw["w_in"], lw["w_tail"], lw["g_aq"], lw["g_ak"], lw["g_cq"], lw["w_uq"], lw["g_ckv"],
      lw["w_ukv"], lw["g_bq"], lw["g_bkn"], lw["g_bkr"], lw["rope_c"][seq_len], lw["rope_s"][seq_len],
      lw["aug_q"], lw["aug_k"], lw["aug_b"], *_group_sum_matrices())


def _flash_kernel(slope_ref, q_ref, k_ref, v_ref, o_ref, m_sc, acc_sc, *, alibi, heads_per_slope):
    kj = pl.program_id(3)
    tq = q_ref.shape[3]
    tk = k_ref.shape[2]

    @pl.when(kj == 0)
    def _():
        m_sc[...] = jnp.full(m_sc.shape, -jnp.inf, _F32)
        acc_sc[...] = jnp.zeros(acc_sc.shape, _F32)

    s = lax.dot_general(q_ref[0, 0, 0], k_ref[0, 0], (((1,), (1,)), ((), ())),
                        preferred_element_type=_F32)
    if alibi:
        slope = slope_ref[pl.program_id(0) // heads_per_slope]
        qpos = pl.program_id(2) * tq + lax.broadcasted_iota(jnp.int32, (tq, tk), 0)
        kpos = kj * tk + lax.broadcasted_iota(jnp.int32, (tq, tk), 1)
        s = s - slope * jnp.abs(qpos - kpos).astype(_F32)
    m_prev = m_sc[...]
    m_new = jnp.maximum(m_prev, jnp.max(s, axis=-1, keepdims=True))
    alpha = jnp.exp(m_prev - m_new)
    p = jnp.exp(s - m_new)
    acc_sc[...] = alpha * acc_sc[...] + jnp.dot(p.astype(_BF16), v_ref[0, 0],
                                                 preferred_element_type=_F32)
    m_sc[...] = m_new

    @pl.when(kj == pl.num_programs(3) - 1)
    def _():
        acc = acc_sc[...]
        o_ref[0, 0] = acc[:, :LANES] / acc[:, LANES:LANES + 1]


def _flash_call(q, k, v, slopes, *, alibi, units_per_v, tq, tk):
    u, _, b, s, dk = q.shape
    kern = functools.partial(_flash_kernel, alibi=alibi, heads_per_slope=units_per_v)
    grid_spec = pltpu.PrefetchScalarGridSpec(
        num_scalar_prefetch=1,
        grid=(u, b, s // tq, s // tk),
        in_specs=[
            pl.BlockSpec((1, 1, 1, tq, dk), lambda ui, bi, qi, ki, sl: (ui, 0, bi, qi, 0)),
            pl.BlockSpec((1, 1, tk, dk), lambda ui, bi, qi, ki, sl: (ui, bi, ki, 0)),
            pl.BlockSpec((1, 1, tk, 256), lambda ui, bi, qi, ki, sl: (ui // units_per_v, bi, ki, 0)),
        ],
        out_specs=pl.BlockSpec((1, 1, tq, LANES), lambda ui, bi, qi, ki, sl: (ui, bi, qi, 0)),
        scratch_shapes=[pltpu.VMEM((tq, 1), _F32), pltpu.VMEM((tq, 256), _F32)],
    )
    return pl.pallas_call(
        kern,
        grid_spec=grid_spec,
        out_shape=jax.ShapeDtypeStruct((u, b, s, LANES), _F32),
        compiler_params=pltpu.CompilerParams(
            dimension_semantics=("parallel", "parallel", "parallel", "arbitrary"),
            vmem_limit_bytes=_VMEM_LIMIT),
        name="flash_a" if alibi else "flash_b",
    )(slopes, q, k, v)


def _attn_kernel(qt_ref, k_ref, vt_ref, o_ref, qv_sc, *, tq, q_steps, alibi):
    n_k, _, tk = vt_ref.shape[2:]
    q_sub = qt_ref.shape[1] * qt_ref.shape[3] // tq
    n_diag = tq // tk
    if alibi:
        head = pl.program_id(0) // 2
        sigma = jnp.float32(_alibi_slope(A_HEADS - 1) * LOG2E)
        for hd in range(A_HEADS - 1):
            sigma = jnp.where(head == hd, jnp.float32(_alibi_slope(hd) * LOG2E), sigma)
        ahead = (lax.broadcasted_iota(jnp.int32, (tk, tq), 0) - lax.broadcasted_iota(jnp.int32, (tk, tq), 1))
        aug_row = lax.broadcasted_iota(jnp.int32, (qt_ref.shape[2], tq), 0)
        aug_row = aug_row - jnp.where(pl.program_id(0) % 2 == 0, A_HEAD_DIM, 0)
        bias_rows = (aug_row >= _AUG_QHI) & (aug_row < _AUG_END)

    def query_block(sub, qi, slot):
        pieces = tq // qt_ref.shape[3]
        qt = jnp.concatenate([qt_ref[0, sub * pieces + piece] for piece in range(pieces)], axis=1)
        qv_sc[slot, 0] = qt
        if alibi:
            qv_sc[slot, 1] = jnp.where(bias_rows, -qt, qt)

        def block(var, j, correction=None):
            k = k_ref[0, 0, pl.ds(pl.multiple_of(j * tk, tk), tk), :]
            st = jnp.dot(k, qv_sc[slot, var], preferred_element_type=_F32)
            if correction is not None:
                st = st + correction
            return jnp.dot(vt_ref[0, 0, j], jnp.exp2(st).astype(_BF16), preferred_element_type=_F32)

        def accumulate(acc, part):
            return part if acc is None else acc + part

        acc = None
        if alibi:
            first_diag = qi * n_diag
            for d in range(n_diag):
                corr = (-2.0 * sigma) * jnp.maximum(ahead + d * tk, 0).astype(_F32)
                acc = accumulate(acc, block(0, first_diag + d, corr))
            for x in range(n_k - n_diag):
                after = (x >= first_diag) * 1
                acc = accumulate(acc, block(after, x + n_diag * after))
        else:
            for x in range(n_k):
                acc = accumulate(acc, block(0, x))
        rows = pl.ds(pl.multiple_of(sub * tq, tq), tq)
        o_ref[0, 0, rows, :] = (acc[:LANES] / acc[LANES:LANES + 1]).T.astype(o_ref.dtype)

    if q_steps == 1 or q_sub == 1:
        for sub in range(q_sub):
            query_block(sub, sub if q_steps == 1 else pl.program_id(2), sub)
    else:
        def body(sub, carry):
            query_block(sub, pl.program_id(2) * q_sub + sub, 0)
            return carry

        lax.fori_loop(0, q_sub, body, 0)


def _attn_call(qt, k, vt, *, alibi, units_per_v, tq):
    u, _, dk, tm = qt.shape
    _, b, s, _ = k.shape
    n_k, _, tk = vt.shape[2:]
    assert tq % tk == 0 and s % tq == 0 and tq % tm == 0, (s, tq, tk, tm)
    n_q = s // tq
    if n_q * n_k <= MAX_BLOCK_PAIRS:
        q_sub, slots = n_q, n_q
    else:
        q_sub, slots = _largest_group(n_q, MAX_LOOPED_QUERY_BLOCKS), 1
    q_steps = n_q // q_sub
    pieces = tq * q_sub // tm
    kern = functools.partial(_attn_kernel, tq=tq, q_steps=q_steps, alibi=alibi)
    return pl.pallas_call(
        kern,
        grid=(u, b, q_steps),
        in_specs=[
            pl.BlockSpec((1, pieces, dk, tm), lambda ui, bi, qi: (ui, bi * q_steps + qi, 0, 0)),
            pl.BlockSpec((1, 1, s, dk), lambda ui, bi, qi: (ui, bi, 0, 0)),
            pl.BlockSpec((1, 1, n_k, VT_ROWS, tk), lambda ui, bi, qi: (ui // units_per_v, bi, 0, 0, 0)),
        ],
        out_specs=pl.BlockSpec((1, 1, tq * q_sub, LANES), lambda ui, bi, qi: (ui, bi, qi, 0)),
        out_shape=jax.ShapeDtypeStruct((u, b, s, LANES), _BF16),
        scratch_shapes=[pltpu.VMEM((slots, 2 if alibi else 1, dk, tq), _BF16)],
        compiler_params=pltpu.CompilerParams(
            dimension_semantics=("parallel", "parallel", "parallel"),
            vmem_limit_bytes=_VMEM_LIMIT),
        name="attn_a" if alibi else "attn_b",
    )(qt, k, vt)


def _out_kernel(oa_ref, ob_ref, ga_ref, gb_ref, x_ref, lq1_ref, lk1_ref, lq2_ref, lk2_ref, subln_ref,
                wout_ref, y_ref, *, lam_init):
    lam = (jnp.exp(jnp.sum(lq1_ref[...] * lk1_ref[...], axis=-1, keepdims=True))
           - jnp.exp(jnp.sum(lq2_ref[...] * lk2_ref[...], axis=-1, keepdims=True)) + lam_init)
    ga = ga_ref[...].astype(_F32)
    gb = gb_ref[...].astype(_F32)
    pieces = []
    for hd in range(A_HEADS):
        d = oa_ref[2 * hd].astype(_F32) - lam * oa_ref[2 * hd + 1].astype(_F32)
        n = d * lax.rsqrt(jnp.mean(d * d, axis=-1, keepdims=True) + EPS) * subln_ref[...]
        pieces.append(n * (1.0 - lam_init) * ga[:, hd * LANES:(hd + 1) * LANES])
    for hd in range(B_HEADS):
        pieces.append(ob_ref[hd].astype(_F32) * gb[:, hd * LANES:(hd + 1) * LANES])
    y = jnp.concatenate(pieces, axis=1).astype(_BF16)
    y_ref[...] = x_ref[...] + jnp.dot(y, wout_ref[...], preferred_element_type=_F32)


def _out_call(oa, ob, ga, gb, x2d, lw, lam_init, tm):
    t = x2d.shape[0]
    const = lambda i: (0, 0)
    row = lambda i: (i, 0)
    unit_row = lambda i: (0, i, 0)
    return pl.pallas_call(
        functools.partial(_out_kernel, lam_init=lam_init),
        grid=(t // tm,),
        in_specs=[
            pl.BlockSpec((A_UNITS, tm, LANES), unit_row),
            pl.BlockSpec((B_HEADS, tm, LANES), unit_row),
            pl.BlockSpec((tm, 512), row),
            pl.BlockSpec((tm, 512), row),
            pl.BlockSpec((tm, D_MODEL), row),
            pl.BlockSpec((1, A_HEAD_DIM), const),
            pl.BlockSpec((1, A_HEAD_DIM), const),
            pl.BlockSpec((1, A_HEAD_DIM), const),
            pl.BlockSpec((1, A_HEAD_DIM), const),
            pl.BlockSpec((1, A_V_DIM), const),
            pl.BlockSpec((None, D_MODEL, D_MODEL), lambda i: (lw["layer"], 0, 0)),
        ],
        out_specs=pl.BlockSpec((tm, D_MODEL), row),
        out_shape=jax.ShapeDtypeStruct((t, D_MODEL), _F32),
        compiler_params=pltpu.CompilerParams(
            dimension_semantics=("parallel",), vmem_limit_bytes=_VMEM_LIMIT),
        name="out",
    )(oa, ob, ga, gb, x2d, lw["lq1"], lw["lk1"], lw["lq2"], lw["lk2"], lw["subln"], lw["w_out"])


def _rope_tables(seq_len):
    inv = ROPE_THETA ** (-np.arange(0, B_ROPE, 2, dtype=np.float64) / B_ROPE)
    n_hi = -(-seq_len // POS_SPLIT)
    ang_hi = (np.arange(n_hi, dtype=np.float64) * POS_SPLIT)[:, None, None] * inv
    ang_lo = np.arange(POS_SPLIT, dtype=np.float64)[None, :, None] * inv

    def lanes(first, second):
        pad = np.zeros(first.shape[:-1] + (LANES - B_ROPE,))
        return jnp.asarray(np.concatenate([first, second, pad], axis=-1), _F32)

    cos_hi, sin_hi, cos_lo, sin_lo = np.cos(ang_hi), np.sin(ang_hi), np.cos(ang_lo), np.sin(ang_lo)
    cos_lo, sin_lo = lanes(cos_lo, cos_lo), lanes(sin_lo, sin_lo)
    cos = lanes(cos_hi, cos_hi) * cos_lo - lanes(sin_hi, sin_hi) * sin_lo
    neg_sin = lanes(-sin_hi, sin_hi) * cos_lo + lanes(-cos_hi, cos_hi) * sin_lo
    return (cos.reshape(n_hi * POS_SPLIT, LANES)[:seq_len], neg_sin.reshape(n_hi * POS_SPLIT, LANES)[:seq_len])


def _score_bounds(l, p):
    amax = lambda v: jnp.max(jnp.abs(v[l].astype(_F32)))
    bound_a = amax(p["a_q_norm"]) * amax(p["a_k_norm"]) * (A_HEAD_DIM ** 0.5)
    bound_b = amax(p["b_q_norm"]) * amax(p["b_k_norm"]) * (B_QK_DIM ** 0.5)
    return bound_a, bound_b


def _aug_rows(bound_a, bound_b):
    aug_q = np.zeros((A_UNITS, A_HEAD_DIM, LANES), np.float32)
    aug_k = np.zeros((A_UNITS, 1, LANES), np.float32)
    for u in range(A_UNITS):
        sig = _sigma_pieces(u // 2)
        base = _aug_base(u % 2)
        aug_k[u, 0, base + _AUG_SHIFT] = 1.0
        for a in range(SIGMA_PIECES):
            aug_k[u, 0, base + _AUG_QHI + a] = -sig[a]
            aug_k[u, 0, base + _AUG_QLO + a] = -sig[a]
            aug_q[u, _AUG_KHI + a, :] = sig[a]
            aug_q[u, _AUG_KLO + a, :] = sig[a]
    shift_row = (np.arange(A_HEAD_DIM) == _AUG_SHIFT).astype(np.float32).reshape(1, A_HEAD_DIM, 1)
    aug_q = jnp.asarray(aug_q) - (bound_a * LOG2E) * shift_row
    b_lane = (np.arange(LANES) == _B_SHIFT).astype(np.float32)
    aug_b = jnp.stack([-(bound_b * LOG2E) * b_lane, jnp.asarray(b_lane)])
    return aug_q, jnp.asarray(aug_k), aug_b


def _stacked_weights(p):
    w = p["w_in"].astype(_BF16)
    lo, hi = _SPLITS[6], _SPLITS[7]
    w_tail = jnp.concatenate([w[:, :, lo:hi], _swap_rope_halves(w[:, :, lo:hi]), w[:, :, hi:]], axis=2)
    return w, w_tail, p["w_out"].astype(_BF16)


def _layer_weights(l, p, stacked, rope_c, rope_s):
    w_in, w_tail, w_out = stacked
    wuq = p["b_w_uq"][l].reshape(Q_LORA, B_HEADS, B_QK_DIM)
    w_uq = jnp.concatenate([wuq, _swap_rope_halves(wuq[:, :, B_NOPE:])], axis=2).reshape(Q_LORA, B_HEADS * 256)
    gq = p["b_q_norm"][l]
    gk = p["b_k_norm"][l]
    row = lambda v: v.reshape(1, -1).astype(_F32)
    bound_a, bound_b = _score_bounds(l, p)
    aug_q, aug_k, aug_b = _aug_rows(bound_a, bound_b)
    return {
        "layer": l,
        "norm_w": row(p["norm_w"][l]),
        "w_in": w_in,
        "w_tail": w_tail,
        "g_aq": row(jnp.tile(p["a_q_norm"][l], A_UNITS)),
        "g_ak": row(jnp.tile(p["a_k_norm"][l], A_UNITS)),
        "g_cq": row(p["b_cq_norm"][l]),
        "w_uq": w_uq.astype(_BF16),
        "g_ckv": row(p["b_ckv_norm"][l]),
        "w_ukv": p["b_w_ukv"][l].astype(_BF16),
        "g_bq": row(jnp.concatenate([gq, _swap_rope_halves(gq[B_NOPE:])])),
        "g_bkn": row(gk[:B_NOPE]),
        "g_bkr": row(jnp.concatenate([gk[B_NOPE:], _swap_rope_halves(gk[B_NOPE:])])),
        "rope_c": rope_c,
        "rope_s": rope_s,
        "aug_q": aug_q, "aug_k": aug_k, "aug_b": aug_b,
        "lq1": row(p["a_lq1"][l]), "lk1": row(p["a_lk1"][l]),
        "lq2": row(p["a_lq2"][l]), "lk2": row(p["a_lk2"][l]),
        "subln": row(p["a_subln"][l]),
        "w_out": w_out,
    }


def _tile(n, pref):
    return pref if n % pref == 0 else n


def _largest_group(n, cap):
    return max(g for g in range(1, cap + 1) if n % g == 0) if n > 0 else 1


def _encoder_layer(x, lw, lam_init, slopes, fixed_shift):
    b, s, _ = x.shape
    t = b * s
    x2d = x.reshape(t, D_MODEL)
    tm = _tile(s, FIXED_TK)
    qa, ka, va, ga, qb, kb, vb, gb = _proj_call(x2d, s, lw, tm, fixed_shift)
    split = lambda a: a.reshape(a.shape[:-2] + (b, s, a.shape[-1]))
    tq = tk = _tile(s, 512)
    if fixed_shift:
        assert s <= POS_SPLIT * POS_SPLIT * 2
        ftq = _tile(s, FIXED_TQ)
        split_t = lambda a: a.reshape((a.shape[0], b, s // tm) + a.shape[2:])
        oa = _attn_call(qa, split(ka), split_t(va), alibi=True, units_per_v=2, tq=ftq)
        ob = _attn_call(qb, split(kb), split_t(vb), alibi=False, units_per_v=1, tq=ftq)
    else:
        oa = _flash_call(split(qa), split(ka), split(va), slopes, alibi=True, units_per_v=2, tq=tq, tk=tk)
        ob = _flash_call(split(qb), split(kb), split(vb), slopes, alibi=False, units_per_v=1, tq=tq, tk=tk)
    y = _out_call(oa.reshape(A_UNITS, t, LANES), ob.reshape(B_HEADS, t, LANES), ga, gb, x2d, lw, lam_init,
                  _tile(s, OUT_ROWS))
    return y.reshape(b, s, D_MODEL)


def _forward(x_prompt, x_sample, p, fixed_shift):
    depth = p["norm_w"].shape[0]
    rope_c, rope_s = {}, {}
    for s in {x_prompt.shape[1], x_sample.shape[1]}:
        rope_c[s], rope_s[s] = _rope_tables(s)
    slopes = jnp.asarray([_alibi_slope(hd) for hd in range(A_HEADS)], _F32)
    stacked = _stacked_weights(p)
    y_prompt, y_sample = x_prompt, x_sample
    for l in range(depth):
        lw = _layer_weights(l, p, stacked, rope_c, rope_s)
        lam_init = _lambda_init(l)
        y_prompt = _encoder_layer(y_prompt, lw, lam_init, slopes, fixed_shift)
        y_sample = _encoder_layer(y_sample, lw, lam_init, slopes, fixed_shift)
    return (y_prompt, y_sample)


def kernel(x_prompt, x_sample, norm_w, w_in, a_q_norm, a_k_norm, a_lq1, a_lk1, a_lq2, a_lk2, a_subln,
           b_cq_norm, b_w_uq, b_ckv_norm, b_w_ukv, b_q_norm, b_k_norm, w_out):
    p = dict(norm_w=norm_w, w_in=w_in, a_q_norm=a_q_norm, a_k_norm=a_k_norm, a_lq1=a_lq1, a_lk1=a_lk1,
             a_lq2=a_lq2, a_lk2=a_lk2, a_subln=a_subln, b_cq_norm=b_cq_norm, b_w_uq=b_w_uq,
             b_ckv_norm=b_ckv_norm, b_w_ukv=b_w_ukv, b_q_norm=b_q_norm, b_k_norm=b_k_norm, w_out=w_out)
    bounds = jnp.stack([jnp.stack(_score_bounds(l, p)) for l in range(norm_w.shape[0])])
    shift_ok = jnp.max(bounds) <= MAX_STATIC_SHIFT
    return lax.cond(shift_ok,
                    lambda xp, xs, pp: _forward(xp, xs, pp, True),
                    lambda xp, xs, pp: _forward(xp, xs, pp, False),
                    x_prompt, x_sample, p)
```

```python
import functools
import math

import numpy as np
import jax
import jax.numpy as jnp
from jax import lax
from jax.experimental import pallas as pl
from jax.experimental.pallas import tpu as pltpu

D_MODEL = 1024
A_HEADS = 4
A_HEAD_DIM = 64
A_V_DIM = 128
A_UNITS = 2 * A_HEADS
B_HEADS = 4
B_NOPE = 128
B_ROPE = 64
B_QK_DIM = B_NOPE + B_ROPE
Q_LORA = 256
KV_LORA = 128
ROPE_THETA = 10000.0
EPS = 1e-6
LANES = 128

_SPLITS = np.cumsum([0, 512, 512, 512, 512, Q_LORA, KV_LORA, B_ROPE, 512])
_P_AQ, _P_AK, _P_AV, _P_AG, _P_CQ, _P_CKV, _P_KR, _P_BG, _P_END = np.cumsum(
    [0, 512, 512, 512, 512, Q_LORA, KV_LORA, 2 * B_ROPE, 512])

_VMEM_LIMIT = 56 * 1024 * 1024
LOG2E = math.log2(math.e)
MAX_STATIC_SHIFT = 32.0
POS_SPLIT = 128
SIGMA_PIECES = 3
FIXED_TQ = 1024
FIXED_TK = 512
MAX_BLOCK_PAIRS = 32
MAX_LOOPED_QUERY_BLOCKS = 4
OUT_ROWS = 1024
VT_ROWS = 144

_AUG_SHIFT = 0
_AUG_QHI = _AUG_SHIFT + 1
_AUG_QLO = _AUG_QHI + SIGMA_PIECES
_AUG_KHI = _AUG_QLO + SIGMA_PIECES
_AUG_KLO = _AUG_KHI + SIGMA_PIECES
_AUG_END = _AUG_KLO + SIGMA_PIECES
_B_SHIFT = B_QK_DIM - LANES


def _aug_base(comp):
    return A_HEAD_DIM if comp == 0 else 0


_F32 = jnp.float32
_BF16 = jnp.bfloat16


def _lambda_init(layer_idx):
    return 0.8 - 0.6 * math.exp(-0.3 * layer_idx)


def _alibi_slope(head):
    return 2.0 ** (-8.0 * (head + 1) / A_HEADS)


def _sigma_pieces(head):
    rest = _alibi_slope(head) * LOG2E
    pieces = []
    for _ in range(SIGMA_PIECES):
        p = float(np.asarray(rest, np.float32).astype(_BF16).astype(np.float32))
        pieces.append(p)
        rest -= p
    return pieces


def _group_sum_matrices():
    unit = np.arange(256) // A_HEAD_DIM
    m_a = (unit[:, None] == unit[None, :]).astype(np.float32)
    m_b = np.zeros((2, 256, 256), np.float32)
    m_b[0, :B_QK_DIM, :] = 1.0
    m_b[1, :B_NOPE, :] = 1.0
    return jnp.asarray(m_a, _BF16), jnp.asarray(m_b, _BF16)


def _swap_rope_halves(a):
    half = B_ROPE // 2
    return jnp.concatenate([a[..., half:], a[..., :half]], axis=-1)


def _silu(x):
    return x / (1.0 + jnp.exp(-x))


def _proj_kernel(x_ref, nw_ref, win_ref, wtail_ref, gaq_ref, gak_ref, gcq_ref, wuq_ref, gckv_ref, wukv_ref,
                 gbq_ref, gbkn_ref, gbkr_ref, ct_ref, st_ref, augq_ref, augk_ref, augb_ref, suma_ref, sumb_ref,
                 qa_ref, ka_ref, va_ref, ga_ref, qb_ref, kb_ref, vb_ref, gb_ref, *, fixed_shift, seq_len):
    tm = x_ref.shape[0]
    rows = slice(0, tm)
    x = x_ref[rows, :]
    h = x * lax.rsqrt(jnp.mean(x * x, axis=-1, keepdims=True) + EPS) * nw_ref[...]
    hb = h.astype(_BF16)

    def proj(lo, hi):
        if lo >= _P_KR:
            return jnp.dot(hb, wtail_ref[:, lo - _P_KR:hi - _P_KR], preferred_element_type=_F32)
        return jnp.dot(hb, win_ref[:, lo:hi], preferred_element_type=_F32)

    lane = lax.broadcasted_iota(jnp.int32, (tm, LANES), 1)
    low_half = lane < B_ROPE
    ones_col = (lane == 0).astype(_F32)
    ct = ct_ref[rows, :]
    st = st_ref[rows, :]

    def aug_select(index, base, hi, lo):
        at = lambda off: (index >= base + off) & (index < base + off + SIGMA_PIECES)
        return at(hi), at(lo)

    if fixed_shift:
        a_qscale = (A_HEAD_DIM ** -0.5) * LOG2E
        b_qscale = (B_QK_DIM ** -0.5) * LOG2E
        first = (pl.program_id(0) % (seq_len // tm)) * tm - seq_len // 2
        split_pos = lambda p: (((p >> 7) * POS_SPLIT).astype(_F32), (p & (POS_SPLIT - 1)).astype(_F32))
        k_hi, k_lo = split_pos(first + lax.broadcasted_iota(jnp.int32, (tm, LANES), 0))
        q_hi, q_lo = split_pos(first + lax.broadcasted_iota(jnp.int32, (A_HEAD_DIM, tm), 1))
        aug_row = lax.broadcasted_iota(jnp.int32, (A_HEAD_DIM, tm), 0)
        at_hi, at_lo = aug_select(aug_row, 0, _AUG_QHI, _AUG_QLO)
        q_pos = jnp.where(at_hi, q_hi, jnp.where(at_lo, q_lo, 0.0))
    else:
        a_qscale = A_HEAD_DIM ** -0.5
        b_qscale = B_QK_DIM ** -0.5

    def group_sums(a, m):
        return jnp.dot((a * a).astype(_BF16), m, preferred_element_type=_F32)

    aq = proj(_P_AQ, _P_AK)
    ak = proj(_P_AK, _P_AV)
    def unit_sums(a):
        return jnp.concatenate([group_sums(a[:, lo:lo + 256], suma_ref[...]) for lo in range(0, a.shape[1], 256)],
                               axis=1)

    qn_all = aq * lax.rsqrt(unit_sums(aq) * (1.0 / A_HEAD_DIM) + EPS) * gaq_ref[...] * a_qscale
    kn_all = ak * lax.rsqrt(unit_sums(ak) * (1.0 / A_HEAD_DIM) + EPS) * gak_ref[...]
    for hd in range(A_HEADS):
        qn2 = qn_all[:, hd * LANES:(hd + 1) * LANES]
        kn2 = kn_all[:, hd * LANES:(hd + 1) * LANES]
        if fixed_shift:
            qn2_t = qn2.T
        for comp in range(2):
            u = 2 * hd + comp
            own = (lane < A_HEAD_DIM) if comp == 0 else (lane >= A_HEAD_DIM)
            if fixed_shift:
                feat = qn2_t[comp * A_HEAD_DIM:(comp + 1) * A_HEAD_DIM]
                aug = jnp.concatenate([augq_ref[u]] * (tm // LANES), axis=1) + q_pos
                stacked = [feat, aug] if comp == 0 else [aug, feat]
                qa_ref[u, 0, :, rows] = jnp.concatenate(stacked, axis=0).astype(_BF16)
                at_hi, at_lo = aug_select(lane, _aug_base(comp), _AUG_KHI, _AUG_KLO)
                k_pos = jnp.where(at_hi, k_hi, jnp.where(at_lo, k_lo, 0.0))
                ka_ref[u, rows, :] = (jnp.where(own, kn2, 0.0) + augk_ref[u] + k_pos).astype(_BF16)
            else:
                qa_ref[u, 0, rows, :] = jnp.where(own, qn2, 0.0).astype(_BF16)
                ka_ref[u, rows, :] = jnp.where(own, kn2, 0.0).astype(_BF16)

    def store_values(v_ref, hd, vv):
        if fixed_shift:
            pad_rows = lax.broadcasted_iota(jnp.int32, (VT_ROWS - LANES, tm), 0)
            v_ref[hd, 0, :, rows] = jnp.concatenate([vv.T, (pad_rows == 0).astype(_F32)], axis=0).astype(_BF16)
        else:
            v_ref[hd, rows, :] = jnp.concatenate([vv, ones_col], axis=1).astype(_BF16)

    av = proj(_P_AV, _P_AG)
    for hd in range(A_HEADS):
        store_values(va_ref, hd, av[:, hd * LANES:(hd + 1) * LANES])
    ga_ref[rows, :] = _silu(proj(_P_AG, _P_CQ)).astype(_BF16)

    cq = proj(_P_CQ, _P_CKV)
    cqn = cq * lax.rsqrt(jnp.mean(cq * cq, axis=-1, keepdims=True) + EPS) * gcq_ref[...]
    qall = jnp.dot(cqn.astype(_BF16), wuq_ref[...], preferred_element_type=_F32)
    ckv = proj(_P_CKV, _P_KR)
    ckvn = ckv * lax.rsqrt(jnp.mean(ckv * ckv, axis=-1, keepdims=True) + EPS) * gckv_ref[...]
    kv = jnp.dot(ckvn.astype(_BF16), wukv_ref[...], preferred_element_type=_F32)
    kr2 = proj(_P_KR, _P_BG)

    for hd in range(B_HEADS):
        q0 = qall[:, hd * 256:hd * 256 + LANES]
        q1 = qall[:, hd * 256 + LANES:(hd + 1) * 256]
        r = lax.rsqrt(group_sums(qall[:, hd * 256:(hd + 1) * 256], sumb_ref[0]) * (1.0 / B_QK_DIM) + EPS)
        q0n = q0 * r[:, :LANES] * gbq_ref[:, :LANES] * b_qscale
        q1n = q1 * r[:, LANES:] * gbq_ref[:, LANES:]
        q1r = (q1n * ct + pltpu.roll(q1n, B_ROPE, 1) * st) * b_qscale
        if fixed_shift:
            q1r = q1r + augb_ref[0:1, :]
            qb_ref[hd, 0, :, rows] = jnp.concatenate([q0n.T, q1r.T], axis=0).astype(_BF16)
        else:
            qb_ref[hd, 0, rows, :] = jnp.concatenate([q0n, q1r], axis=1).astype(_BF16)

    kr_ss = jnp.sum(jnp.where(low_half, kr2 * kr2, 0.0), axis=-1, keepdims=True)
    krg = kr2 * gbkr_ref[...]
    krr = krg * ct + pltpu.roll(krg, B_ROPE, 1) * st
    for hd in range(B_HEADS):
        kn = kv[:, hd * 256:hd * 256 + LANES]
        vv = kv[:, hd * 256 + LANES:(hd + 1) * 256]
        kn_ss = group_sums(kv[:, hd * 256:(hd + 1) * 256], sumb_ref[1])[:, :LANES]
        r = lax.rsqrt((kn_ss + kr_ss) * (1.0 / B_QK_DIM) + EPS)
        k1 = krr * r
        if fixed_shift:
            k1 = k1 + augb_ref[1:2, :]
        kb_ref[hd, rows, :] = jnp.concatenate([kn * r * gbkn_ref[...], k1], axis=1).astype(_BF16)
        store_values(vb_ref, hd, vv)
    gb_ref[rows, :] = _silu(proj(_P_BG, _P_END)).astype(_BF16)


def _proj_call(x2d, seq_len, lw, tm, fixed_shift):
    t = x2d.shape[0]
    blocks_per_seq = seq_len // tm
    const = lambda i: (0, 0)
    const3 = lambda i: (0, 0, 0)
    row = lambda i: (i, 0)
    unit_row = lambda i: (0, i, 0)
    var_row = lambda i: (0, 0, i, 0)
    rope_row = lambda i: (i % blocks_per_seq, 0)
    in_specs = [
        pl.BlockSpec((tm, D_MODEL), row),
        pl.BlockSpec((1, D_MODEL), const),
        pl.BlockSpec((None, D_MODEL, int(_SPLITS[-1])), lambda i: (lw["layer"], 0, 0)),
        pl.BlockSpec((None, D_MODEL, int(_P_END - _P_KR)), lambda i: (lw["layer"], 0, 0)),
        pl.BlockSpec((1, A_UNITS * A_HEAD_DIM), const),
        pl.BlockSpec((1, A_UNITS * A_HEAD_DIM), const),
        pl.BlockSpec((1, Q_LORA), const),
        pl.BlockSpec((Q_LORA, B_HEADS * 256), const),
        pl.BlockSpec((1, KV_LORA), const),
        pl.BlockSpec((KV_LORA, B_HEADS * 256), const),
        pl.BlockSpec((1, 256), const),
        pl.BlockSpec((1, LANES), const),
        pl.BlockSpec((1, LANES), const),
        pl.BlockSpec((tm, LANES), rope_row),
        pl.BlockSpec((tm, LANES), rope_row),
        pl.BlockSpec((A_UNITS, A_HEAD_DIM, LANES), const3),
        pl.BlockSpec((A_UNITS, 1, LANES), const3),
        pl.BlockSpec((2, LANES), const),
        pl.BlockSpec((256, 256), const),
        pl.BlockSpec((2, 256, 256), const3),
    ]
    if fixed_shift:
        q_shape = lambda units, dk: jax.ShapeDtypeStruct((units, t // tm, dk, tm), _BF16)
        q_spec = lambda units, dk: pl.BlockSpec((units, 1, dk, tm), lambda i: (0, i, 0, 0))
        v_shape = lambda heads: jax.ShapeDtypeStruct((heads, t // tm, VT_ROWS, tm), _BF16)
        v_spec = lambda heads: pl.BlockSpec((heads, 1, VT_ROWS, tm), lambda i: (0, i, 0, 0))
    else:
        q_shape = lambda units, dk: jax.ShapeDtypeStruct((units, 1, t, dk), _BF16)
        q_spec = lambda units, dk: pl.BlockSpec((units, 1, tm, dk), var_row)
        v_shape = lambda heads: jax.ShapeDtypeStruct((heads, t, 256), _BF16)
        v_spec = lambda heads: pl.BlockSpec((heads, tm, 256), unit_row)
    out_shape = [
        q_shape(A_UNITS, LANES),
        jax.ShapeDtypeStruct((A_UNITS, t, LANES), _BF16),
        v_shape(A_HEADS),
        jax.ShapeDtypeStruct((t, 512), _BF16),
        q_shape(B_HEADS, 256),
        jax.ShapeDtypeStruct((B_HEADS, t, 256), _BF16),
        v_shape(B_HEADS),
        jax.ShapeDtypeStruct((t, 512), _BF16),
    ]
    out_specs = [
        q_spec(A_UNITS, LANES),
        pl.BlockSpec((A_UNITS, tm, LANES), unit_row),
        v_spec(A_HEADS),
        pl.BlockSpec((tm, 512), row),
        q_spec(B_HEADS, 256),
        pl.BlockSpec((B_HEADS, tm, 256), unit_row),
        v_spec(B_HEADS),
        pl.BlockSpec((tm, 512), row),
    ]
    return pl.pallas_call(
        functools.partial(_proj_kernel, fixed_shift=fixed_shift, seq_len=seq_len),
        grid=(t // tm,),
        in_specs=in_specs,
        out_specs=out_specs,
        out_shape=out_shape,
        compiler_params=pltpu.CompilerParams(
            dimension_semantics=("parallel",), vmem_limit_bytes=_VMEM_LIMIT),
        name="proj",
    )(x2d, lw["norm_w"], lw["w_in"], lw["w_tail"], lw["g_aq"], lw["g_ak"], lw["g_cq"], lw["w_uq"], lw["g_ckv"],
      lw["w_ukv"], lw["g_bq"], lw["g_bkn"], lw["g_bkr"], lw["rope_c"][seq_len], lw["rope_s"][seq_len],
      lw["aug_q"], lw["aug_k"], lw["aug_b"], *_group_sum_matrices())


def _flash_kernel(slope_ref, q_ref, k_ref, v_ref, o_ref, m_sc, acc_sc, *, alibi, heads_per_slope):
    kj = pl.program_id(3)
    tq = q_ref.shape[3]
    tk = k_ref.shape[2]

    @pl.when(kj == 0)
    def _():
        m_sc[...] = jnp.full(m_sc.shape, -jnp.inf, _F32)
        acc_sc[...] = jnp.zeros(acc_sc.shape, _F32)

    s = lax.dot_general(q_ref[0, 0, 0], k_ref[0, 0], (((1,), (1,)), ((), ())),
                        preferred_element_type=_F32)
    if alibi:
        slope = slope_ref[pl.program_id(0) // heads_per_slope]
        qpos = pl.program_id(2) * tq + lax.broadcasted_iota(jnp.int32, (tq, tk), 0)
        kpos = kj * tk + lax.broadcasted_iota(jnp.int32, (tq, tk), 1)
        s = s - slope * jnp.abs(qpos - kpos).astype(_F32)
    m_prev = m_sc[...]
    m_new = jnp.maximum(m_prev, jnp.max(s, axis=-1, keepdims=True))
    alpha = jnp.exp(m_prev - m_new)
    p = jnp.exp(s - m_new)
    acc_sc[...] = alpha * acc_sc[...] + jnp.dot(p.astype(_BF16), v_ref[0, 0],
                                                 preferred_element_type=_F32)
    m_sc[...] = m_new

    @pl.when(kj == pl.num_programs(3) - 1)
    def _():
        acc = acc_sc[...]
        o_ref[0, 0] = acc[:, :LANES] / acc[:, LANES:LANES + 1]


def _flash_call(q, k, v, slopes, *, alibi, units_per_v, tq, tk):
    u, _, b, s, dk = q.shape
    kern = functools.partial(_flash_kernel, alibi=alibi, heads_per_slope=units_per_v)
    grid_spec = pltpu.PrefetchScalarGridSpec(
        num_scalar_prefetch=1,
        grid=(u, b, s // tq, s // tk),
        in_specs=[
            pl.BlockSpec((1, 1, 1, tq, dk), lambda ui, bi, qi, ki, sl: (ui, 0, bi, qi, 0)),
            pl.BlockSpec((1, 1, tk, dk), lambda ui, bi, qi, ki, sl: (ui, bi, ki, 0)),
            pl.BlockSpec((1, 1, tk, 256), lambda ui, bi, qi, ki, sl: (ui // units_per_v, bi, ki, 0)),
        ],
        out_specs=pl.BlockSpec((1, 1, tq, LANES), lambda ui, bi, qi, ki, sl: (ui, bi, qi, 0)),
        scratch_shapes=[pltpu.VMEM((tq, 1), _F32), pltpu.VMEM((tq, 256), _F32)],
    )
    return pl.pallas_call(
        kern,
        grid_spec=grid_spec,
        out_shape=jax.ShapeDtypeStruct((u, b, s, LANES), _F32),
        compiler_params=pltpu.CompilerParams(
            dimension_semantics=("parallel", "parallel", "parallel", "arbitrary"),
            vmem_limit_bytes=_VMEM_LIMIT),
        name="flash_a" if alibi else "flash_b",
    )(slopes, q, k, v)


def _attn_kernel(qt_ref, k_ref, vt_ref, o_ref, qv_sc, *, tq, q_steps, alibi):
    n_k, _, tk = vt_ref.shape[2:]
    q_sub = qt_ref.shape[1] * qt_ref.shape[3] // tq
    n_diag = tq // tk
    if alibi:
        head = pl.program_id(0) // 2
        sigma = jnp.float32(_alibi_slope(A_HEADS - 1) * LOG2E)
        for hd in range(A_HEADS - 1):
            sigma = jnp.where(head == hd, jnp.float32(_alibi_slope(hd) * LOG2E), sigma)
        ahead = (lax.broadcasted_iota(jnp.int32, (tk, tq), 0) - lax.broadcasted_iota(jnp.int32, (tk, tq), 1))
        aug_row = lax.broadcasted_iota(jnp.int32, (qt_ref.shape[2], tq), 0)
        aug_row = aug_row - jnp.where(pl.program_id(0) % 2 == 0, A_HEAD_DIM, 0)
        bias_rows = (aug_row >= _AUG_QHI) & (aug_row < _AUG_END)

    def query_block(sub, qi, slot):
        pieces = tq // qt_ref.shape[3]
        qt = jnp.concatenate([qt_ref[0, sub * pieces + piece] for piece in range(pieces)], axis=1)
        qv_sc[slot, 0] = qt
        if alibi:
            qv_sc[slot, 1] = jnp.where(bias_rows, -qt, qt)

        def block(var, j, correction=None):
            k = k_ref[0, 0, pl.ds(pl.multiple_of(j * tk, tk), tk), :]
            st = jnp.dot(k, qv_sc[slot, var], preferred_element_type=_F32)
            if correction is not None:
                st = st + correction
            return jnp.dot(vt_ref[0, 0, j], jnp.exp2(st).astype(_BF16), preferred_element_type=_F32)

        def accumulate(acc, part):
            return part if acc is None else acc + part

        acc = None
        if alibi:
            first_diag = qi * n_diag
            for d in range(n_diag):
                corr = (-2.0 * sigma) * jnp.maximum(ahead + d * tk, 0).astype(_F32)
                acc = accumulate(acc, block(0, first_diag + d, corr))
            for x in range(n_k - n_diag):
                after = (x >= first_diag) * 1
                acc = accumulate(acc, block(after, x + n_diag * after))
        else:
            for x in range(n_k):
                acc = accumulate(acc, block(0, x))
        rows = pl.ds(pl.multiple_of(sub * tq, tq), tq)
        o_ref[0, 0, rows, :] = (acc[:LANES] / acc[LANES:LANES + 1]).T.astype(o_ref.dtype)

    if q_steps == 1 or q_sub == 1:
        for sub in range(q_sub):
            query_block(sub, sub if q_steps == 1 else pl.program_id(2), sub)
    else:
        in_flight = qv_sc.shape[0]

        def body(it, carry):
            for slot in range(in_flight):
                sub = it * in_flight + slot
                query_block(sub, pl.program_id(2) * q_sub + sub, slot)
            return carry

        lax.fori_loop(0, q_sub // in_flight, body, 0)


def _attn_call(qt, k, vt, *, alibi, units_per_v, tq):
    u, _, dk, tm = qt.shape
    _, b, s, _ = k.shape
    n_k, _, tk = vt.shape[2:]
    assert tq % tk == 0 and s % tq == 0 and tq % tm == 0, (s, tq, tk, tm)
    n_q = s // tq
    if n_q * n_k <= MAX_BLOCK_PAIRS:
        q_sub, slots = n_q, n_q
    else:
        q_sub = _largest_group(n_q, MAX_LOOPED_QUERY_BLOCKS)
        slots = _largest_group(q_sub, 2 if dk == LANES else 1)
    q_steps = n_q // q_sub
    pieces = tq * q_sub // tm
    kern = functools.partial(_attn_kernel, tq=tq, q_steps=q_steps, alibi=alibi)
    return pl.pallas_call(
        kern,
        grid=(u, b, q_steps),
        in_specs=[
            pl.BlockSpec((1, pieces, dk, tm), lambda ui, bi, qi: (ui, bi * q_steps + qi, 0, 0)),
            pl.BlockSpec((1, 1, s, dk), lambda ui, bi, qi: (ui, bi, 0, 0)),
            pl.BlockSpec((1, 1, n_k, VT_ROWS, tk), lambda ui, bi, qi: (ui // units_per_v, bi, 0, 0, 0)),
        ],
        out_specs=pl.BlockSpec((1, 1, tq * q_sub, LANES), lambda ui, bi, qi: (ui, bi, qi, 0)),
        out_shape=jax.ShapeDtypeStruct((u, b, s, LANES), _BF16),
        scratch_shapes=[pltpu.VMEM((slots, 2 if alibi else 1, dk, tq), _BF16)],
        compiler_params=pltpu.CompilerParams(
            dimension_semantics=("parallel", "parallel", "parallel"),
            vmem_limit_bytes=_VMEM_LIMIT),
        name="attn_a" if alibi else "attn_b",
    )(qt, k, vt)


def _out_kernel(oa_ref, ob_ref, ga_ref, gb_ref, x_ref, lq1_ref, lk1_ref, lq2_ref, lk2_ref, subln_ref,
                wout_ref, y_ref, *, lam_init):
    lam = (jnp.exp(jnp.sum(lq1_ref[...] * lk1_ref[...], axis=-1, keepdims=True))
           - jnp.exp(jnp.sum(lq2_ref[...] * lk2_ref[...], axis=-1, keepdims=True)) + lam_init)
    ga = ga_ref[...].astype(_F32)
    gb = gb_ref[...].astype(_F32)
    pieces = []
    for hd in range(A_HEADS):
        d = oa_ref[2 * hd].astype(_F32) - lam * oa_ref[2 * hd + 1].astype(_F32)
        n = d * lax.rsqrt(jnp.mean(d * d, axis=-1, keepdims=True) + EPS) * subln_ref[...]
        pieces.append(n * (1.0 - lam_init) * ga[:, hd * LANES:(hd + 1) * LANES])
    for hd in range(B_HEADS):
        pieces.append(ob_ref[hd].astype(_F32) * gb[:, hd * LANES:(hd + 1) * LANES])
    y = jnp.concatenate(pieces, axis=1).astype(_BF16)
    y_ref[...] = x_ref[...] + jnp.dot(y, wout_ref[...], preferred_element_type=_F32)


def _out_call(oa, ob, ga, gb, x2d, lw, lam_init, tm):
    t = x2d.shape[0]
    const = lambda i: (0, 0)
    row = lambda i: (i, 0)
    unit_row = lambda i: (0, i, 0)
    return pl.pallas_call(
        functools.partial(_out_kernel, lam_init=lam_init),
        grid=(t // tm,),
        in_specs=[
            pl.BlockSpec((A_UNITS, tm, LANES), unit_row),
            pl.BlockSpec((B_HEADS, tm, LANES), unit_row),
            pl.BlockSpec((tm, 512), row),
            pl.BlockSpec((tm, 512), row),
            pl.BlockSpec((tm, D_MODEL), row),
            pl.BlockSpec((1, A_HEAD_DIM), const),
            pl.BlockSpec((1, A_HEAD_DIM), const),
            pl.BlockSpec((1, A_HEAD_DIM), const),
            pl.BlockSpec((1, A_HEAD_DIM), const),
            pl.BlockSpec((1, A_V_DIM), const),
            pl.BlockSpec((None, D_MODEL, D_MODEL), lambda i: (lw["layer"], 0, 0)),
        ],
        out_specs=pl.BlockSpec((tm, D_MODEL), row),
        out_shape=jax.ShapeDtypeStruct((t, D_MODEL), _F32),
        compiler_params=pltpu.CompilerParams(
            dimension_semantics=("parallel",), vmem_limit_bytes=_VMEM_LIMIT),
        name="out",
    )(oa, ob, ga, gb, x2d, lw["lq1"], lw["lk1"], lw["lq2"], lw["lk2"], lw["subln"], lw["w_out"])


def _rope_tables(seq_len):
    inv = ROPE_THETA ** (-np.arange(0, B_ROPE, 2, dtype=np.float64) / B_ROPE)
    n_hi = -(-seq_len // POS_SPLIT)
    ang_hi = (np.arange(n_hi, dtype=np.float64) * POS_SPLIT)[:, None, None] * inv
    ang_lo = np.arange(POS_SPLIT, dtype=np.float64)[None, :, None] * inv

    def lanes(first, second):
        pad = np.zeros(first.shape[:-1] + (LANES - B_ROPE,))
        return jnp.asarray(np.concatenate([first, second, pad], axis=-1), _F32)

    cos_hi, sin_hi, cos_lo, sin_lo = np.cos(ang_hi), np.sin(ang_hi), np.cos(ang_lo), np.sin(ang_lo)
    cos_lo, sin_lo = lanes(cos_lo, cos_lo), lanes(sin_lo, sin_lo)
    cos = lanes(cos_hi, cos_hi) * cos_lo - lanes(sin_hi, sin_hi) * sin_lo
    neg_sin = lanes(-sin_hi, sin_hi) * cos_lo + lanes(-cos_hi, cos_hi) * sin_lo
    return (cos.reshape(n_hi * POS_SPLIT, LANES)[:seq_len], neg_sin.reshape(n_hi * POS_SPLIT, LANES)[:seq_len])


def _score_bounds(l, p):
    amax = lambda v: jnp.max(jnp.abs(v[l].astype(_F32)))
    bound_a = amax(p["a_q_norm"]) * amax(p["a_k_norm"]) * (A_HEAD_DIM ** 0.5)
    bound_b = amax(p["b_q_norm"]) * amax(p["b_k_norm"]) * (B_QK_DIM ** 0.5)
    return bound_a, bound_b


def _aug_rows(bound_a, bound_b):
    aug_q = np.zeros((A_UNITS, A_HEAD_DIM, LANES), np.float32)
    aug_k = np.zeros((A_UNITS, 1, LANES), np.float32)
    for u in range(A_UNITS):
        sig = _sigma_pieces(u // 2)
        base = _aug_base(u % 2)
        aug_k[u, 0, base + _AUG_SHIFT] = 1.0
        for a in range(SIGMA_PIECES):
            aug_k[u, 0, base + _AUG_QHI + a] = -sig[a]
            aug_k[u, 0, base + _AUG_QLO + a] = -sig[a]
            aug_q[u, _AUG_KHI + a, :] = sig[a]
            aug_q[u, _AUG_KLO + a, :] = sig[a]
    shift_row = (np.arange(A_HEAD_DIM) == _AUG_SHIFT).astype(np.float32).reshape(1, A_HEAD_DIM, 1)
    aug_q = jnp.asarray(aug_q) - (bound_a * LOG2E) * shift_row
    b_lane = (np.arange(LANES) == _B_SHIFT).astype(np.float32)
    aug_b = jnp.stack([-(bound_b * LOG2E) * b_lane, jnp.asarray(b_lane)])
    return aug_q, jnp.asarray(aug_k), aug_b


def _stacked_weights(p):
    w = p["w_in"].astype(_BF16)
    lo, hi = _SPLITS[6], _SPLITS[7]
    w_tail = jnp.concatenate([w[:, :, lo:hi], _swap_rope_halves(w[:, :, lo:hi]), w[:, :, hi:]], axis=2)
    return w, w_tail, p["w_out"].astype(_BF16)


def _layer_weights(l, p, stacked, rope_c, rope_s):
    w_in, w_tail, w_out = stacked
    wuq = p["b_w_uq"][l].reshape(Q_LORA, B_HEADS, B_QK_DIM)
    w_uq = jnp.concatenate([wuq, _swap_rope_halves(wuq[:, :, B_NOPE:])], axis=2).reshape(Q_LORA, B_HEADS * 256)
    gq = p["b_q_norm"][l]
    gk = p["b_k_norm"][l]
    row = lambda v: v.reshape(1, -1).astype(_F32)
    bound_a, bound_b = _score_bounds(l, p)
    aug_q, aug_k, aug_b = _aug_rows(bound_a, bound_b)
    return {
        "layer": l,
        "norm_w": row(p["norm_w"][l]),
        "w_in": w_in,
        "w_tail": w_tail,
        "g_aq": row(jnp.tile(p["a_q_norm"][l], A_UNITS)),
        "g_ak": row(jnp.tile(p["a_k_norm"][l], A_UNITS)),
        "g_cq": row(p["b_cq_norm"][l]),
        "w_uq": w_uq.astype(_BF16),
        "g_ckv": row(p["b_ckv_norm"][l]),
        "w_ukv": p["b_w_ukv"][l].astype(_BF16),
        "g_bq": row(jnp.concatenate([gq, _swap_rope_halves(gq[B_NOPE:])])),
        "g_bkn": row(gk[:B_NOPE]),
        "g_bkr": row(jnp.concatenate([gk[B_NOPE:], _swap_rope_halves(gk[B_NOPE:])])),
        "rope_c": rope_c,
        "rope_s": rope_s,
        "aug_q": aug_q, "aug_k": aug_k, "aug_b": aug_b,
        "lq1": row(p["a_lq1"][l]), "lk1": row(p["a_lk1"][l]),
        "lq2": row(p["a_lq2"][l]), "lk2": row(p["a_lk2"][l]),
        "subln": row(p["a_subln"][l]),
        "w_out": w_out,
    }


def _tile(n, pref):
    return pref if n % pref == 0 else n


def _largest_group(n, cap):
    return max(g for g in range(1, cap + 1) if n % g == 0) if n > 0 else 1


def _encoder_layer(x, lw, lam_init, slopes, fixed_shift):
    b, s, _ = x.shape
    t = b * s
    x2d = x.reshape(t, D_MODEL)
    tm = _tile(s, FIXED_TK)
    qa, ka, va, ga, qb, kb, vb, gb = _proj_call(x2d, s, lw, tm, fixed_shift)
    split = lambda a: a.reshape(a.shape[:-2] + (b, s, a.shape[-1]))
    tq = tk = _tile(s, 512)
    if fixed_shift:
        assert s <= POS_SPLIT * POS_SPLIT * 2
        ftq = _tile(s, FIXED_TQ)
        split_t = lambda a: a.reshape((a.shape[0], b, s // tm) + a.shape[2:])
        oa = _attn_call(qa, split(ka), split_t(va), alibi=True, units_per_v=2, tq=ftq)
        ob = _attn_call(qb, split(kb), split_t(vb), alibi=False, units_per_v=1, tq=ftq)
    else:
        oa = _flash_call(split(qa), split(ka), split(va), slopes, alibi=True, units_per_v=2, tq=tq, tk=tk)
        ob = _flash_call(split(qb), split(kb), split(vb), slopes, alibi=False, units_per_v=1, tq=tq, tk=tk)
    y = _out_call(oa.reshape(A_UNITS, t, LANES), ob.reshape(B_HEADS, t, LANES), ga, gb, x2d, lw, lam_init,
                  _tile(s, OUT_ROWS))
    return y.reshape(b, s, D_MODEL)


def _forward(x_prompt, x_sample, p, fixed_shift):
    depth = p["norm_w"].shape[0]
    rope_c, rope_s = {}, {}
    for s in {x_prompt.shape[1], x_sample.shape[1]}:
        rope_c[s], rope_s[s] = _rope_tables(s)
    slopes = jnp.asarray([_alibi_slope(hd) for hd in range(A_HEADS)], _F32)
    stacked = _stacked_weights(p)
    y_prompt, y_sample = x_prompt, x_sample
    for l in range(depth):
        lw = _layer_weights(l, p, stacked, rope_c, rope_s)
        lam_init = _lambda_init(l)
        y_prompt = _encoder_layer(y_prompt, lw, lam_init, slopes, fixed_shift)
        y_sample = _encoder_layer(y_sample, lw, lam_init, slopes, fixed_shift)
    return (y_prompt, y_sample)


def kernel(x_prompt, x_sample, norm_w, w_in, a_q_norm, a_k_norm, a_lq1, a_lk1, a_lq2, a_lk2, a_subln,
           b_cq_norm, b_w_uq, b_ckv_norm, b_w_ukv, b_q_norm, b_k_norm, w_out):
    p = dict(norm_w=norm_w, w_in=w_in, a_q_norm=a_q_norm, a_k_norm=a_k_norm, a_lq1=a_lq1, a_lk1=a_lk1,
             a_lq2=a_lq2, a_lk2=a_lk2, a_subln=a_subln, b_cq_norm=b_cq_norm, b_w_uq=b_w_uq,
             b_ckv_norm=b_ckv_norm, b_w_ukv=b_w_ukv, b_q_norm=b_q_norm, b_k_norm=b_k_norm, w_out=w_out)
    bounds = jnp.stack([jnp.stack(_score_bounds(l, p)) for l in range(norm_w.shape[0])])
    shift_ok = jnp.max(bounds) <= MAX_STATIC_SHIFT
    return lax.cond(shift_ok,
                    lambda xp, xs, pp: _forward(xp, xs, pp, True),
                    lambda xp, xs, pp: _forward(xp, xs, pp, False),
                    x_prompt, x_sample, p)
```

```python
import functools
import math

import numpy as np
import jax
import jax.numpy as jnp
from jax import lax
from jax.experimental import pallas as pl
from jax.experimental.pallas import tpu as pltpu

D_MODEL = 1024
A_HEADS = 4
A_HEAD_DIM = 64
A_V_DIM = 128
A_UNITS = 2 * A_HEADS
B_HEADS = 4
B_NOPE = 128
B_ROPE = 64
B_QK_DIM = B_NOPE + B_ROPE
Q_LORA = 256
KV_LORA = 128
ROPE_THETA = 10000.0
EPS = 1e-6
LANES = 128

_SPLITS = np.cumsum([0, 512, 512, 512, 512, Q_LORA, KV_LORA, B_ROPE, 512])
_P_AQ, _P_AK, _P_AV, _P_AG, _P_CQ, _P_CKV, _P_KR, _P_BG, _P_END = np.cumsum(
    [0, 512, 512, 512, 512, Q_LORA, KV_LORA, 2 * B_ROPE, 512])

_VMEM_LIMIT = 56 * 1024 * 1024
LOG2E = math.log2(math.e)
MAX_STATIC_SHIFT = 32.0
POS_SPLIT = 128
SIGMA_PIECES = 3
FIXED_TQ = 1024
FIXED_TK = 512
MAX_BLOCK_PAIRS = 32
MAX_LOOPED_QUERY_BLOCKS = 4
OUT_ROWS = 1024
VT_ROWS = 144

_AUG_SHIFT = 0
_AUG_QHI = _AUG_SHIFT + 1
_AUG_QLO = _AUG_QHI + SIGMA_PIECES
_AUG_KHI = _AUG_QLO + SIGMA_PIECES
_AUG_KLO = _AUG_KHI + SIGMA_PIECES
_AUG_END = _AUG_KLO + SIGMA_PIECES
_B_SHIFT = B_QK_DIM - LANES


def _aug_base(comp):
    return A_HEAD_DIM if comp == 0 else 0


_F32 = jnp.float32
_BF16 = jnp.bfloat16


def _lambda_init(layer_idx):
    return 0.8 - 0.6 * math.exp(-0.3 * layer_idx)


def _alibi_slope(head):
    return 2.0 ** (-8.0 * (head + 1) / A_HEADS)


def _sigma_pieces(head):
    rest = _alibi_slope(head) * LOG2E
    pieces = []
    for _ in range(SIGMA_PIECES):
        p = float(np.asarray(rest, np.float32).astype(_BF16).astype(np.float32))
        pieces.append(p)
        rest -= p
    return pieces


def _group_sum_matrices():
    unit = np.arange(256) // A_HEAD_DIM
    m_a = (unit[:, None] == unit[None, :]).astype(np.float32)
    m_b = np.zeros((2, 256, 256), np.float32)
    m_b[0, :B_QK_DIM, :] = 1.0
    m_b[1, :B_NOPE, :] = 1.0
    return jnp.asarray(m_a, _BF16), jnp.asarray(m_b, _BF16)


def _swap_rope_halves(a):
    half = B_ROPE // 2
    return jnp.concatenate([a[..., half:], a[..., :half]], axis=-1)


def _silu(x):
    return x / (1.0 + jnp.exp(-x))


def _proj_kernel(x_ref, nw_ref, win_ref, wtail_ref, gaq_ref, gak_ref, gcq_ref, wuq_ref, gckv_ref, wukv_ref,
                 gbq_ref, gbkn_ref, gbkr_ref, ct_ref, st_ref, augq_ref, augk_ref, augb_ref, suma_ref, sumb_ref,
                 qa_ref, ka_ref, va_ref, ga_ref, qb_ref, kb_ref, vb_ref, gb_ref, *, fixed_shift, seq_len):
    tm = x_ref.shape[0]
    rows = slice(0, tm)
    x = x_ref[rows, :]
    h = x * lax.rsqrt(jnp.mean(x * x, axis=-1, keepdims=True) + EPS) * nw_ref[...]
    hb = h.astype(_BF16)

    def proj(lo, hi):
        if lo >= _P_KR:
            return jnp.dot(hb, wtail_ref[:, lo - _P_KR:hi - _P_KR], preferred_element_type=_F32)
        return jnp.dot(hb, win_ref[:, lo:hi], preferred_element_type=_F32)

    lane = lax.broadcasted_iota(jnp.int32, (tm, LANES), 1)
    low_half = lane < B_ROPE
    ones_col = (lane == 0).astype(_F32)
    ct = ct_ref[rows, :]
    st = st_ref[rows, :]

    def aug_select(index, base, hi, lo):
        at = lambda off: (index >= base + off) & (index < base + off + SIGMA_PIECES)
        return at(hi), at(lo)

    if fixed_shift:
        a_qscale = (A_HEAD_DIM ** -0.5) * LOG2E
        b_qscale = (B_QK_DIM ** -0.5) * LOG2E
        first = (pl.program_id(0) % (seq_len // tm)) * tm - seq_len // 2
        split_pos = lambda p: (((p >> 7) * POS_SPLIT).astype(_F32), (p & (POS_SPLIT - 1)).astype(_F32))
        k_hi, k_lo = split_pos(first + lax.broadcasted_iota(jnp.int32, (tm, LANES), 0))
        q_hi, q_lo = split_pos(first + lax.broadcasted_iota(jnp.int32, (A_HEAD_DIM, tm), 1))
        aug_row = lax.broadcasted_iota(jnp.int32, (A_HEAD_DIM, tm), 0)
        at_hi, at_lo = aug_select(aug_row, 0, _AUG_QHI, _AUG_QLO)
        q_pos = jnp.where(at_hi, q_hi, jnp.where(at_lo, q_lo, 0.0))
    else:
        a_qscale = A_HEAD_DIM ** -0.5
        b_qscale = B_QK_DIM ** -0.5

    def group_sums(a, m):
        return jnp.dot((a * a).astype(_BF16), m, preferred_element_type=_F32)

    aq = proj(_P_AQ, _P_AK)
    ak = proj(_P_AK, _P_AV)
    def unit_sums(a):
        return jnp.concatenate([group_sums(a[:, lo:lo + 256], suma_ref[...]) for lo in range(0, a.shape[1], 256)],
                               axis=1)

    qn_all = aq * lax.rsqrt(unit_sums(aq) * (1.0 / A_HEAD_DIM) + EPS) * gaq_ref[...] * a_qscale
    kn_all = ak * lax.rsqrt(unit_sums(ak) * (1.0 / A_HEAD_DIM) + EPS) * gak_ref[...]
    for hd in range(A_HEADS):
        qn2 = qn_all[:, hd * LANES:(hd + 1) * LANES]
        kn2 = kn_all[:, hd * LANES:(hd + 1) * LANES]
        if fixed_shift:
            qn2_t = qn2.T
        for comp in range(2):
            u = 2 * hd + comp
            own = (lane < A_HEAD_DIM) if comp == 0 else (lane >= A_HEAD_DIM)
            if fixed_shift:
                feat = qn2_t[comp * A_HEAD_DIM:(comp + 1) * A_HEAD_DIM]
                aug = jnp.concatenate([augq_ref[u]] * (tm // LANES), axis=1) + q_pos
                stacked = [feat, aug] if comp == 0 else [aug, feat]
                qa_ref[u, 0, :, rows] = jnp.concatenate(stacked, axis=0).astype(_BF16)
                at_hi, at_lo = aug_select(lane, _aug_base(comp), _AUG_KHI, _AUG_KLO)
                k_pos = jnp.where(at_hi, k_hi, jnp.where(at_lo, k_lo, 0.0))
                ka_ref[u, rows, :] = (jnp.where(own, kn2, 0.0) + augk_ref[u] + k_pos).astype(_BF16)
            else:
                qa_ref[u, 0, rows, :] = jnp.where(own, qn2, 0.0).astype(_BF16)
                ka_ref[u, rows, :] = jnp.where(own, kn2, 0.0).astype(_BF16)

    def store_values(v_ref, hd, vv):
        if fixed_shift:
            pad_rows = lax.broadcasted_iota(jnp.int32, (VT_ROWS - LANES, tm), 0)
            v_ref[hd, 0, :, rows] = jnp.concatenate([vv.T, (pad_rows == 0).astype(_F32)], axis=0).astype(_BF16)
        else:
            v_ref[hd, rows, :] = jnp.concatenate([vv, ones_col], axis=1).astype(_BF16)

    av = proj(_P_AV, _P_AG)
    for hd in range(A_HEADS):
        store_values(va_ref, hd, av[:, hd * LANES:(hd + 1) * LANES])
    ga_ref[rows, :] = _silu(proj(_P_AG, _P_CQ)).astype(_BF16)

    cq = proj(_P_CQ, _P_CKV)
    cqn = cq * lax.rsqrt(jnp.mean(cq * cq, axis=-1, keepdims=True) + EPS) * gcq_ref[...]
    qall = jnp.dot(cqn.astype(_BF16), wuq_ref[...], preferred_element_type=_F32)
    ckv = proj(_P_CKV, _P_KR)
    ckvn = ckv * lax.rsqrt(jnp.mean(ckv * ckv, axis=-1, keepdims=True) + EPS) * gckv_ref[...]
    kv = jnp.dot(ckvn.astype(_BF16), wukv_ref[...], preferred_element_type=_F32)
    kr2 = proj(_P_KR, _P_BG)

    for hd in range(B_HEADS):
        q0 = qall[:, hd * 256:hd * 256 + LANES]
        q1 = qall[:, hd * 256 + LANES:(hd + 1) * 256]
        r = lax.rsqrt(group_sums(qall[:, hd * 256:(hd + 1) * 256], sumb_ref[0]) * (1.0 / B_QK_DIM) + EPS)
        q0n = q0 * r[:, :LANES] * gbq_ref[:, :LANES] * b_qscale
        q1n = q1 * r[:, LANES:] * gbq_ref[:, LANES:]
        q1r = (q1n * ct + pltpu.roll(q1n, B_ROPE, 1) * st) * b_qscale
        if fixed_shift:
            q1r = q1r + augb_ref[0:1, :]
            qb_ref[hd, 0, :, rows] = jnp.concatenate([q0n.T, q1r.T], axis=0).astype(_BF16)
        else:
            qb_ref[hd, 0, rows, :] = jnp.concatenate([q0n, q1r], axis=1).astype(_BF16)

    kr_ss = jnp.sum(jnp.where(low_half, kr2 * kr2, 0.0), axis=-1, keepdims=True)
    krg = kr2 * gbkr_ref[...]
    krr = krg * ct + pltpu.roll(krg, B_ROPE, 1) * st
    for hd in range(B_HEADS):
        kn = kv[:, hd * 256:hd * 256 + LANES]
        vv = kv[:, hd * 256 + LANES:(hd + 1) * 256]
        kn_ss = group_sums(kv[:, hd * 256:(hd + 1) * 256], sumb_ref[1])[:, :LANES]
        r = lax.rsqrt((kn_ss + kr_ss) * (1.0 / B_QK_DIM) + EPS)
        k1 = krr * r
        if fixed_shift:
            k1 = k1 + augb_ref[1:2, :]
        kb_ref[hd, rows, :] = jnp.concatenate([kn * r * gbkn_ref[...], k1], axis=1).astype(_BF16)
        store_values(vb_ref, hd, vv)
    gb_ref[rows, :] = _silu(proj(_P_BG, _P_END)).astype(_BF16)


def _proj_call(x2d, seq_len, lw, tm, fixed_shift):
    t = x2d.shape[0]
    blocks_per_seq = seq_len // tm
    const = lambda i: (0, 0)
    const3 = lambda i: (0, 0, 0)
    row = lambda i: (i, 0)
    unit_row = lambda i: (0, i, 0)
    var_row = lambda i: (0, 0, i, 0)
    rope_row = lambda i: (i % blocks_per_seq, 0)
    in_specs = [
        pl.BlockSpec((tm, D_MODEL), row),
        pl.BlockSpec((1, D_MODEL), const),
        pl.BlockSpec((None, D_MODEL, int(_SPLITS[-1])), lambda i: (lw["layer"], 0, 0)),
        pl.BlockSpec((None, D_MODEL, int(_P_END - _P_KR)), lambda i: (lw["layer"], 0, 0)),
        pl.BlockSpec((1, A_UNITS * A_HEAD_DIM), const),
        pl.BlockSpec((1, A_UNITS * A_HEAD_DIM), const),
        pl.BlockSpec((1, Q_LORA), const),
        pl.BlockSpec((Q_LORA, B_HEADS * 256), const),
        pl.BlockSpec((1, KV_LORA), const),
        pl.BlockSpec((KV_LORA, B_HEADS * 256), const),
        pl.BlockSpec((1, 256), const),
        pl.BlockSpec((1, LANES), const),
        pl.BlockSpec((1, LANES), const),
        pl.BlockSpec((tm, LANES), rope_row),
        pl.BlockSpec((tm, LANES), rope_row),
        pl.BlockSpec((A_UNITS, A_HEAD_DIM, LANES), const3),
        pl.BlockSpec((A_UNITS, 1, LANES), const3),
        pl.BlockSpec((2, LANES), const),
        pl.BlockSpec((256, 256), const),
        pl.BlockSpec((2, 256, 256), const3),
    ]
    if fixed_shift:
        q_shape = lambda units, dk: jax.ShapeDtypeStruct((units, t // tm, dk, tm), _BF16)
        q_spec = lambda units, dk: pl.BlockSpec((units, 1, dk, tm), lambda i: (0, i, 0, 0))
        v_shape = lambda heads: jax.ShapeDtypeStruct((heads, t // tm, VT_ROWS, tm), _BF16)
        v_spec = lambda heads: pl.BlockSpec((heads, 1, VT_ROWS, tm), lambda i: (0, i, 0, 0))
    else:
        q_shape = lambda units, dk: jax.ShapeDtypeStruct((units, 1, t, dk), _BF16)
        q_spec = lambda units, dk: pl.BlockSpec((units, 1, tm, dk), var_row)
        v_shape = lambda heads: jax.ShapeDtypeStruct((heads, t, 256), _BF16)
        v_spec = lambda heads: pl.BlockSpec((heads, tm, 256), unit_row)
    out_shape = [
        q_shape(A_UNITS, LANES),
        jax.ShapeDtypeStruct((A_UNITS, t, LANES), _BF16),
        v_shape(A_HEADS),
        jax.ShapeDtypeStruct((t, 512), _BF16),
        q_shape(B_HEADS, 256),
        jax.ShapeDtypeStruct((B_HEADS, t, 256), _BF16),
        v_shape(B_HEADS),
        jax.ShapeDtypeStruct((t, 512), _BF16),
    ]
    out_specs = [
        q_spec(A_UNITS, LANES),
        pl.BlockSpec((A_UNITS, tm, LANES), unit_row),
        v_spec(A_HEADS),
        pl.BlockSpec((tm, 512), row),
        q_spec(B_HEADS, 256),
        pl.BlockSpec((B_HEADS, tm, 256), unit_row),
        v_spec(B_HEADS),
        pl.BlockSpec((tm, 512), row),
    ]
    return pl.pallas_call(
        functools.partial(_proj_kernel, fixed_shift=fixed_shift, seq_len=seq_len),
        grid=(t // tm,),
        in_specs=in_specs,
        out_specs=out_specs,
        out_shape=out_shape,
        compiler_params=pltpu.CompilerParams(
            dimension_semantics=("parallel",), vmem_limit_bytes=_VMEM_LIMIT),
        name="proj",
    )(x2d, lw["norm_w"], lw["w_in"], lw["w_tail"], lw["g_aq"], lw["g_ak"], lw["g_cq"], lw["w_uq"], lw["g_ckv"],
      lw["w_ukv"], lw["g_bq"], lw["g_bkn"], lw["g_bkr"], lw["rope_c"][seq_len], lw["rope_s"][seq_len],
      lw["aug_q"], lw["aug_k"], lw["aug_b"], *_group_sum_matrices())


def _flash_kernel(slope_ref, q_ref, k_ref, v_ref, o_ref, m_sc, acc_sc, *, alibi, heads_per_slope):
    kj = pl.program_id(3)
    tq = q_ref.shape[3]
    tk = k_ref.shape[2]

    @pl.when(kj == 0)
    def _():
        m_sc[...] = jnp.full(m_sc.shape, -jnp.inf, _F32)
        acc_sc[...] = jnp.zeros(acc_sc.shape, _F32)

    s = lax.dot_general(q_ref[0, 0, 0], k_ref[0, 0], (((1,), (1,)), ((), ())),
                        preferred_element_type=_F32)
    if alibi:
        slope = slope_ref[pl.program_id(0) // heads_per_slope]
        qpos = pl.program_id(2) * tq + lax.broadcasted_iota(jnp.int32, (tq, tk), 0)
        kpos = kj * tk + lax.broadcasted_iota(jnp.int32, (tq, tk), 1)
        s = s - slope * jnp.abs(qpos - kpos).astype(_F32)
    m_prev = m_sc[...]
    m_new = jnp.maximum(m_prev, jnp.max(s, axis=-1, keepdims=True))
    alpha = jnp.exp(m_prev - m_new)
    p = jnp.exp(s - m_new)
    acc_sc[...] = alpha * acc_sc[...] + jnp.dot(p.astype(_BF16), v_ref[0, 0],
                                                 preferred_element_type=_F32)
    m_sc[...] = m_new

    @pl.when(kj == pl.num_programs(3) - 1)
    def _():
        acc = acc_sc[...]
        o_ref[0, 0] = acc[:, :LANES] / acc[:, LANES:LANES + 1]


def _flash_call(q, k, v, slopes, *, alibi, units_per_v, tq, tk):
    u, _, b, s, dk = q.shape
    kern = functools.partial(_flash_kernel, alibi=alibi, heads_per_slope=units_per_v)
    grid_spec = pltpu.PrefetchScalarGridSpec(
        num_scalar_prefetch=1,
        grid=(u, b, s // tq, s // tk),
        in_specs=[
            pl.BlockSpec((1, 1, 1, tq, dk), lambda ui, bi, qi, ki, sl: (ui, 0, bi, qi, 0)),
            pl.BlockSpec((1, 1, tk, dk), lambda ui, bi, qi, ki, sl: (ui, bi, ki, 0)),
            pl.BlockSpec((1, 1, tk, 256), lambda ui, bi, qi, ki, sl: (ui // units_per_v, bi, ki, 0)),
        ],
        out_specs=pl.BlockSpec((1, 1, tq, LANES), lambda ui, bi, qi, ki, sl: (ui, bi, qi, 0)),
        scratch_shapes=[pltpu.VMEM((tq, 1), _F32), pltpu.VMEM((tq, 256), _F32)],
    )
    return pl.pallas_call(
        kern,
        grid_spec=grid_spec,
        out_shape=jax.ShapeDtypeStruct((u, b, s, LANES), _F32),
        compiler_params=pltpu.CompilerParams(
            dimension_semantics=("parallel", "parallel", "parallel", "arbitrary"),
            vmem_limit_bytes=_VMEM_LIMIT),
        name="flash_a" if alibi else "flash_b",
    )(slopes, q, k, v)


def _attn_kernel(qt_ref, k_ref, vt_ref, o_ref, qv_sc, *, tq, q_steps, alibi):
    n_k, _, tk = vt_ref.shape[2:]
    q_sub = qt_ref.shape[1] * qt_ref.shape[3] // tq
    n_diag = tq // tk
    if alibi:
        head = pl.program_id(0) // 2
        sigma = jnp.float32(_alibi_slope(A_HEADS - 1) * LOG2E)
        for hd in range(A_HEADS - 1):
            sigma = jnp.where(head == hd, jnp.float32(_alibi_slope(hd) * LOG2E), sigma)
        ahead = (lax.broadcasted_iota(jnp.int32, (tk, tq), 0) - lax.broadcasted_iota(jnp.int32, (tk, tq), 1))
        aug_row = lax.broadcasted_iota(jnp.int32, (qt_ref.shape[2], tq), 0)
        aug_row = aug_row - jnp.where(pl.program_id(0) % 2 == 0, A_HEAD_DIM, 0)
        bias_rows = (aug_row >= _AUG_QHI) & (aug_row < _AUG_END)

    def query_block(sub, qi, slot):
        pieces = tq // qt_ref.shape[3]
        qt = jnp.concatenate([qt_ref[0, sub * pieces + piece] for piece in range(pieces)], axis=1)
        qv_sc[slot, 0] = qt
        if alibi:
            qv_sc[slot, 1] = jnp.where(bias_rows, -qt, qt)

        def block(var, j, correction=None):
            k = k_ref[0, 0, pl.ds(pl.multiple_of(j * tk, tk), tk), :]
            st = jnp.dot(k, qv_sc[slot, var], preferred_element_type=_F32)
            if correction is not None:
                st = st + correction
            return jnp.dot(vt_ref[0, 0, j], jnp.exp2(st).astype(_BF16), preferred_element_type=_F32)

        def accumulate(acc, part):
            return part if acc is None else acc + part

        acc = None
        if alibi:
            first_diag = qi * n_diag
            for d in range(n_diag):
                corr = (-2.0 * sigma) * jnp.maximum(ahead + d * tk, 0).astype(_F32)
                acc = accumulate(acc, block(0, first_diag + d, corr))
            for x in range(n_k - n_diag):
                after = (x >= first_diag) * 1
                acc = accumulate(acc, block(after, x + n_diag * after))
        else:
            for x in range(n_k):
                acc = accumulate(acc, block(0, x))
        rows = pl.ds(pl.multiple_of(sub * tq, tq), tq)
        o_ref[0, 0, rows, :] = (acc[:LANES] / acc[LANES:LANES + 1]).T.astype(o_ref.dtype)

    if q_steps == 1 or q_sub == 1:
        for sub in range(q_sub):
            query_block(sub, sub if q_steps == 1 else pl.program_id(2), sub)
    else:
        def body(sub, carry):
            query_block(sub, pl.program_id(2) * q_sub + sub, 0)
            return carry

        lax.fori_loop(0, q_sub, body, 0)


def _attn_call(qt, k, vt, *, alibi, units_per_v, tq):
    u, _, dk, tm = qt.shape
    _, b, s, _ = k.shape
    n_k, _, tk = vt.shape[2:]
    assert tq % tk == 0 and s % tq == 0 and tq % tm == 0, (s, tq, tk, tm)
    n_q = s // tq
    if n_q * n_k <= MAX_BLOCK_PAIRS:
        q_sub, slots = n_q, n_q
    else:
        q_sub, slots = _largest_group(n_q, MAX_LOOPED_QUERY_BLOCKS), 1
    q_steps = n_q // q_sub
    pieces = tq * q_sub // tm
    kern = functools.partial(_attn_kernel, tq=tq, q_steps=q_steps, alibi=alibi)
    return pl.pallas_call(
        kern,
        grid=(u, b, q_steps),
        in_specs=[
            pl.BlockSpec((1, pieces, dk, tm), lambda ui, bi, qi: (ui, bi * q_steps + qi, 0, 0)),
            pl.BlockSpec((1, 1, s, dk), lambda ui, bi, qi: (ui, bi, 0, 0)),
            pl.BlockSpec((1, 1, n_k, VT_ROWS, tk), lambda ui, bi, qi: (ui // units_per_v, bi, 0, 0, 0)),
        ],
        out_specs=pl.BlockSpec((1, 1, tq * q_sub, LANES), lambda ui, bi, qi: (ui, bi, qi, 0)),
        out_shape=jax.ShapeDtypeStruct((u, b, s, LANES), _BF16),
        scratch_shapes=[pltpu.VMEM((slots, 2 if alibi else 1, dk, tq), _BF16)],
        compiler_params=pltpu.CompilerParams(
            dimension_semantics=("parallel", "parallel", "parallel"),
            vmem_limit_bytes=_VMEM_LIMIT),
        name="attn_a" if alibi else "attn_b",
    )(qt, k, vt)


def _out_kernel(oa_ref, ob_ref, ga_ref, gb_ref, x_ref, lq1_ref, lk1_ref, lq2_ref, lk2_ref, subln_ref,
                wout_ref, y_ref, *, lam_init):
    lam = (jnp.exp(jnp.sum(lq1_ref[...] * lk1_ref[...], axis=-1, keepdims=True))
           - jnp.exp(jnp.sum(lq2_ref[...] * lk2_ref[...], axis=-1, keepdims=True)) + lam_init)
    ga = ga_ref[...].astype(_F32)
    gb = gb_ref[...].astype(_F32)
    pieces = []
    for hd in range(A_HEADS):
        d = oa_ref[2 * hd].astype(_F32) - lam * oa_ref[2 * hd + 1].astype(_F32)
        n = d * lax.rsqrt(jnp.mean(d * d, axis=-1, keepdims=True) + EPS) * subln_ref[...]
        pieces.append(n * (1.0 - lam_init) * ga[:, hd * LANES:(hd + 1) * LANES])
    for hd in range(B_HEADS):
        pieces.append(ob_ref[hd].astype(_F32) * gb[:, hd * LANES:(hd + 1) * LANES])
    y = jnp.concatenate(pieces, axis=1).astype(_BF16)
    y_ref[...] = x_ref[...] + jnp.dot(y, wout_ref[...], preferred_element_type=_F32)


def _out_call(oa, ob, ga, gb, x2d, lw, lam_init, tm):
    t = x2d.shape[0]
    const = lambda i: (0, 0)
    row = lambda i: (i, 0)
    unit_row = lambda i: (0, i, 0)
    return pl.pallas_call(
        functools.partial(_out_kernel, lam_init=lam_init),
        grid=(t // tm,),
        in_specs=[
            pl.BlockSpec((A_UNITS, tm, LANES), unit_row),
            pl.BlockSpec((B_HEADS, tm, LANES), unit_row),
            pl.BlockSpec((tm, 512), row),
            pl.BlockSpec((tm, 512), row),
            pl.BlockSpec((tm, D_MODEL), row),
            pl.BlockSpec((1, A_HEAD_DIM), const),
            pl.BlockSpec((1, A_HEAD_DIM), const),
            pl.BlockSpec((1, A_HEAD_DIM), const),
            pl.BlockSpec((1, A_HEAD_DIM), const),
            pl.BlockSpec((1, A_V_DIM), const),
            pl.BlockSpec((None, D_MODEL, D_MODEL), lambda i: (lw["layer"], 0, 0)),
        ],
        out_specs=pl.BlockSpec((tm, D_MODEL), row),
        out_shape=jax.ShapeDtypeStruct((t, D_MODEL), _F32),
        compiler_params=pltpu.CompilerParams(
            dimension_semantics=("parallel",), vmem_limit_bytes=_VMEM_LIMIT),
        name="out",
    )(oa, ob, ga, gb, x2d, lw["lq1"], lw["lk1"], lw["lq2"], lw["lk2"], lw["subln"], lw["w_out"])


def _rope_tables(seq_len):
    inv = ROPE_THETA ** (-np.arange(0, B_ROPE, 2, dtype=np.float64) / B_ROPE)
    n_hi = -(-seq_len // POS_SPLIT)
    ang_hi = (np.arange(n_hi, dtype=np.float64) * POS_SPLIT)[:, None, None] * inv
    ang_lo = np.arange(POS_SPLIT, dtype=np.float64)[None, :, None] * inv

    def lanes(first, second):
        pad = np.zeros(first.shape[:-1] + (LANES - B_ROPE,))
        return jnp.asarray(np.concatenate([first, second, pad], axis=-1), _F32)

    cos_hi, sin_hi, cos_lo, sin_lo = np.cos(ang_hi), np.sin(ang_hi), np.cos(ang_lo), np.sin(ang_lo)
    cos_lo, sin_lo = lanes(cos_lo, cos_lo), lanes(sin_lo, sin_lo)
    cos = lanes(cos_hi, cos_hi) * cos_lo - lanes(sin_hi, sin_hi) * sin_lo
    neg_sin = lanes(-sin_hi, sin_hi) * cos_lo + lanes(-cos_hi, cos_hi) * sin_lo
    return (cos.reshape(n_hi * POS_SPLIT, LANES)[:seq_len], neg_sin.reshape(n_hi * POS_SPLIT, LANES)[:seq_len])


def _score_bounds(l, p):
    amax = lambda v: jnp.max(jnp.abs(v[l].astype(_F32)))
    bound_a = amax(p["a_q_norm"]) * amax(p["a_k_norm"]) * (A_HEAD_DIM ** 0.5)
    bound_b = amax(p["b_q_norm"]) * amax(p["b_k_norm"]) * (B_QK_DIM ** 0.5)
    return bound_a, bound_b


def _aug_rows(bound_a, bound_b):
    aug_q = np.zeros((A_UNITS, A_HEAD_DIM, LANES), np.float32)
    aug_k = np.zeros((A_UNITS, 1, LANES), np.float32)
    for u in range(A_UNITS):
        sig = _sigma_pieces(u // 2)
        base = _aug_base(u % 2)
        aug_k[u, 0, base + _AUG_SHIFT] = 1.0
        for a in range(SIGMA_PIECES):
            aug_k[u, 0, base + _AUG_QHI + a] = -sig[a]
            aug_k[u, 0, base + _AUG_QLO + a] = -sig[a]
            aug_q[u, _AUG_KHI + a, :] = sig[a]
            aug_q[u, _AUG_KLO + a, :] = sig[a]
    shift_row = (np.arange(A_HEAD_DIM) == _AUG_SHIFT).astype(np.float32).reshape(1, A_HEAD_DIM, 1)
    aug_q = jnp.asarray(aug_q) - (bound_a * LOG2E) * shift_row
    b_lane = (np.arange(LANES) == _B_SHIFT).astype(np.float32)
    aug_b = jnp.stack([-(bound_b * LOG2E) * b_lane, jnp.asarray(b_lane)])
    return aug_q, jnp.asarray(aug_k), aug_b


def _stacked_weights(p):
    w = p["w_in"].astype(_BF16)
    lo, hi = _SPLITS[6], _SPLITS[7]
    w_tail = jnp.concatenate([w[:, :, lo:hi], _swap_rope_halves(w[:, :, lo:hi]), w[:, :, hi:]], axis=2)
    return w, w_tail, p["w_out"].astype(_BF16)


def _layer_weights(l, p, stacked, rope_c, rope_s):
    w_in, w_tail, w_out = stacked
    wuq = p["b_w_uq"][l].reshape(Q_LORA, B_HEADS, B_QK_DIM)
    w_uq = jnp.concatenate([wuq, _swap_rope_halves(wuq[:, :, B_NOPE:])], axis=2).reshape(Q_LORA, B_HEADS * 256)
    gq = p["b_q_norm"][l]
    gk = p["b_k_norm"][l]
    row = lambda v: v.reshape(1, -1).astype(_F32)
    bound_a, bound_b = _score_bounds(l, p)
    aug_q, aug_k, aug_b = _aug_rows(bound_a, bound_b)
    return {
        "layer": l,
        "norm_w": row(p["norm_w"][l]),
        "w_in": w_in,
        "w_tail": w_tail,
        "g_aq": row(jnp.tile(p["a_q_norm"][l], A_UNITS)),
        "g_ak": row(jnp.tile(p["a_k_norm"][l], A_UNITS)),
        "g_cq": row(p["b_cq_norm"][l]),
        "w_uq": w_uq.astype(_BF16),
        "g_ckv": row(p["b_ckv_norm"][l]),
        "w_ukv": p["b_w_ukv"][l].astype(_BF16),
        "g_bq": row(jnp.concatenate([gq, _swap_rope_halves(gq[B_NOPE:])])),
        "g_bkn": row(gk[:B_NOPE]),
        "g_bkr": row(jnp.concatenate([gk[B_NOPE:], _swap_rope_halves(gk[B_NOPE:])])),
        "rope_c": rope_c,
        "rope_s": rope_s,
        "aug_q": aug_q, "aug_k": aug_k, "aug_b": aug_b,
        "lq1": row(p["a_lq1"][l]), "lk1": row(p["a_lk1"][l]),
        "lq2": row(p["a_lq2"][l]), "lk2": row(p["a_lk2"][l]),
        "subln": row(p["a_subln"][l]),
        "w_out": w_out,
    }


def _tile(n, pref):
    return pref if n % pref == 0 else n


def _largest_group(n, cap):
    return max(g for g in range(1, cap + 1) if n % g == 0) if n > 0 else 1


def _encoder_layer(x, lw, lam_init, slopes, fixed_shift):
    b, s, _ = x.shape
    t = b * s
    x2d = x.reshape(t, D_MODEL)
    tm = _tile(s, FIXED_TK)
    qa, ka, va, ga, qb, kb, vb, gb = _proj_call(x2d, s, lw, tm, fixed_shift)
    split = lambda a: a.reshape(a.shape[:-2] + (b, s, a.shape[-1]))
    tq = tk = _tile(s, 512)
    if fixed_shift:
        assert s <= POS_SPLIT * POS_SPLIT * 2
        ftq = _tile(s, FIXED_TQ)
        split_t = lambda a: a.reshape((a.shape[0], b, s // tm) + a.shape[2:])
        oa = _attn_call(qa, split(ka), split_t(va), alibi=True, units_per_v=2, tq=ftq)
        ob = _attn_call(qb, split(kb), split_t(vb), alibi=False, units_per_v=1, tq=ftq)
    else:
        oa = _flash_call(split(qa), split(ka), split(va), slopes, alibi=True, units_per_v=2, tq=tq, tk=tk)
        ob = _flash_call(split(qb), split(kb), split(vb), slopes, alibi=False, units_per_v=1, tq=tq, tk=tk)
    y = _out_call(oa.reshape(A_UNITS, t, LANES), ob.reshape(B_HEADS, t, LANES), ga, gb, x2d, lw, lam_init,
                  _tile(s, OUT_ROWS))
    return y.reshape(b, s, D_MODEL)


def _forward(x_prompt, x_sample, p, fixed_shift):
    depth = p["norm_w"].shape[0]
    rope_c, rope_s = {}, {}
    for s in {x_prompt.shape[1], x_sample.shape[1]}:
        rope_c[s], rope_s[s] = _rope_tables(s)
    slopes = jnp.asarray([_alibi_slope(hd) for hd in range(A_HEADS)], _F32)
    stacked = _stacked_weights(p)
    y_prompt, y_sample = x_prompt, x_sample
    for l in range(depth):
        lw = _layer_weights(l, p, stacked, rope_c, rope_s)
        lam_init = _lambda_init(l)
        y_prompt = _encoder_layer(y_prompt, lw, lam_init, slopes, fixed_shift)
        y_sample = _encoder_layer(y_sample, lw, lam_init, slopes, fixed_shift)
    return (y_prompt, y_sample)


def kernel(x_prompt, x_sample, norm_w, w_in, a_q_norm, a_k_norm, a_lq1, a_lk1, a_lq2, a_lk2, a_subln,
           b_cq_norm, b_w_uq, b_ckv_norm, b_w_ukv, b_q_norm, b_k_norm, w_out):
    p = dict(norm_w=norm_w, w_in=w_in, a_q_norm=a_q_norm, a_k_norm=a_k_norm, a_lq1=a_lq1, a_lk1=a_lk1,
             a_lq2=a_lq2, a_lk2=a_lk2, a_subln=a_subln, b_cq_norm=b_cq_norm, b_w_uq=b_w_uq,
             b_ckv_norm=b_ckv_norm, b_w_ukv=b_w_ukv, b_q_norm=b_q_norm, b_k_norm=b_k_norm, w_out=w_out)
    bounds = jnp.stack([jnp.stack(_score_bounds(l, p)) for l in range(norm_w.shape[0])])
    shift_ok = jnp.max(bounds) <= MAX_STATIC_SHIFT
    return lax.cond(shift_ok,
                    lambda xp, xs, pp: _forward(xp, xs, pp, True),
                    lambda xp, xs, pp: _forward(xp, xs, pp, False),
                    x_prompt, x_sample, p)
```

```python
import functools
import math

import numpy as np
import jax
import jax.numpy as jnp
from jax import lax
from jax.experimental import pallas as pl
from jax.experimental.pallas import tpu as pltpu

D_MODEL = 1024
A_HEADS = 4
A_HEAD_DIM = 64
A_V_DIM = 128
A_UNITS = 2 * A_HEADS
B_HEADS = 4
B_NOPE = 128
B_ROPE = 64
B_QK_DIM = B_NOPE + B_ROPE
Q_LORA = 256
KV_LORA = 128
ROPE_THETA = 10000.0
EPS = 1e-6
LANES = 128

_SPLITS = np.cumsum([0, 512, 512, 512, 512, Q_LORA, KV_LORA, B_ROPE, 512])
_P_AQ, _P_AK, _P_AV, _P_AG, _P_CQ, _P_CKV, _P_KR, _P_BG, _P_END = np.cumsum(
    [0, 512, 512, 512, 512, Q_LORA, KV_LORA, 2 * B_ROPE, 512])

_VMEM_LIMIT = 56 * 1024 * 1024
LOG2E = math.log2(math.e)
MAX_STATIC_SHIFT = 32.0
POS_SPLIT = 128
SIGMA_PIECES = 3
FIXED_TQ = 1024
FIXED_TK = 512
MAX_BLOCK_PAIRS = 32
MAX_LOOPED_QUERY_BLOCKS = 4
OUT_ROWS = 1024
VT_ROWS = 144

_AUG_SHIFT = 0
_AUG_QHI = _AUG_SHIFT + 1
_AUG_QLO = _AUG_QHI + SIGMA_PIECES
_AUG_KHI = _AUG_QLO + SIGMA_PIECES
_AUG_KLO = _AUG_KHI + SIGMA_PIECES
_AUG_END = _AUG_KLO + SIGMA_PIECES
_B_SHIFT = B_QK_DIM - LANES


def _aug_base(comp):
    return A_HEAD_DIM if comp == 0 else 0


_F32 = jnp.float32
_BF16 = jnp.bfloat16


def _lambda_init(layer_idx):
    return 0.8 - 0.6 * math.exp(-0.3 * layer_idx)


def _alibi_slope(head):
    return 2.0 ** (-8.0 * (head + 1) / A_HEADS)


def _sigma_pieces(head):
    rest = _alibi_slope(head) * LOG2E
    pieces = []
    for _ in range(SIGMA_PIECES):
        p = float(np.asarray(rest, np.float32).astype(_BF16).astype(np.float32))
        pieces.append(p)
        rest -= p
    return pieces


def _group_sum_matrices():
    unit = np.arange(256) // A_HEAD_DIM
    m_a = (unit[:, None] == unit[None, :]).astype(np.float32)
    m_b = np.zeros((2, 256, 256), np.float32)
    m_b[0, :B_QK_DIM, :] = 1.0
    m_b[1, :B_NOPE, :] = 1.0
    return jnp.asarray(m_a, _BF16), jnp.asarray(m_b, _BF16)


def _swap_rope_halves(a):
    half = B_ROPE // 2
    return jnp.concatenate([a[..., half:], a[..., :half]], axis=-1)


def _silu(x):
    return x / (1.0 + jnp.exp(-x))


def _proj_kernel(x_ref, nw_ref, win_ref, wtail_ref, gaq_ref, gak_ref, gcq_ref, wuq_ref, gckv_ref, wukv_ref,
                 gbq_ref, gbkn_ref, gbkr_ref, ct_ref, st_ref, augq_ref, augk_ref, augb_ref, suma_ref, sumb_ref,
                 qa_ref, ka_ref, va_ref, ga_ref, qb_ref, kb_ref, vb_ref, gb_ref, *, fixed_shift, seq_len):
    tm = x_ref.shape[0]
    rows = slice(0, tm)
    x = x_ref[rows, :]
    h = x * lax.rsqrt(jnp.mean(x * x, axis=-1, keepdims=True) + EPS) * nw_ref[...]
    hb = h.astype(_BF16)

    def proj(lo, hi):
        if lo >= _P_KR:
            return jnp.dot(hb, wtail_ref[:, lo - _P_KR:hi - _P_KR], preferred_element_type=_F32)
        return jnp.dot(hb, win_ref[:, lo:hi], preferred_element_type=_F32)

    lane = lax.broadcasted_iota(jnp.int32, (tm, LANES), 1)
    low_half = lane < B_ROPE
    ones_col = (lane == 0).astype(_F32)
    ct = ct_ref[rows, :]
    st = st_ref[rows, :]

    def aug_select(index, base, hi, lo):
        at = lambda off: (index >= base + off) & (index < base + off + SIGMA_PIECES)
        return at(hi), at(lo)

    if fixed_shift:
        a_qscale = (A_HEAD_DIM ** -0.5) * LOG2E
        b_qscale = (B_QK_DIM ** -0.5) * LOG2E
        first = (pl.program_id(0) % (seq_len // tm)) * tm - seq_len // 2
        split_pos = lambda p: (((p >> 7) * POS_SPLIT).astype(_F32), (p & (POS_SPLIT - 1)).astype(_F32))
        k_hi, k_lo = split_pos(first + lax.broadcasted_iota(jnp.int32, (tm, LANES), 0))
        q_hi, q_lo = split_pos(first + lax.broadcasted_iota(jnp.int32, (A_HEAD_DIM, tm), 1))
        aug_row = lax.broadcasted_iota(jnp.int32, (A_HEAD_DIM, tm), 0)
        at_hi, at_lo = aug_select(aug_row, 0, _AUG_QHI, _AUG_QLO)
        q_pos = jnp.where(at_hi, q_hi, jnp.where(at_lo, q_lo, 0.0))
    else:
        a_qscale = A_HEAD_DIM ** -0.5
        b_qscale = B_QK_DIM ** -0.5

    def group_sums(a, m):
        return jnp.dot((a * a).astype(_BF16), m, preferred_element_type=_F32)

    aq = proj(_P_AQ, _P_AK)
    ak = proj(_P_AK, _P_AV)
    def unit_sums(a):
        return jnp.concatenate([group_sums(a[:, lo:lo + 256], suma_ref[...]) for lo in range(0, a.shape[1], 256)],
                               axis=1)

    qn_all = aq * lax.rsqrt(unit_sums(aq) * (1.0 / A_HEAD_DIM) + EPS) * gaq_ref[...] * a_qscale
    kn_all = ak * lax.rsqrt(unit_sums(ak) * (1.0 / A_HEAD_DIM) + EPS) * gak_ref[...]
    for hd in range(A_HEADS):
        qn2 = qn_all[:, hd * LANES:(hd + 1) * LANES]
        kn2 = kn_all[:, hd * LANES:(hd + 1) * LANES]
        if fixed_shift:
            qn2_t = qn2.T
        for comp in range(2):
            u = 2 * hd + comp
            own = (lane < A_HEAD_DIM) if comp == 0 else (lane >= A_HEAD_DIM)
            if fixed_shift:
                feat = qn2_t[comp * A_HEAD_DIM:(comp + 1) * A_HEAD_DIM]
                aug = jnp.concatenate([augq_ref[u]] * (tm // LANES), axis=1) + q_pos
                stacked = [feat, aug] if comp == 0 else [aug, feat]
                qa_ref[u, 0, :, rows] = jnp.concatenate(stacked, axis=0).astype(_BF16)
                at_hi, at_lo = aug_select(lane, _aug_base(comp), _AUG_KHI, _AUG_KLO)
                k_pos = jnp.where(at_hi, k_hi, jnp.where(at_lo, k_lo, 0.0))
                ka_ref[u, rows, :] = (jnp.where(own, kn2, 0.0) + augk_ref[u] + k_pos).astype(_BF16)
            else:
                qa_ref[u, 0, rows, :] = jnp.where(own, qn2, 0.0).astype(_BF16)
                ka_ref[u, rows, :] = jnp.where(own, kn2, 0.0).astype(_BF16)

    def store_values(v_ref, hd, vv):
        if fixed_shift:
            pad_rows = lax.broadcasted_iota(jnp.int32, (VT_ROWS - LANES, tm), 0)
            v_ref[hd, 0, :, rows] = jnp.concatenate([vv.T, (pad_rows == 0).astype(_F32)], axis=0).astype(_BF16)
        else:
            v_ref[hd, rows, :] = jnp.concatenate([vv, ones_col], axis=1).astype(_BF16)

    av = proj(_P_AV, _P_AG)
    for hd in range(A_HEADS):
        store_values(va_ref, hd, av[:, hd * LANES:(hd + 1) * LANES])
    ga_ref[rows, :] = _silu(proj(_P_AG, _P_CQ)).astype(_BF16)

    cq = proj(_P_CQ, _P_CKV)
    cqn = cq * lax.rsqrt(jnp.mean(cq * cq, axis=-1, keepdims=True) + EPS) * gcq_ref[...]
    qall = jnp.dot(cqn.astype(_BF16), wuq_ref[...], preferred_element_type=_F32)
    ckv = proj(_P_CKV, _P_KR)
    ckvn = ckv * lax.rsqrt(jnp.mean(ckv * ckv, axis=-1, keepdims=True) + EPS) * gckv_ref[...]
    kv = jnp.dot(ckvn.astype(_BF16), wukv_ref[...], preferred_element_type=_F32)
    kr2 = proj(_P_KR, _P_BG)

    for hd in range(B_HEADS):
        q0 = qall[:, hd * 256:hd * 256 + LANES]
        q1 = qall[:, hd * 256 + LANES:(hd + 1) * 256]
        r = lax.rsqrt(group_sums(qall[:, hd * 256:(hd + 1) * 256], sumb_ref[0]) * (1.0 / B_QK_DIM) + EPS)
        q0n = q0 * r[:, :LANES] * gbq_ref[:, :LANES] * b_qscale
        q1n = q1 * r[:, LANES:] * gbq_ref[:, LANES:]
        q1r = (q1n * ct + pltpu.roll(q1n, B_ROPE, 1) * st) * b_qscale
        if fixed_shift:
            q1r = q1r + augb_ref[0:1, :]
            qb_ref[hd, 0, :, rows] = jnp.concatenate([q0n.T, q1r.T], axis=0).astype(_BF16)
        else:
            qb_ref[hd, 0, rows, :] = jnp.concatenate([q0n, q1r], axis=1).astype(_BF16)

    kr_ss = jnp.sum(jnp.where(low_half, kr2 * kr2, 0.0), axis=-1, keepdims=True)
    krg = kr2 * gbkr_ref[...]
    krr = krg * ct + pltpu.roll(krg, B_ROPE, 1) * st
    for hd in range(B_HEADS):
        kn = kv[:, hd * 256:hd * 256 + LANES]
        vv = kv[:, hd * 256 + LANES:(hd + 1) * 256]
        kn_ss = group_sums(kv[:, hd * 256:(hd + 1) * 256], sumb_ref[1])[:, :LANES]
        r = lax.rsqrt((kn_ss + kr_ss) * (1.0 / B_QK_DIM) + EPS)
        k1 = krr * r
        if fixed_shift:
            k1 = k1 + augb_ref[1:2, :]
        kb_ref[hd, rows, :] = jnp.concatenate([kn * r * gbkn_ref[...], k1], axis=1).astype(_BF16)
        store_values(vb_ref, hd, vv)
    gb_ref[rows, :] = _silu(proj(_P_BG, _P_END)).astype(_BF16)


def _proj_call(x2d, seq_len, lw, tm, fixed_shift):
    t = x2d.shape[0]
    blocks_per_seq = seq_len // tm
    const = lambda i: (0, 0)
    const3 = lambda i: (0, 0, 0)
    row = lambda i: (i, 0)
    unit_row = lambda i: (0, i, 0)
    var_row = lambda i: (0, 0, i, 0)
    rope_row = lambda i: (i % blocks_per_seq, 0)
    in_specs = [
        pl.BlockSpec((tm, D_MODEL), row),
        pl.BlockSpec((1, D_MODEL), const),
        pl.BlockSpec((None, D_MODEL, int(_SPLITS[-1])), lambda i: (lw["layer"], 0, 0)),
        pl.BlockSpec((None, D_MODEL, int(_P_END - _P_KR)), lambda i: (lw["layer"], 0, 0)),
        pl.BlockSpec((1, A_UNITS * A_HEAD_DIM), const),
        pl.BlockSpec((1, A_UNITS * A_HEAD_DIM), const),
        pl.BlockSpec((1, Q_LORA), const),
        pl.BlockSpec((Q_LORA, B_HEADS * 256), const),
        pl.BlockSpec((1, KV_LORA), const),
        pl.BlockSpec((KV_LORA, B_HEADS * 256), const),
        pl.BlockSpec((1, 256), const),
        pl.BlockSpec((1, LANES), const),
        pl.BlockSpec((1, LANES), const),
        pl.BlockSpec((tm, LANES), rope_row),
        pl.BlockSpec((tm, LANES), rope_row),
        pl.BlockSpec((A_UNITS, A_HEAD_DIM, LANES), const3),
        pl.BlockSpec((A_UNITS, 1, LANES), const3),
        pl.BlockSpec((2, LANES), const),
        pl.BlockSpec((256, 256), const),
        pl.BlockSpec((2, 256, 256), const3),
    ]
    if fixed_shift:
        q_shape = lambda units, dk: jax.ShapeDtypeStruct((units, t // tm, dk, tm), _BF16)
        q_spec = lambda units, dk: pl.BlockSpec((units, 1, dk, tm), lambda i: (0, i, 0, 0))
        v_shape = lambda heads: jax.ShapeDtypeStruct((heads, t // tm, VT_ROWS, tm), _BF16)
        v_spec = lambda heads: pl.BlockSpec((heads, 1, VT_ROWS, tm), lambda i: (0, i, 0, 0))
    else:
        q_shape = lambda units, dk: jax.ShapeDtypeStruct((units, 1, t, dk), _BF16)
        q_spec = lambda units, dk: pl.BlockSpec((units, 1, tm, dk), var_row)
        v_shape = lambda heads: jax.ShapeDtypeStruct((heads, t, 256), _BF16)
        v_spec = lambda heads: pl.BlockSpec((heads, tm, 256), unit_row)
    out_shape = [
        q_shape(A_UNITS, LANES),
        jax.ShapeDtypeStruct((A_UNITS, t, LANES), _BF16),
        v_shape(A_HEADS),
        jax.ShapeDtypeStruct((t, 512), _BF16),
        q_shape(B_HEADS, 256),
        jax.ShapeDtypeStruct((B_HEADS, t, 256), _BF16),
        v_shape(B_HEADS),
        jax.ShapeDtypeStruct((t, 512), _BF16),
    ]
    out_specs = [
        q_spec(A_UNITS, LANES),
        pl.BlockSpec((A_UNITS, tm, LANES), unit_row),
        v_spec(A_HEADS),
        pl.BlockSpec((tm, 512), row),
        q_spec(B_HEADS, 256),
        pl.BlockSpec((B_HEADS, tm, 256), unit_row),
        v_spec(B_HEADS),
        pl.BlockSpec((tm, 512), row),
    ]
    return pl.pallas_call(
        functools.partial(_proj_kernel, fixed_shift=fixed_shift, seq_len=seq_len),
        grid=(t // tm,),
        in_specs=in_specs,
        out_specs=out_specs,
        out_shape=out_shape,
        compiler_params=pltpu.CompilerParams(
            dimension_semantics=("parallel",), vmem_limit_bytes=_VMEM_LIMIT),
        name="proj",
    )(x2d, lw["norm_w"], lw["w_in"], lw["w_tail"], lw["g_aq"], lw["g_ak"], lw["g_cq"], lw["w_uq"], lw["g_ckv"],
      lw["w_ukv"], lw["g_bq"], lw["g_bkn"], lw["g_bkr"], lw["rope_c"][seq_len], lw["rope_s"][seq_len],
      lw["aug_q"], lw["aug_k"], lw["aug_b"], *_group_sum_matrices())


def _flash_kernel(slope_ref, q_ref, k_ref, v_ref, o_ref, m_sc, acc_sc, *, alibi, heads_per_slope):
    kj = pl.program_id(3)
    tq = q_ref.shape[3]
    tk = k_ref.shape[2]

    @pl.when(kj == 0)
    def _():
        m_sc[...] = jnp.full(m_sc.shape, -jnp.inf, _F32)
        acc_sc[...] = jnp.zeros(acc_sc.shape, _F32)

    s = lax.dot_general(q_ref[0, 0, 0], k_ref[0, 0], (((1,), (1,)), ((), ())),
                        preferred_element_type=_F32)
    if alibi:
        slope = slope_ref[pl.program_id(0) // heads_per_slope]
        qpos = pl.program_id(2) * tq + lax.broadcasted_iota(jnp.int32, (tq, tk), 0)
        kpos = kj * tk + lax.broadcasted_iota(jnp.int32, (tq, tk), 1)
        s = s - slope * jnp.abs(qpos - kpos).astype(_F32)
    m_prev = m_sc[...]
    m_new = jnp.maximum(m_prev, jnp.max(s, axis=-1, keepdims=True))
    alpha = jnp.exp(m_prev - m_new)
    p = jnp.exp(s - m_new)
    acc_sc[...] = alpha * acc_sc[...] + jnp.dot(p.astype(_BF16), v_ref[0, 0],
                                                 preferred_element_type=_F32)
    m_sc[...] = m_new

    @pl.when(kj == pl.num_programs(3) - 1)
    def _():
        acc = acc_sc[...]
        o_ref[0, 0] = acc[:, :LANES] / acc[:, LANES:LANES + 1]


def _flash_call(q, k, v, slopes, *, alibi, units_per_v, tq, tk):
    u, _, b, s, dk = q.shape
    kern = functools.partial(_flash_kernel, alibi=alibi, heads_per_slope=units_per_v)
    grid_spec = pltpu.PrefetchScalarGridSpec(
        num_scalar_prefetch=1,
        grid=(u, b, s // tq, s // tk),
        in_specs=[
            pl.BlockSpec((1, 1, 1, tq, dk), lambda ui, bi, qi, ki, sl: (ui, 0, bi, qi, 0)),
            pl.BlockSpec((1, 1, tk, dk), lambda ui, bi, qi, ki, sl: (ui, bi, ki, 0)),
            pl.BlockSpec((1, 1, tk, 256), lambda ui, bi, qi, ki, sl: (ui // units_per_v, bi, ki, 0)),
        ],
        out_specs=pl.BlockSpec((1, 1, tq, LANES), lambda ui, bi, qi, ki, sl: (ui, bi, qi, 0)),
        scratch_shapes=[pltpu.VMEM((tq, 1), _F32), pltpu.VMEM((tq, 256), _F32)],
    )
    return pl.pallas_call(
        kern,
        grid_spec=grid_spec,
        out_shape=jax.ShapeDtypeStruct((u, b, s, LANES), _F32),
        compiler_params=pltpu.CompilerParams(
            dimension_semantics=("parallel", "parallel", "parallel", "arbitrary"),
            vmem_limit_bytes=_VMEM_LIMIT),
        name="flash_a" if alibi else "flash_b",
    )(slopes, q, k, v)


def _attn_kernel(qt_ref, k_ref, vt_ref, *rest, tq, q_steps, alibi, gated):
    gate_ref = rest[0] if gated else None
    o_ref, qv_sc = rest[-2:]
    _attn_body(qt_ref, k_ref, vt_ref, gate_ref, o_ref, qv_sc, tq=tq, q_steps=q_steps, alibi=alibi)


def _attn_body(qt_ref, k_ref, vt_ref, gate_ref, o_ref, qv_sc, *, tq, q_steps, alibi):
    n_k, _, tk = vt_ref.shape[2:]
    q_sub = qt_ref.shape[1] * qt_ref.shape[3] // tq
    n_diag = tq // tk
    if alibi:
        head = pl.program_id(0) // 2
        sigma = jnp.float32(_alibi_slope(A_HEADS - 1) * LOG2E)
        for hd in range(A_HEADS - 1):
            sigma = jnp.where(head == hd, jnp.float32(_alibi_slope(hd) * LOG2E), sigma)
        ahead = (lax.broadcasted_iota(jnp.int32, (tk, tq), 0) - lax.broadcasted_iota(jnp.int32, (tk, tq), 1))
        aug_row = lax.broadcasted_iota(jnp.int32, (qt_ref.shape[2], tq), 0)
        aug_row = aug_row - jnp.where(pl.program_id(0) % 2 == 0, A_HEAD_DIM, 0)
        bias_rows = (aug_row >= _AUG_QHI) & (aug_row < _AUG_END)

    def query_block(sub, qi, slot):
        pieces = tq // qt_ref.shape[3]
        qt = jnp.concatenate([qt_ref[0, sub * pieces + piece] for piece in range(pieces)], axis=1)
        qv_sc[slot, 0] = qt
        if alibi:
            qv_sc[slot, 1] = jnp.where(bias_rows, -qt, qt)

        def block(var, j, correction=None):
            k = k_ref[0, 0, pl.ds(pl.multiple_of(j * tk, tk), tk), :]
            st = jnp.dot(k, qv_sc[slot, var], preferred_element_type=_F32)
            if correction is not None:
                st = st + correction
            return jnp.dot(vt_ref[0, 0, j], jnp.exp2(st).astype(_BF16), preferred_element_type=_F32)

        def accumulate(acc, part):
            return part if acc is None else acc + part

        acc = None
        if alibi:
            first_diag = qi * n_diag
            for d in range(n_diag):
                corr = (-2.0 * sigma) * jnp.maximum(ahead + d * tk, 0).astype(_F32)
                acc = accumulate(acc, block(0, first_diag + d, corr))
            for x in range(n_k - n_diag):
                after = (x >= first_diag) * 1
                acc = accumulate(acc, block(after, x + n_diag * after))
        else:
            for x in range(n_k):
                acc = accumulate(acc, block(0, x))
        rows = pl.ds(pl.multiple_of(sub * tq, tq), tq)
        out = (acc[:LANES] / acc[LANES:LANES + 1]).T
        if gate_ref is not None:
            out = out * gate_ref[rows, :].astype(_F32)
        o_ref[0, 0, rows, :] = out.astype(o_ref.dtype)

    if q_steps == 1 or q_sub == 1:
        for sub in range(q_sub):
            query_block(sub, sub if q_steps == 1 else pl.program_id(2), sub)
    else:
        def body(sub, carry):
            query_block(sub, pl.program_id(2) * q_sub + sub, 0)
            return carry

        lax.fori_loop(0, q_sub, body, 0)


def _attn_call(qt, k, vt, *, alibi, units_per_v, tq, gate=None):
    u, _, dk, tm = qt.shape
    _, b, s, _ = k.shape
    n_k, _, tk = vt.shape[2:]
    assert tq % tk == 0 and s % tq == 0 and tq % tm == 0, (s, tq, tk, tm)
    n_q = s // tq
    if n_q * n_k <= MAX_BLOCK_PAIRS:
        q_sub, slots = n_q, n_q
    else:
        q_sub, slots = _largest_group(n_q, MAX_LOOPED_QUERY_BLOCKS), 1
    q_steps = n_q // q_sub
    pieces = tq * q_sub // tm
    kern = functools.partial(_attn_kernel, tq=tq, q_steps=q_steps, alibi=alibi, gated=gate is not None)
    gates = [] if gate is None else [gate]
    return pl.pallas_call(
        kern,
        grid=(u, b, q_steps),
        in_specs=[
            pl.BlockSpec((1, pieces, dk, tm), lambda ui, bi, qi: (ui, bi * q_steps + qi, 0, 0)),
            pl.BlockSpec((1, 1, s, dk), lambda ui, bi, qi: (ui, bi, 0, 0)),
            pl.BlockSpec((1, 1, n_k, VT_ROWS, tk), lambda ui, bi, qi: (ui // units_per_v, bi, 0, 0, 0)),
            *[pl.BlockSpec((tq * q_sub, LANES), lambda ui, bi, qi: (bi * q_steps + qi, ui)) for _ in gates],
        ],
        out_specs=pl.BlockSpec((1, 1, tq * q_sub, LANES), lambda ui, bi, qi: (ui, bi, qi, 0)),
        out_shape=jax.ShapeDtypeStruct((u, b, s, LANES), _BF16),
        scratch_shapes=[pltpu.VMEM((slots, 2 if alibi else 1, dk, tq), _BF16)],
        compiler_params=pltpu.CompilerParams(
            dimension_semantics=("parallel", "parallel", "parallel"),
            vmem_limit_bytes=_VMEM_LIMIT),
        name="attn_a" if alibi else "attn_b",
    )(qt, k, vt, *gates)


def _out_kernel(oa_ref, ob_ref, ga_ref, *rest, lam_init, gated_b):
    gb_ref = None if gated_b else rest[0]
    x_ref, lq1_ref, lk1_ref, lq2_ref, lk2_ref, subln_ref, wout_ref, y_ref = rest[0 if gated_b else 1:]
    lam = (jnp.exp(jnp.sum(lq1_ref[...] * lk1_ref[...], axis=-1, keepdims=True))
           - jnp.exp(jnp.sum(lq2_ref[...] * lk2_ref[...], axis=-1, keepdims=True)) + lam_init)
    ga = ga_ref[...].astype(_F32)
    pieces = []
    for hd in range(A_HEADS):
        d = oa_ref[2 * hd].astype(_F32) - lam * oa_ref[2 * hd + 1].astype(_F32)
        n = d * lax.rsqrt(jnp.mean(d * d, axis=-1, keepdims=True) + EPS) * subln_ref[...]
        pieces.append(n * (1.0 - lam_init) * ga[:, hd * LANES:(hd + 1) * LANES])
    for hd in range(B_HEADS):
        yb = ob_ref[hd].astype(_F32)
        if gb_ref is not None:
            yb = yb * gb_ref[:, hd * LANES:(hd + 1) * LANES].astype(_F32)
        pieces.append(yb)
    y = jnp.concatenate(pieces, axis=1).astype(_BF16)
    y_ref[...] = x_ref[...] + jnp.dot(y, wout_ref[...], preferred_element_type=_F32)


def _out_call(oa, ob, ga, gb, x2d, lw, lam_init, tm):
    t = x2d.shape[0]
    const = lambda i: (0, 0)
    row = lambda i: (i, 0)
    unit_row = lambda i: (0, i, 0)
    gates_b = [] if gb is None else [gb]
    return pl.pallas_call(
        functools.partial(_out_kernel, lam_init=lam_init, gated_b=gb is None),
        grid=(t // tm,),
        in_specs=[
            pl.BlockSpec((A_UNITS, tm, LANES), unit_row),
            pl.BlockSpec((B_HEADS, tm, LANES), unit_row),
            pl.BlockSpec((tm, 512), row),
            *[pl.BlockSpec((tm, 512), row) for _ in gates_b],
            pl.BlockSpec((tm, D_MODEL), row),
            pl.BlockSpec((1, A_HEAD_DIM), const),
            pl.BlockSpec((1, A_HEAD_DIM), const),
            pl.BlockSpec((1, A_HEAD_DIM), const),
            pl.BlockSpec((1, A_HEAD_DIM), const),
            pl.BlockSpec((1, A_V_DIM), const),
            pl.BlockSpec((None, D_MODEL, D_MODEL), lambda i: (lw["layer"], 0, 0)),
        ],
        out_specs=pl.BlockSpec((tm, D_MODEL), row),
        out_shape=jax.ShapeDtypeStruct((t, D_MODEL), _F32),
        compiler_params=pltpu.CompilerParams(
            dimension_semantics=("parallel",), vmem_limit_bytes=_VMEM_LIMIT),
        name="out",
    )(oa, ob, ga, *gates_b, x2d, lw["lq1"], lw["lk1"], lw["lq2"], lw["lk2"], lw["subln"], lw["w_out"])


def _rope_tables(seq_len):
    inv = ROPE_THETA ** (-np.arange(0, B_ROPE, 2, dtype=np.float64) / B_ROPE)
    n_hi = -(-seq_len // POS_SPLIT)
    ang_hi = (np.arange(n_hi, dtype=np.float64) * POS_SPLIT)[:, None, None] * inv
    ang_lo = np.arange(POS_SPLIT, dtype=np.float64)[None, :, None] * inv

    def lanes(first, second):
        pad = np.zeros(first.shape[:-1] + (LANES - B_ROPE,))
        return jnp.asarray(np.concatenate([first, second, pad], axis=-1), _F32)

    cos_hi, sin_hi, cos_lo, sin_lo = np.cos(ang_hi), np.sin(ang_hi), np.cos(ang_lo), np.sin(ang_lo)
    cos_lo, sin_lo = lanes(cos_lo, cos_lo), lanes(sin_lo, sin_lo)
    cos = lanes(cos_hi, cos_hi) * cos_lo - lanes(sin_hi, sin_hi) * sin_lo
    neg_sin = lanes(-sin_hi, sin_hi) * cos_lo + lanes(-cos_hi, cos_hi) * sin_lo
    return (cos.reshape(n_hi * POS_SPLIT, LANES)[:seq_len], neg_sin.reshape(n_hi * POS_SPLIT, LANES)[:seq_len])


def _score_bounds(l, p):
    amax = lambda v: jnp.max(jnp.abs(v[l].astype(_F32)))
    bound_a = amax(p["a_q_norm"]) * amax(p["a_k_norm"]) * (A_HEAD_DIM ** 0.5)
    bound_b = amax(p["b_q_norm"]) * amax(p["b_k_norm"]) * (B_QK_DIM ** 0.5)
    return bound_a, bound_b


def _aug_rows(bound_a, bound_b):
    aug_q = np.zeros((A_UNITS, A_HEAD_DIM, LANES), np.float32)
    aug_k = np.zeros((A_UNITS, 1, LANES), np.float32)
    for u in range(A_UNITS):
        sig = _sigma_pieces(u // 2)
        base = _aug_base(u % 2)
        aug_k[u, 0, base + _AUG_SHIFT] = 1.0
        for a in range(SIGMA_PIECES):
            aug_k[u, 0, base + _AUG_QHI + a] = -sig[a]
            aug_k[u, 0, base + _AUG_QLO + a] = -sig[a]
            aug_q[u, _AUG_KHI + a, :] = sig[a]
            aug_q[u, _AUG_KLO + a, :] = sig[a]
    shift_row = (np.arange(A_HEAD_DIM) == _AUG_SHIFT).astype(np.float32).reshape(1, A_HEAD_DIM, 1)
    aug_q = jnp.asarray(aug_q) - (bound_a * LOG2E) * shift_row
    b_lane = (np.arange(LANES) == _B_SHIFT).astype(np.float32)
    aug_b = jnp.stack([-(bound_b * LOG2E) * b_lane, jnp.asarray(b_lane)])
    return aug_q, jnp.asarray(aug_k), aug_b


def _stacked_weights(p):
    w = p["w_in"].astype(_BF16)
    lo, hi = _SPLITS[6], _SPLITS[7]
    w_tail = jnp.concatenate([w[:, :, lo:hi], _swap_rope_halves(w[:, :, lo:hi]), w[:, :, hi:]], axis=2)
    return w, w_tail, p["w_out"].astype(_BF16)


def _layer_weights(l, p, stacked, rope_c, rope_s):
    w_in, w_tail, w_out = stacked
    wuq = p["b_w_uq"][l].reshape(Q_LORA, B_HEADS, B_QK_DIM)
    w_uq = jnp.concatenate([wuq, _swap_rope_halves(wuq[:, :, B_NOPE:])], axis=2).reshape(Q_LORA, B_HEADS * 256)
    gq = p["b_q_norm"][l]
    gk = p["b_k_norm"][l]
    row = lambda v: v.reshape(1, -1).astype(_F32)
    bound_a, bound_b = _score_bounds(l, p)
    aug_q, aug_k, aug_b = _aug_rows(bound_a, bound_b)
    return {
        "layer": l,
        "norm_w": row(p["norm_w"][l]),
        "w_in": w_in,
        "w_tail": w_tail,
        "g_aq": row(jnp.tile(p["a_q_norm"][l], A_UNITS)),
        "g_ak": row(jnp.tile(p["a_k_norm"][l], A_UNITS)),
        "g_cq": row(p["b_cq_norm"][l]),
        "w_uq": w_uq.astype(_BF16),
        "g_ckv": row(p["b_ckv_norm"][l]),
        "w_ukv": p["b_w_ukv"][l].astype(_BF16),
        "g_bq": row(jnp.concatenate([gq, _swap_rope_halves(gq[B_NOPE:])])),
        "g_bkn": row(gk[:B_NOPE]),
        "g_bkr": row(jnp.concatenate([gk[B_NOPE:], _swap_rope_halves(gk[B_NOPE:])])),
        "rope_c": rope_c,
        "rope_s": rope_s,
        "aug_q": aug_q, "aug_k": aug_k, "aug_b": aug_b,
        "lq1": row(p["a_lq1"][l]), "lk1": row(p["a_lk1"][l]),
        "lq2": row(p["a_lq2"][l]), "lk2": row(p["a_lk2"][l]),
        "subln": row(p["a_subln"][l]),
        "w_out": w_out,
    }


def _tile(n, pref):
    return pref if n % pref == 0 else n


def _largest_group(n, cap):
    return max(g for g in range(1, cap + 1) if n % g == 0) if n > 0 else 1


def _encoder_layer(x, lw, lam_init, slopes, fixed_shift):
    b, s, _ = x.shape
    t = b * s
    x2d = x.reshape(t, D_MODEL)
    tm = _tile(s, FIXED_TK)
    qa, ka, va, ga, qb, kb, vb, gb = _proj_call(x2d, s, lw, tm, fixed_shift)
    split = lambda a: a.reshape(a.shape[:-2] + (b, s, a.shape[-1]))
    tq = tk = _tile(s, 512)
    if fixed_shift:
        assert s <= POS_SPLIT * POS_SPLIT * 2
        ftq = _tile(s, FIXED_TQ)
        split_t = lambda a: a.reshape((a.shape[0], b, s // tm) + a.shape[2:])
        oa = _attn_call(qa, split(ka), split_t(va), alibi=True, units_per_v=2, tq=ftq)
        ob = _attn_call(qb, split(kb), split_t(vb), alibi=False, units_per_v=1, tq=ftq, gate=gb)
        gb = None
    else:
        oa = _flash_call(split(qa), split(ka), split(va), slopes, alibi=True, units_per_v=2, tq=tq, tk=tk)
        ob = _flash_call(split(qb), split(kb), split(vb), slopes, alibi=False, units_per_v=1, tq=tq, tk=tk)
    y = _out_call(oa.reshape(A_UNITS, t, LANES), ob.reshape(B_HEADS, t, LANES), ga, gb, x2d, lw, lam_init,
                  _tile(s, OUT_ROWS))
    return y.reshape(b, s, D_MODEL)


def _forward(x_prompt, x_sample, p, fixed_shift):
    depth = p["norm_w"].shape[0]
    rope_c, rope_s = {}, {}
    for s in {x_prompt.shape[1], x_sample.shape[1]}:
        rope_c[s], rope_s[s] = _rope_tables(s)
    slopes = jnp.asarray([_alibi_slope(hd) for hd in range(A_HEADS)], _F32)
    stacked = _stacked_weights(p)
    y_prompt, y_sample = x_prompt, x_sample
    for l in range(depth):
        lw = _layer_weights(l, p, stacked, rope_c, rope_s)
        lam_init = _lambda_init(l)
        y_prompt = _encoder_layer(y_prompt, lw, lam_init, slopes, fixed_shift)
        y_sample = _encoder_layer(y_sample, lw, lam_init, slopes, fixed_shift)
    return (y_prompt, y_sample)


def kernel(x_prompt, x_sample, norm_w, w_in, a_q_norm, a_k_norm, a_lq1, a_lk1, a_lq2, a_lk2, a_subln,
           b_cq_norm, b_w_uq, b_ckv_norm, b_w_ukv, b_q_norm, b_k_norm, w_out):
    p = dict(norm_w=norm_w, w_in=w_in, a_q_norm=a_q_norm, a_k_norm=a_k_norm, a_lq1=a_lq1, a_lk1=a_lk1,
             a_lq2=a_lq2, a_lk2=a_lk2, a_subln=a_subln, b_cq_norm=b_cq_norm, b_w_uq=b_w_uq,
             b_ckv_norm=b_ckv_norm, b_w_ukv=b_w_ukv, b_q_norm=b_q_norm, b_k_norm=b_k_norm, w_out=w_out)
    bounds = jnp.stack([jnp.stack(_score_bounds(l, p)) for l in range(norm_w.shape[0])])
    shift_ok = jnp.max(bounds) <= MAX_STATIC_SHIFT
    return lax.cond(shift_ok,
                    lambda xp, xs, pp: _forward(xp, xs, pp, True),
                    lambda xp, xs, pp: _forward(xp, xs, pp, False),
                    x_prompt, x_sample, p)
```

```python
import functools
import math

import numpy as np
import jax
import jax.numpy as jnp
from jax import lax
from jax.experimental import pallas as pl
from jax.experimental.pallas import tpu as pltpu

D_MODEL = 1024
A_HEADS = 4
A_HEAD_DIM = 64
A_V_DIM = 128
A_UNITS = 2 * A_HEADS
B_HEADS = 4
B_NOPE = 128
B_ROPE = 64
B_QK_DIM = B_NOPE + B_ROPE
Q_LORA = 256
KV_LORA = 128
ROPE_THETA = 10000.0
EPS = 1e-6
LANES = 128

_SPLITS = np.cumsum([0, 512, 512, 512, 512, Q_LORA, KV_LORA, B_ROPE, 512])
_P_AQ, _P_AK, _P_AV, _P_AG, _P_CQ, _P_CKV, _P_KR, _P_BG, _P_END = np.cumsum(
    [0, 512, 512, 512, 512, Q_LORA, KV_LORA, 2 * B_ROPE, 512])

_VMEM_LIMIT = 56 * 1024 * 1024
LOG2E = math.log2(math.e)
MAX_STATIC_SHIFT = 32.0
POS_SPLIT = 128
SIGMA_PIECES = 3
FIXED_TQ = 1024
FIXED_TK = 512
MAX_BLOCK_PAIRS = 32
MAX_LOOPED_QUERY_BLOCKS = 4
OUT_ROWS = 1024
VT_ROWS = 144

_AUG_SHIFT = 0
_AUG_QHI = _AUG_SHIFT + 1
_AUG_QLO = _AUG_QHI + SIGMA_PIECES
_AUG_KHI = _AUG_QLO + SIGMA_PIECES
_AUG_KLO = _AUG_KHI + SIGMA_PIECES
_AUG_END = _AUG_KLO + SIGMA_PIECES
_B_SHIFT = B_QK_DIM - LANES


def _aug_base(comp):
    return A_HEAD_DIM if comp == 0 else 0


_F32 = jnp.float32
_BF16 = jnp.bfloat16


def _lambda_init(layer_idx):
    return 0.8 - 0.6 * math.exp(-0.3 * layer_idx)


def _alibi_slope(head):
    return 2.0 ** (-8.0 * (head + 1) / A_HEADS)


def _sigma_pieces(head):
    rest = _alibi_slope(head) * LOG2E
    pieces = []
    for _ in range(SIGMA_PIECES):
        p = float(np.asarray(rest, np.float32).astype(_BF16).astype(np.float32))
        pieces.append(p)
        rest -= p
    return pieces


def _group_sum_matrices():
    unit = np.arange(256) // A_HEAD_DIM
    m_a = (unit[:, None] == unit[None, :]).astype(np.float32)
    m_b = np.zeros((2, 256, 256), np.float32)
    m_b[0, :B_QK_DIM, :] = 1.0
    m_b[1, :B_NOPE, :] = 1.0
    return jnp.asarray(m_a, _BF16), jnp.asarray(m_b, _BF16)


def _swap_rope_halves(a):
    half = B_ROPE // 2
    return jnp.concatenate([a[..., half:], a[..., :half]], axis=-1)


def _silu(x):
    return x / (1.0 + jnp.exp(-x))


def _proj_kernel(x_ref, nw_ref, win_ref, wtail_ref, gaq_ref, gak_ref, gcq_ref, wuq_ref, gckv_ref, wukv_ref,
                 gbq_ref, gbkn_ref, gbkr_ref, ct_ref, st_ref, augq_ref, augk_ref, augb_ref, suma_ref, sumb_ref,
                 qa_ref, ka_ref, va_ref, ga_ref, qb_ref, kb_ref, vb_ref, gb_ref, *, fixed_shift, seq_len):
    tm = x_ref.shape[0]
    rows = slice(0, tm)
    x = x_ref[rows, :]
    h = x * lax.rsqrt(jnp.mean(x * x, axis=-1, keepdims=True) + EPS) * nw_ref[...]
    hb = h.astype(_BF16)

    def proj(lo, hi):
        if lo >= _P_KR:
            return jnp.dot(hb, wtail_ref[:, lo - _P_KR:hi - _P_KR], preferred_element_type=_F32)
        return jnp.dot(hb, win_ref[:, lo:hi], preferred_element_type=_F32)

    lane = lax.broadcasted_iota(jnp.int32, (tm, LANES), 1)
    low_half = lane < B_ROPE
    ones_col = (lane == 0).astype(_F32)
    ct = ct_ref[rows, :]
    st = st_ref[rows, :]

    def aug_select(index, base, hi, lo):
        at = lambda off: (index >= base + off) & (index < base + off + SIGMA_PIECES)
        return at(hi), at(lo)

    if fixed_shift:
        a_qscale = (A_HEAD_DIM ** -0.5) * LOG2E
        b_qscale = (B_QK_DIM ** -0.5) * LOG2E
        first = (pl.program_id(0) % (seq_len // tm)) * tm - seq_len // 2
        split_pos = lambda p: (((p >> 7) * POS_SPLIT).astype(_F32), (p & (POS_SPLIT - 1)).astype(_F32))
        k_hi, k_lo = split_pos(first + lax.broadcasted_iota(jnp.int32, (tm, LANES), 0))
        q_hi, q_lo = split_pos(first + lax.broadcasted_iota(jnp.int32, (A_HEAD_DIM, tm), 1))
        aug_row = lax.broadcasted_iota(jnp.int32, (A_HEAD_DIM, tm), 0)
        at_hi, at_lo = aug_select(aug_row, 0, _AUG_QHI, _AUG_QLO)
        q_pos = jnp.where(at_hi, q_hi, jnp.where(at_lo, q_lo, 0.0))
    else:
        a_qscale = A_HEAD_DIM ** -0.5
        b_qscale = B_QK_DIM ** -0.5

    def group_sums(a, m):
        return jnp.dot((a * a).astype(_BF16), m, preferred_element_type=_F32)

    aq = proj(_P_AQ, _P_AK)
    ak = proj(_P_AK, _P_AV)
    def unit_sums(a):
        return jnp.concatenate([group_sums(a[:, lo:lo + 256], suma_ref[...]) for lo in range(0, a.shape[1], 256)],
                               axis=1)

    qn_all = aq * lax.rsqrt(unit_sums(aq) * (1.0 / A_HEAD_DIM) + EPS) * gaq_ref[...] * a_qscale
    kn_all = ak * lax.rsqrt(unit_sums(ak) * (1.0 / A_HEAD_DIM) + EPS) * gak_ref[...]
    for hd in range(A_HEADS):
        qn2 = qn_all[:, hd * LANES:(hd + 1) * LANES]
        kn2 = kn_all[:, hd * LANES:(hd + 1) * LANES]
        if fixed_shift:
            qn2_t = qn2.T
        for comp in range(2):
            u = 2 * hd + comp
            own = (lane < A_HEAD_DIM) if comp == 0 else (lane >= A_HEAD_DIM)
            if fixed_shift:
                feat = qn2_t[comp * A_HEAD_DIM:(comp + 1) * A_HEAD_DIM]
                aug = jnp.concatenate([augq_ref[u]] * (tm // LANES), axis=1) + q_pos
                stacked = [feat, aug] if comp == 0 else [aug, feat]
                qa_ref[u, 0, :, rows] = jnp.concatenate(stacked, axis=0).astype(_BF16)
                at_hi, at_lo = aug_select(lane, _aug_base(comp), _AUG_KHI, _AUG_KLO)
                k_pos = jnp.where(at_hi, k_hi, jnp.where(at_lo, k_lo, 0.0))
                ka_ref[u, rows, :] = (jnp.where(own, kn2, 0.0) + augk_ref[u] + k_pos).astype(_BF16)
            else:
                qa_ref[u, 0, rows, :] = jnp.where(own, qn2, 0.0).astype(_BF16)
                ka_ref[u, rows, :] = jnp.where(own, kn2, 0.0).astype(_BF16)

    def store_values(v_ref, hd, vv):
        if fixed_shift:
            pad_rows = lax.broadcasted_iota(jnp.int32, (VT_ROWS - LANES, tm), 0)
            v_ref[hd, 0, :, rows] = jnp.concatenate([vv.T, (pad_rows == 0).astype(_F32)], axis=0).astype(_BF16)
        else:
            v_ref[hd, rows, :] = jnp.concatenate([vv, ones_col], axis=1).astype(_BF16)

    av = proj(_P_AV, _P_AG)
    for hd in range(A_HEADS):
        store_values(va_ref, hd, av[:, hd * LANES:(hd + 1) * LANES])
    ga_ref[rows, :] = _silu(proj(_P_AG, _P_CQ)).astype(_BF16)

    cq = proj(_P_CQ, _P_CKV)
    cqn = cq * lax.rsqrt(jnp.mean(cq * cq, axis=-1, keepdims=True) + EPS) * gcq_ref[...]
    qall = jnp.dot(cqn.astype(_BF16), wuq_ref[...], preferred_element_type=_F32)
    ckv = proj(_P_CKV, _P_KR)
    ckvn = ckv * lax.rsqrt(jnp.mean(ckv * ckv, axis=-1, keepdims=True) + EPS) * gckv_ref[...]
    kv = jnp.dot(ckvn.astype(_BF16), wukv_ref[...], preferred_element_type=_F32)
    kr2 = proj(_P_KR, _P_BG)

    for hd in range(B_HEADS):
        q0 = qall[:, hd * 256:hd * 256 + LANES]
        q1 = qall[:, hd * 256 + LANES:(hd + 1) * 256]
        r = lax.rsqrt(group_sums(qall[:, hd * 256:(hd + 1) * 256], sumb_ref[0]) * (1.0 / B_QK_DIM) + EPS)
        q0n = q0 * r[:, :LANES] * gbq_ref[:, :LANES] * b_qscale
        q1n = q1 * r[:, LANES:] * gbq_ref[:, LANES:]
        q1r = (q1n * ct + pltpu.roll(q1n, B_ROPE, 1) * st) * b_qscale
        if fixed_shift:
            q1r = q1r + augb_ref[0:1, :]
            qb_ref[hd, 0, :, rows] = jnp.concatenate([q0n.T, q1r.T], axis=0).astype(_BF16)
        else:
            qb_ref[hd, 0, rows, :] = jnp.concatenate([q0n, q1r], axis=1).astype(_BF16)

    kr_ss = jnp.sum(jnp.where(low_half, kr2 * kr2, 0.0), axis=-1, keepdims=True)
    krg = kr2 * gbkr_ref[...]
    krr = krg * ct + pltpu.roll(krg, B_ROPE, 1) * st
    for hd in range(B_HEADS):
        kn = kv[:, hd * 256:hd * 256 + LANES]
        vv = kv[:, hd * 256 + LANES:(hd + 1) * 256]
        kn_ss = group_sums(kv[:, hd * 256:(hd + 1) * 256], sumb_ref[1])[:, :LANES]
        r = lax.rsqrt((kn_ss + kr_ss) * (1.0 / B_QK_DIM) + EPS)
        k1 = krr * r
        if fixed_shift:
            k1 = k1 + augb_ref[1:2, :]
        kb_ref[hd, rows, :] = jnp.concatenate([kn * r * gbkn_ref[...], k1], axis=1).astype(_BF16)
        store_values(vb_ref, hd, vv)
    gb_ref[rows, :] = _silu(proj(_P_BG, _P_END)).astype(_BF16)


def _proj_call(x2d, seq_len, lw, tm, fixed_shift):
    t = x2d.shape[0]
    blocks_per_seq = seq_len // tm
    const = lambda i: (0, 0)
    const3 = lambda i: (0, 0, 0)
    row = lambda i: (i, 0)
    unit_row = lambda i: (0, i, 0)
    var_row = lambda i: (0, 0, i, 0)
    rope_row = lambda i: (i % blocks_per_seq, 0)
    in_specs = [
        pl.BlockSpec((tm, D_MODEL), row),
        pl.BlockSpec((1, D_MODEL), const),
        pl.BlockSpec((None, D_MODEL, int(_SPLITS[-1])), lambda i: (lw["layer"], 0, 0)),
        pl.BlockSpec((None, D_MODEL, int(_P_END - _P_KR)), lambda i: (lw["layer"], 0, 0)),
        pl.BlockSpec((1, A_UNITS * A_HEAD_DIM), const),
        pl.BlockSpec((1, A_UNITS * A_HEAD_DIM), const),
        pl.BlockSpec((1, Q_LORA), const),
        pl.BlockSpec((Q_LORA, B_HEADS * 256), const),
        pl.BlockSpec((1, KV_LORA), const),
        pl.BlockSpec((KV_LORA, B_HEADS * 256), const),
        pl.BlockSpec((1, 256), const),
        pl.BlockSpec((1, LANES), const),
        pl.BlockSpec((1, LANES), const),
        pl.BlockSpec((tm, LANES), rope_row),
        pl.BlockSpec((tm, LANES), rope_row),
        pl.BlockSpec((A_UNITS, A_HEAD_DIM, LANES), const3),
        pl.BlockSpec((A_UNITS, 1, LANES), const3),
        pl.BlockSpec((2, LANES), const),
        pl.BlockSpec((256, 256), const),
        pl.BlockSpec((2, 256, 256), const3),
    ]
    if fixed_shift:
        q_shape = lambda units, dk: jax.ShapeDtypeStruct((units, t // tm, dk, tm), _BF16)
        q_spec = lambda units, dk: pl.BlockSpec((units, 1, dk, tm), lambda i: (0, i, 0, 0))
        v_shape = lambda heads: jax.ShapeDtypeStruct((heads, t // tm, VT_ROWS, tm), _BF16)
        v_spec = lambda heads: pl.BlockSpec((heads, 1, VT_ROWS, tm), lambda i: (0, i, 0, 0))
    else:
        q_shape = lambda units, dk: jax.ShapeDtypeStruct((units, 1, t, dk), _BF16)
        q_spec = lambda units, dk: pl.BlockSpec((units, 1, tm, dk), var_row)
        v_shape = lambda heads: jax.ShapeDtypeStruct((heads, t, 256), _BF16)
        v_spec = lambda heads: pl.BlockSpec((heads, tm, 256), unit_row)
    out_shape = [
        q_shape(A_UNITS, LANES),
        jax.ShapeDtypeStruct((A_UNITS, t, LANES), _BF16),
        v_shape(A_HEADS),
        jax.ShapeDtypeStruct((t, 512), _BF16),
        q_shape(B_HEADS, 256),
        jax.ShapeDtypeStruct((B_HEADS, t, 256), _BF16),
        v_shape(B_HEADS),
        jax.ShapeDtypeStruct((t, 512), _BF16),
    ]
    out_specs = [
        q_spec(A_UNITS, LANES),
        pl.BlockSpec((A_UNITS, tm, LANES), unit_row),
        v_spec(A_HEADS),
        pl.BlockSpec((tm, 512), row),
        q_spec(B_HEADS, 256),
        pl.BlockSpec((B_HEADS, tm, 256), unit_row),
        v_spec(B_HEADS),
        pl.BlockSpec((tm, 512), row),
    ]
    return pl.pallas_call(
        functools.partial(_proj_kernel, fixed_shift=fixed_shift, seq_len=seq_len),
        grid=(t // tm,),
        in_specs=in_specs,
        out_specs=out_specs,
        out_shape=out_shape,
        compiler_params=pltpu.CompilerParams(
            dimension_semantics=("parallel",), vmem_limit_bytes=_VMEM_LIMIT),
        name="proj",
    )(x2d, lw["norm_w"], lw["w_in"], lw["w_tail"], lw["g_aq"], lw["g_ak"], lw["g_cq"], lw["w_uq"], lw["g_ckv"],
      lw["w_ukv"], lw["g_bq"], lw["g_bkn"], lw["g_bkr"], lw["rope_c"][seq_len], lw["rope_s"][seq_len],
      lw["aug_q"], lw["aug_k"], lw["aug_b"], *_group_sum_matrices())


def _flash_kernel(slope_ref, q_ref, k_ref, v_ref, o_ref, m_sc, acc_sc, *, alibi, heads_per_slope):
    kj = pl.program_id(3)
    tq = q_ref.shape[3]
    tk = k_ref.shape[2]

    @pl.when(kj == 0)
    def _():
        m_sc[...] = jnp.full(m_sc.shape, -jnp.inf, _F32)
        acc_sc[...] = jnp.zeros(acc_sc.shape, _F32)

    s = lax.dot_general(q_ref[0, 0, 0], k_ref[0, 0], (((1,), (1,)), ((), ())),
                        preferred_element_type=_F32)
    if alibi:
        slope = slope_ref[pl.program_id(0) // heads_per_slope]
        qpos = pl.program_id(2) * tq + lax.broadcasted_iota(jnp.int32, (tq, tk), 0)
        kpos = kj * tk + lax.broadcasted_iota(jnp.int32, (tq, tk), 1)
        s = s - slope * jnp.abs(qpos - kpos).astype(_F32)
    m_prev = m_sc[...]
    m_new = jnp.maximum(m_prev, jnp.max(s, axis=-1, keepdims=True))
    alpha = jnp.exp(m_prev - m_new)
    p = jnp.exp(s - m_new)
    acc_sc[...] = alpha * acc_sc[...] + jnp.dot(p.astype(_BF16), v_ref[0, 0],
                                                 preferred_element_type=_F32)
    m_sc[...] = m_new

    @pl.when(kj == pl.num_programs(3) - 1)
    def _():
        acc = acc_sc[...]
        o_ref[0, 0] = acc[:, :LANES] / acc[:, LANES:LANES + 1]


def _flash_call(q, k, v, slopes, *, alibi, units_per_v, tq, tk):
    u, _, b, s, dk = q.shape
    kern = functools.partial(_flash_kernel, alibi=alibi, heads_per_slope=units_per_v)
    grid_spec = pltpu.PrefetchScalarGridSpec(
        num_scalar_prefetch=1,
        grid=(u, b, s // tq, s // tk),
        in_specs=[
            pl.BlockSpec((1, 1, 1, tq, dk), lambda ui, bi, qi, ki, sl: (ui, 0, bi, qi, 0)),
            pl.BlockSpec((1, 1, tk, dk), lambda ui, bi, qi, ki, sl: (ui, bi, ki, 0)),
            pl.BlockSpec((1, 1, tk, 256), lambda ui, bi, qi, ki, sl: (ui // units_per_v, bi, ki, 0)),
        ],
        out_specs=pl.BlockSpec((1, 1, tq, LANES), lambda ui, bi, qi, ki, sl: (ui, bi, qi, 0)),
        scratch_shapes=[pltpu.VMEM((tq, 1), _F32), pltpu.VMEM((tq, 256), _F32)],
    )
    return pl.pallas_call(
        kern,
        grid_spec=grid_spec,
        out_shape=jax.ShapeDtypeStruct((u, b, s, LANES), _F32),
        compiler_params=pltpu.CompilerParams(
            dimension_semantics=("parallel", "parallel", "parallel", "arbitrary"),
            vmem_limit_bytes=_VMEM_LIMIT),
        name="flash_a" if alibi else "flash_b",
    )(slopes, q, k, v)


def _attn_kernel(qt_ref, k_ref, vt_ref, *rest, tq, q_steps, alibi, gated):
    gate_ref = rest[0] if gated else None
    o_ref, qv_sc = rest[-2:]
    _attn_body(qt_ref, k_ref, vt_ref, gate_ref, o_ref, qv_sc, tq=tq, q_steps=q_steps, alibi=alibi)


def _attn_body(qt_ref, k_ref, vt_ref, gate_ref, o_ref, qv_sc, *, tq, q_steps, alibi):
    n_k, _, tk = vt_ref.shape[2:]
    q_sub = qt_ref.shape[1] * qt_ref.shape[3] // tq
    n_diag = tq // tk
    if alibi:
        head = pl.program_id(0) // 2
        sigma = jnp.float32(_alibi_slope(A_HEADS - 1) * LOG2E)
        for hd in range(A_HEADS - 1):
            sigma = jnp.where(head == hd, jnp.float32(_alibi_slope(hd) * LOG2E), sigma)
        aug_row = lax.broadcasted_iota(jnp.int32, (qt_ref.shape[2], tq), 0)
        aug_row = aug_row - jnp.where(pl.program_id(0) % 2 == 0, A_HEAD_DIM, 0)
        bias_rows = (aug_row >= _AUG_QHI) & (aug_row < _AUG_END)

    def query_block(sub, qi, slot):
        pieces = tq // qt_ref.shape[3]
        qt = jnp.concatenate([qt_ref[0, sub * pieces + piece] for piece in range(pieces)], axis=1)
        qv_sc[slot, 0] = qt
        if alibi:
            qv_sc[slot, 1] = jnp.where(bias_rows, -qt, qt)

        def block(var, j, correction=None):
            k = k_ref[0, 0, pl.ds(pl.multiple_of(j * tk, tk), tk), :]
            st = jnp.dot(k, qv_sc[slot, var], preferred_element_type=_F32)
            if correction is not None:
                st = st + correction
            return jnp.dot(vt_ref[0, 0, j], jnp.exp2(st).astype(_BF16), preferred_element_type=_F32)

        def accumulate(acc, part):
            return part if acc is None else acc + part

        acc = None
        if alibi:
            first_diag = qi * n_diag
            for d in range(n_diag):
                ahead = (lax.broadcasted_iota(jnp.int32, (tk, tq), 0) - lax.broadcasted_iota(jnp.int32, (tk, tq), 1)
                         + d * tk)
                corr = (-2.0 * sigma) * jnp.maximum(ahead, 0).astype(_F32)
                acc = accumulate(acc, block(0, first_diag + d, corr))
            for x in range(n_k - n_diag):
                after = (x >= first_diag) * 1
                acc = accumulate(acc, block(after, x + n_diag * after))
        else:
            for x in range(n_k):
                acc = accumulate(acc, block(0, x))
        rows = pl.ds(pl.multiple_of(sub * tq, tq), tq)
        out = (acc[:LANES] / acc[LANES:LANES + 1]).T
        if gate_ref is not None:
            out = out * gate_ref[rows, :].astype(_F32)
        o_ref[0, 0, rows, :] = out.astype(o_ref.dtype)

    if q_steps == 1 or q_sub == 1:
        for sub in range(q_sub):
            query_block(sub, sub if q_steps == 1 else pl.program_id(2), sub)
    else:
        def body(sub, carry):
            query_block(sub, pl.program_id(2) * q_sub + sub, 0)
            return carry

        lax.fori_loop(0, q_sub, body, 0)


def _attn_call(qt, k, vt, *, alibi, units_per_v, tq, gate=None):
    u, _, dk, tm = qt.shape
    _, b, s, _ = k.shape
    n_k, _, tk = vt.shape[2:]
    assert tq % tk == 0 and s % tq == 0 and tq % tm == 0, (s, tq, tk, tm)
    n_q = s // tq
    if n_q * n_k <= MAX_BLOCK_PAIRS:
        q_sub, slots = n_q, n_q
    else:
        q_sub, slots = _largest_group(n_q, MAX_LOOPED_QUERY_BLOCKS), 1
    q_steps = n_q // q_sub
    pieces = tq * q_sub // tm
    kern = functools.partial(_attn_kernel, tq=tq, q_steps=q_steps, alibi=alibi, gated=gate is not None)
    gates = [] if gate is None else [gate]
    return pl.pallas_call(
        kern,
        grid=(u, b, q_steps),
        in_specs=[
            pl.BlockSpec((1, pieces, dk, tm), lambda ui, bi, qi: (ui, bi * q_steps + qi, 0, 0)),
            pl.BlockSpec((1, 1, s, dk), lambda ui, bi, qi: (ui, bi, 0, 0)),
            pl.BlockSpec((1, 1, n_k, VT_ROWS, tk), lambda ui, bi, qi: (ui // units_per_v, bi, 0, 0, 0)),
            *[pl.BlockSpec((tq * q_sub, LANES), lambda ui, bi, qi: (bi * q_steps + qi, ui)) for _ in gates],
        ],
        out_specs=pl.BlockSpec((1, 1, tq * q_sub, LANES), lambda ui, bi, qi: (ui, bi, qi, 0)),
        out_shape=jax.ShapeDtypeStruct((u, b, s, LANES), _BF16),
        scratch_shapes=[pltpu.VMEM((slots, 2 if alibi else 1, dk, tq), _BF16)],
        compiler_params=pltpu.CompilerParams(
            dimension_semantics=("parallel", "parallel", "parallel"),
            vmem_limit_bytes=_VMEM_LIMIT),
        name="attn_a" if alibi else "attn_b",
    )(qt, k, vt, *gates)


def _out_kernel(oa_ref, ob_ref, ga_ref, *rest, lam_init, gated_b):
    gb_ref = None if gated_b else rest[0]
    x_ref, lq1_ref, lk1_ref, lq2_ref, lk2_ref, subln_ref, wout_ref, y_ref = rest[0 if gated_b else 1:]
    lam = (jnp.exp(jnp.sum(lq1_ref[...] * lk1_ref[...], axis=-1, keepdims=True))
           - jnp.exp(jnp.sum(lq2_ref[...] * lk2_ref[...], axis=-1, keepdims=True)) + lam_init)
    ga = ga_ref[...].astype(_F32)
    pieces = []
    for hd in range(A_HEADS):
        d = oa_ref[2 * hd].astype(_F32) - lam * oa_ref[2 * hd + 1].astype(_F32)
        n = d * lax.rsqrt(jnp.mean(d * d, axis=-1, keepdims=True) + EPS) * subln_ref[...]
        pieces.append(n * (1.0 - lam_init) * ga[:, hd * LANES:(hd + 1) * LANES])
    for hd in range(B_HEADS):
        yb = ob_ref[hd].astype(_F32)
        if gb_ref is not None:
            yb = yb * gb_ref[:, hd * LANES:(hd + 1) * LANES].astype(_F32)
        pieces.append(yb)
    y = jnp.concatenate(pieces, axis=1).astype(_BF16)
    y_ref[...] = x_ref[...] + jnp.dot(y, wout_ref[...], preferred_element_type=_F32)


def _out_call(oa, ob, ga, gb, x2d, lw, lam_init, tm):
    t = x2d.shape[0]
    const = lambda i: (0, 0)
    row = lambda i: (i, 0)
    unit_row = lambda i: (0, i, 0)
    gates_b = [] if gb is None else [gb]
    return pl.pallas_call(
        functools.partial(_out_kernel, lam_init=lam_init, gated_b=gb is None),
        grid=(t // tm,),
        in_specs=[
            pl.BlockSpec((A_UNITS, tm, LANES), unit_row),
            pl.BlockSpec((B_HEADS, tm, LANES), unit_row),
            pl.BlockSpec((tm, 512), row),
            *[pl.BlockSpec((tm, 512), row) for _ in gates_b],
            pl.BlockSpec((tm, D_MODEL), row),
            pl.BlockSpec((1, A_HEAD_DIM), const),
            pl.BlockSpec((1, A_HEAD_DIM), const),
            pl.BlockSpec((1, A_HEAD_DIM), const),
            pl.BlockSpec((1, A_HEAD_DIM), const),
            pl.BlockSpec((1, A_V_DIM), const),
            pl.BlockSpec((None, D_MODEL, D_MODEL), lambda i: (lw["layer"], 0, 0)),
        ],
        out_specs=pl.BlockSpec((tm, D_MODEL), row),
        out_shape=jax.ShapeDtypeStruct((t, D_MODEL), _F32),
        compiler_params=pltpu.CompilerParams(
            dimension_semantics=("parallel",), vmem_limit_bytes=_VMEM_LIMIT),
        name="out",
    )(oa, ob, ga, *gates_b, x2d, lw["lq1"], lw["lk1"], lw["lq2"], lw["lk2"], lw["subln"], lw["w_out"])


def _rope_tables(seq_len):
    inv = ROPE_THETA ** (-np.arange(0, B_ROPE, 2, dtype=np.float64) / B_ROPE)
    n_hi = -(-seq_len // POS_SPLIT)
    ang_hi = (np.arange(n_hi, dtype=np.float64) * POS_SPLIT)[:, None, None] * inv
    ang_lo = np.arange(POS_SPLIT, dtype=np.float64)[None, :, None] * inv

    def lanes(first, second):
        pad = np.zeros(first.shape[:-1] + (LANES - B_ROPE,))
        return jnp.asarray(np.concatenate([first, second, pad], axis=-1), _F32)

    cos_hi, sin_hi, cos_lo, sin_lo = np.cos(ang_hi), np.sin(ang_hi), np.cos(ang_lo), np.sin(ang_lo)
    cos_lo, sin_lo = lanes(cos_lo, cos_lo), lanes(sin_lo, sin_lo)
    cos = lanes(cos_hi, cos_hi) * cos_lo - lanes(sin_hi, sin_hi) * sin_lo
    neg_sin = lanes(-sin_hi, sin_hi) * cos_lo + lanes(-cos_hi, cos_hi) * sin_lo
    return (cos.reshape(n_hi * POS_SPLIT, LANES)[:seq_len], neg_sin.reshape(n_hi * POS_SPLIT, LANES)[:seq_len])


def _score_bounds(l, p):
    amax = lambda v: jnp.max(jnp.abs(v[l].astype(_F32)))
    bound_a = amax(p["a_q_norm"]) * amax(p["a_k_norm"]) * (A_HEAD_DIM ** 0.5)
    bound_b = amax(p["b_q_norm"]) * amax(p["b_k_norm"]) * (B_QK_DIM ** 0.5)
    return bound_a, bound_b


def _aug_rows(bound_a, bound_b):
    aug_q = np.zeros((A_UNITS, A_HEAD_DIM, LANES), np.float32)
    aug_k = np.zeros((A_UNITS, 1, LANES), np.float32)
    for u in range(A_UNITS):
        sig = _sigma_pieces(u // 2)
        base = _aug_base(u % 2)
        aug_k[u, 0, base + _AUG_SHIFT] = 1.0
        for a in range(SIGMA_PIECES):
            aug_k[u, 0, base + _AUG_QHI + a] = -sig[a]
            aug_k[u, 0, base + _AUG_QLO + a] = -sig[a]
            aug_q[u, _AUG_KHI + a, :] = sig[a]
            aug_q[u, _AUG_KLO + a, :] = sig[a]
    shift_row = (np.arange(A_HEAD_DIM) == _AUG_SHIFT).astype(np.float32).reshape(1, A_HEAD_DIM, 1)
    aug_q = jnp.asarray(aug_q) - (bound_a * LOG2E) * shift_row
    b_lane = (np.arange(LANES) == _B_SHIFT).astype(np.float32)
    aug_b = jnp.stack([-(bound_b * LOG2E) * b_lane, jnp.asarray(b_lane)])
    return aug_q, jnp.asarray(aug_k), aug_b


def _stacked_weights(p):
    w = p["w_in"].astype(_BF16)
    lo, hi = _SPLITS[6], _SPLITS[7]
    w_tail = jnp.concatenate([w[:, :, lo:hi], _swap_rope_halves(w[:, :, lo:hi]), w[:, :, hi:]], axis=2)
    return w, w_tail, p["w_out"].astype(_BF16)


def _layer_weights(l, p, stacked, rope_c, rope_s):
    w_in, w_tail, w_out = stacked
    wuq = p["b_w_uq"][l].reshape(Q_LORA, B_HEADS, B_QK_DIM)
    w_uq = jnp.concatenate([wuq, _swap_rope_halves(wuq[:, :, B_NOPE:])], axis=2).reshape(Q_LORA, B_HEADS * 256)
    gq = p["b_q_norm"][l]
    gk = p["b_k_norm"][l]
    row = lambda v: v.reshape(1, -1).astype(_F32)
    bound_a, bound_b = _score_bounds(l, p)
    aug_q, aug_k, aug_b = _aug_rows(bound_a, bound_b)
    return {
        "layer": l,
        "norm_w": row(p["norm_w"][l]),
        "w_in": w_in,
        "w_tail": w_tail,
        "g_aq": row(jnp.tile(p["a_q_norm"][l], A_UNITS)),
        "g_ak": row(jnp.tile(p["a_k_norm"][l], A_UNITS)),
        "g_cq": row(p["b_cq_norm"][l]),
        "w_uq": w_uq.astype(_BF16),
        "g_ckv": row(p["b_ckv_norm"][l]),
        "w_ukv": p["b_w_ukv"][l].astype(_BF16),
        "g_bq": row(jnp.concatenate([gq, _swap_rope_halves(gq[B_NOPE:])])),
        "g_bkn": row(gk[:B_NOPE]),
        "g_bkr": row(jnp.concatenate([gk[B_NOPE:], _swap_rope_halves(gk[B_NOPE:])])),
        "rope_c": rope_c,
        "rope_s": rope_s,
        "aug_q": aug_q, "aug_k": aug_k, "aug_b": aug_b,
        "lq1": row(p["a_lq1"][l]), "lk1": row(p["a_lk1"][l]),
        "lq2": row(p["a_lq2"][l]), "lk2": row(p["a_lk2"][l]),
        "subln": row(p["a_subln"][l]),
        "w_out": w_out,
    }


def _tile(n, pref):
    return pref if n % pref == 0 else n


def _largest_group(n, cap):
    return max(g for g in range(1, cap + 1) if n % g == 0) if n > 0 else 1


def _encoder_layer(x, lw, lam_init, slopes, fixed_shift):
    b, s, _ = x.shape
    t = b * s
    x2d = x.reshape(t, D_MODEL)
    tm = _tile(s, FIXED_TK)
    qa, ka, va, ga, qb, kb, vb, gb = _proj_call(x2d, s, lw, tm, fixed_shift)
    split = lambda a: a.reshape(a.shape[:-2] + (b, s, a.shape[-1]))
    tq = tk = _tile(s, 512)
    if fixed_shift:
        assert s <= POS_SPLIT * POS_SPLIT * 2
        ftq = _tile(s, FIXED_TQ)
        split_t = lambda a: a.reshape((a.shape[0], b, s // tm) + a.shape[2:])
        oa = _attn_call(qa, split(ka), split_t(va), alibi=True, units_per_v=2, tq=ftq)
        ob = _attn_call(qb, split(kb), split_t(vb), alibi=False, units_per_v=1, tq=ftq, gate=gb)
        gb = None
    else:
        oa = _flash_call(split(qa), split(ka), split(va), slopes, alibi=True, units_per_v=2, tq=tq, tk=tk)
        ob = _flash_call(split(qb), split(kb), split(vb), slopes, alibi=False, units_per_v=1, tq=tq, tk=tk)
    y = _out_call(oa.reshape(A_UNITS, t, LANES), ob.reshape(B_HEADS, t, LANES), ga, gb, x2d, lw, lam_init,
                  _tile(s, OUT_ROWS))
    return y.reshape(b, s, D_MODEL)


def _forward(x_prompt, x_sample, p, fixed_shift):
    depth = p["norm_w"].shape[0]
    rope_c, rope_s = {}, {}
    for s in {x_prompt.shape[1], x_sample.shape[1]}:
        rope_c[s], rope_s[s] = _rope_tables(s)
    slopes = jnp.asarray([_alibi_slope(hd) for hd in range(A_HEADS)], _F32)
    stacked = _stacked_weights(p)
    y_prompt, y_sample = x_prompt, x_sample
    for l in range(depth):
        lw = _layer_weights(l, p, stacked, rope_c, rope_s)
        lam_init = _lambda_init(l)
        y_prompt = _encoder_layer(y_prompt, lw, lam_init, slopes, fixed_shift)
        y_sample = _encoder_layer(y_sample, lw, lam_init, slopes, fixed_shift)
    return (y_prompt, y_sample)


def kernel(x_prompt, x_sample, norm_w, w_in, a_q_norm, a_k_norm, a_lq1, a_lk1, a_lq2, a_lk2, a_subln,
           b_cq_norm, b_w_uq, b_ckv_norm, b_w_ukv, b_q_norm, b_k_norm, w_out):
    p = dict(norm_w=norm_w, w_in=w_in, a_q_norm=a_q_norm, a_k_norm=a_k_norm, a_lq1=a_lq1, a_lk1=a_lk1,
             a_lq2=a_lq2, a_lk2=a_lk2, a_subln=a_subln, b_cq_norm=b_cq_norm, b_w_uq=b_w_uq,
             b_ckv_norm=b_ckv_norm, b_w_ukv=b_w_ukv, b_q_norm=b_q_norm, b_k_norm=b_k_norm, w_out=w_out)
    bounds = jnp.stack([jnp.stack(_score_bounds(l, p)) for l in range(norm_w.shape[0])])
    shift_ok = jnp.max(bounds) <= MAX_STATIC_SHIFT
    return lax.cond(shift_ok,
                    lambda xp, xs, pp: _forward(xp, xs, pp, True),
                    lambda xp, xs, pp: _forward(xp, xs, pp, False),
                    x_prompt, x_sample, p)
```
